```python
import jax, jax.numpy as jnp
from jax import lax

D_MODEL = 1024
BATCH = 4
SEQ = 4096
DEPTH = 2

GRID_W = 64
CTX_LEN = 256
NORM_EPS = 1e-6
ROPE_THETA = 10000.0
RET_HEADS = 4
RET_DK = 128
RET_DV = 256
RET_CHUNK = 128
SWA_HEADS = 16
SWA_KV_HEADS = 2
SWA_HD = 64
WINDOW = 128
GA_HEADS = 8
GA_KV_HEADS = 2
GA_HD = 128
Q_BLOCK = 128
N_EXPERTS = 16
N_GROUPS = 4
EXPERTS_PER_GROUP = N_EXPERTS // N_GROUPS
TOP_K = 2
D_EXPERT = 512
MOE_BLOCK = 128

RET_QK = RET_HEADS * RET_DK
RET_V = RET_HEADS * RET_DV
SWA_Q = SWA_HEADS * SWA_HD
SWA_KV = SWA_KV_HEADS * SWA_HD
GA_Q = GA_HEADS * GA_HD
GA_KV = GA_KV_HEADS * GA_HD
SPLITS = (RET_QK, RET_QK, RET_V, RET_V, SWA_Q, SWA_KV, SWA_KV, GA_Q, GA_KV, GA_KV, D_MODEL, D_MODEL, D_MODEL)
D_IN = 2 * RET_QK + 2 * RET_V + SWA_Q + 2 * SWA_KV + GA_Q + 2 * GA_KV + 3 * D_MODEL

kernel_name = 'hybrid_retention_swa_gqa_moe_dit'


def rms_normalize(x):
    x32 = x.astype(jnp.float32)
    return (x32 * lax.rsqrt(jnp.mean(x32 * x32, axis=-1, keepdims=True) + NORM_EPS)).astype(x.dtype)


def rms_norm(x, g):
    return rms_normalize(x) * g.astype(x.dtype)


def split_columns(p):
    parts = []
    start = 0
    for width in SPLITS:
        parts.append(p[..., start:start + width])
        start += width
    return parts


def heads(t, n_heads):
    return t.reshape(t.shape[0], t.shape[1], n_heads, -1)


def grid_positions(n_tokens):
    rows = n_tokens // GRID_W
    row = jnp.repeat(jnp.arange(rows, dtype=jnp.int32), GRID_W)
    col = jnp.tile(jnp.arange(GRID_W, dtype=jnp.int32), rows)
    return row, col


def rope_2d(x, row, col):
    hd = x.shape[-1]
    half = hd // 2
    quarter = hd // 4
    freqs = ROPE_THETA ** (-jnp.arange(quarter, dtype=jnp.float32) / quarter)
    parts = []
    for a, pos in enumerate((row, col)):
        ang = pos.astype(jnp.float32)[:, None] * freqs[None, :]
        cos = jnp.cos(ang)[None, :, None, :]
        sin = jnp.sin(ang)[None, :, None, :]
        xa = x[..., a * half:(a + 1) * half].astype(jnp.float32)
        x1, x2 = xa[..., :quarter], xa[..., quarter:]
        parts.append(x1 * cos - x2 * sin)
        parts.append(x1 * sin + x2 * cos)
    return jnp.concatenate(parts, axis=-1).astype(x.dtype)


def retention_chunkwise(q, k, v, log_gamma, state0, strict):
    b, length, h, _ = q.shape
    dv = v.shape[-1]
    n_chunks = length // RET_CHUNK
    idx = jnp.arange(RET_CHUNK, dtype=jnp.float32)
    diff = idx[:, None] - idx[None, :]
    keep = (diff > 0) if strict else (diff >= 0)
    decay_in = jnp.where(keep[None], jnp.exp(jnp.maximum(diff, 0.0)[None] * log_gamma[:, None, None]), 0.0)
    q_decay = jnp.exp((idx[:, None] + 1.0) * log_gamma[None, :])[None, :, :, None]
    k_decay = jnp.exp((RET_CHUNK - 1.0 - idx)[:, None] * log_gamma[None, :])[None, :, :, None]
    chunk_decay = jnp.exp(RET_CHUNK * log_gamma)[None, :, None, None]

    def chunks(t):
        return t.reshape(b, n_chunks, RET_CHUNK, h, t.shape[-1]).swapaxes(0, 1)

    def step(state, qkv):
        qc, kc, vc = qkv
        s = jnp.einsum('bihd,bjhd->bhij', qc, kc) * decay_in
        o = jnp.einsum('bhij,bjhe->bihe', s, vc) + jnp.einsum('bihd,bhde->bihe', qc * q_decay, state)
        state = chunk_decay * state + jnp.einsum('bjhd,bjhe->bhde', kc * k_decay, vc)
        return state, o

    _, o = lax.scan(step, state0, (chunks(q), chunks(k), chunks(v)))
    return o.swapaxes(0, 1).reshape(b, length, h, dv)


def retention_context(q, k, v, lg_f, lg_b):
    n = k.shape[1]
    pos = jnp.arange(n, dtype=jnp.float32)
    diff = pos[:, None] - pos[None, :]
    decay = jnp.where(diff[None] >= 0,
                      jnp.exp(jnp.maximum(diff, 0.0)[None] * lg_f[:, None, None]),
                      jnp.exp(jnp.maximum(-diff, 0.0)[None] * lg_b[:, None, None]))
    s = jnp.einsum('bihd,bjhd->bhij', q, k) * decay
    return jnp.einsum('bhij,bjhe->bihe', s, v)


def retention_context_states(k, v, lg_f, lg_b):
    n = k.shape[1]
    pos = jnp.arange(n, dtype=jnp.float32)
    w_f = jnp.exp((n - 1.0 - pos)[:, None] * lg_f[None, :])
    w_b = jnp.exp(pos[:, None] * lg_b[None, :])
    s_f = jnp.einsum('lh,blhd,blhe->bhde', w_f, k, v).astype(jnp.float32)
    s_b = jnp.einsum('lh,blhd,blhe->bhde', w_b, k, v).astype(jnp.float32)
    return s_f, s_b


def retention_output(o, u):
    b, length, h, dv = o.shape
    y = rms_normalize(o.astype(jnp.float32)) * jax.nn.silu(u.astype(jnp.float32)).reshape(b, length, h, dv)
    return y.reshape(b, length, h * dv).astype(u.dtype)


def window_attention(q, k, v, k_ctx, v_ctx, sink):
    b, length, n_heads, hd = q.shape
    kvh = k.shape[2]
    g = n_heads // kvh
    n_ctx = k_ctx.shape[1]
    nb = length // Q_BLOCK
    band = Q_BLOCK + 2 * WINDOW
    qg = (q * hd ** -0.5).reshape(b, nb, Q_BLOCK, kvh, g, hd).swapaxes(0, 1)
    kp = jnp.pad(k, ((0, 0), (WINDOW, WINDOW), (0, 0), (0, 0)))
    vp = jnp.pad(v, ((0, 0), (WINDOW, WINDOW), (0, 0), (0, 0)))
    sink_g = sink.reshape(kvh, g).astype(jnp.float32)[None, :, :, None, None]
    q_off = jnp.arange(Q_BLOCK, dtype=jnp.int32)
    k_off = jnp.arange(band, dtype=jnp.int32)

    def block(args):
        qb, j = args
        start = j * Q_BLOCK
        kb = lax.dynamic_slice_in_dim(kp, start, band, axis=1)
        vb = lax.dynamic_slice_in_dim(vp, start, band, axis=1)
        qpos = start + q_off
        kpos = start - WINDOW + k_off
        valid = (jnp.abs(qpos[:, None] - kpos[None, :]) <= WINDOW) & (kpos[None, :] >= 0) & (kpos[None, :] < length)
        s_loc = jnp.where(valid, jnp.einsum('bqkgd,bckd->bkgqc', qb, kb).astype(jnp.float32), -jnp.inf)
        s_ctx = jnp.einsum('bqkgd,bckd->bkgqc', qb, k_ctx).astype(jnp.float32)
        s_sink = jnp.broadcast_to(sink_g, s_ctx.shape[:-1] + (1,))
        p = jax.nn.softmax(jnp.concatenate([s_loc, s_ctx, s_sink], axis=-1), axis=-1).astype(v.dtype)
        o = (jnp.einsum('bkgqc,bckd->bqkgd', p[..., :band], vb)
             + jnp.einsum('bkgqc,bckd->bqkgd', p[..., band:band + n_ctx], v_ctx))
        return o.reshape(b, Q_BLOCK, n_heads * hd)

    out = lax.map(block, (qg, jnp.arange(nb, dtype=jnp.int32)))
    return out.swapaxes(0, 1).reshape(b, length, n_heads * hd)


def global_attention(q, k_all, v_all):
    b, length, n_heads, hd = q.shape
    kvh = k_all.shape[2]
    g = n_heads // kvh
    nb = length // Q_BLOCK
    qg = (q * hd ** -0.5).reshape(b, nb, Q_BLOCK, kvh, g, hd).swapaxes(0, 1)

    def block(qb):
        s = jnp.einsum('bqkgd,bckd->bkgqc', qb, k_all).astype(jnp.float32)
        p = jax.nn.softmax(s, axis=-1).astype(v_all.dtype)
        return jnp.einsum('bkgqc,bckd->bqkgd', p, v_all).reshape(b, Q_BLOCK, n_heads * hd)

    out = lax.map(block, qg)
    return out.swapaxes(0, 1).reshape(b, length, n_heads * hd)


def context_attention(q, k, v, sink):
    b, n, n_heads, hd = q.shape
    kvh = k.shape[2]
    g = n_heads // kvh
    qg = (q * hd ** -0.5).reshape(b, n, kvh, g, hd)
    s = jnp.einsum('bqkgd,bckd->bkgqc', qg, k).astype(jnp.float32)
    if sink is not None:
        s_sink = jnp.broadcast_to(sink.reshape(kvh, g).astype(jnp.float32)[None, :, :, None, None], s.shape[:-1] + (1,))
        p = jax.nn.softmax(jnp.concatenate([s, s_sink], axis=-1), axis=-1)[..., :n]
    else:
        p = jax.nn.softmax(s, axis=-1)
    o = jnp.einsum('bkgqc,bckd->bqkgd', p.astype(v.dtype), v)
    return o.reshape(b, n, n_heads * hd)


def merge_branches(o_ret, o_swa, o_ga, a_ret, a_swa, a_ga, w_br_ret, w_br_swa, w_br_ga, w_out):
    y = (jax.nn.sigmoid(a_ret) * (o_ret @ w_br_ret)
         + jax.nn.sigmoid(a_swa) * (o_swa @ w_br_swa)
         + jax.nn.sigmoid(a_ga) * (o_ga @ w_br_ga))
    return y @ w_out


def mixer_sublayer(h, hc, row, col, w_in, ret_logit, sink, g_q, g_k,
                   w_br_ret, w_br_swa, w_br_ga, w_out, need_ctx):
    (q_r, k_r, v_r, u_r, q_s, k_s, v_s, q_a, k_a, v_a, a_r, a_s, a_a) = split_columns(h @ w_in)
    (cq_r, ck_r, cv_r, cu_r, cq_s, ck_s, cv_s, cq_a, ck_a, cv_a, ca_r, ca_s, ca_a) = split_columns(hc @ w_in)
    log_gamma = jax.nn.log_sigmoid(ret_logit.astype(jnp.float32))
    lg_f, lg_b = log_gamma[0], log_gamma[1]
    k_scale = RET_DK ** -0.5

    ckr = heads(ck_r, RET_HEADS) * k_scale
    cvr = heads(cv_r, RET_HEADS)
    s_f, s_b = retention_context_states(ckr, cvr, lg_f, lg_b)
    cks = heads(ck_s, SWA_KV_HEADS)
    cvs = heads(cv_s, SWA_KV_HEADS)
    cka = rms_norm(heads(ck_a, GA_KV_HEADS), g_k)
    cva = heads(cv_a, GA_KV_HEADS)

    qr = rope_2d(heads(q_r, RET_HEADS), row, col)
    kr = rope_2d(heads(k_r, RET_HEADS), row, col) * k_scale
    vr = heads(v_r, RET_HEADS)
    o_f = retention_chunkwise(qr, kr, vr, lg_f, s_f, False)
    o_b = retention_chunkwise(qr[:, ::-1], kr[:, ::-1], vr[:, ::-1], lg_b, s_b, True)[:, ::-1]
    o_ret = retention_output(o_f + o_b, u_r)

    o_swa = window_attention(rope_2d(heads(q_s, SWA_HEADS), row, col),
                             rope_2d(heads(k_s, SWA_KV_HEADS), row, col),
                             heads(v_s, SWA_KV_HEADS), cks, cvs, sink)

    qa = rope_2d(rms_norm(heads(q_a, GA_HEADS), g_q), row, col)
    ka = rope_2d(rms_norm(heads(k_a, GA_KV_HEADS), g_k), row, col)
    k_all = jnp.concatenate([ka, cka], axis=1)
    v_all = jnp.concatenate([heads(v_a, GA_KV_HEADS), cva], axis=1)
    o_ga = global_attention(qa, k_all, v_all)

    y = merge_branches(o_ret, o_swa, o_ga, a_r, a_s, a_a, w_br_ret, w_br_swa, w_br_ga, w_out)
    yc = None
    if need_ctx:
        co_ret = retention_output(retention_context(heads(cq_r, RET_HEADS), ckr, cvr, lg_f, lg_b), cu_r)
        co_swa = context_attention(heads(cq_s, SWA_HEADS), cks, cvs, sink)
        co_ga = context_attention(rms_norm(heads(cq_a, GA_HEADS), g_q), cka, cva, None)
        yc = merge_branches(co_ret, co_swa, co_ga, ca_r, ca_s, ca_a, w_br_ret, w_br_swa, w_br_ga, w_out)
    return y, yc


def moe_ffn(tokens, w_router, b_router, w_gate, w_up, w_down):
    n_tok, d = tokens.shape
    scores = jax.nn.sigmoid(tokens.astype(jnp.float32) @ w_router.astype(jnp.float32))
    biased = (scores + b_router.astype(jnp.float32)).reshape(n_tok, N_GROUPS, EXPERTS_PER_GROUP)
    group_score = lax.top_k(biased, TOP_K)[0].sum(-1)
    group = jnp.argmax(group_score, axis=-1).astype(jnp.int32)
    in_group = jnp.take_along_axis(biased, group[:, None, None], axis=1)[:, 0]
    local = lax.top_k(in_group, TOP_K)[1]
    expert = group[:, None] * EXPERTS_PER_GROUP + local
    weight = jnp.take_along_axis(scores, expert, axis=1)
    weight = weight / jnp.sum(weight, axis=-1, keepdims=True)

    n_assign = n_tok * TOP_K
    flat_e = expert.reshape(-1)
    flat_t = jnp.repeat(jnp.arange(n_tok, dtype=jnp.int32), TOP_K)
    flat_w = weight.reshape(-1)
    order = jnp.argsort(flat_e)
    se, st, sw = flat_e[order], flat_t[order], flat_w[order]
    counts = jax.ops.segment_sum(jnp.ones_like(flat_e), flat_e, num_segments=N_EXPERTS)
    padded = (counts + MOE_BLOCK - 1) // MOE_BLOCK * MOE_BLOCK
    starts = jnp.cumsum(counts) - counts
    pad_ends = jnp.cumsum(padded)
    pad_starts = pad_ends - padded
    dest = pad_starts[se] + jnp.arange(n_assign, dtype=jnp.int32) - starts[se]
    n_blocks = (n_assign + N_EXPERTS * (MOE_BLOCK - 1) + MOE_BLOCK - 1) // MOE_BLOCK
    buf = jnp.zeros((n_blocks * MOE_BLOCK, d), tokens.dtype).at[dest].set(tokens[st])
    block_e = jnp.minimum(jnp.searchsorted(pad_ends, jnp.arange(n_blocks, dtype=jnp.int32) * MOE_BLOCK, side='right'),
                          N_EXPERTS - 1)

    def expert_block(args):
        xb, e = args
        hid = jax.nn.silu(xb @ w_gate[e]) * (xb @ w_up[e])
        return hid @ w_down[e]

    y = lax.map(expert_block, (buf.reshape(n_blocks, MOE_BLOCK, d), block_e)).reshape(-1, d)
    contrib = y[dest] * sw[:, None].astype(y.dtype)
    return jnp.zeros_like(tokens).at[st].add(contrib)


def setup_inputs(seed: int = 0) -> dict:
    key = jax.random.key(seed)
    ks = jax.random.split(key, 24)
    f32 = jnp.float32
    nrm = jax.random.normal

    def w(k, shape, fan_in, scale=1.0):
        return nrm(k, shape, f32) * (scale * fan_in ** -0.5)

    def gain(k, shape):
        return 1.0 + 0.02 * nrm(k, shape, f32)

    decay0 = jnp.log(2.0 ** (5.0 + jnp.arange(RET_HEADS, dtype=f32)) - 1.0)
    return {
        'x': nrm(ks[0], (BATCH, SEQ, D_MODEL), f32),
        'c': nrm(ks[1], (BATCH, D_MODEL), f32),
        'ctx': nrm(ks[2], (BATCH, CTX_LEN, D_MODEL), f32),
        'c_ctx': nrm(ks[3], (D_MODEL,), f32),
        'w_mod': w(ks[4], (DEPTH, D_MODEL, 6 * D_MODEL), D_MODEL, 0.5),
        'b_mod': 0.01 * nrm(ks[5], (DEPTH, 6 * D_MODEL), f32),
        'g_norm1': gain(ks[6], (DEPTH, D_MODEL)),
        'g_norm2': gain(ks[7], (DEPTH, D_MODEL)),
        'w_in': w(ks[8], (DEPTH, D_MODEL, D_IN), D_MODEL),
        'ret_decay_logit': decay0 + 0.1 * nrm(ks[9], (DEPTH, 2, RET_HEADS), f32),
        'swa_sink': 0.5 * nrm(ks[10], (DEPTH, SWA_HEADS), f32),
        'g_qnorm': gain(ks[11], (DEPTH, GA_HD)),
        'g_knorm': gain(ks[12], (DEPTH, GA_HD)),
        'w_br_ret': w(ks[13], (DEPTH, RET_V, D_MODEL), RET_V),
        'w_br_swa': w(ks[14], (DEPTH, SWA_Q, D_MODEL), SWA_Q),
        'w_br_ga': w(ks[15], (DEPTH, GA_Q, D_MODEL), GA_Q),
        'w_out': w(ks[16], (DEPTH, D_MODEL, D_MODEL), D_MODEL),
        'w_router': w(ks[17], (D_MODEL, N_EXPERTS), D_MODEL),
        'b_router': 0.01 * nrm(ks[18], (N_EXPERTS,), f32),
        'w_gate': w(ks[19], (DEPTH, N_EXPERTS, D_MODEL, D_EXPERT), D_MODEL),
        'w_up': w(ks[20], (DEPTH, N_EXPERTS, D_MODEL, D_EXPERT), D_MODEL),
        'w_down': w(ks[21], (DEPTH, N_EXPERTS, D_EXPERT, D_MODEL), D_EXPERT),
        'g_final': gain(ks[22], (D_MODEL,)),
    }


def reference(x, c, ctx, c_ctx, w_mod, b_mod, g_norm1, g_norm2, w_in, ret_decay_logit, swa_sink,
              g_qnorm, g_knorm, w_br_ret, w_br_swa, w_br_ga, w_out, w_router, b_router,
              w_gate, w_up, w_down, g_final):
    b, length, d = x.shape
    row, col = grid_positions(length)
    for l in range(DEPTH):
        need_ctx = l < DEPTH - 1
        mod = (jax.nn.silu(c) @ w_mod[l] + b_mod[l])[:, None, :]
        mod_c = (jax.nn.silu(c_ctx) @ w_mod[l] + b_mod[l])[None, None, :]
        sh1, sc1, gt1, sh2, sc2, gt2 = jnp.split(mod, 6, axis=-1)
        csh1, csc1, cgt1, csh2, csc2, cgt2 = jnp.split(mod_c, 6, axis=-1)

        h = rms_norm(x, g_norm1[l]) * (1.0 + sc1) + sh1
        hc = rms_norm(ctx, g_norm1[l]) * (1.0 + csc1) + csh1
        y, yc = mixer_sublayer(h, hc, row, col, w_in[l], ret_decay_logit[l], swa_sink[l], g_qnorm[l], g_knorm[l],
                               w_br_ret[l], w_br_swa[l], w_br_ga[l], w_out[l], need_ctx)
        x = x + gt1 * y

        h = rms_norm(x, g_norm2[l]) * (1.0 + sc2) + sh2
        if need_ctx:
            ctx = ctx + cgt1 * yc
            hc = rms_norm(ctx, g_norm2[l]) * (1.0 + csc2) + csh2
            tokens = jnp.concatenate([h.reshape(-1, d), hc.reshape(-1, d)], axis=0)
            out = moe_ffn(tokens, w_router, b_router, w_gate[l], w_up[l], w_down[l])
            x = x + gt2 * out[:b * length].reshape(x.shape)
            ctx = ctx + cgt2 * out[b * length:].reshape(ctx.shape)
        else:
            x = x + gt2 * moe_ffn(h.reshape(-1, d), w_router, b_router, w_gate[l], w_up[l], w_down[l]).reshape(x.shape)
    return rms_norm(x, g_final)
```

```python
import functools

import jax
import jax.numpy as jnp
from jax import lax
from jax.experimental import pallas as pl
from jax.experimental.pallas import tpu as pltpu

F32 = jnp.float32
BF16 = jnp.bfloat16

D_MODEL = 1024
GRID_W = 64
NORM_EPS = 1e-6
ROPE_THETA = 10000.0
RET_HEADS, RET_DK, RET_DV, RET_CHUNK = 4, 128, 256, 128
SWA_HEADS, SWA_KV_HEADS, SWA_HD, WINDOW = 16, 2, 64, 128
GA_HEADS, GA_KV_HEADS, GA_HD = 8, 2, 128
N_EXPERTS, N_GROUPS, EXPERTS_PER_GROUP, D_EXPERT = 16, 4, 4, 512

RET_QK = RET_HEADS * RET_DK
RET_V = RET_HEADS * RET_DV
SWA_Q = SWA_HEADS * SWA_HD
SWA_KV = SWA_KV_HEADS * SWA_HD
GA_Q = GA_HEADS * GA_HD
GA_KV = GA_KV_HEADS * GA_HD
SPLITS = (RET_QK, RET_QK, RET_V, RET_V, SWA_Q, SWA_KV, SWA_KV, GA_Q, GA_KV, GA_KV,
          D_MODEL, D_MODEL, D_MODEL)

LANES = 128
TOKEN_TILE = 256
QUERY_TILE = 128
MOE_ROWS = 256
MOD_COLS = 1536
VMEM_LIMIT = 56 * 1024 * 1024
MASKED = -1e30

_W_WIDTHS = (RET_QK, RET_QK, RET_V, RET_V, SWA_Q, 2 * SWA_KV, 2 * SWA_KV, GA_Q, GA_KV, GA_KV,
             D_MODEL, D_MODEL, D_MODEL)
_W_OFFS = tuple(sum(_W_WIDTHS[:i]) for i in range(len(_W_WIDTHS)))
W_ALL = sum(_W_WIDTHS)


def _dot(a, b):
    return jnp.dot(a, b, preferred_element_type=F32)


def _dot_nt(a, b):
    return lax.dot_general(a, b, (((1,), (1,)), ((), ())), preferred_element_type=F32)


def _dot_tn(a, b):
    return lax.dot_general(a, b, (((0,), (0,)), ((), ())), preferred_element_type=F32)


def _silu(x):
    return x * jax.nn.sigmoid(x)


def _rms_normalize(x):
    return x * lax.rsqrt(jnp.mean(x * x, axis=-1, keepdims=True) + NORM_EPS)


def _params(**kw):
    return pltpu.CompilerParams(vmem_limit_bytes=VMEM_LIMIT, **kw)


def _mod_kernel(c_ref, w_ref, b_ref, o_ref):
    a = _silu(c_ref[...])
    o_ref[0] = _dot(a.astype(BF16), w_ref[0].astype(BF16)) + b_ref[0]


def _modulation(c_rows, w_mod, b_mod):
    depth, d, n = w_mod.shape
    rows = c_rows.shape[0]
    return pl.pallas_call(
        _mod_kernel,
        out_shape=jax.ShapeDtypeStruct((depth, rows, n), F32),
        grid=(depth, n // MOD_COLS),
        in_specs=[pl.BlockSpec((rows, d), lambda l, j: (0, 0)),
                  pl.BlockSpec((1, d, MOD_COLS), lambda l, j: (l, 0, j)),
                  pl.BlockSpec((1, 1, MOD_COLS), lambda l, j: (l, 0, j))],
        out_specs=pl.BlockSpec((1, rows, MOD_COLS), lambda l, j: (l, 0, j)),
        compiler_params=_params(),
        name="modulation",
    )(c_rows, w_mod, b_mod.reshape(depth, 1, n))


def _mod_row(mod_ref, n_lat_tiles, n_batch):
    b = pl.program_id(0)
    i = pl.program_id(1)
    r = jnp.where(i < n_lat_tiles, b, n_batch)
    return mod_ref[pl.ds(r, 1), :]


def _mod_chunk(m, k):
    return m[:, k * D_MODEL:(k + 1) * D_MODEL]


def _prenorm_kernel(x_ref, mod_ref, g_ref, h_ref, *, n_lat_tiles, n_batch):
    m = _mod_row(mod_ref, n_lat_tiles, n_batch)
    h = _rms_normalize(x_ref[0]) * g_ref[...] * (1.0 + _mod_chunk(m, 1)) + _mod_chunk(m, 0)
    h_ref[0] = h.astype(BF16)


def _prenorm(xa, mod, g, n_lat_tiles):
    nb, nt, d = xa.shape
    kern = functools.partial(_prenorm_kernel, n_lat_tiles=n_lat_tiles, n_batch=nb)
    return pl.pallas_call(
        kern,
        out_shape=jax.ShapeDtypeStruct((nb, nt, d), BF16),
        grid=(nb, nt // TOKEN_TILE),
        in_specs=[pl.BlockSpec((1, TOKEN_TILE, d), lambda b, i: (b, i, 0)),
                  pl.BlockSpec(mod.shape, lambda b, i: (0, 0)),
                  pl.BlockSpec((1, d), lambda b, i: (0, 0))],
        out_specs=pl.BlockSpec((1, TOKEN_TILE, d), lambda b, i: (b, i, 0)),
        compiler_params=_params(),
        name="prenorm",
    )(xa, mod, g.reshape(1, d))


def _rope(x, cos, sin, quarter):
    lane = lax.broadcasted_iota(jnp.int32, x.shape, 1)
    first = (lane % (2 * quarter)) < quarter
    partner = jnp.where(first, pltpu.roll(x, LANES - quarter, 1), pltpu.roll(x, quarter, 1))
    return x * cos + partner * sin


def _inproj_kernel(h_ref, w_ref, c128_ref, s128_ref, c64_ref, s64_ref, gq_ref, gk_ref,
                   rq, rk, rv, ru, sq, sk, sv, gq, gk, gv, ar, as_, aa):
    h = h_ref[0]
    c128, s128 = c128_ref[...], s128_ref[...]
    c64, s64 = c64_ref[...], s64_ref[...]

    def proj(idx):
        return _dot(h, w_ref[:, _W_OFFS[idx]:_W_OFFS[idx] + _W_WIDTHS[idx]])

    def slabs(acc):
        return [acc[:, s * LANES:(s + 1) * LANES] for s in range(acc.shape[1] // LANES)]

    def store(ref, s, val):
        ref[0, :, s * LANES:(s + 1) * LANES] = val.astype(ref.dtype)

    for s, xs in enumerate(slabs(proj(0))):
        store(rq, s, _rope(xs, c128, s128, RET_DK // 4))
    for s, xs in enumerate(slabs(proj(1))):
        store(rk, s, _rope(xs, c128, s128, RET_DK // 4) * (RET_DK ** -0.5))
    rv[0] = proj(2).astype(rv.dtype)
    ru[0] = proj(3).astype(ru.dtype)
    for s, xs in enumerate(slabs(proj(4))):
        store(sq, s, _rope(xs, c64, s64, SWA_HD // 4) * (SWA_HD ** -0.5))
    for s, xs in enumerate(slabs(proj(5))):
        store(sk, s, _rope(xs, c64, s64, SWA_HD // 4))
    sv[0] = proj(6).astype(sv.dtype)
    for s, xs in enumerate(slabs(proj(7))):
        xn = _rms_normalize(xs) * gq_ref[...]
        store(gq, s, _rope(xn, c128, s128, GA_HD // 4) * (GA_HD ** -0.5))
    for s, xs in enumerate(slabs(proj(8))):
        xn = _rms_normalize(xs) * gk_ref[...]
        store(gk, s, _rope(xn, c128, s128, GA_HD // 4))
    gv[0] = proj(9).astype(gv.dtype)
    ar[0] = proj(10).astype(ar.dtype)
    as_[0] = proj(11).astype(as_.dtype)
    aa[0] = proj(12).astype(aa.dtype)


def _inproj(h, w_all, tables, g_q, g_k):
    nb, nt, d = h.shape
    tile = lambda width: pl.BlockSpec((1, TOKEN_TILE, width), lambda b, i: (b, i, 0))
    tab = pl.BlockSpec((TOKEN_TILE, LANES), lambda b, i: (i, 0))
    vec = pl.BlockSpec((1, LANES), lambda b, i: (0, 0))
    outs = [jax.ShapeDtypeStruct((nb, nt, w), BF16) for w in _W_WIDTHS]
    return pl.pallas_call(
        _inproj_kernel,
        out_shape=outs,
        grid=(nb, nt // TOKEN_TILE),
        in_specs=[tile(d),
                  pl.BlockSpec((d, W_ALL), lambda b, i: (0, 0), pipeline_mode=pl.Buffered(1)),
                  tab, tab, tab, tab, vec, vec],
        out_specs=[tile(w) for w in _W_WIDTHS],
        compiler_params=_params(),
        name="inproj",
    )(h, w_all, *tables, g_q.reshape(1, LANES), g_k.reshape(1, LANES))


def _log_sigmoid(x):
    return jnp.minimum(x, 0.0) - jnp.log1p(jnp.exp(-jnp.abs(x)))


def _ret_kernel(logit_ref, q_ref, k_ref, v_ref, u_ref, o_ref, acc_ref, sf_ref, sb_ref,
                *, length, n_ctx, need_ctx):
    hh = pl.program_id(1)
    chunk = RET_CHUNK
    n_chunks = length // chunk
    lgf = _log_sigmoid(jnp.full((1, 1), logit_ref[0, hh], F32))
    lgb = _log_sigmoid(jnp.full((1, 1), logit_ref[1, hh], F32))

    def rows_f32(shape):
        return lax.broadcasted_iota(jnp.int32, shape, 0).astype(F32)

    def both_ways_decay(n):
        diff = rows_f32((n, n)) - lax.broadcasted_iota(jnp.int32, (n, n), 1).astype(F32)
        return jnp.where(diff >= 0, jnp.exp(jnp.maximum(diff, 0.0) * lgf),
                         jnp.exp(jnp.maximum(-diff, 0.0) * lgb))

    def finish(o, u):
        return (_rms_normalize(o) * _silu(u.astype(F32))).astype(o_ref.dtype)

    kx = k_ref[0, length:length + n_ctx, :].astype(F32)
    vx = v_ref[0, length:length + n_ctx, :]
    lx = rows_f32((n_ctx, RET_DK))
    sf_ref[...] = _dot_tn((kx * jnp.exp((n_ctx - 1.0 - lx) * lgf)).astype(BF16), vx)
    sb_ref[...] = _dot_tn((kx * jnp.exp(lx * lgb)).astype(BF16), vx)

    if need_ctx:
        qx = q_ref[0, length:length + n_ctx, :]
        sx = _dot_nt(qx, kx.astype(BF16)) * both_ways_decay(n_ctx)
        o_ref[0, length:length + n_ctx, :] = finish(_dot(sx.astype(BF16), vx),
                                                    u_ref[0, length:length + n_ctx, :])

    ci = rows_f32((chunk, RET_DK))
    q_dec_f = jnp.exp((ci + 1.0) * lgf)
    k_dec_f = jnp.exp((chunk - 1.0 - ci) * lgf)
    q_dec_b = jnp.exp((chunk - ci) * lgb)
    k_dec_b = jnp.exp(ci * lgb)
    chunk_dec_f = jnp.exp(chunk * lgf)
    chunk_dec_b = jnp.exp(chunk * lgb)
    intra = both_ways_decay(chunk)
    acc_ref[...] = jnp.zeros_like(acc_ref)

    def step(s, carry):
        rf = pl.multiple_of(s * chunk, chunk)
        rb = pl.multiple_of((n_chunks - 1 - s) * chunk, chunk)
        qf = q_ref[0, pl.ds(rf, chunk), :]
        kf = k_ref[0, pl.ds(rf, chunk), :]
        vf = v_ref[0, pl.ds(rf, chunk), :]
        sc = _dot_nt(qf, kf) * intra
        state_f = sf_ref[...]
        o = _dot(sc.astype(BF16), vf) + _dot((qf.astype(F32) * q_dec_f).astype(BF16), state_f.astype(BF16))
        acc_ref[pl.ds(rf, chunk), :] += o
        sf_ref[...] = state_f * chunk_dec_f + _dot_tn((kf.astype(F32) * k_dec_f).astype(BF16), vf)
        qb = q_ref[0, pl.ds(rb, chunk), :]
        kb = k_ref[0, pl.ds(rb, chunk), :]
        vb = v_ref[0, pl.ds(rb, chunk), :]
        state_b = sb_ref[...]
        acc_ref[pl.ds(rb, chunk), :] += _dot((qb.astype(F32) * q_dec_b).astype(BF16), state_b.astype(BF16))
        sb_ref[...] = state_b * chunk_dec_b + _dot_tn((kb.astype(F32) * k_dec_b).astype(BF16), vb)
        return carry

    lax.fori_loop(0, n_chunks, step, 0)

    def out_step(c, carry):
        r = pl.multiple_of(c * chunk, chunk)
        o_ref[0, pl.ds(r, chunk), :] = finish(acc_ref[pl.ds(r, chunk), :], u_ref[0, pl.ds(r, chunk), :])
        return carry

    lax.fori_loop(0, n_chunks, out_step, 0)


def _retention(logit, rq, rk, rv, ru, length, n_ctx, need_ctx):
    nb, nt, _ = rq.shape
    out_rows = nt if need_ctx else length
    kern = functools.partial(_ret_kernel, length=length, n_ctx=n_ctx, need_ctx=need_ctx)
    head = lambda width: pl.BlockSpec((1, nt, width), lambda b, h: (b, 0, h))
    return pl.pallas_call(
        kern,
        out_shape=jax.ShapeDtypeStruct((nb, out_rows, RET_V), BF16),
        grid=(nb, RET_HEADS),
        in_specs=[pl.BlockSpec(memory_space=pltpu.SMEM),
                  head(RET_DK), head(RET_DK), head(RET_DV), head(RET_DV)],
        out_specs=pl.BlockSpec((1, out_rows, RET_DV), lambda b, h: (b, 0, h)),
        scratch_shapes=[pltpu.VMEM((length, RET_DV), F32),
                        pltpu.VMEM((RET_DK, RET_DV), F32),
                        pltpu.VMEM((RET_DK, RET_DV), F32)],
        compiler_params=_params(),
        name="retention",
    )(logit, rq, rk, rv, ru)


def _swa_group(q_ref, g, kcat, vcat, bias, sink_ref, o_ref):
    tq = q_ref.shape[1]
    heads_per_group = SWA_HEADS // SWA_KV_HEADS
    slabs_per_group = heads_per_group // 2
    low = lax.broadcasted_iota(jnp.int32, (tq, LANES), 1) < SWA_HD
    zero = jnp.zeros((tq, LANES), q_ref.dtype)
    stacked = []
    for a in range(slabs_per_group):
        s0 = (g * slabs_per_group + a) * LANES
        slab = q_ref[0, :, s0:s0 + LANES]
        stacked.append(jnp.where(low, slab, zero))
        stacked.append(jnp.where(low, zero, slab))
    q = jnp.concatenate(stacked, axis=0)
    s = _dot_nt(q, kcat)
    if bias is not None:
        nk = s.shape[1]
        s = (s.reshape(heads_per_group, tq, nk) + bias[None]).reshape(heads_per_group * tq, nk)
    sink = jnp.concatenate(
        [jnp.full((tq, 1), sink_ref[g * heads_per_group + h], F32) for h in range(heads_per_group)], axis=0)
    m = jnp.maximum(jnp.max(s, axis=-1, keepdims=True), sink)
    p = jnp.exp(s - m)
    den = jnp.sum(p, axis=-1, keepdims=True) + jnp.exp(sink - m)
    o = _dot(p.astype(BF16), vcat) / den
    for a in range(slabs_per_group):
        s0 = (g * slabs_per_group + a) * LANES
        even = o[(2 * a) * tq:(2 * a + 1) * tq]
        odd = o[(2 * a + 1) * tq:(2 * a + 2) * tq]
        o_ref[0, :, s0:s0 + LANES] = jnp.where(low, even, odd).astype(o_ref.dtype)


def _swa_kernel(sink_ref, q_ref, kp_ref, kc_ref, kn_ref, kx_ref, vp_ref, vc_ref, vn_ref, vx_ref, o_ref,
                *, n_lat_tiles, need_ctx):
    j = pl.program_id(1)
    tq = q_ref.shape[1]
    n_ctx = kx_ref.shape[1]

    def lanes_of(ref, g):
        return ref[0, :, g * LANES:(g + 1) * LANES]

    @pl.when(j < n_lat_tiles)
    def _latent():
        nk = 3 * tq + n_ctx
        ci = lax.broadcasted_iota(jnp.int32, (tq, nk), 1)
        ri = lax.broadcasted_iota(jnp.int32, (tq, nk), 0)
        far = 4 * tq
        ri_prev = ri + jnp.where(j > 0, 0, far)
        ri_next = ri - jnp.where(j < n_lat_tiles - 1, 0, far)
        zero = jnp.zeros((tq, nk), F32)
        masked = jnp.full((tq, nk), MASKED, F32)
        bias = jnp.where(ci < tq, jnp.where(ci >= ri_prev, zero, masked),
                         jnp.where((ci >= 2 * tq) & (ci < 3 * tq),
                                   jnp.where((ci - 2 * tq) <= ri_next, zero, masked), zero))
        for g in range(SWA_KV_HEADS):
            kcat = jnp.concatenate([lanes_of(kp_ref, g), lanes_of(kc_ref, g), lanes_of(kn_ref, g),
                                    lanes_of(kx_ref, g)], axis=0)
            vcat = jnp.concatenate([lanes_of(vp_ref, g), lanes_of(vc_ref, g), lanes_of(vn_ref, g),
                                    lanes_of(vx_ref, g)], axis=0)
            _swa_group(q_ref, g, kcat, vcat, bias, sink_ref, o_ref)

    if need_ctx:
        @pl.when(j >= n_lat_tiles)
        def _context():
            for g in range(SWA_KV_HEADS):
                _swa_group(q_ref, g, lanes_of(kx_ref, g), lanes_of(vx_ref, g), None, sink_ref, o_ref)


def _window_attention(sink, sq, sk, sv, length, n_ctx, need_ctx):
    nb, nt, _ = sq.shape
    tq = QUERY_TILE
    n_lat = length // tq
    out_rows = nt if need_ctx else length
    kvw = 2 * SWA_KV
    kern = functools.partial(_swa_kernel, n_lat_tiles=n_lat, need_ctx=need_ctx)
    prev = pl.BlockSpec((1, tq, kvw), lambda b, j: (b, jnp.clip(j - 1, 0, n_lat - 1), 0))
    cur = pl.BlockSpec((1, tq, kvw), lambda b, j: (b, jnp.minimum(j, n_lat - 1), 0))
    nxt = pl.BlockSpec((1, tq, kvw), lambda b, j: (b, jnp.minimum(j + 1, n_lat - 1), 0))
    ctx = pl.BlockSpec((1, n_ctx, kvw), lambda b, j: (b, length // n_ctx, 0))
    return pl.pallas_call(
        kern,
        out_shape=jax.ShapeDtypeStruct((nb, out_rows, SWA_Q), BF16),
        grid=(nb, out_rows // tq),
        in_specs=[pl.BlockSpec(memory_space=pltpu.SMEM),
                  pl.BlockSpec((1, tq, SWA_Q), lambda b, j: (b, j, 0)),
                  prev, cur, nxt, ctx, prev, cur, nxt, ctx],
        out_specs=pl.BlockSpec((1, tq, SWA_Q), lambda b, j: (b, j, 0)),
        compiler_params=_params(),
        name="window_attention",
    )(sink, sq, sk, sk, sk, sk, sv, sv, sv, sv)


def _ga_kernel(q_ref, k_ref, v_ref, o_ref, *, length, n_ctx, n_lat_tiles, need_ctx):
    j = pl.program_id(2)
    tq = q_ref.shape[1]
    heads_per_group = GA_HEADS // GA_KV_HEADS
    q = jnp.concatenate([q_ref[0, :, h * LANES:(h + 1) * LANES] for h in range(heads_per_group)], axis=0)

    def attend(k, v):
        s = _dot_nt(q, k)
        m = jnp.max(s, axis=-1, keepdims=True)
        p = jnp.exp(s - m)
        den = jnp.sum(p, axis=-1, keepdims=True)
        o = _dot(p.astype(BF16), v) / den
        for h in range(heads_per_group):
            o_ref[0, :, h * LANES:(h + 1) * LANES] = o[h * tq:(h + 1) * tq].astype(o_ref.dtype)

    @pl.when(j < n_lat_tiles)
    def _latent():
        attend(k_ref[0], v_ref[0])

    if need_ctx:
        @pl.when(j >= n_lat_tiles)
        def _context():
            attend(k_ref[0, length:length + n_ctx, :], v_ref[0, length:length + n_ctx, :])


def _global_attention(gq, gk, gv, length, n_ctx, need_ctx):
    nb, nt, _ = gq.shape
    tq = QUERY_TILE
    out_rows = nt if need_ctx else length
    group_w = GA_Q // GA_KV_HEADS
    kern = functools.partial(_ga_kernel, length=length, n_ctx=n_ctx, n_lat_tiles=length // tq,
                             need_ctx=need_ctx)
    kv = pl.BlockSpec((1, nt, GA_HD), lambda b, g, j: (b, 0, g))
    return pl.pallas_call(
        kern,
        out_shape=jax.ShapeDtypeStruct((nb, out_rows, GA_Q), BF16),
        grid=(nb, GA_KV_HEADS, out_rows // tq),
        in_specs=[pl.BlockSpec((1, tq, group_w), lambda b, g, j: (b, j, g)), kv, kv],
        out_specs=pl.BlockSpec((1, tq, group_w), lambda b, g, j: (b, j, g)),
        compiler_params=_params(),
        name="global_attention",
    )(gq, gk, gv)


def _route(scores, biased):
    rows = [biased[r:r + 1, :] for r in range(N_EXPERTS)]
    raw = [scores[r:r + 1, :] for r in range(N_EXPERTS)]

    def top2_sum(vals):
        best = None
        for a in range(len(vals)):
            for b in range(a + 1, len(vals)):
                pair = vals[a] + vals[b]
                best = pair if best is None else jnp.maximum(best, pair)
        return best

    group_scores = [top2_sum(rows[g * EXPERTS_PER_GROUP:(g + 1) * EXPERTS_PER_GROUP]) for g in range(N_GROUPS)]
    group = jnp.zeros_like(group_scores[0], dtype=jnp.int32)
    best = group_scores[0]
    for g in range(1, N_GROUPS):
        better = group_scores[g] > best
        group = jnp.where(better, g, group)
        best = jnp.where(better, group_scores[g], best)

    def in_group(table, k):
        val = table[k]
        for g in range(1, N_GROUPS):
            val = jnp.where(group == g, table[g * EXPERTS_PER_GROUP + k], val)
        return val

    vals = [in_group(rows, k) for k in range(EXPERTS_PER_GROUP)]
    unbiased = [in_group(raw, k) for k in range(EXPERTS_PER_GROUP)]

    def first_argmax(cands):
        idx = jnp.zeros_like(group)
        top = cands[0]
        for k in range(1, len(cands)):
            better = cands[k] > top
            idx = jnp.where(better, k, idx)
            top = jnp.where(better, cands[k], top)
        return idx

    i1 = first_argmax(vals)
    i2 = first_argmax([jnp.where(i1 == k, -jnp.inf, vals[k]) for k in range(EXPERTS_PER_GROUP)])

    def pick(idx):
        val = unbiased[0]
        for k in range(1, EXPERTS_PER_GROUP):
            val = jnp.where(idx == k, unbiased[k], val)
        return val

    s1, s2 = pick(i1), pick(i2)
    total = s1 + s2
    experts = jnp.concatenate([group * EXPERTS_PER_GROUP + i1, group * EXPERTS_PER_GROUP + i2], axis=0)
    weights = jnp.concatenate([s1 / total, s2 / total], axis=0)
    return experts, weights


def _merge_kernel(oret, oswa, oga, ar, as_, aa, x_ref, mod_ref, wbr_ref, wout_ref, g2_ref, wrt_ref, br_ref,
                  xo_ref, h2_ref, e_ref, w_ref, *, n_lat_tiles, n_batch):
    m = _mod_row(mod_ref, n_lat_tiles, n_batch)

    def gate(a_ref):
        return jax.nn.sigmoid(a_ref[0].astype(F32))

    y = (gate(ar) * _dot(oret[0], wbr_ref[0]) + gate(as_) * _dot(oswa[0], wbr_ref[1])
         + gate(aa) * _dot(oga[0], wbr_ref[2]))
    x = x_ref[0] + _mod_chunk(m, 2) * _dot(y.astype(BF16), wout_ref[...])
    xo_ref[0] = x
    h2 = _rms_normalize(x) * g2_ref[...] * (1.0 + _mod_chunk(m, 4)) + _mod_chunk(m, 3)
    h2_ref[0] = h2
    logits = lax.dot_general(wrt_ref[...], h2, (((1,), (1,)), ((), ())),
                             precision=lax.Precision.HIGHEST, preferred_element_type=F32)
    scores = jax.nn.sigmoid(logits)
    experts, weights = _route(scores, scores + br_ref[...])
    e_ref[0] = experts
    w_ref[0] = weights


def _merge(oret, oswa, oga, ar, as_, aa, xa, mod, wbr, wout, g2, wrt, br, n_lat_tiles, n_tiles):
    nb, nt, d = xa.shape
    rows = n_tiles * TOKEN_TILE
    kern = functools.partial(_merge_kernel, n_lat_tiles=n_lat_tiles, n_batch=nb)
    tile = pl.BlockSpec((1, TOKEN_TILE, d), lambda b, i: (b, i, 0))
    whole = lambda arr: pl.BlockSpec(arr.shape, lambda b, i: (0,) * arr.ndim)
    small = pl.BlockSpec((1, 2, TOKEN_TILE), lambda b, i: (b, 0, i))
    return pl.pallas_call(
        kern,
        out_shape=[jax.ShapeDtypeStruct((nb, nt, d), F32),
                   jax.ShapeDtypeStruct((nb, rows, d), F32),
                   jax.ShapeDtypeStruct((nb, 2, rows), jnp.int32),
                   jax.ShapeDtypeStruct((nb, 2, rows), F32)],
        grid=(nb, n_tiles),
        in_specs=[tile, tile, tile, tile, tile, tile, tile, whole(mod), whole(wbr), whole(wout),
                  pl.BlockSpec((1, d), lambda b, i: (0, 0)), whole(wrt), whole(br)],
        out_specs=[tile, tile, small, small],
        input_output_aliases={6: 0},
        compiler_params=_params(),
        name="merge_route",
    )(oret, oswa, oga, ar, as_, aa, xa, mod, wbr, wout, g2.reshape(1, d), wrt, br)


def _row_copy(src, dst, sem):
    return pltpu.make_async_copy(src, dst, sem)


def _dispatch_kernel(dest_ref, h2_hbm, buf_in, buf_out, sem):
    del buf_in
    b = pl.program_id(0)
    i = pl.program_id(1)
    base = i * TOKEN_TILE

    def issue(r, carry):
        for k in range(2):
            d = dest_ref[0, 0, k * TOKEN_TILE + r]
            _row_copy(h2_hbm.at[b, pl.ds(base + r, 1), :], buf_out.at[pl.ds(d, 1), :], sem).start()
        return carry

    lax.fori_loop(0, TOKEN_TILE, issue, 0, unroll=8)

    def drain(r, carry):
        for k in range(2):
            _row_copy(h2_hbm.at[b, pl.ds(0, 1), :], buf_out.at[pl.ds(0, 1), :], sem).wait()
        return carry

    lax.fori_loop(0, TOKEN_TILE, drain, 0, unroll=8)


def _dispatch(dest_tiles, h2, n_slots):
    nb, rows, d = h2.shape
    n_tiles = rows // TOKEN_TILE
    return pl.pallas_call(
        _dispatch_kernel,
        out_shape=jax.ShapeDtypeStruct((n_slots, d), F32),
        grid=(nb, n_tiles),
        in_specs=[pl.BlockSpec((1, 1, 2 * TOKEN_TILE), lambda b, i: (b * n_tiles + i, 0, 0),
                               memory_space=pltpu.SMEM),
                  pl.BlockSpec(memory_space=pl.ANY),
                  pl.BlockSpec(memory_space=pl.ANY)],
        out_specs=pl.BlockSpec(memory_space=pl.ANY),
        scratch_shapes=[pltpu.SemaphoreType.DMA(())],
        input_output_aliases={2: 0},
        compiler_params=_params(has_side_effects=True),
        name="moe_dispatch",
    )(dest_tiles, h2, jnp.zeros((n_slots, d), F32))


def _expert_kernel(be_ref, x_ref, wg_ref, wu_ref, wd_ref, y_ref):
    del be_ref
    x = x_ref[...].astype(BF16)
    hid = _silu(_dot(x, wg_ref[0])) * _dot(x, wu_ref[0])
    y_ref[...] = _dot(hid.astype(BF16), wd_ref[0])


def _experts(block_e, buf, wg, wu, wd):
    n_slots, d = buf.shape
    grid_spec = pltpu.PrefetchScalarGridSpec(
        num_scalar_prefetch=1,
        grid=(n_slots // MOE_ROWS,),
        in_specs=[pl.BlockSpec((MOE_ROWS, d), lambda i, be: (i, 0)),
                  pl.BlockSpec((1, d, D_EXPERT), lambda i, be: (be[i], 0, 0)),
                  pl.BlockSpec((1, d, D_EXPERT), lambda i, be: (be[i], 0, 0)),
                  pl.BlockSpec((1, D_EXPERT, d), lambda i, be: (be[i], 0, 0))],
        out_specs=pl.BlockSpec((MOE_ROWS, d), lambda i, be: (i, 0)),
    )
    return pl.pallas_call(
        _expert_kernel,
        out_shape=jax.ShapeDtypeStruct((n_slots, d), F32),
        grid_spec=grid_spec,
        compiler_params=_params(),
        name="moe_experts",
    )(block_e, buf, wg, wu, wd)


def _combine_kernel(dest_ref, y_hbm, wt_ref, x_ref, mod_ref, g_ref, *rest,
                    n_lat_tiles, n_batch, last):
    if last:
        out_ref, gbuf, sem = rest
    else:
        xo_ref, h_ref, gbuf, sem = rest

    def issue(r, carry):
        for k in range(2):
            d = dest_ref[0, 0, k * TOKEN_TILE + r]
            _row_copy(y_hbm.at[pl.ds(d, 1), :], gbuf.at[k, pl.ds(r, 1), :], sem).start()
        return carry

    lax.fori_loop(0, TOKEN_TILE, issue, 0, unroll=8)

    def drain(r, carry):
        for k in range(2):
            _row_copy(y_hbm.at[pl.ds(0, 1), :], gbuf.at[0, pl.ds(0, 1), :], sem).wait()
        return carry

    lax.fori_loop(0, TOKEN_TILE, drain, 0, unroll=8)

    m = _mod_row(mod_ref, n_lat_tiles, n_batch)
    wt = wt_ref[0]
    moe = gbuf[0] * wt[:, 0:1] + gbuf[1] * wt[:, 1:2]
    x = x_ref[0] + _mod_chunk(m, 5) * moe
    if last:
        out_ref[0] = _rms_normalize(x) * g_ref[...]
    else:
        xo_ref[0] = x
        h = _rms_normalize(x) * g_ref[...] * (1.0 + _mod_chunk(m, 7)) + _mod_chunk(m, 6)
        h_ref[0] = h.astype(BF16)


def _combine(dest_tiles, y, wt, xa, mod, g, n_lat_tiles, n_tiles, last):
    nb, nt, d = xa.shape
    rows = n_tiles * TOKEN_TILE
    kern = functools.partial(_combine_kernel, n_lat_tiles=n_lat_tiles, n_batch=nb, last=last)
    tile = pl.BlockSpec((1, TOKEN_TILE, d), lambda b, i: (b, i, 0))
    if last:
        out_shape = jax.ShapeDtypeStruct((nb, rows, d), F32)
        out_specs = tile
        aliases = {}
    else:
        out_shape = [jax.ShapeDtypeStruct((nb, nt, d), F32), jax.ShapeDtypeStruct((nb, nt, d), BF16)]
        out_specs = [tile, tile]
        aliases = {3: 0}
    return pl.pallas_call(
        kern,
        out_shape=out_shape,
        grid=(nb, n_tiles),
        in_specs=[pl.BlockSpec((1, 1, 2 * TOKEN_TILE), lambda b, i: (b * n_tiles + i, 0, 0),
                               memory_space=pltpu.SMEM),
                  pl.BlockSpec(memory_space=pl.ANY),
                  pl.BlockSpec((1, TOKEN_TILE, 2), lambda b, i: (b, i, 0)),
                  tile,
                  pl.BlockSpec(mod.shape, lambda b, i: (0, 0)),
                  pl.BlockSpec((1, d), lambda b, i: (0, 0))],
        out_specs=out_specs,
        scratch_shapes=[pltpu.VMEM((2, TOKEN_TILE, d), F32), pltpu.SemaphoreType.DMA(())],
        input_output_aliases=aliases,
        compiler_params=_params(),
        name="moe_combine",
    )(dest_tiles, y, wt, xa, mod, g.reshape(1, d))


def _slot_plan(experts):
    nb, _, rows = experts.shape
    flat = experts.reshape(-1)
    one_hot = (flat[:, None] == jnp.arange(N_EXPERTS, dtype=jnp.int32)[None, :]).astype(jnp.int32)
    running = jnp.cumsum(one_hot, axis=0)
    rank = jnp.sum(one_hot * running, axis=1) - 1
    counts = running[-1]
    padded = (counts + MOE_ROWS - 1) // MOE_ROWS * MOE_ROWS
    pad_ends = jnp.cumsum(padded)
    pad_starts = pad_ends - padded
    dest = jnp.sum(one_hot * pad_starts[None, :], axis=1) + rank
    n_assign = flat.shape[0]
    n_blocks = (n_assign + N_EXPERTS * (MOE_ROWS - 1) + MOE_ROWS - 1) // MOE_ROWS
    block_e = jnp.minimum(
        jnp.searchsorted(pad_ends, jnp.arange(n_blocks, dtype=jnp.int32) * MOE_ROWS, side='right'),
        N_EXPERTS - 1).astype(jnp.int32)
    n_tiles = rows // TOKEN_TILE
    dest_tiles = (dest.astype(jnp.int32).reshape(nb, 2, n_tiles, TOKEN_TILE)
                  .transpose(0, 2, 1, 3).reshape(nb * n_tiles, 1, 2 * TOKEN_TILE))
    return dest_tiles, block_e, n_blocks * MOE_ROWS


def _rope_tables(length, n_ctx):
    rows = length // GRID_W
    row = jnp.repeat(jnp.arange(rows, dtype=jnp.int32), GRID_W).astype(F32)
    col = jnp.tile(jnp.arange(GRID_W, dtype=jnp.int32), rows).astype(F32)

    def table(hd):
        quarter = hd // 4
        freqs = ROPE_THETA ** (-jnp.arange(quarter, dtype=F32) / quarter)
        ang_r = row[:, None] * freqs[None, :]
        ang_c = col[:, None] * freqs[None, :]
        cos = jnp.concatenate([jnp.cos(ang_r), jnp.cos(ang_r), jnp.cos(ang_c), jnp.cos(ang_c)], axis=-1)
        sin = jnp.concatenate([-jnp.sin(ang_r), jnp.sin(ang_r), -jnp.sin(ang_c), jnp.sin(ang_c)], axis=-1)
        cos = jnp.concatenate([cos, jnp.ones((n_ctx, hd), F32)], axis=0)
        sin = jnp.concatenate([sin, jnp.zeros((n_ctx, hd), F32)], axis=0)
        reps = LANES // hd
        return jnp.tile(cos, (1, reps)), jnp.tile(sin, (1, reps))

    c128, s128 = table(RET_DK)
    c64, s64 = table(SWA_HD)
    return c128, s128, c64, s64


def _relayout_w_in(w):
    parts = []
    start = 0
    for width in SPLITS:
        parts.append(w[:, start:start + width])
        start += width

    def duplicate_heads(p):
        return jnp.concatenate([p[:, h * SWA_HD:(h + 1) * SWA_HD] for h in range(SWA_KV_HEADS) for _ in range(2)],
                               axis=1)

    parts[5] = duplicate_heads(parts[5])
    parts[6] = duplicate_heads(parts[6])
    return jnp.concatenate(parts, axis=1).astype(BF16)


def kernel(x, c, ctx, c_ctx, w_mod, b_mod, g_norm1, g_norm2, w_in, ret_decay_logit, swa_sink, g_qnorm, g_knorm,
           w_br_ret, w_br_swa, w_br_ga, w_out, w_router, b_router, w_gate, w_up, w_down, g_final):
    nb, length, d = x.shape
    n_ctx = ctx.shape[1]
    depth = w_mod.shape[0]
    nt = length + n_ctx
    n_lat_tiles = length // TOKEN_TILE
    n_all_tiles = nt // TOKEN_TILE
    assert GA_HD == RET_DK == LANES and 2 * SWA_HD == LANES
    assert length % TOKEN_TILE == 0 and n_ctx == TOKEN_TILE and length % n_ctx == 0

    mod_rows = 8
    c_rows = jnp.concatenate([c, c_ctx[None, :], jnp.zeros((mod_rows - nb - 1, d), F32)], axis=0)
    mods = _modulation(c_rows, w_mod, b_mod)
    tables = _rope_tables(length, n_ctx)
    wrt = w_router.astype(F32).T
    br = b_router.astype(F32).reshape(N_EXPERTS, 1)

    xa = jnp.concatenate([x, ctx], axis=1)
    h = _prenorm(xa, mods[0], g_norm1[0], n_lat_tiles)
    out = None
    for l in range(depth):
        need_ctx = l < depth - 1
        n_tiles = n_all_tiles if need_ctx else n_lat_tiles
        (rq, rk, rv, ru, sq, sk, sv, gq, gk, gv, ar, as_, aa) = _inproj(
            h, _relayout_w_in(w_in[l]), tables, g_qnorm[l], g_knorm[l])
        o_ret = _retention(ret_decay_logit[l].astype(F32), rq, rk, rv, ru, length, n_ctx, need_ctx)
        o_swa = _window_attention(swa_sink[l].astype(F32), sq, sk, sv, length, n_ctx, need_ctx)
        o_ga = _global_attention(gq, gk, gv, length, n_ctx, need_ctx)
        wbr = jnp.stack([w_br_ret[l], w_br_swa[l], w_br_ga[l]]).astype(BF16)
        xa, h2, experts, weights = _merge(o_ret, o_swa, o_ga, ar, as_, aa, xa, mods[l], wbr,
                                          w_out[l].astype(BF16), g_norm2[l], wrt, br, n_lat_tiles, n_tiles)
        dest_tiles, block_e, n_slots = _slot_plan(experts)
        buf = _dispatch(dest_tiles, h2, n_slots)
        y = _experts(block_e, buf, w_gate[l].astype(BF16), w_up[l].astype(BF16), w_down[l].astype(BF16))
        wt = weights.transpose(0, 2, 1)
        if need_ctx:
            mod_pair = jnp.concatenate([mods[l], mods[l + 1][:, :2 * d]], axis=1)
            xa, h = _combine(dest_tiles, y, wt, xa, mod_pair, g_norm1[l + 1], n_lat_tiles, n_tiles, last=False)
        else:
            out = _combine(dest_tiles, y, wt, xa, mods[l], g_final, n_lat_tiles, n_tiles, last=True)
    return out
```

```python
import functools

import jax
import jax.numpy as jnp
from jax import lax
from jax.experimental import pallas as pl
from jax.experimental.pallas import tpu as pltpu

F32 = jnp.float32
BF16 = jnp.bfloat16

D_MODEL = 1024
GRID_W = 64
NORM_EPS = 1e-6
ROPE_THETA = 10000.0
RET_HEADS, RET_DK, RET_DV, RET_CHUNK = 4, 128, 256, 128
SWA_HEADS, SWA_KV_HEADS, SWA_HD, WINDOW = 16, 2, 64, 128
GA_HEADS, GA_KV_HEADS, GA_HD = 8, 2, 128
N_EXPERTS, N_GROUPS, EXPERTS_PER_GROUP, D_EXPERT = 16, 4, 4, 512

RET_QK = RET_HEADS * RET_DK
RET_V = RET_HEADS * RET_DV
SWA_Q = SWA_HEADS * SWA_HD
SWA_KV = SWA_KV_HEADS * SWA_HD
GA_Q = GA_HEADS * GA_HD
GA_KV = GA_KV_HEADS * GA_HD
SPLITS = (RET_QK, RET_QK, RET_V, RET_V, SWA_Q, SWA_KV, SWA_KV, GA_Q, GA_KV, GA_KV,
          D_MODEL, D_MODEL, D_MODEL)

LANES = 128
TOKEN_TILE = 256
QUERY_TILE = 128
MOE_ROWS = 256
GA_KEY_CHUNK = 256
MOD_COLS = 1536
VMEM_LIMIT = 56 * 1024 * 1024
MASKED = -1e30

_W_WIDTHS = (RET_QK, RET_QK, RET_V, RET_V, SWA_Q, 2 * SWA_KV, 2 * SWA_KV, GA_Q, GA_KV, GA_KV,
             D_MODEL, D_MODEL, D_MODEL)
_W_OFFS = tuple(sum(_W_WIDTHS[:i]) for i in range(len(_W_WIDTHS)))
W_ALL = sum(_W_WIDTHS)


def _dot(a, b):
    return jnp.dot(a, b, preferred_element_type=F32)


def _dot_nt(a, b):
    return lax.dot_general(a, b, (((1,), (1,)), ((), ())), preferred_element_type=F32)


def _dot_tn(a, b):
    return lax.dot_general(a, b, (((0,), (0,)), ((), ())), preferred_element_type=F32)


def _silu(x):
    return x * jax.nn.sigmoid(x)


def _rms_normalize(x):
    return x * lax.rsqrt(jnp.mean(x * x, axis=-1, keepdims=True) + NORM_EPS)


def _params(**kw):
    return pltpu.CompilerParams(vmem_limit_bytes=VMEM_LIMIT, **kw)


def _mod_kernel(c_ref, w_ref, b_ref, o_ref):
    a = _silu(c_ref[...])
    o_ref[0] = _dot(a.astype(BF16), w_ref[0].astype(BF16)) + b_ref[0]


def _modulation(c_rows, w_mod, b_mod):
    depth, d, n = w_mod.shape
    rows = c_rows.shape[0]
    return pl.pallas_call(
        _mod_kernel,
        out_shape=jax.ShapeDtypeStruct((depth, rows, n), F32),
        grid=(depth, n // MOD_COLS),
        in_specs=[pl.BlockSpec((rows, d), lambda l, j: (0, 0)),
                  pl.BlockSpec((1, d, MOD_COLS), lambda l, j: (l, 0, j)),
                  pl.BlockSpec((1, 1, MOD_COLS), lambda l, j: (l, 0, j))],
        out_specs=pl.BlockSpec((1, rows, MOD_COLS), lambda l, j: (l, 0, j)),
        compiler_params=_params(),
        name="modulation",
    )(c_rows, w_mod, b_mod.reshape(depth, 1, n))


def _mod_row(mod_ref, n_lat_tiles, n_batch):
    b = pl.program_id(0)
    i = pl.program_id(1)
    r = jnp.where(i < n_lat_tiles, b, n_batch)
    return mod_ref[pl.ds(r, 1), :]


def _mod_chunk(m, k):
    return m[:, k * D_MODEL:(k + 1) * D_MODEL]


def _prenorm_kernel(x_ref, mod_ref, g_ref, h_ref, *, n_lat_tiles, n_batch):
    m = _mod_row(mod_ref, n_lat_tiles, n_batch)
    h = _rms_normalize(x_ref[0]) * g_ref[...] * (1.0 + _mod_chunk(m, 1)) + _mod_chunk(m, 0)
    h_ref[0] = h.astype(BF16)


def _prenorm(xa, mod, g, n_lat_tiles):
    nb, nt, d = xa.shape
    kern = functools.partial(_prenorm_kernel, n_lat_tiles=n_lat_tiles, n_batch=nb)
    return pl.pallas_call(
        kern,
        out_shape=jax.ShapeDtypeStruct((nb, nt, d), BF16),
        grid=(nb, nt // TOKEN_TILE),
        in_specs=[pl.BlockSpec((1, TOKEN_TILE, d), lambda b, i: (b, i, 0)),
                  pl.BlockSpec(mod.shape, lambda b, i: (0, 0)),
                  pl.BlockSpec((1, d), lambda b, i: (0, 0))],
        out_specs=pl.BlockSpec((1, TOKEN_TILE, d), lambda b, i: (b, i, 0)),
        compiler_params=_params(),
        name="prenorm",
    )(xa, mod, g.reshape(1, d))


def _rope(x, cos, sin, quarter):
    lane = lax.broadcasted_iota(jnp.int32, x.shape, 1)
    first = (lane % (2 * quarter)) < quarter
    partner = jnp.where(first, pltpu.roll(x, LANES - quarter, 1), pltpu.roll(x, quarter, 1))
    return x * cos + partner * sin


def _inproj_kernel(h_ref, w_ref, c128_ref, s128_ref, c64_ref, s64_ref, gq_ref, gk_ref,
                   rq, rk, rv, ru, sq, sk, sv, gq, gk, gv, ar, as_, aa):
    h = h_ref[0]
    c128, s128 = c128_ref[...], s128_ref[...]
    c64, s64 = c64_ref[...], s64_ref[...]

    def proj(idx):
        return _dot(h, w_ref[:, _W_OFFS[idx]:_W_OFFS[idx] + _W_WIDTHS[idx]])

    def slabs(acc):
        return [acc[:, s * LANES:(s + 1) * LANES] for s in range(acc.shape[1] // LANES)]

    def store(ref, s, val):
        ref[0, :, s * LANES:(s + 1) * LANES] = val.astype(ref.dtype)

    for s, xs in enumerate(slabs(proj(0))):
        store(rq, s, _rope(xs, c128, s128, RET_DK // 4))
    for s, xs in enumerate(slabs(proj(1))):
        store(rk, s, _rope(xs, c128, s128, RET_DK // 4) * (RET_DK ** -0.5))
    rv[0] = proj(2).astype(rv.dtype)
    ru[0] = proj(3).astype(ru.dtype)
    for s, xs in enumerate(slabs(proj(4))):
        store(sq, s, _rope(xs, c64, s64, SWA_HD // 4) * (SWA_HD ** -0.5))
    for s, xs in enumerate(slabs(proj(5))):
        store(sk, s, _rope(xs, c64, s64, SWA_HD // 4))
    sv[0] = proj(6).astype(sv.dtype)
    for s, xs in enumerate(slabs(proj(7))):
        xn = _rms_normalize(xs) * gq_ref[...]
        store(gq, s, _rope(xn, c128, s128, GA_HD // 4) * (GA_HD ** -0.5))
    for s, xs in enumerate(slabs(proj(8))):
        xn = _rms_normalize(xs) * gk_ref[...]
        store(gk, s, _rope(xn, c128, s128, GA_HD // 4))
    gv[0] = proj(9).astype(gv.dtype)
    ar[0] = proj(10).astype(ar.dtype)
    as_[0] = proj(11).astype(as_.dtype)
    aa[0] = proj(12).astype(aa.dtype)


def _inproj(h, w_all, tables, g_q, g_k):
    nb, nt, d = h.shape
    tile = lambda width: pl.BlockSpec((1, TOKEN_TILE, width), lambda b, i: (b, i, 0))
    tab = pl.BlockSpec((TOKEN_TILE, LANES), lambda b, i: (i, 0))
    vec = pl.BlockSpec((1, LANES), lambda b, i: (0, 0))
    outs = [jax.ShapeDtypeStruct((nb, nt, w), BF16) for w in _W_WIDTHS]
    return pl.pallas_call(
        _inproj_kernel,
        out_shape=outs,
        grid=(nb, nt // TOKEN_TILE),
        in_specs=[tile(d),
                  pl.BlockSpec((d, W_ALL), lambda b, i: (0, 0), pipeline_mode=pl.Buffered(1)),
                  tab, tab, tab, tab, vec, vec],
        out_specs=[tile(w) for w in _W_WIDTHS],
        compiler_params=_params(),
        name="inproj",
    )(h, w_all, *tables, g_q.reshape(1, LANES), g_k.reshape(1, LANES))


def _log_sigmoid(x):
    return jnp.minimum(x, 0.0) - jnp.log1p(jnp.exp(-jnp.abs(x)))


def _ret_kernel(logit_ref, q_ref, k_ref, v_ref, u_ref, o_ref, acc_ref, sf_ref, sb_ref,
                *, length, n_ctx, need_ctx):
    hh = pl.program_id(1)
    chunk = RET_CHUNK
    n_chunks = length // chunk
    lgf = _log_sigmoid(jnp.full((1, 1), logit_ref[0, hh], F32))
    lgb = _log_sigmoid(jnp.full((1, 1), logit_ref[1, hh], F32))

    def rows_f32(shape):
        return lax.broadcasted_iota(jnp.int32, shape, 0).astype(F32)

    def both_ways_decay(n):
        diff = rows_f32((n, n)) - lax.broadcasted_iota(jnp.int32, (n, n), 1).astype(F32)
        return jnp.where(diff >= 0, jnp.exp(jnp.maximum(diff, 0.0) * lgf),
                         jnp.exp(jnp.maximum(-diff, 0.0) * lgb))

    def finish(o, u):
        return (_rms_normalize(o) * _silu(u.astype(F32))).astype(o_ref.dtype)

    kx = k_ref[0, length:length + n_ctx, :].astype(F32)
    vx = v_ref[0, length:length + n_ctx, :]
    lx = rows_f32((n_ctx, RET_DK))
    sf_ref[...] = _dot_tn((kx * jnp.exp((n_ctx - 1.0 - lx) * lgf)).astype(BF16), vx)
    sb_ref[...] = _dot_tn((kx * jnp.exp(lx * lgb)).astype(BF16), vx)

    if need_ctx:
        qx = q_ref[0, length:length + n_ctx, :]
        sx = _dot_nt(qx, kx.astype(BF16)) * both_ways_decay(n_ctx)
        o_ref[0, length:length + n_ctx, :] = finish(_dot(sx.astype(BF16), vx),
                                                    u_ref[0, length:length + n_ctx, :])

    ci = rows_f32((chunk, RET_DK))
    q_dec_f = jnp.exp((ci + 1.0) * lgf)
    k_dec_f = jnp.exp((chunk - 1.0 - ci) * lgf)
    q_dec_b = jnp.exp((chunk - ci) * lgb)
    k_dec_b = jnp.exp(ci * lgb)
    chunk_dec_f = jnp.exp(chunk * lgf)
    chunk_dec_b = jnp.exp(chunk * lgb)
    intra = both_ways_decay(chunk)
    acc_ref[...] = jnp.zeros_like(acc_ref)

    def step(s, carry):
        rf = pl.multiple_of(s * chunk, chunk)
        rb = pl.multiple_of((n_chunks - 1 - s) * chunk, chunk)
        qf = q_ref[0, pl.ds(rf, chunk), :]
        kf = k_ref[0, pl.ds(rf, chunk), :]
        vf = v_ref[0, pl.ds(rf, chunk), :]
        sc = _dot_nt(qf, kf) * intra
        state_f = sf_ref[...]
        o = _dot(sc.astype(BF16), vf) + _dot((qf.astype(F32) * q_dec_f).astype(BF16), state_f.astype(BF16))
        acc_ref[pl.ds(rf, chunk), :] += o
        sf_ref[...] = state_f * chunk_dec_f + _dot_tn((kf.astype(F32) * k_dec_f).astype(BF16), vf)
        qb = q_ref[0, pl.ds(rb, chunk), :]
        kb = k_ref[0, pl.ds(rb, chunk), :]
        vb = v_ref[0, pl.ds(rb, chunk), :]
        state_b = sb_ref[...]
        acc_ref[pl.ds(rb, chunk), :] += _dot((qb.astype(F32) * q_dec_b).astype(BF16), state_b.astype(BF16))
        sb_ref[...] = state_b * chunk_dec_b + _dot_tn((kb.astype(F32) * k_dec_b).astype(BF16), vb)
        return carry

    lax.fori_loop(0, n_chunks, step, 0)

    def out_step(c, carry):
        r = pl.multiple_of(c * chunk, chunk)
        o_ref[0, pl.ds(r, chunk), :] = finish(acc_ref[pl.ds(r, chunk), :], u_ref[0, pl.ds(r, chunk), :])
        return carry

    lax.fori_loop(0, n_chunks, out_step, 0)


def _retention(logit, rq, rk, rv, ru, length, n_ctx, need_ctx):
    nb, nt, _ = rq.shape
    out_rows = nt if need_ctx else length
    kern = functools.partial(_ret_kernel, length=length, n_ctx=n_ctx, need_ctx=need_ctx)
    head = lambda width: pl.BlockSpec((1, nt, width), lambda b, h: (b, 0, h))
    return pl.pallas_call(
        kern,
        out_shape=jax.ShapeDtypeStruct((nb, out_rows, RET_V), BF16),
        grid=(nb, RET_HEADS),
        in_specs=[pl.BlockSpec(memory_space=pltpu.SMEM),
                  head(RET_DK), head(RET_DK), head(RET_DV), head(RET_DV)],
        out_specs=pl.BlockSpec((1, out_rows, RET_DV), lambda b, h: (b, 0, h)),
        scratch_shapes=[pltpu.VMEM((length, RET_DV), F32),
                        pltpu.VMEM((RET_DK, RET_DV), F32),
                        pltpu.VMEM((RET_DK, RET_DV), F32)],
        compiler_params=_params(),
        name="retention",
    )(logit, rq, rk, rv, ru)


def _swa_group(q_ref, g, kcat, vcat, bias, sink_ref, o_ref):
    tq = q_ref.shape[1]
    heads_per_group = SWA_HEADS // SWA_KV_HEADS
    slabs_per_group = heads_per_group // 2
    low = lax.broadcasted_iota(jnp.int32, (tq, LANES), 1) < SWA_HD
    zero = jnp.zeros((tq, LANES), q_ref.dtype)
    stacked = []
    for a in range(slabs_per_group):
        s0 = (g * slabs_per_group + a) * LANES
        slab = q_ref[0, :, s0:s0 + LANES]
        stacked.append(jnp.where(low, slab, zero))
        stacked.append(jnp.where(low, zero, slab))
    q = jnp.concatenate(stacked, axis=0)
    s = _dot_nt(q, kcat)
    if bias is not None:
        nk = s.shape[1]
        s = (s.reshape(heads_per_group, tq, nk) + bias[None]).reshape(heads_per_group * tq, nk)
    sink = jnp.concatenate(
        [jnp.full((tq, 1), sink_ref[g * heads_per_group + h], F32) for h in range(heads_per_group)], axis=0)
    m = jnp.maximum(jnp.max(s, axis=-1, keepdims=True), sink)
    p = jnp.exp(s - m)
    den = jnp.sum(p, axis=-1, keepdims=True) + jnp.exp(sink - m)
    o = _dot(p.astype(BF16), vcat) / den
    for a in range(slabs_per_group):
        s0 = (g * slabs_per_group + a) * LANES
        even = o[(2 * a) * tq:(2 * a + 1) * tq]
        odd = o[(2 * a + 1) * tq:(2 * a + 2) * tq]
        o_ref[0, :, s0:s0 + LANES] = jnp.where(low, even, odd).astype(o_ref.dtype)


def _swa_kernel(sink_ref, q_ref, kp_ref, kc_ref, kn_ref, kx_ref, vp_ref, vc_ref, vn_ref, vx_ref, o_ref,
                *, n_lat_tiles, need_ctx):
    j = pl.program_id(1)
    tq = q_ref.shape[1]
    n_ctx = kx_ref.shape[1]

    def lanes_of(ref, g):
        return ref[0, :, g * LANES:(g + 1) * LANES]

    @pl.when(j < n_lat_tiles)
    def _latent():
        nk = 3 * tq + n_ctx
        ci = lax.broadcasted_iota(jnp.int32, (tq, nk), 1)
        ri = lax.broadcasted_iota(jnp.int32, (tq, nk), 0)
        far = 4 * tq
        ri_prev = ri + jnp.where(j > 0, 0, far)
        ri_next = ri - jnp.where(j < n_lat_tiles - 1, 0, far)
        zero = jnp.zeros((tq, nk), F32)
        masked = jnp.full((tq, nk), MASKED, F32)
        bias = jnp.where(ci < tq, jnp.where(ci >= ri_prev, zero, masked),
                         jnp.where((ci >= 2 * tq) & (ci < 3 * tq),
                                   jnp.where((ci - 2 * tq) <= ri_next, zero, masked), zero))
        for g in range(SWA_KV_HEADS):
            kcat = jnp.concatenate([lanes_of(kp_ref, g), lanes_of(kc_ref, g), lanes_of(kn_ref, g),
                                    lanes_of(kx_ref, g)], axis=0)
            vcat = jnp.concatenate([lanes_of(vp_ref, g), lanes_of(vc_ref, g), lanes_of(vn_ref, g),
                                    lanes_of(vx_ref, g)], axis=0)
            _swa_group(q_ref, g, kcat, vcat, bias, sink_ref, o_ref)

    if need_ctx:
        @pl.when(j >= n_lat_tiles)
        def _context():
            for g in range(SWA_KV_HEADS):
                _swa_group(q_ref, g, lanes_of(kx_ref, g), lanes_of(vx_ref, g), None, sink_ref, o_ref)


def _window_attention(sink, sq, sk, sv, length, n_ctx, need_ctx):
    nb, nt, _ = sq.shape
    tq = QUERY_TILE
    n_lat = length // tq
    out_rows = nt if need_ctx else length
    kvw = 2 * SWA_KV
    kern = functools.partial(_swa_kernel, n_lat_tiles=n_lat, need_ctx=need_ctx)
    prev = pl.BlockSpec((1, tq, kvw), lambda b, j: (b, jnp.clip(j - 1, 0, n_lat - 1), 0))
    cur = pl.BlockSpec((1, tq, kvw), lambda b, j: (b, jnp.minimum(j, n_lat - 1), 0))
    nxt = pl.BlockSpec((1, tq, kvw), lambda b, j: (b, jnp.minimum(j + 1, n_lat - 1), 0))
    ctx = pl.BlockSpec((1, n_ctx, kvw), lambda b, j: (b, length // n_ctx, 0))
    return pl.pallas_call(
        kern,
        out_shape=jax.ShapeDtypeStruct((nb, out_rows, SWA_Q), BF16),
        grid=(nb, out_rows // tq),
        in_specs=[pl.BlockSpec(memory_space=pltpu.SMEM),
                  pl.BlockSpec((1, tq, SWA_Q), lambda b, j: (b, j, 0)),
                  prev, cur, nxt, ctx, prev, cur, nxt, ctx],
        out_specs=pl.BlockSpec((1, tq, SWA_Q), lambda b, j: (b, j, 0)),
        compiler_params=_params(),
        name="window_attention",
    )(sink, sq, sk, sk, sk, sk, sv, sv, sv, sv)


def _ga_kernel(q_ref, k_ref, v_ref, o_ref, *, length, n_ctx, n_lat_tiles, need_ctx):
    j = pl.program_id(2)
    tq = q_ref.shape[1]
    heads_per_group = GA_HEADS // GA_KV_HEADS
    q = jnp.concatenate([q_ref[0, :, h * LANES:(h + 1) * LANES] for h in range(heads_per_group)], axis=0)

    rows = heads_per_group * tq

    def attend(key_lo, key_hi):
        m = l = acc = None
        for c0 in range(key_lo, key_hi, GA_KEY_CHUNK):
            s = _dot_nt(q, k_ref[0, c0:c0 + GA_KEY_CHUNK, :])
            parts = [s[:, t * LANES:(t + 1) * LANES] for t in range(GA_KEY_CHUNK // LANES)]
            lane_max = functools.reduce(jnp.maximum, parts)
            mc = jnp.broadcast_to(jnp.max(lane_max, axis=-1, keepdims=True), (rows, LANES))
            m_new = mc if m is None else jnp.maximum(m, mc)
            ps = [jnp.exp(part - m_new) for part in parts]
            pv = _dot(jnp.concatenate(ps, axis=1).astype(BF16), v_ref[0, c0:c0 + GA_KEY_CHUNK, :])
            psum = functools.reduce(jnp.add, ps)
            if m is None:
                l, acc = psum, pv
            else:
                alpha = jnp.exp(m - m_new)
                l, acc = alpha * l + psum, alpha * acc + pv
            m = m_new
        o = acc / jnp.sum(l, axis=-1, keepdims=True)
        for h in range(heads_per_group):
            o_ref[0, :, h * LANES:(h + 1) * LANES] = o[h * tq:(h + 1) * tq].astype(o_ref.dtype)

    @pl.when(j < n_lat_tiles)
    def _latent():
        attend(0, length + n_ctx)

    if need_ctx:
        @pl.when(j >= n_lat_tiles)
        def _context():
            attend(length, length + n_ctx)


def _global_attention(gq, gk, gv, length, n_ctx, need_ctx):
    nb, nt, _ = gq.shape
    tq = QUERY_TILE
    out_rows = nt if need_ctx else length
    group_w = GA_Q // GA_KV_HEADS
    kern = functools.partial(_ga_kernel, length=length, n_ctx=n_ctx, n_lat_tiles=length // tq,
                             need_ctx=need_ctx)
    kv = pl.BlockSpec((1, nt, GA_HD), lambda b, g, j: (b, 0, g))
    return pl.pallas_call(
        kern,
        out_shape=jax.ShapeDtypeStruct((nb, out_rows, GA_Q), BF16),
        grid=(nb, GA_KV_HEADS, out_rows // tq),
        in_specs=[pl.BlockSpec((1, tq, group_w), lambda b, g, j: (b, j, g)), kv, kv],
        out_specs=pl.BlockSpec((1, tq, group_w), lambda b, g, j: (b, j, g)),
        compiler_params=_params(),
        name="global_attention",
    )(gq, gk, gv)


def _route(scores, biased):
    rows = [biased[r:r + 1, :] for r in range(N_EXPERTS)]
    raw = [scores[r:r + 1, :] for r in range(N_EXPERTS)]

    def top2_sum(vals):
        best = None
        for a in range(len(vals)):
            for b in range(a + 1, len(vals)):
                pair = vals[a] + vals[b]
                best = pair if best is None else jnp.maximum(best, pair)
        return best

    group_scores = [top2_sum(rows[g * EXPERTS_PER_GROUP:(g + 1) * EXPERTS_PER_GROUP]) for g in range(N_GROUPS)]
    group = jnp.zeros_like(group_scores[0], dtype=jnp.int32)
    best = group_scores[0]
    for g in range(1, N_GROUPS):
        better = group_scores[g] > best
        group = jnp.where(better, g, group)
        best = jnp.where(better, group_scores[g], best)

    def in_group(table, k):
        val = table[k]
        for g in range(1, N_GROUPS):
            val = jnp.where(group == g, table[g * EXPERTS_PER_GROUP + k], val)
        return val

    vals = [in_group(rows, k) for k in range(EXPERTS_PER_GROUP)]
    unbiased = [in_group(raw, k) for k in range(EXPERTS_PER_GROUP)]

    def first_argmax(cands):
        idx = jnp.zeros_like(group)
        top = cands[0]
        for k in range(1, len(cands)):
            better = cands[k] > top
            idx = jnp.where(better, k, idx)
            top = jnp.where(better, cands[k], top)
        return idx

    i1 = first_argmax(vals)
    i2 = first_argmax([jnp.where(i1 == k, -jnp.inf, vals[k]) for k in range(EXPERTS_PER_GROUP)])

    def pick(idx):
        val = unbiased[0]
        for k in range(1, EXPERTS_PER_GROUP):
            val = jnp.where(idx == k, unbiased[k], val)
        return val

    s1, s2 = pick(i1), pick(i2)
    total = s1 + s2
    experts = jnp.concatenate([group * EXPERTS_PER_GROUP + i1, group * EXPERTS_PER_GROUP + i2], axis=0)
    weights = jnp.concatenate([s1 / total, s2 / total], axis=0)
    return experts, weights


def _merge_kernel(oret, oswa, oga, ar, as_, aa, x_ref, mod_ref, wbr_ref, wout_ref, g2_ref, wrt_ref, br_ref,
                  xo_ref, h2_ref, e_ref, w_ref, *, n_lat_tiles, n_batch):
    m = _mod_row(mod_ref, n_lat_tiles, n_batch)

    def gate(a_ref):
        return jax.nn.sigmoid(a_ref[0].astype(F32))

    y = (gate(ar) * _dot(oret[0], wbr_ref[0]) + gate(as_) * _dot(oswa[0], wbr_ref[1])
         + gate(aa) * _dot(oga[0], wbr_ref[2]))
    x = x_ref[0] + _mod_chunk(m, 2) * _dot(y.astype(BF16), wout_ref[...])
    xo_ref[0] = x
    h2 = _rms_normalize(x) * g2_ref[...] * (1.0 + _mod_chunk(m, 4)) + _mod_chunk(m, 3)
    h2_ref[0] = h2
    logits = lax.dot_general(wrt_ref[...], h2, (((1,), (1,)), ((), ())),
                             precision=lax.Precision.HIGHEST, preferred_element_type=F32)
    scores = jax.nn.sigmoid(logits)
    experts, weights = _route(scores, scores + br_ref[...])
    e_ref[0] = experts
    w_ref[0] = weights


def _merge(oret, oswa, oga, ar, as_, aa, xa, mod, wbr, wout, g2, wrt, br, n_lat_tiles, n_tiles):
    nb, nt, d = xa.shape
    rows = n_tiles * TOKEN_TILE
    kern = functools.partial(_merge_kernel, n_lat_tiles=n_lat_tiles, n_batch=nb)
    tile = pl.BlockSpec((1, TOKEN_TILE, d), lambda b, i: (b, i, 0))
    whole = lambda arr: pl.BlockSpec(arr.shape, lambda b, i: (0,) * arr.ndim)
    small = pl.BlockSpec((1, 2, TOKEN_TILE), lambda b, i: (b, 0, i))
    return pl.pallas_call(
        kern,
        out_shape=[jax.ShapeDtypeStruct((nb, nt, d), F32),
                   jax.ShapeDtypeStruct((nb, rows, d), F32),
                   jax.ShapeDtypeStruct((nb, 2, rows), jnp.int32),
                   jax.ShapeDtypeStruct((nb, 2, rows), F32)],
        grid=(nb, n_tiles),
        in_specs=[tile, tile, tile, tile, tile, tile, tile, whole(mod), whole(wbr), whole(wout),
                  pl.BlockSpec((1, d), lambda b, i: (0, 0)), whole(wrt), whole(br)],
        out_specs=[tile, tile, small, small],
        input_output_aliases={6: 0},
        compiler_params=_params(),
        name="merge_route",
    )(oret, oswa, oga, ar, as_, aa, xa, mod, wbr, wout, g2.reshape(1, d), wrt, br)


def _row_copy(src, dst, sem):
    return pltpu.make_async_copy(src, dst, sem)


def _dispatch_kernel(dest_ref, h2_ref, buf_in, buf_out, sem):
    del buf_in

    def issue(r, carry):
        for k in range(2):
            d = dest_ref[0, 0, k * TOKEN_TILE + r]
            _row_copy(h2_ref.at[0, pl.ds(r, 1), :], buf_out.at[pl.ds(d, 1), :], sem).start()
        return carry

    lax.fori_loop(0, TOKEN_TILE, issue, 0, unroll=8)

    def drain(r, carry):
        for k in range(2):
            _row_copy(h2_ref.at[0, pl.ds(0, 1), :], buf_out.at[pl.ds(0, 1), :], sem).wait()
        return carry

    lax.fori_loop(0, TOKEN_TILE, drain, 0, unroll=8)


def _dispatch(dest_tiles, h2, n_slots):
    nb, rows, d = h2.shape
    n_tiles = rows // TOKEN_TILE
    return pl.pallas_call(
        _dispatch_kernel,
        out_shape=jax.ShapeDtypeStruct((n_slots, d), F32),
        grid=(nb, n_tiles),
        in_specs=[pl.BlockSpec((1, 1, 2 * TOKEN_TILE), lambda b, i: (b * n_tiles + i, 0, 0),
                               memory_space=pltpu.SMEM),
                  pl.BlockSpec((1, TOKEN_TILE, d), lambda b, i: (b, i, 0)),
                  pl.BlockSpec(memory_space=pl.ANY)],
        out_specs=pl.BlockSpec(memory_space=pl.ANY),
        scratch_shapes=[pltpu.SemaphoreType.DMA(())],
        input_output_aliases={2: 0},
        compiler_params=_params(has_side_effects=True),
        name="moe_dispatch",
    )(dest_tiles, h2, jnp.zeros((n_slots, d), F32))


def _expert_kernel(be_ref, x_ref, wg_ref, wu_ref, wd_ref, y_ref):
    del be_ref
    x = x_ref[...].astype(BF16)
    hid = _silu(_dot(x, wg_ref[0])) * _dot(x, wu_ref[0])
    y_ref[...] = _dot(hid.astype(BF16), wd_ref[0])


def _experts(block_e, buf, wg, wu, wd):
    n_slots, d = buf.shape
    grid_spec = pltpu.PrefetchScalarGridSpec(
        num_scalar_prefetch=1,
        grid=(n_slots // MOE_ROWS,),
        in_specs=[pl.BlockSpec((MOE_ROWS, d), lambda i, be: (i, 0)),
                  pl.BlockSpec((1, d, D_EXPERT), lambda i, be: (be[i], 0, 0)),
                  pl.BlockSpec((1, d, D_EXPERT), lambda i, be: (be[i], 0, 0)),
                  pl.BlockSpec((1, D_EXPERT, d), lambda i, be: (be[i], 0, 0))],
        out_specs=pl.BlockSpec((MOE_ROWS, d), lambda i, be: (i, 0)),
    )
    return pl.pallas_call(
        _expert_kernel,
        out_shape=jax.ShapeDtypeStruct((n_slots, d), F32),
        grid_spec=grid_spec,
        compiler_params=_params(),
        name="moe_experts",
    )(block_e, buf, wg, wu, wd)


def _combine_kernel(dest_ref, y_hbm, wt_ref, x_ref, mod_ref, g_ref, *rest,
                    n_lat_tiles, n_batch, last):
    if last:
        out_ref, gbuf, sem = rest
    else:
        xo_ref, h_ref, gbuf, sem = rest

    def issue(r, carry):
        for k in range(2):
            d = dest_ref[0, 0, k * TOKEN_TILE + r]
            _row_copy(y_hbm.at[pl.ds(d, 1), :], gbuf.at[k, pl.ds(r, 1), :], sem).start()
        return carry

    lax.fori_loop(0, TOKEN_TILE, issue, 0, unroll=8)

    def drain(r, carry):
        for k in range(2):
            _row_copy(y_hbm.at[pl.ds(0, 1), :], gbuf.at[0, pl.ds(0, 1), :], sem).wait()
        return carry

    lax.fori_loop(0, TOKEN_TILE, drain, 0, unroll=8)

    m = _mod_row(mod_ref, n_lat_tiles, n_batch)
    wt = wt_ref[0]
    moe = gbuf[0] * wt[:, 0:1] + gbuf[1] * wt[:, 1:2]
    x = x_ref[0] + _mod_chunk(m, 5) * moe
    if last:
        out_ref[0] = _rms_normalize(x) * g_ref[...]
    else:
        xo_ref[0] = x
        h = _rms_normalize(x) * g_ref[...] * (1.0 + _mod_chunk(m, 7)) + _mod_chunk(m, 6)
        h_ref[0] = h.astype(BF16)


def _combine(dest_tiles, y, wt, xa, mod, g, n_lat_tiles, n_tiles, last):
    nb, nt, d = xa.shape
    rows = n_tiles * TOKEN_TILE
    kern = functools.partial(_combine_kernel, n_lat_tiles=n_lat_tiles, n_batch=nb, last=last)
    tile = pl.BlockSpec((1, TOKEN_TILE, d), lambda b, i: (b, i, 0))
    if last:
        out_shape = jax.ShapeDtypeStruct((nb, rows, d), F32)
        out_specs = tile
        aliases = {}
    else:
        out_shape = [jax.ShapeDtypeStruct((nb, nt, d), F32), jax.ShapeDtypeStruct((nb, nt, d), BF16)]
        out_specs = [tile, tile]
        aliases = {3: 0}
    return pl.pallas_call(
        kern,
        out_shape=out_shape,
        grid=(nb, n_tiles),
        in_specs=[pl.BlockSpec((1, 1, 2 * TOKEN_TILE), lambda b, i: (b * n_tiles + i, 0, 0),
                               memory_space=pltpu.SMEM),
                  pl.BlockSpec(memory_space=pl.ANY),
                  pl.BlockSpec((1, TOKEN_TILE, 2), lambda b, i: (b, i, 0)),
                  tile,
                  pl.BlockSpec(mod.shape, lambda b, i: (0, 0)),
                  pl.BlockSpec((1, d), lambda b, i: (0, 0))],
        out_specs=out_specs,
        scratch_shapes=[pltpu.VMEM((2, TOKEN_TILE, d), F32), pltpu.SemaphoreType.DMA(())],
        input_output_aliases=aliases,
        compiler_params=_params(),
        name="moe_combine",
    )(dest_tiles, y, wt, xa, mod, g.reshape(1, d))


def _slot_plan(experts):
    nb, _, rows = experts.shape
    flat = experts.reshape(-1)
    one_hot = (flat[:, None] == jnp.arange(N_EXPERTS, dtype=jnp.int32)[None, :]).astype(jnp.int32)
    running = jnp.cumsum(one_hot, axis=0)
    rank = jnp.sum(one_hot * running, axis=1) - 1
    counts = running[-1]
    padded = (counts + MOE_ROWS - 1) // MOE_ROWS * MOE_ROWS
    pad_ends = jnp.cumsum(padded)
    pad_starts = pad_ends - padded
    dest = jnp.sum(one_hot * pad_starts[None, :], axis=1) + rank
    n_assign = flat.shape[0]
    n_blocks = (n_assign + N_EXPERTS * (MOE_ROWS - 1) + MOE_ROWS - 1) // MOE_ROWS
    block_e = jnp.minimum(
        jnp.searchsorted(pad_ends, jnp.arange(n_blocks, dtype=jnp.int32) * MOE_ROWS, side='right'),
        N_EXPERTS - 1).astype(jnp.int32)
    n_tiles = rows // TOKEN_TILE
    dest_tiles = (dest.astype(jnp.int32).reshape(nb, 2, n_tiles, TOKEN_TILE)
                  .transpose(0, 2, 1, 3).reshape(nb * n_tiles, 1, 2 * TOKEN_TILE))
    return dest_tiles, block_e, n_blocks * MOE_ROWS


def _rope_tables(length, n_ctx):
    rows = length // GRID_W
    row = jnp.repeat(jnp.arange(rows, dtype=jnp.int32), GRID_W).astype(F32)
    col = jnp.tile(jnp.arange(GRID_W, dtype=jnp.int32), rows).astype(F32)

    def table(hd):
        quarter = hd // 4
        freqs = ROPE_THETA ** (-jnp.arange(quarter, dtype=F32) / quarter)
        ang_r = row[:, None] * freqs[None, :]
        ang_c = col[:, None] * freqs[None, :]
        cos = jnp.concatenate([jnp.cos(ang_r), jnp.cos(ang_r), jnp.cos(ang_c), jnp.cos(ang_c)], axis=-1)
        sin = jnp.concatenate([-jnp.sin(ang_r), jnp.sin(ang_r), -jnp.sin(ang_c), jnp.sin(ang_c)], axis=-1)
        cos = jnp.concatenate([cos, jnp.ones((n_ctx, hd), F32)], axis=0)
        sin = jnp.concatenate([sin, jnp.zeros((n_ctx, hd), F32)], axis=0)
        reps = LANES // hd
        return jnp.tile(cos, (1, reps)), jnp.tile(sin, (1, reps))

    c128, s128 = table(RET_DK)
    c64, s64 = table(SWA_HD)
    return c128, s128, c64, s64


def _relayout_w_in(w):
    parts = []
    start = 0
    for width in SPLITS:
        parts.append(w[:, start:start + width])
        start += width

    def duplicate_heads(p):
        return jnp.concatenate([p[:, h * SWA_HD:(h + 1) * SWA_HD] for h in range(SWA_KV_HEADS) for _ in range(2)],
                               axis=1)

    parts[5] = duplicate_heads(parts[5])
    parts[6] = duplicate_heads(parts[6])
    return jnp.concatenate(parts, axis=1).astype(BF16)


def kernel(x, c, ctx, c_ctx, w_mod, b_mod, g_norm1, g_norm2, w_in, ret_decay_logit, swa_sink, g_qnorm, g_knorm,
           w_br_ret, w_br_swa, w_br_ga, w_out, w_router, b_router, w_gate, w_up, w_down, g_final):
    nb, length, d = x.shape
    n_ctx = ctx.shape[1]
    depth = w_mod.shape[0]
    nt = length + n_ctx
    n_lat_tiles = length // TOKEN_TILE
    n_all_tiles = nt // TOKEN_TILE
    assert GA_HD == RET_DK == LANES and 2 * SWA_HD == LANES
    assert length % TOKEN_TILE == 0 and n_ctx == TOKEN_TILE and length % n_ctx == 0

    mod_rows = 8
    c_rows = jnp.concatenate([c, c_ctx[None, :], jnp.zeros((mod_rows - nb - 1, d), F32)], axis=0)
    mods = _modulation(c_rows, w_mod, b_mod)
    tables = _rope_tables(length, n_ctx)
    wrt = w_router.astype(F32).T
    br = b_router.astype(F32).reshape(N_EXPERTS, 1)

    xa = jnp.concatenate([x, ctx], axis=1)
    h = _prenorm(xa, mods[0], g_norm1[0], n_lat_tiles)
    out = None
    for l in range(depth):
        need_ctx = l < depth - 1
        n_tiles = n_all_tiles if need_ctx else n_lat_tiles
        (rq, rk, rv, ru, sq, sk, sv, gq, gk, gv, ar, as_, aa) = _inproj(
            h, _relayout_w_in(w_in[l]), tables, g_qnorm[l], g_knorm[l])
        o_ret = _retention(ret_decay_logit[l].astype(F32), rq, rk, rv, ru, length, n_ctx, need_ctx)
        o_swa = _window_attention(swa_sink[l].astype(F32), sq, sk, sv, length, n_ctx, need_ctx)
        o_ga = _global_attention(gq, gk, gv, length, n_ctx, need_ctx)
        wbr = jnp.stack([w_br_ret[l], w_br_swa[l], w_br_ga[l]]).astype(BF16)
        xa, h2, experts, weights = _merge(o_ret, o_swa, o_ga, ar, as_, aa, xa, mods[l], wbr,
                                          w_out[l].astype(BF16), g_norm2[l], wrt, br, n_lat_tiles, n_tiles)
        dest_tiles, block_e, n_slots = _slot_plan(experts)
        buf = _dispatch(dest_tiles, h2, n_slots)
        y = _experts(block_e, buf, w_gate[l].astype(BF16), w_up[l].astype(BF16), w_down[l].astype(BF16))
        wt = weights.transpose(0, 2, 1)
        if need_ctx:
            mod_pair = jnp.concatenate([mods[l], mods[l + 1][:, :2 * d]], axis=1)
            xa, h = _combine(dest_tiles, y, wt, xa, mod_pair, g_norm1[l + 1], n_lat_tiles, n_tiles, last=False)
        else:
            out = _combine(dest_tiles, y, wt, xa, mods[l], g_final, n_lat_tiles, n_tiles, last=True)
    return out
```

```python
import functools

import jax
import jax.numpy as jnp
from jax import lax
from jax.experimental import pallas as pl
from jax.experimental.pallas import tpu as pltpu

F32 = jnp.float32
BF16 = jnp.bfloat16

D_MODEL = 1024
GRID_W = 64
NORM_EPS = 1e-6
ROPE_THETA = 10000.0
RET_HEADS, RET_DK, RET_DV, RET_CHUNK = 4, 128, 256, 128
SWA_HEADS, SWA_KV_HEADS, SWA_HD, WINDOW = 16, 2, 64, 128
GA_HEADS, GA_KV_HEADS, GA_HD = 8, 2, 128
N_EXPERTS, N_GROUPS, EXPERTS_PER_GROUP, D_EXPERT = 16, 4, 4, 512

RET_QK = RET_HEADS * RET_DK
RET_V = RET_HEADS * RET_DV
SWA_Q = SWA_HEADS * SWA_HD
SWA_KV = SWA_KV_HEADS * SWA_HD
GA_Q = GA_HEADS * GA_HD
GA_KV = GA_KV_HEADS * GA_HD
SPLITS = (RET_QK, RET_QK, RET_V, RET_V, SWA_Q, SWA_KV, SWA_KV, GA_Q, GA_KV, GA_KV,
          D_MODEL, D_MODEL, D_MODEL)

LANES = 128
TOKEN_TILE = 256
QUERY_TILE = 128
MOE_ROWS = 256
GA_KEY_CHUNK = 256
MOD_COLS = 1536
VMEM_LIMIT = 56 * 1024 * 1024
MASKED = -1e30
LOG2E = 1.4426950408889634

_W_WIDTHS = (RET_QK, RET_QK, RET_V, RET_V, SWA_Q, 2 * SWA_KV, 2 * SWA_KV, GA_Q, GA_KV, GA_KV,
             D_MODEL, D_MODEL, D_MODEL)
_W_OFFS = tuple(sum(_W_WIDTHS[:i]) for i in range(len(_W_WIDTHS)))
W_ALL = sum(_W_WIDTHS)


def _dot(a, b):
    return jnp.dot(a, b, preferred_element_type=F32)


def _dot_nt(a, b):
    return lax.dot_general(a, b, (((1,), (1,)), ((), ())), preferred_element_type=F32)


def _dot_tn(a, b):
    return lax.dot_general(a, b, (((0,), (0,)), ((), ())), preferred_element_type=F32)


def _silu(x):
    return x * jax.nn.sigmoid(x)


def _rms_normalize(x):
    return x * lax.rsqrt(jnp.mean(x * x, axis=-1, keepdims=True) + NORM_EPS)


def _params(**kw):
    return pltpu.CompilerParams(vmem_limit_bytes=VMEM_LIMIT, **kw)


def _mod_kernel(c_ref, w_ref, b_ref, o_ref):
    a = _silu(c_ref[...])
    o_ref[0] = _dot(a.astype(BF16), w_ref[0].astype(BF16)) + b_ref[0]


def _modulation(c_rows, w_mod, b_mod):
    depth, d, n = w_mod.shape
    rows = c_rows.shape[0]
    return pl.pallas_call(
        _mod_kernel,
        out_shape=jax.ShapeDtypeStruct((depth, rows, n), F32),
        grid=(depth, n // MOD_COLS),
        in_specs=[pl.BlockSpec((rows, d), lambda l, j: (0, 0)),
                  pl.BlockSpec((1, d, MOD_COLS), lambda l, j: (l, 0, j)),
                  pl.BlockSpec((1, 1, MOD_COLS), lambda l, j: (l, 0, j))],
        out_specs=pl.BlockSpec((1, rows, MOD_COLS), lambda l, j: (l, 0, j)),
        compiler_params=_params(),
        name="modulation",
    )(c_rows, w_mod, b_mod.reshape(depth, 1, n))


def _mod_row(mod_ref, n_lat_tiles, n_batch):
    b = pl.program_id(0)
    i = pl.program_id(1)
    r = jnp.where(i < n_lat_tiles, b, n_batch)
    return mod_ref[pl.ds(r, 1), :]


def _mod_chunk(m, k):
    return m[:, k * D_MODEL:(k + 1) * D_MODEL]


def _prenorm_kernel(x_ref, mod_ref, g_ref, h_ref, *, n_lat_tiles, n_batch):
    m = _mod_row(mod_ref, n_lat_tiles, n_batch)
    h = _rms_normalize(x_ref[0]) * g_ref[...] * (1.0 + _mod_chunk(m, 1)) + _mod_chunk(m, 0)
    h_ref[0] = h.astype(BF16)


def _prenorm(xa, mod, g, n_lat_tiles):
    nb, nt, d = xa.shape
    kern = functools.partial(_prenorm_kernel, n_lat_tiles=n_lat_tiles, n_batch=nb)
    return pl.pallas_call(
        kern,
        out_shape=jax.ShapeDtypeStruct((nb, nt, d), BF16),
        grid=(nb, nt // TOKEN_TILE),
        in_specs=[pl.BlockSpec((1, TOKEN_TILE, d), lambda b, i: (b, i, 0)),
                  pl.BlockSpec(mod.shape, lambda b, i: (0, 0)),
                  pl.BlockSpec((1, d), lambda b, i: (0, 0))],
        out_specs=pl.BlockSpec((1, TOKEN_TILE, d), lambda b, i: (b, i, 0)),
        compiler_params=_params(),
        name="prenorm",
    )(xa, mod, g.reshape(1, d))


def _rope(x, cos, sin, quarter):
    lane = lax.broadcasted_iota(jnp.int32, x.shape, 1)
    first = (lane % (2 * quarter)) < quarter
    partner = jnp.where(first, pltpu.roll(x, LANES - quarter, 1), pltpu.roll(x, quarter, 1))
    return x * cos + partner * sin


def _inproj_kernel(h_ref, w_ref, c128_ref, s128_ref, c64_ref, s64_ref, gq_ref, gk_ref,
                   rq, rk, rv, ru, sq, sk, sv, gq, gk, gv, ar, as_, aa):
    h = h_ref[0]
    c128, s128 = c128_ref[...], s128_ref[...]
    c64, s64 = c64_ref[...], s64_ref[...]

    def proj(idx):
        return _dot(h, w_ref[:, _W_OFFS[idx]:_W_OFFS[idx] + _W_WIDTHS[idx]])

    def slabs(acc):
        return [acc[:, s * LANES:(s + 1) * LANES] for s in range(acc.shape[1] // LANES)]

    def store(ref, s, val):
        ref[0, :, s * LANES:(s + 1) * LANES] = val.astype(ref.dtype)

    for s, xs in enumerate(slabs(proj(0))):
        store(rq, s, _rope(xs, c128, s128, RET_DK // 4))
    for s, xs in enumerate(slabs(proj(1))):
        store(rk, s, _rope(xs, c128, s128, RET_DK // 4) * (RET_DK ** -0.5))
    rv[0] = proj(2).astype(rv.dtype)
    ru[0] = proj(3).astype(ru.dtype)
    for s, xs in enumerate(slabs(proj(4))):
        store(sq, s, _rope(xs, c64, s64, SWA_HD // 4) * (SWA_HD ** -0.5 * LOG2E))
    for s, xs in enumerate(slabs(proj(5))):
        store(sk, s, _rope(xs, c64, s64, SWA_HD // 4))
    sv[0] = proj(6).astype(sv.dtype)
    for s, xs in enumerate(slabs(proj(7))):
        xn = _rms_normalize(xs) * gq_ref[...]
        store(gq, s, _rope(xn, c128, s128, GA_HD // 4) * (GA_HD ** -0.5 * LOG2E))
    for s, xs in enumerate(slabs(proj(8))):
        xn = _rms_normalize(xs) * gk_ref[...]
        store(gk, s, _rope(xn, c128, s128, GA_HD // 4))
    gv[0] = proj(9).astype(gv.dtype)
    ar[0] = proj(10).astype(ar.dtype)
    as_[0] = proj(11).astype(as_.dtype)
    aa[0] = proj(12).astype(aa.dtype)


def _inproj(h, w_all, tables, g_q, g_k):
    nb, nt, d = h.shape
    tile = lambda width: pl.BlockSpec((1, TOKEN_TILE, width), lambda b, i: (b, i, 0))
    tab = pl.BlockSpec((TOKEN_TILE, LANES), lambda b, i: (i, 0))
    vec = pl.BlockSpec((1, LANES), lambda b, i: (0, 0))
    outs = [jax.ShapeDtypeStruct((nb, nt, w), BF16) for w in _W_WIDTHS]
    return pl.pallas_call(
        _inproj_kernel,
        out_shape=outs,
        grid=(nb, nt // TOKEN_TILE),
        in_specs=[tile(d),
                  pl.BlockSpec((d, W_ALL), lambda b, i: (0, 0), pipeline_mode=pl.Buffered(1)),
                  tab, tab, tab, tab, vec, vec],
        out_specs=[tile(w) for w in _W_WIDTHS],
        compiler_params=_params(),
        name="inproj",
    )(h, w_all, *tables, g_q.reshape(1, LANES), g_k.reshape(1, LANES))


def _log_sigmoid(x):
    return jnp.minimum(x, 0.0) - jnp.log1p(jnp.exp(-jnp.abs(x)))


def _ret_kernel(logit_ref, q_ref, k_ref, v_ref, u_ref, o_ref, acc_ref, sf_ref, sb_ref,
                *, length, n_ctx, need_ctx):
    hh = pl.program_id(1)
    chunk = RET_CHUNK
    n_chunks = length // chunk
    lgf = _log_sigmoid(jnp.full((1, 1), logit_ref[0, hh], F32))
    lgb = _log_sigmoid(jnp.full((1, 1), logit_ref[1, hh], F32))

    def rows_f32(shape):
        return lax.broadcasted_iota(jnp.int32, shape, 0).astype(F32)

    def both_ways_decay(n):
        diff = rows_f32((n, n)) - lax.broadcasted_iota(jnp.int32, (n, n), 1).astype(F32)
        return jnp.where(diff >= 0, jnp.exp(jnp.maximum(diff, 0.0) * lgf),
                         jnp.exp(jnp.maximum(-diff, 0.0) * lgb))

    def finish(o, u):
        return (_rms_normalize(o) * _silu(u.astype(F32))).astype(o_ref.dtype)

    kx = k_ref[0, length:length + n_ctx, :].astype(F32)
    vx = v_ref[0, length:length + n_ctx, :]
    lx = rows_f32((n_ctx, RET_DK))
    sf_ref[...] = _dot_tn((kx * jnp.exp((n_ctx - 1.0 - lx) * lgf)).astype(BF16), vx)
    sb_ref[...] = _dot_tn((kx * jnp.exp(lx * lgb)).astype(BF16), vx)

    if need_ctx:
        qx = q_ref[0, length:length + n_ctx, :]
        sx = _dot_nt(qx, kx.astype(BF16)) * both_ways_decay(n_ctx)
        o_ref[0, length:length + n_ctx, :] = finish(_dot(sx.astype(BF16), vx),
                                                    u_ref[0, length:length + n_ctx, :])

    ci = rows_f32((chunk, RET_DK))
    q_dec_f = jnp.exp((ci + 1.0) * lgf)
    k_dec_f = jnp.exp((chunk - 1.0 - ci) * lgf)
    q_dec_b = jnp.exp((chunk - ci) * lgb)
    k_dec_b = jnp.exp(ci * lgb)
    chunk_dec_f = jnp.exp(chunk * lgf)
    chunk_dec_b = jnp.exp(chunk * lgb)
    intra = both_ways_decay(chunk)
    acc_ref[...] = jnp.zeros_like(acc_ref)

    def step(s, carry):
        rf = pl.multiple_of(s * chunk, chunk)
        rb = pl.multiple_of((n_chunks - 1 - s) * chunk, chunk)
        qf = q_ref[0, pl.ds(rf, chunk), :]
        kf = k_ref[0, pl.ds(rf, chunk), :]
        vf = v_ref[0, pl.ds(rf, chunk), :]
        sc = _dot_nt(qf, kf) * intra
        state_f = sf_ref[...]
        o = _dot(sc.astype(BF16), vf) + _dot((qf.astype(F32) * q_dec_f).astype(BF16), state_f.astype(BF16))
        acc_ref[pl.ds(rf, chunk), :] += o
        sf_ref[...] = state_f * chunk_dec_f + _dot_tn((kf.astype(F32) * k_dec_f).astype(BF16), vf)
        qb = q_ref[0, pl.ds(rb, chunk), :]
        kb = k_ref[0, pl.ds(rb, chunk), :]
        vb = v_ref[0, pl.ds(rb, chunk), :]
        state_b = sb_ref[...]
        acc_ref[pl.ds(rb, chunk), :] += _dot((qb.astype(F32) * q_dec_b).astype(BF16), state_b.astype(BF16))
        sb_ref[...] = state_b * chunk_dec_b + _dot_tn((kb.astype(F32) * k_dec_b).astype(BF16), vb)
        return carry

    lax.fori_loop(0, n_chunks, step, 0)

    def out_step(c, carry):
        r = pl.multiple_of(c * chunk, chunk)
        o_ref[0, pl.ds(r, chunk), :] = finish(acc_ref[pl.ds(r, chunk), :], u_ref[0, pl.ds(r, chunk), :])
        return carry

    lax.fori_loop(0, n_chunks, out_step, 0)


def _retention(logit, rq, rk, rv, ru, length, n_ctx, need_ctx):
    nb, nt, _ = rq.shape
    out_rows = nt if need_ctx else length
    kern = functools.partial(_ret_kernel, length=length, n_ctx=n_ctx, need_ctx=need_ctx)
    head = lambda width: pl.BlockSpec((1, nt, width), lambda b, h: (b, 0, h))
    return pl.pallas_call(
        kern,
        out_shape=jax.ShapeDtypeStruct((nb, out_rows, RET_V), BF16),
        grid=(nb, RET_HEADS),
        in_specs=[pl.BlockSpec(memory_space=pltpu.SMEM),
                  head(RET_DK), head(RET_DK), head(RET_DV), head(RET_DV)],
        out_specs=pl.BlockSpec((1, out_rows, RET_DV), lambda b, h: (b, 0, h)),
        scratch_shapes=[pltpu.VMEM((length, RET_DV), F32),
                        pltpu.VMEM((RET_DK, RET_DV), F32),
                        pltpu.VMEM((RET_DK, RET_DV), F32)],
        compiler_params=_params(),
        name="retention",
    )(logit, rq, rk, rv, ru)


def _swa_group(q_ref, g, kcat, vcat, biases, sink_ref, o_ref):
    tq = q_ref.shape[1]
    heads_per_group = SWA_HEADS // SWA_KV_HEADS
    slabs_per_group = heads_per_group // 2
    low = lax.broadcasted_iota(jnp.int32, (tq, LANES), 1) < SWA_HD
    zero = jnp.zeros((tq, LANES), q_ref.dtype)
    stacked = []
    for a in range(slabs_per_group):
        s0 = (g * slabs_per_group + a) * LANES
        slab = q_ref[0, :, s0:s0 + LANES]
        stacked.append(jnp.where(low, slab, zero))
        stacked.append(jnp.where(low, zero, slab))
    heads_per_part = heads_per_group // 2
    rows = heads_per_part * tq
    for part in range(2):
        q = jnp.concatenate(stacked[part * heads_per_part:(part + 1) * heads_per_part], axis=0)
        s = _dot_nt(q, kcat)
        tiles = [s[:, t * LANES:(t + 1) * LANES] for t in range(s.shape[1] // LANES)]
        for t, bias in biases.items():
            tiles[t] = tiles[t] + jnp.concatenate([bias] * heads_per_part, axis=0)
        sink = jnp.concatenate(
            [jnp.full((tq, LANES), sink_ref[g * heads_per_group + part * heads_per_part + h] * LOG2E, F32)
             for h in range(heads_per_part)], axis=0)
        lane_max = functools.reduce(jnp.maximum, tiles)
        m = jnp.maximum(jnp.broadcast_to(jnp.max(lane_max, axis=-1, keepdims=True), (rows, LANES)), sink)
        ps = [jnp.exp2(tile - m) for tile in tiles]
        den = (jnp.broadcast_to(jnp.sum(functools.reduce(jnp.add, ps), axis=-1, keepdims=True), (rows, LANES))
               + jnp.exp2(sink - m))
        o = _dot(jnp.concatenate(ps, axis=1).astype(BF16), vcat) / den
        for a in range(heads_per_part // 2):
            s0 = (g * slabs_per_group + part * (heads_per_part // 2) + a) * LANES
            even = o[(2 * a) * tq:(2 * a + 1) * tq]
            odd = o[(2 * a + 1) * tq:(2 * a + 2) * tq]
            o_ref[0, :, s0:s0 + LANES] = jnp.where(low, even, odd).astype(o_ref.dtype)


def _swa_kernel(sink_ref, q_ref, kp_ref, kc_ref, kn_ref, kx_ref, vp_ref, vc_ref, vn_ref, vx_ref, o_ref,
                *, n_lat_tiles, need_ctx):
    j = pl.program_id(1)
    tq = q_ref.shape[1]
    n_ctx = kx_ref.shape[1]

    def lanes_of(ref, g):
        return ref[0, :, g * LANES:(g + 1) * LANES]

    @pl.when(j < n_lat_tiles)
    def _latent():
        ci = lax.broadcasted_iota(jnp.int32, (tq, tq), 1)
        ri = lax.broadcasted_iota(jnp.int32, (tq, tq), 0)
        far = 4 * tq
        ri_prev = ri + jnp.where(j > 0, 0, far)
        ri_next = ri - jnp.where(j < n_lat_tiles - 1, 0, far)
        zero = jnp.zeros((tq, tq), F32)
        masked = jnp.full((tq, tq), MASKED, F32)
        bias = {0: jnp.where(ci >= ri_prev, zero, masked), 2: jnp.where(ci <= ri_next, zero, masked)}
        for g in range(SWA_KV_HEADS):
            kcat = jnp.concatenate([lanes_of(kp_ref, g), lanes_of(kc_ref, g), lanes_of(kn_ref, g),
                                    lanes_of(kx_ref, g)], axis=0)
            vcat = jnp.concatenate([lanes_of(vp_ref, g), lanes_of(vc_ref, g), lanes_of(vn_ref, g),
                                    lanes_of(vx_ref, g)], axis=0)
            _swa_group(q_ref, g, kcat, vcat, bias, sink_ref, o_ref)

    if need_ctx:
        @pl.when(j >= n_lat_tiles)
        def _context():
            for g in range(SWA_KV_HEADS):
                _swa_group(q_ref, g, lanes_of(kx_ref, g), lanes_of(vx_ref, g), {}, sink_ref, o_ref)


def _window_attention(sink, sq, sk, sv, length, n_ctx, need_ctx):
    nb, nt, _ = sq.shape
    tq = QUERY_TILE
    n_lat = length // tq
    out_rows = nt if need_ctx else length
    kvw = 2 * SWA_KV
    kern = functools.partial(_swa_kernel, n_lat_tiles=n_lat, need_ctx=need_ctx)
    prev = pl.BlockSpec((1, tq, kvw), lambda b, j: (b, jnp.clip(j - 1, 0, n_lat - 1), 0))
    cur = pl.BlockSpec((1, tq, kvw), lambda b, j: (b, jnp.minimum(j, n_lat - 1), 0))
    nxt = pl.BlockSpec((1, tq, kvw), lambda b, j: (b, jnp.minimum(j + 1, n_lat - 1), 0))
    ctx = pl.BlockSpec((1, n_ctx, kvw), lambda b, j: (b, length // n_ctx, 0))
    return pl.pallas_call(
        kern,
        out_shape=jax.ShapeDtypeStruct((nb, out_rows, SWA_Q), BF16),
        grid=(nb, out_rows // tq),
        in_specs=[pl.BlockSpec(memory_space=pltpu.SMEM),
                  pl.BlockSpec((1, tq, SWA_Q), lambda b, j: (b, j, 0)),
                  prev, cur, nxt, ctx, prev, cur, nxt, ctx],
        out_specs=pl.BlockSpec((1, tq, SWA_Q), lambda b, j: (b, j, 0)),
        compiler_params=_params(),
        name="window_attention",
    )(sink, sq, sk, sk, sk, sk, sv, sv, sv, sv)


def _ga_kernel(q_ref, k_ref, v_ref, o_ref, *, length, n_ctx, n_lat_tiles, need_ctx):
    j = pl.program_id(2)
    tq = q_ref.shape[1]
    heads_per_group = GA_HEADS // GA_KV_HEADS
    q = jnp.concatenate([q_ref[0, :, h * LANES:(h + 1) * LANES] for h in range(heads_per_group)], axis=0)

    rows = heads_per_group * tq

    def attend(key_lo, key_hi):
        m = l = acc = None
        for c0 in range(key_lo, key_hi, GA_KEY_CHUNK):
            s = _dot_nt(q, k_ref[0, c0:c0 + GA_KEY_CHUNK, :])
            parts = [s[:, t * LANES:(t + 1) * LANES] for t in range(GA_KEY_CHUNK // LANES)]
            lane_max = functools.reduce(jnp.maximum, parts)
            mc = jnp.broadcast_to(jnp.max(lane_max, axis=-1, keepdims=True), (rows, LANES))
            m_new = mc if m is None else jnp.maximum(m, mc)
            ps = [jnp.exp2(part - m_new) for part in parts]
            pv = _dot(jnp.concatenate(ps, axis=1).astype(BF16), v_ref[0, c0:c0 + GA_KEY_CHUNK, :])
            psum = functools.reduce(jnp.add, ps)
            if m is None:
                l, acc = psum, pv
            else:
                alpha = jnp.exp2(m - m_new)
                l, acc = alpha * l + psum, alpha * acc + pv
            m = m_new
        o = acc / jnp.sum(l, axis=-1, keepdims=True)
        for h in range(heads_per_group):
            o_ref[0, :, h * LANES:(h + 1) * LANES] = o[h * tq:(h + 1) * tq].astype(o_ref.dtype)

    @pl.when(j < n_lat_tiles)
    def _latent():
        attend(0, length + n_ctx)

    if need_ctx:
        @pl.when(j >= n_lat_tiles)
        def _context():
            attend(length, length + n_ctx)


def _global_attention(gq, gk, gv, length, n_ctx, need_ctx):
    nb, nt, _ = gq.shape
    tq = QUERY_TILE
    out_rows = nt if need_ctx else length
    group_w = GA_Q // GA_KV_HEADS
    kern = functools.partial(_ga_kernel, length=length, n_ctx=n_ctx, n_lat_tiles=length // tq,
                             need_ctx=need_ctx)
    kv = pl.BlockSpec((1, nt, GA_HD), lambda b, g, j: (b, 0, g))
    return pl.pallas_call(
        kern,
        out_shape=jax.ShapeDtypeStruct((nb, out_rows, GA_Q), BF16),
        grid=(nb, GA_KV_HEADS, out_rows // tq),
        in_specs=[pl.BlockSpec((1, tq, group_w), lambda b, g, j: (b, j, g)), kv, kv],
        out_specs=pl.BlockSpec((1, tq, group_w), lambda b, g, j: (b, j, g)),
        compiler_params=_params(),
        name="global_attention",
    )(gq, gk, gv)


def _route(scores, biased):
    rows = [biased[r:r + 1, :] for r in range(N_EXPERTS)]
    raw = [scores[r:r + 1, :] for r in range(N_EXPERTS)]

    def top2_sum(vals):
        best = None
        for a in range(len(vals)):
            for b in range(a + 1, len(vals)):
                pair = vals[a] + vals[b]
                best = pair if best is None else jnp.maximum(best, pair)
        return best

    group_scores = [top2_sum(rows[g * EXPERTS_PER_GROUP:(g + 1) * EXPERTS_PER_GROUP]) for g in range(N_GROUPS)]
    group = jnp.zeros_like(group_scores[0], dtype=jnp.int32)
    best = group_scores[0]
    for g in range(1, N_GROUPS):
        better = group_scores[g] > best
        group = jnp.where(better, g, group)
        best = jnp.where(better, group_scores[g], best)

    def in_group(table, k):
        val = table[k]
        for g in range(1, N_GROUPS):
            val = jnp.where(group == g, table[g * EXPERTS_PER_GROUP + k], val)
        return val

    vals = [in_group(rows, k) for k in range(EXPERTS_PER_GROUP)]
    unbiased = [in_group(raw, k) for k in range(EXPERTS_PER_GROUP)]

    def first_argmax(cands):
        idx = jnp.zeros_like(group)
        top = cands[0]
        for k in range(1, len(cands)):
            better = cands[k] > top
            idx = jnp.where(better, k, idx)
            top = jnp.where(better, cands[k], top)
        return idx

    i1 = first_argmax(vals)
    i2 = first_argmax([jnp.where(i1 == k, -jnp.inf, vals[k]) for k in range(EXPERTS_PER_GROUP)])

    def pick(idx):
        val = unbiased[0]
        for k in range(1, EXPERTS_PER_GROUP):
            val = jnp.where(idx == k, unbiased[k], val)
        return val

    s1, s2 = pick(i1), pick(i2)
    total = s1 + s2
    experts = jnp.concatenate([group * EXPERTS_PER_GROUP + i1, group * EXPERTS_PER_GROUP + i2], axis=0)
    weights = jnp.concatenate([s1 / total, s2 / total], axis=0)
    return experts, weights


def _merge_kernel(oret, oswa, oga, ar, as_, aa, x_ref, mod_ref, wbr_ref, wout_ref, g2_ref, wrt_ref, br_ref,
                  xo_ref, h2_ref, e_ref, w_ref, *, n_lat_tiles, n_batch):
    m = _mod_row(mod_ref, n_lat_tiles, n_batch)

    def gate(a_ref):
        return jax.nn.sigmoid(a_ref[0].astype(F32))

    y = (gate(ar) * _dot(oret[0], wbr_ref[0]) + gate(as_) * _dot(oswa[0], wbr_ref[1])
         + gate(aa) * _dot(oga[0], wbr_ref[2]))
    x = x_ref[0] + _mod_chunk(m, 2) * _dot(y.astype(BF16), wout_ref[...])
    xo_ref[0] = x
    h2 = _rms_normalize(x) * g2_ref[...] * (1.0 + _mod_chunk(m, 4)) + _mod_chunk(m, 3)
    h2_ref[0] = h2
    logits = lax.dot_general(wrt_ref[...], h2, (((1,), (1,)), ((), ())),
                             precision=lax.Precision.HIGHEST, preferred_element_type=F32)
    scores = jax.nn.sigmoid(logits)
    experts, weights = _route(scores, scores + br_ref[...])
    e_ref[0] = experts
    w_ref[0] = weights


def _merge(oret, oswa, oga, ar, as_, aa, xa, mod, wbr, wout, g2, wrt, br, n_lat_tiles, n_tiles):
    nb, nt, d = xa.shape
    rows = n_tiles * TOKEN_TILE
    kern = functools.partial(_merge_kernel, n_lat_tiles=n_lat_tiles, n_batch=nb)
    tile = pl.BlockSpec((1, TOKEN_TILE, d), lambda b, i: (b, i, 0))
    whole = lambda arr: pl.BlockSpec(arr.shape, lambda b, i: (0,) * arr.ndim)
    small = pl.BlockSpec((1, 2, TOKEN_TILE), lambda b, i: (b, 0, i))
    return pl.pallas_call(
        kern,
        out_shape=[jax.ShapeDtypeStruct((nb, nt, d), F32),
                   jax.ShapeDtypeStruct((nb, rows, d), F32),
                   jax.ShapeDtypeStruct((nb, 2, rows), jnp.int32),
                   jax.ShapeDtypeStruct((nb, 2, rows), F32)],
        grid=(nb, n_tiles),
        in_specs=[tile, tile, tile, tile, tile, tile, tile, whole(mod), whole(wbr), whole(wout),
                  pl.BlockSpec((1, d), lambda b, i: (0, 0)), whole(wrt), whole(br)],
        out_specs=[tile, tile, small, small],
        input_output_aliases={6: 0},
        compiler_params=_params(),
        name="merge_route",
    )(oret, oswa, oga, ar, as_, aa, xa, mod, wbr, wout, g2.reshape(1, d), wrt, br)


def _row_copy(src, dst, sem):
    return pltpu.make_async_copy(src, dst, sem)


def _dispatch_kernel(dest_ref, h2_ref, buf_in, buf_out, sem):
    del buf_in

    def issue(r, carry):
        for k in range(2):
            d = dest_ref[0, 0, k * TOKEN_TILE + r]
            _row_copy(h2_ref.at[0, pl.ds(r, 1), :], buf_out.at[pl.ds(d, 1), :], sem).start()
        return carry

    lax.fori_loop(0, TOKEN_TILE, issue, 0, unroll=8)

    def drain(r, carry):
        for k in range(2):
            _row_copy(h2_ref.at[0, pl.ds(0, 1), :], buf_out.at[pl.ds(0, 1), :], sem).wait()
        return carry

    lax.fori_loop(0, TOKEN_TILE, drain, 0, unroll=8)


def _dispatch(dest_tiles, h2, n_slots):
    nb, rows, d = h2.shape
    n_tiles = rows // TOKEN_TILE
    return pl.pallas_call(
        _dispatch_kernel,
        out_shape=jax.ShapeDtypeStruct((n_slots, d), F32),
        grid=(nb, n_tiles),
        in_specs=[pl.BlockSpec((1, 1, 2 * TOKEN_TILE), lambda b, i: (b * n_tiles + i, 0, 0),
                               memory_space=pltpu.SMEM),
                  pl.BlockSpec((1, TOKEN_TILE, d), lambda b, i: (b, i, 0)),
                  pl.BlockSpec(memory_space=pl.ANY)],
        out_specs=pl.BlockSpec(memory_space=pl.ANY),
        scratch_shapes=[pltpu.SemaphoreType.DMA(())],
        input_output_aliases={2: 0},
        compiler_params=_params(has_side_effects=True),
        name="moe_dispatch",
    )(dest_tiles, h2, jnp.zeros((n_slots, d), F32))


def _expert_kernel(be_ref, x_ref, wg_ref, wu_ref, wd_ref, y_ref):
    del be_ref
    x = x_ref[...].astype(BF16)
    hid = _silu(_dot(x, wg_ref[0])) * _dot(x, wu_ref[0])
    y_ref[...] = _dot(hid.astype(BF16), wd_ref[0])


def _experts(block_e, buf, wg, wu, wd):
    n_slots, d = buf.shape
    grid_spec = pltpu.PrefetchScalarGridSpec(
        num_scalar_prefetch=1,
        grid=(n_slots // MOE_ROWS,),
        in_specs=[pl.BlockSpec((MOE_ROWS, d), lambda i, be: (i, 0)),
                  pl.BlockSpec((1, d, D_EXPERT), lambda i, be: (be[i], 0, 0)),
                  pl.BlockSpec((1, d, D_EXPERT), lambda i, be: (be[i], 0, 0)),
                  pl.BlockSpec((1, D_EXPERT, d), lambda i, be: (be[i], 0, 0))],
        out_specs=pl.BlockSpec((MOE_ROWS, d), lambda i, be: (i, 0)),
    )
    return pl.pallas_call(
        _expert_kernel,
        out_shape=jax.ShapeDtypeStruct((n_slots, d), F32),
        grid_spec=grid_spec,
        compiler_params=_params(),
        name="moe_experts",
    )(block_e, buf, wg, wu, wd)


def _combine_kernel(dest_ref, y_hbm, wt_ref, x_ref, mod_ref, g_ref, *rest,
                    n_lat_tiles, n_batch, last):
    if last:
        out_ref, gbuf, sem = rest
    else:
        xo_ref, h_ref, gbuf, sem = rest

    def issue(r, carry):
        for k in range(2):
            d = dest_ref[0, 0, k * TOKEN_TILE + r]
            _row_copy(y_hbm.at[pl.ds(d, 1), :], gbuf.at[k, pl.ds(r, 1), :], sem).start()
        return carry

    lax.fori_loop(0, TOKEN_TILE, issue, 0, unroll=8)

    def drain(r, carry):
        for k in range(2):
            _row_copy(y_hbm.at[pl.ds(0, 1), :], gbuf.at[0, pl.ds(0, 1), :], sem).wait()
        return carry

    lax.fori_loop(0, TOKEN_TILE, drain, 0, unroll=8)

    m = _mod_row(mod_ref, n_lat_tiles, n_batch)
    wt = wt_ref[0]
    moe = gbuf[0] * wt[:, 0:1] + gbuf[1] * wt[:, 1:2]
    x = x_ref[0] + _mod_chunk(m, 5) * moe
    if last:
        out_ref[0] = _rms_normalize(x) * g_ref[...]
    else:
        xo_ref[0] = x
        h = _rms_normalize(x) * g_ref[...] * (1.0 + _mod_chunk(m, 7)) + _mod_chunk(m, 6)
        h_ref[0] = h.astype(BF16)


def _combine(dest_tiles, y, wt, xa, mod, g, n_lat_tiles, n_tiles, last):
    nb, nt, d = xa.shape
    rows = n_tiles * TOKEN_TILE
    kern = functools.partial(_combine_kernel, n_lat_tiles=n_lat_tiles, n_batch=nb, last=last)
    tile = pl.BlockSpec((1, TOKEN_TILE, d), lambda b, i: (b, i, 0))
    if last:
        out_shape = jax.ShapeDtypeStruct((nb, rows, d), F32)
        out_specs = tile
        aliases = {}
    else:
        out_shape = [jax.ShapeDtypeStruct((nb, nt, d), F32), jax.ShapeDtypeStruct((nb, nt, d), BF16)]
        out_specs = [tile, tile]
        aliases = {3: 0}
    return pl.pallas_call(
        kern,
        out_shape=out_shape,
        grid=(nb, n_tiles),
        in_specs=[pl.BlockSpec((1, 1, 2 * TOKEN_TILE), lambda b, i: (b * n_tiles + i, 0, 0),
                               memory_space=pltpu.SMEM),
                  pl.BlockSpec(memory_space=pl.ANY),
                  pl.BlockSpec((1, TOKEN_TILE, 2), lambda b, i: (b, i, 0)),
                  tile,
                  pl.BlockSpec(mod.shape, lambda b, i: (0, 0)),
                  pl.BlockSpec((1, d), lambda b, i: (0, 0))],
        out_specs=out_specs,
        scratch_shapes=[pltpu.VMEM((2, TOKEN_TILE, d), F32), pltpu.SemaphoreType.DMA(())],
        input_output_aliases=aliases,
        compiler_params=_params(),
        name="moe_combine",
    )(dest_tiles, y, wt, xa, mod, g.reshape(1, d))


def _slot_plan(experts):
    nb, _, rows = experts.shape
    flat = experts.reshape(-1)
    one_hot = (flat[:, None] == jnp.arange(N_EXPERTS, dtype=jnp.int32)[None, :]).astype(jnp.int32)
    running = jnp.cumsum(one_hot, axis=0)
    rank = jnp.sum(one_hot * running, axis=1) - 1
    counts = running[-1]
    padded = (counts + MOE_ROWS - 1) // MOE_ROWS * MOE_ROWS
    pad_ends = jnp.cumsum(padded)
    pad_starts = pad_ends - padded
    dest = jnp.sum(one_hot * pad_starts[None, :], axis=1) + rank
    n_assign = flat.shape[0]
    n_blocks = (n_assign + N_EXPERTS * (MOE_ROWS - 1) + MOE_ROWS - 1) // MOE_ROWS
    block_e = jnp.minimum(
        jnp.searchsorted(pad_ends, jnp.arange(n_blocks, dtype=jnp.int32) * MOE_ROWS, side='right'),
        N_EXPERTS - 1).astype(jnp.int32)
    n_tiles = rows // TOKEN_TILE
    dest_tiles = (dest.astype(jnp.int32).reshape(nb, 2, n_tiles, TOKEN_TILE)
                  .transpose(0, 2, 1, 3).reshape(nb * n_tiles, 1, 2 * TOKEN_TILE))
    return dest_tiles, block_e, n_blocks * MOE_ROWS


def _rope_tables(length, n_ctx):
    rows = length // GRID_W
    row = jnp.repeat(jnp.arange(rows, dtype=jnp.int32), GRID_W).astype(F32)
    col = jnp.tile(jnp.arange(GRID_W, dtype=jnp.int32), rows).astype(F32)

    def table(hd):
        quarter = hd // 4
        freqs = ROPE_THETA ** (-jnp.arange(quarter, dtype=F32) / quarter)
        ang_r = row[:, None] * freqs[None, :]
        ang_c = col[:, None] * freqs[None, :]
        cos = jnp.concatenate([jnp.cos(ang_r), jnp.cos(ang_r), jnp.cos(ang_c), jnp.cos(ang_c)], axis=-1)
        sin = jnp.concatenate([-jnp.sin(ang_r), jnp.sin(ang_r), -jnp.sin(ang_c), jnp.sin(ang_c)], axis=-1)
        cos = jnp.concatenate([cos, jnp.ones((n_ctx, hd), F32)], axis=0)
        sin = jnp.concatenate([sin, jnp.zeros((n_ctx, hd), F32)], axis=0)
        reps = LANES // hd
        return jnp.tile(cos, (1, reps)), jnp.tile(sin, (1, reps))

    c128, s128 = table(RET_DK)
    c64, s64 = table(SWA_HD)
    return c128, s128, c64, s64


def _relayout_w_in(w):
    parts = []
    start = 0
    for width in SPLITS:
        parts.append(w[:, start:start + width])
        start += width

    def duplicate_heads(p):
        return jnp.concatenate([p[:, h * SWA_HD:(h + 1) * SWA_HD] for h in range(SWA_KV_HEADS) for _ in range(2)],
                               axis=1)

    parts[5] = duplicate_heads(parts[5])
    parts[6] = duplicate_heads(parts[6])
    return jnp.concatenate(parts, axis=1).astype(BF16)


def kernel(x, c, ctx, c_ctx, w_mod, b_mod, g_norm1, g_norm2, w_in, ret_decay_logit, swa_sink, g_qnorm, g_knorm,
           w_br_ret, w_br_swa, w_br_ga, w_out, w_router, b_router, w_gate, w_up, w_down, g_final):
    nb, length, d = x.shape
    n_ctx = ctx.shape[1]
    depth = w_mod.shape[0]
    nt = length + n_ctx
    n_lat_tiles = length // TOKEN_TILE
    n_all_tiles = nt // TOKEN_TILE
    assert GA_HD == RET_DK == LANES and 2 * SWA_HD == LANES
    assert length % TOKEN_TILE == 0 and n_ctx == TOKEN_TILE and length % n_ctx == 0

    mod_rows = 8
    c_rows = jnp.concatenate([c, c_ctx[None, :], jnp.zeros((mod_rows - nb - 1, d), F32)], axis=0)
    mods = _modulation(c_rows, w_mod, b_mod)
    tables = _rope_tables(length, n_ctx)
    wrt = w_router.astype(F32).T
    br = b_router.astype(F32).reshape(N_EXPERTS, 1)

    xa = jnp.concatenate([x, ctx], axis=1)
    h = _prenorm(xa, mods[0], g_norm1[0], n_lat_tiles)
    out = None
    for l in range(depth):
        need_ctx = l < depth - 1
        n_tiles = n_all_tiles if need_ctx else n_lat_tiles
        (rq, rk, rv, ru, sq, sk, sv, gq, gk, gv, ar, as_, aa) = _inproj(
            h, _relayout_w_in(w_in[l]), tables, g_qnorm[l], g_knorm[l])
        o_ret = _retention(ret_decay_logit[l].astype(F32), rq, rk, rv, ru, length, n_ctx, need_ctx)
        o_swa = _window_attention(swa_sink[l].astype(F32), sq, sk, sv, length, n_ctx, need_ctx)
        o_ga = _global_attention(gq, gk, gv, length, n_ctx, need_ctx)
        wbr = jnp.stack([w_br_ret[l], w_br_swa[l], w_br_ga[l]]).astype(BF16)
        xa, h2, experts, weights = _merge(o_ret, o_swa, o_ga, ar, as_, aa, xa, mods[l], wbr,
                                          w_out[l].astype(BF16), g_norm2[l], wrt, br, n_lat_tiles, n_tiles)
        dest_tiles, block_e, n_slots = _slot_plan(experts)
        buf = _dispatch(dest_tiles, h2, n_slots)
        y = _experts(block_e, buf, w_gate[l].astype(BF16), w_up[l].astype(BF16), w_down[l].astype(BF16))
        wt = weights.transpose(0, 2, 1)
        if need_ctx:
            mod_pair = jnp.concatenate([mods[l], mods[l + 1][:, :2 * d]], axis=1)
            xa, h = _combine(dest_tiles, y, wt, xa, mod_pair, g_norm1[l + 1], n_lat_tiles, n_tiles, last=False)
        else:
            out = _combine(dest_tiles, y, wt, xa, mods[l], g_final, n_lat_tiles, n_tiles, last=True)
    return out
```

```python
import functools

import jax
import jax.numpy as jnp
from jax import lax
from jax.experimental import pallas as pl
from jax.experimental.pallas import tpu as pltpu

F32 = jnp.float32
BF16 = jnp.bfloat16

D_MODEL = 1024
GRID_W = 64
NORM_EPS = 1e-6
ROPE_THETA = 10000.0
RET_HEADS, RET_DK, RET_DV, RET_CHUNK = 4, 128, 256, 128
SWA_HEADS, SWA_KV_HEADS, SWA_HD, WINDOW = 16, 2, 64, 128
GA_HEADS, GA_KV_HEADS, GA_HD = 8, 2, 128
N_EXPERTS, N_GROUPS, EXPERTS_PER_GROUP, D_EXPERT = 16, 4, 4, 512

RET_QK = RET_HEADS * RET_DK
RET_V = RET_HEADS * RET_DV
SWA_Q = SWA_HEADS * SWA_HD
SWA_KV = SWA_KV_HEADS * SWA_HD
GA_Q = GA_HEADS * GA_HD
GA_KV = GA_KV_HEADS * GA_HD
SPLITS = (RET_QK, RET_QK, RET_V, RET_V, SWA_Q, SWA_KV, SWA_KV, GA_Q, GA_KV, GA_KV,
          D_MODEL, D_MODEL, D_MODEL)

LANES = 128
SUBLANES = 8
TOKEN_TILE = 256
QUERY_TILE = 128
MOE_ROWS = 256
GA_KEY_CHUNK = 256
RET_HEADS_PER_STEP = 2
MOD_COLS = 1536
VMEM_LIMIT = 56 * 1024 * 1024
MASKED = -1e30
LOG2E = 1.4426950408889634

_W_WIDTHS = (RET_QK, RET_QK, RET_V, RET_V, SWA_Q, 2 * SWA_KV, 2 * SWA_KV, GA_Q, GA_KV, GA_KV,
             D_MODEL, D_MODEL, D_MODEL)
_W_OFFS = tuple(sum(_W_WIDTHS[:i]) for i in range(len(_W_WIDTHS)))
W_ALL = sum(_W_WIDTHS)


def _dot(a, b):
    return jnp.dot(a, b, preferred_element_type=F32)


def _dot_nt(a, b):
    return lax.dot_general(a, b, (((1,), (1,)), ((), ())), preferred_element_type=F32)


def _dot_tn(a, b):
    return lax.dot_general(a, b, (((0,), (0,)), ((), ())), preferred_element_type=F32)


def _silu(x):
    return x * jax.nn.sigmoid(x)


def _rms_normalize(x):
    return x * lax.rsqrt(jnp.mean(x * x, axis=-1, keepdims=True) + NORM_EPS)


def _params(**kw):
    return pltpu.CompilerParams(vmem_limit_bytes=VMEM_LIMIT, **kw)


def _mod_kernel(c_ref, w_ref, b_ref, o_ref):
    a = _silu(c_ref[...])
    o_ref[0] = _dot(a.astype(BF16), w_ref[0].astype(BF16)) + b_ref[0]


def _modulation(c_rows, w_mod, b_mod):
    depth, d, n = w_mod.shape
    rows = c_rows.shape[0]
    return pl.pallas_call(
        _mod_kernel,
        out_shape=jax.ShapeDtypeStruct((depth, rows, n), F32),
        grid=(depth, n // MOD_COLS),
        in_specs=[pl.BlockSpec((rows, d), lambda l, j: (0, 0)),
                  pl.BlockSpec((1, d, MOD_COLS), lambda l, j: (l, 0, j)),
                  pl.BlockSpec((1, 1, MOD_COLS), lambda l, j: (l, 0, j))],
        out_specs=pl.BlockSpec((1, rows, MOD_COLS), lambda l, j: (l, 0, j)),
        compiler_params=_params(),
        name="modulation",
    )(c_rows, w_mod, b_mod.reshape(depth, 1, n))


def _mod_row(mod_ref, n_lat_tiles, n_batch):
    b = pl.program_id(0)
    i = pl.program_id(1)
    r = jnp.where(i < n_lat_tiles, b, n_batch)
    return mod_ref[pl.ds(r, 1), :]


def _mod_chunk(m, k):
    return m[:, k * D_MODEL:(k + 1) * D_MODEL]


def _prenorm_kernel(x_ref, mod_ref, g_ref, h_ref, *, n_lat_tiles, n_batch):
    m = _mod_row(mod_ref, n_lat_tiles, n_batch)
    h = _rms_normalize(x_ref[0]) * g_ref[...] * (1.0 + _mod_chunk(m, 1)) + _mod_chunk(m, 0)
    h_ref[0] = h.astype(BF16)


def _prenorm(xa, mod, g, n_lat_tiles):
    nb, nt, d = xa.shape
    kern = functools.partial(_prenorm_kernel, n_lat_tiles=n_lat_tiles, n_batch=nb)
    return pl.pallas_call(
        kern,
        out_shape=jax.ShapeDtypeStruct((nb, nt, d), BF16),
        grid=(nb, nt // TOKEN_TILE),
        in_specs=[pl.BlockSpec((1, TOKEN_TILE, d), lambda b, i: (b, i, 0)),
                  pl.BlockSpec(mod.shape, lambda b, i: (0, 0)),
                  pl.BlockSpec((1, d), lambda b, i: (0, 0))],
        out_specs=pl.BlockSpec((1, TOKEN_TILE, d), lambda b, i: (b, i, 0)),
        compiler_params=_params(),
        name="prenorm",
    )(xa, mod, g.reshape(1, d))


def _rope(x, cos, sin, quarter):
    lane = lax.broadcasted_iota(jnp.int32, x.shape, 1)
    first = (lane % (2 * quarter)) < quarter
    partner = jnp.where(first, pltpu.roll(x, LANES - quarter, 1), pltpu.roll(x, quarter, 1))
    return x * cos + partner * sin


def _inproj_kernel(h_ref, w_ref, c128_ref, s128_ref, c64_ref, s64_ref, gq_ref, gk_ref,
                   rq, rk, rv, ru, sq, sk, sv, gq, gk, gv, ar, as_, aa):
    h = h_ref[0]
    c128, s128 = c128_ref[...], s128_ref[...]
    c64, s64 = c64_ref[...], s64_ref[...]

    def proj(idx):
        return _dot(h, w_ref[:, _W_OFFS[idx]:_W_OFFS[idx] + _W_WIDTHS[idx]])

    def slabs(acc):
        return [acc[:, s * LANES:(s + 1) * LANES] for s in range(acc.shape[1] // LANES)]

    def store(ref, s, val):
        ref[0, :, s * LANES:(s + 1) * LANES] = val.astype(ref.dtype)

    for s, xs in enumerate(slabs(proj(0))):
        store(rq, s, _rope(xs, c128, s128, RET_DK // 4))
    for s, xs in enumerate(slabs(proj(1))):
        store(rk, s, _rope(xs, c128, s128, RET_DK // 4) * (RET_DK ** -0.5))
    rv[0] = proj(2).astype(rv.dtype)
    ru[0] = proj(3).astype(ru.dtype)
    for s, xs in enumerate(slabs(proj(4))):
        store(sq, s, _rope(xs, c64, s64, SWA_HD // 4) * (SWA_HD ** -0.5 * LOG2E))
    for s, xs in enumerate(slabs(proj(5))):
        store(sk, s, _rope(xs, c64, s64, SWA_HD // 4))
    sv[0] = proj(6).astype(sv.dtype)
    for s, xs in enumerate(slabs(proj(7))):
        xn = _rms_normalize(xs) * gq_ref[...]
        store(gq, s, _rope(xn, c128, s128, GA_HD // 4) * (GA_HD ** -0.5 * LOG2E))
    for s, xs in enumerate(slabs(proj(8))):
        xn = _rms_normalize(xs) * gk_ref[...]
        store(gk, s, _rope(xn, c128, s128, GA_HD // 4))
    gv[0] = proj(9).astype(gv.dtype)
    ar[0] = proj(10).astype(ar.dtype)
    as_[0] = proj(11).astype(as_.dtype)
    aa[0] = proj(12).astype(aa.dtype)


def _inproj(h, w_all, tables, g_q, g_k):
    nb, nt, d = h.shape
    tile = lambda width: pl.BlockSpec((1, TOKEN_TILE, width), lambda b, i: (b, i, 0))
    tab = pl.BlockSpec((TOKEN_TILE, LANES), lambda b, i: (i, 0))
    vec = pl.BlockSpec((1, LANES), lambda b, i: (0, 0))
    outs = [jax.ShapeDtypeStruct((nb, nt, w), BF16) for w in _W_WIDTHS]
    return pl.pallas_call(
        _inproj_kernel,
        out_shape=outs,
        grid=(nb, nt // TOKEN_TILE),
        in_specs=[tile(d),
                  pl.BlockSpec((d, W_ALL), lambda b, i: (0, 0), pipeline_mode=pl.Buffered(1)),
                  tab, tab, tab, tab, vec, vec],
        out_specs=[tile(w) for w in _W_WIDTHS],
        compiler_params=_params(),
        name="inproj",
    )(h, w_all, *tables, g_q.reshape(1, LANES), g_k.reshape(1, LANES))


def _log_sigmoid(x):
    return jnp.minimum(x, 0.0) - jnp.log1p(jnp.exp(-jnp.abs(x)))


def _ret_kernel(logit_ref, q_ref, k_ref, v_ref, u_ref, o_ref, acc_ref, sf_ref, sb_ref,
                *, length, n_ctx, need_ctx):
    chunk = RET_CHUNK
    n_chunks = length // chunk
    half = n_chunks // 2

    def rows_f32(shape):
        return lax.broadcasted_iota(jnp.int32, shape, 0).astype(F32)

    def finish(o, u):
        return (_rms_normalize(o) * _silu(u.astype(F32))).astype(o_ref.dtype)

    ci = rows_f32((chunk, RET_DK))
    heads = []
    for hi in range(RET_HEADS_PER_STEP):
        hh = pl.program_id(1) * RET_HEADS_PER_STEP + hi
        lgf = _log_sigmoid(jnp.full((1, 1), logit_ref[0, hh], F32))
        lgb = _log_sigmoid(jnp.full((1, 1), logit_ref[1, hh], F32))
        qk = slice(hi * RET_DK, (hi + 1) * RET_DK)
        vu = slice(hi * RET_DV, (hi + 1) * RET_DV)

        def both_ways_decay(n, lgf=lgf, lgb=lgb):
            diff = rows_f32((n, n)) - lax.broadcasted_iota(jnp.int32, (n, n), 1).astype(F32)
            return jnp.where(diff >= 0, jnp.exp(jnp.maximum(diff, 0.0) * lgf),
                             jnp.exp(jnp.maximum(-diff, 0.0) * lgb))

        kx = k_ref[0, length:length + n_ctx, qk].astype(F32)
        vx = v_ref[0, length:length + n_ctx, vu]
        lx = rows_f32((n_ctx, RET_DK))
        sf_ref[hi] = _dot_tn((kx * jnp.exp((n_ctx - 1.0 - lx) * lgf)).astype(BF16), vx)
        sb_ref[hi] = _dot_tn((kx * jnp.exp(lx * lgb)).astype(BF16), vx)
        if need_ctx:
            qx = q_ref[0, length:length + n_ctx, qk]
            sx = _dot_nt(qx, kx.astype(BF16)) * both_ways_decay(n_ctx)
            o_ref[0, length:length + n_ctx, vu] = finish(_dot(sx.astype(BF16), vx),
                                                         u_ref[0, length:length + n_ctx, vu])
        heads.append(dict(
            qk=qk, vu=vu, intra=both_ways_decay(chunk),
            q_dec_f=jnp.exp((ci + 1.0) * lgf), k_dec_f=jnp.exp((chunk - 1.0 - ci) * lgf),
            q_dec_b=jnp.exp((chunk - ci) * lgb), k_dec_b=jnp.exp(ci * lgb),
            chunk_dec_f=jnp.exp(chunk * lgf), chunk_dec_b=jnp.exp(chunk * lgb)))

    def sweep(s, first_touch):
        rf = pl.ds(pl.multiple_of(s * chunk, chunk), chunk)
        rb = pl.ds(pl.multiple_of((n_chunks - 1 - s) * chunk, chunk), chunk)
        for hi, hd in enumerate(heads):
            qk, vu = hd["qk"], hd["vu"]
            qf, kf, vf = q_ref[0, rf, qk], k_ref[0, rf, qk], v_ref[0, rf, vu]
            sc = _dot_nt(qf, kf) * hd["intra"]
            state_f = sf_ref[hi]
            o_f = (_dot(sc.astype(BF16), vf)
                   + _dot((qf.astype(F32) * hd["q_dec_f"]).astype(BF16), state_f.astype(BF16)))
            sf_ref[hi] = state_f * hd["chunk_dec_f"] + _dot_tn((kf.astype(F32) * hd["k_dec_f"]).astype(BF16), vf)
            qb, kb, vb = q_ref[0, rb, qk], k_ref[0, rb, qk], v_ref[0, rb, vu]
            state_b = sb_ref[hi]
            o_b = _dot((qb.astype(F32) * hd["q_dec_b"]).astype(BF16), state_b.astype(BF16))
            sb_ref[hi] = state_b * hd["chunk_dec_b"] + _dot_tn((kb.astype(F32) * hd["k_dec_b"]).astype(BF16), vb)
            if first_touch:
                acc_ref[hi, rf, :] = o_f
                acc_ref[hi, rb, :] = o_b
            else:
                o_ref[0, rf, vu] = finish(acc_ref[hi, rf, :] + o_f, u_ref[0, rf, vu])
                o_ref[0, rb, vu] = finish(acc_ref[hi, rb, :] + o_b, u_ref[0, rb, vu])

    def first_half(s, carry):
        sweep(s, True)
        return carry

    def second_half(s, carry):
        sweep(s, False)
        return carry

    lax.fori_loop(0, half, first_half, 0)
    lax.fori_loop(half, n_chunks, second_half, 0)


def _retention(logit, rq, rk, rv, ru, length, n_ctx, need_ctx):
    nb, nt, _ = rq.shape
    assert (length // RET_CHUNK) % 2 == 0 and RET_HEADS % RET_HEADS_PER_STEP == 0
    out_rows = nt if need_ctx else length
    kern = functools.partial(_ret_kernel, length=length, n_ctx=n_ctx, need_ctx=need_ctx)
    head = lambda width: pl.BlockSpec((1, nt, RET_HEADS_PER_STEP * width), lambda b, h: (b, 0, h))
    return pl.pallas_call(
        kern,
        out_shape=jax.ShapeDtypeStruct((nb, out_rows, RET_V), BF16),
        grid=(nb, RET_HEADS // RET_HEADS_PER_STEP),
        in_specs=[pl.BlockSpec(memory_space=pltpu.SMEM),
                  head(RET_DK), head(RET_DK), head(RET_DV), head(RET_DV)],
        out_specs=pl.BlockSpec((1, out_rows, RET_HEADS_PER_STEP * RET_DV), lambda b, h: (b, 0, h)),
        scratch_shapes=[pltpu.VMEM((RET_HEADS_PER_STEP, length, RET_DV), F32),
                        pltpu.VMEM((RET_HEADS_PER_STEP, RET_DK, RET_DV), F32),
                        pltpu.VMEM((RET_HEADS_PER_STEP, RET_DK, RET_DV), F32)],
        compiler_params=_params(),
        name="retention",
    )(logit, rq, rk, rv, ru)


def _swa_group(q_ref, g, kcat, vcat, biases, sink_ref, o_ref):
    tq = q_ref.shape[1]
    heads_per_group = SWA_HEADS // SWA_KV_HEADS
    slabs_per_group = heads_per_group // 2
    low = lax.broadcasted_iota(jnp.int32, (tq, LANES), 1) < SWA_HD
    zero = jnp.zeros((tq, LANES), q_ref.dtype)
    stacked = []
    for a in range(slabs_per_group):
        s0 = (g * slabs_per_group + a) * LANES
        slab = q_ref[0, :, s0:s0 + LANES]
        stacked.append(jnp.where(low, slab, zero))
        stacked.append(jnp.where(low, zero, slab))
    heads_per_part = heads_per_group // 2
    rows = heads_per_part * tq
    for part in range(2):
        q = jnp.concatenate(stacked[part * heads_per_part:(part + 1) * heads_per_part], axis=0)
        s = _dot_nt(q, kcat)
        tiles = [s[:, t * LANES:(t + 1) * LANES] for t in range(s.shape[1] // LANES)]
        for t, bias in biases.items():
            tiles[t] = tiles[t] + jnp.concatenate([bias] * heads_per_part, axis=0)
        sink = jnp.concatenate(
            [jnp.full((tq, LANES), sink_ref[g * heads_per_group + part * heads_per_part + h] * LOG2E, F32)
             for h in range(heads_per_part)], axis=0)
        lane_max = functools.reduce(jnp.maximum, tiles)
        m = jnp.maximum(jnp.broadcast_to(jnp.max(lane_max, axis=-1, keepdims=True), (rows, LANES)), sink)
        ps = [jnp.exp2(tile - m) for tile in tiles]
        den = (jnp.broadcast_to(jnp.sum(functools.reduce(jnp.add, ps), axis=-1, keepdims=True), (rows, LANES))
               + jnp.exp2(sink - m))
        o = _dot(jnp.concatenate(ps, axis=1).astype(BF16), vcat) / den
        for a in range(heads_per_part // 2):
            s0 = (g * slabs_per_group + part * (heads_per_part // 2) + a) * LANES
            even = o[(2 * a) * tq:(2 * a + 1) * tq]
            odd = o[(2 * a + 1) * tq:(2 * a + 2) * tq]
            o_ref[0, :, s0:s0 + LANES] = jnp.where(low, even, odd).astype(o_ref.dtype)


def _swa_kernel(sink_ref, q_ref, kp_ref, kc_ref, kn_ref, kx_ref, vp_ref, vc_ref, vn_ref, vx_ref, o_ref,
                *, n_lat_tiles, need_ctx):
    j = pl.program_id(1)
    tq = q_ref.shape[1]
    n_ctx = kx_ref.shape[1]

    def lanes_of(ref, g):
        return ref[0, :, g * LANES:(g + 1) * LANES]

    @pl.when(j < n_lat_tiles)
    def _latent():
        ci = lax.broadcasted_iota(jnp.int32, (tq, tq), 1)
        ri = lax.broadcasted_iota(jnp.int32, (tq, tq), 0)
        far = 4 * tq
        ri_prev = ri + jnp.where(j > 0, 0, far)
        ri_next = ri - jnp.where(j < n_lat_tiles - 1, 0, far)
        zero = jnp.zeros((tq, tq), F32)
        masked = jnp.full((tq, tq), MASKED, F32)
        bias = {0: jnp.where(ci >= ri_prev, zero, masked), 2: jnp.where(ci <= ri_next, zero, masked)}
        for g in range(SWA_KV_HEADS):
            kcat = jnp.concatenate([lanes_of(kp_ref, g), lanes_of(kc_ref, g), lanes_of(kn_ref, g),
                                    lanes_of(kx_ref, g)], axis=0)
            vcat = jnp.concatenate([lanes_of(vp_ref, g), lanes_of(vc_ref, g), lanes_of(vn_ref, g),
                                    lanes_of(vx_ref, g)], axis=0)
            _swa_group(q_ref, g, kcat, vcat, bias, sink_ref, o_ref)

    if need_ctx:
        @pl.when(j >= n_lat_tiles)
        def _context():
            for g in range(SWA_KV_HEADS):
                _swa_group(q_ref, g, lanes_of(kx_ref, g), lanes_of(vx_ref, g), {}, sink_ref, o_ref)


def _window_attention(sink, sq, sk, sv, length, n_ctx, need_ctx):
    nb, nt, _ = sq.shape
    tq = QUERY_TILE
    n_lat = length // tq
    out_rows = nt if need_ctx else length
    kvw = 2 * SWA_KV
    kern = functools.partial(_swa_kernel, n_lat_tiles=n_lat, need_ctx=need_ctx)
    prev = pl.BlockSpec((1, tq, kvw), lambda b, j: (b, jnp.clip(j - 1, 0, n_lat - 1), 0))
    cur = pl.BlockSpec((1, tq, kvw), lambda b, j: (b, jnp.minimum(j, n_lat - 1), 0))
    nxt = pl.BlockSpec((1, tq, kvw), lambda b, j: (b, jnp.minimum(j + 1, n_lat - 1), 0))
    ctx = pl.BlockSpec((1, n_ctx, kvw), lambda b, j: (b, length // n_ctx, 0))
    return pl.pallas_call(
        kern,
        out_shape=jax.ShapeDtypeStruct((nb, out_rows, SWA_Q), BF16),
        grid=(nb, out_rows // tq),
        in_specs=[pl.BlockSpec(memory_space=pltpu.SMEM),
                  pl.BlockSpec((1, tq, SWA_Q), lambda b, j: (b, j, 0)),
                  prev, cur, nxt, ctx, prev, cur, nxt, ctx],
        out_specs=pl.BlockSpec((1, tq, SWA_Q), lambda b, j: (b, j, 0)),
        compiler_params=_params(),
        name="window_attention",
    )(sink, sq, sk, sk, sk, sk, sv, sv, sv, sv)


def _ga_kernel(q_ref, k_ref, v_ref, o_ref, *, length, n_ctx, n_lat_tiles, need_ctx):
    j = pl.program_id(2)
    tq = q_ref.shape[1]
    heads_per_group = GA_HEADS // GA_KV_HEADS
    q = jnp.concatenate([q_ref[0, :, h * LANES:(h + 1) * LANES] for h in range(heads_per_group)], axis=0)

    rows = heads_per_group * tq

    def attend(key_lo, key_hi):
        m = l = acc = None
        for c0 in range(key_lo, key_hi, GA_KEY_CHUNK):
            s = _dot_nt(q, k_ref[0, c0:c0 + GA_KEY_CHUNK, :])
            parts = [s[:, t * LANES:(t + 1) * LANES] for t in range(GA_KEY_CHUNK // LANES)]
            lane_max = functools.reduce(jnp.maximum, parts)
            mc = jnp.broadcast_to(jnp.max(lane_max, axis=-1, keepdims=True), (rows, LANES))
            m_new = mc if m is None else jnp.maximum(m, mc)
            ps = [jnp.exp2(part - m_new) for part in parts]
            pv = _dot(jnp.concatenate(ps, axis=1).astype(BF16), v_ref[0, c0:c0 + GA_KEY_CHUNK, :])
            psum = functools.reduce(jnp.add, ps)
            if m is None:
                l, acc = psum, pv
            else:
                alpha = jnp.exp2(m - m_new)
                l, acc = alpha * l + psum, alpha * acc + pv
            m = m_new
        o = acc / jnp.sum(l, axis=-1, keepdims=True)
        for h in range(heads_per_group):
            o_ref[0, :, h * LANES:(h + 1) * LANES] = o[h * tq:(h + 1) * tq].astype(o_ref.dtype)

    @pl.when(j < n_lat_tiles)
    def _latent():
        attend(0, length + n_ctx)

    if need_ctx:
        @pl.when(j >= n_lat_tiles)
        def _context():
            attend(length, length + n_ctx)


def _global_attention(gq, gk, gv, length, n_ctx, need_ctx):
    nb, nt, _ = gq.shape
    tq = QUERY_TILE
    out_rows = nt if need_ctx else length
    group_w = GA_Q // GA_KV_HEADS
    kern = functools.partial(_ga_kernel, length=length, n_ctx=n_ctx, n_lat_tiles=length // tq,
                             need_ctx=need_ctx)
    kv = pl.BlockSpec((1, nt, GA_HD), lambda b, g, j: (b, 0, g))
    return pl.pallas_call(
        kern,
        out_shape=jax.ShapeDtypeStruct((nb, out_rows, GA_Q), BF16),
        grid=(nb, GA_KV_HEADS, out_rows // tq),
        in_specs=[pl.BlockSpec((1, tq, group_w), lambda b, g, j: (b, j, g)), kv, kv],
        out_specs=pl.BlockSpec((1, tq, group_w), lambda b, g, j: (b, j, g)),
        compiler_params=_params(),
        name="global_attention",
    )(gq, gk, gv)


def _route(scores, biased):
    rows = [biased[r:r + 1, :] for r in range(N_EXPERTS)]
    raw = [scores[r:r + 1, :] for r in range(N_EXPERTS)]

    def top2_sum(vals):
        best = None
        for a in range(len(vals)):
            for b in range(a + 1, len(vals)):
                pair = vals[a] + vals[b]
                best = pair if best is None else jnp.maximum(best, pair)
        return best

    group_scores = [top2_sum(rows[g * EXPERTS_PER_GROUP:(g + 1) * EXPERTS_PER_GROUP]) for g in range(N_GROUPS)]
    group = jnp.zeros_like(group_scores[0], dtype=jnp.int32)
    best = group_scores[0]
    for g in range(1, N_GROUPS):
        better = group_scores[g] > best
        group = jnp.where(better, g, group)
        best = jnp.where(better, group_scores[g], best)

    def in_group(table, k):
        val = table[k]
        for g in range(1, N_GROUPS):
            val = jnp.where(group == g, table[g * EXPERTS_PER_GROUP + k], val)
        return val

    vals = [in_group(rows, k) for k in range(EXPERTS_PER_GROUP)]
    unbiased = [in_group(raw, k) for k in range(EXPERTS_PER_GROUP)]

    def first_argmax(cands):
        idx = jnp.zeros_like(group)
        top = cands[0]
        for k in range(1, len(cands)):
            better = cands[k] > top
            idx = jnp.where(better, k, idx)
            top = jnp.where(better, cands[k], top)
        return idx

    i1 = first_argmax(vals)
    i2 = first_argmax([jnp.where(i1 == k, -jnp.inf, vals[k]) for k in range(EXPERTS_PER_GROUP)])

    def pick(idx):
        val = unbiased[0]
        for k in range(1, EXPERTS_PER_GROUP):
            val = jnp.where(idx == k, unbiased[k], val)
        return val

    s1, s2 = pick(i1), pick(i2)
    total = s1 + s2
    experts = jnp.concatenate([group * EXPERTS_PER_GROUP + i1, group * EXPERTS_PER_GROUP + i2], axis=0)
    weights = jnp.concatenate([s1 / total, s2 / total], axis=0)
    return experts, weights


def _slot_ranks(experts, carry):
    tm = experts.shape[1]
    expert_id = lax.broadcasted_iota(jnp.int32, (N_EXPERTS, tm), 0)
    upper = jnp.where(lax.broadcasted_iota(jnp.int32, (tm, tm), 0) <= lax.broadcasted_iota(jnp.int32, (tm, tm), 1),
                      1.0, 0.0).astype(BF16)
    ranks = []
    for k in range(2):
        hit = jnp.where(expert_id == experts[k:k + 1, :], 1.0, 0.0)
        inclusive = _dot(hit.astype(BF16), upper)
        ranks.append(jnp.sum(hit * (carry + inclusive - 1.0), axis=0, keepdims=True))
        carry = carry + jnp.sum(hit, axis=1, keepdims=True)
    return jnp.concatenate(ranks, axis=0).astype(jnp.int32), carry


def _store_row_tiles(ref, val):
    rows = val.shape[0]
    for j in range(val.shape[1] // LANES):
        ref[pl.ds(j, rows, stride=SUBLANES), :] = val[:, j * LANES:(j + 1) * LANES]


def _load_row_tiles(ref, rows):
    return jnp.concatenate([ref[pl.ds(j, rows, stride=SUBLANES), :] for j in range(ref.shape[0] // rows)], axis=1)


def _merge_kernel(oret, oswa, oga, ar, as_, aa, x_ref, mod_ref, wbr_ref, wout_ref, g2_ref, wrt_ref, br_ref,
                  xo_ref, h2t_ref, e_ref, r_ref, w_ref, cnt_ref, h2_prev, carry_ref,
                  *, n_lat_tiles, n_tiles, n_batch):
    s = pl.program_id(0)
    n_steps = n_batch * n_tiles

    def merge_tile():
        b = s // n_tiles
        i = s - b * n_tiles
        m = mod_ref[pl.ds(jnp.where(i < n_lat_tiles, b, n_batch), 1), :]

        def gate(a_ref):
            return jax.nn.sigmoid(a_ref[0].astype(F32))

        y = (gate(ar) * _dot(oret[0], wbr_ref[0]) + gate(as_) * _dot(oswa[0], wbr_ref[1])
             + gate(aa) * _dot(oga[0], wbr_ref[2]))
        x = x_ref[0] + _mod_chunk(m, 2) * _dot(y.astype(BF16), wout_ref[...])
        xo_ref[0] = x
        h2 = _rms_normalize(x) * g2_ref[...] * (1.0 + _mod_chunk(m, 4)) + _mod_chunk(m, 3)
        _store_row_tiles(h2t_ref.at[0], h2)
        h2_prev[...] = h2

    def route_previous_tile():
        h2 = h2_prev[...]
        h_hi = h2.astype(BF16)
        h_lo = (h2 - h_hi.astype(F32)).astype(BF16)
        w = wrt_ref[...]
        w_hi = w.astype(BF16)
        w_lo = (w - w_hi.astype(F32)).astype(BF16)
        logits = _dot_nt(w_hi, h_hi) + _dot_nt(w_hi, h_lo) + _dot_nt(w_lo, h_hi)
        scores = jax.nn.sigmoid(logits)
        experts, weights = _route(scores, scores + br_ref[...])
        ranks, carry = _slot_ranks(experts, carry_ref[:, 0:1])
        e_ref[0] = experts
        r_ref[0] = ranks
        w_ref[0] = weights
        carry_ref[...] = jnp.broadcast_to(carry, carry_ref.shape)
        cnt_ref[...] = jnp.broadcast_to(carry, cnt_ref.shape)

    @pl.when(s == 0)
    def _init():
        carry_ref[...] = jnp.zeros_like(carry_ref)

    @pl.when(s > 0)
    def _routing_step():
        route_previous_tile()

    @pl.when(s < n_steps)
    def _merge_step():
        merge_tile()


def _merge(oret, oswa, oga, ar, as_, aa, xa, mod, wbr, wout, g2, wrt, br, n_lat_tiles, n_tiles):
    nb, nt, d = xa.shape
    rows = n_tiles * TOKEN_TILE
    n_steps = nb * n_tiles
    kern = functools.partial(_merge_kernel, n_lat_tiles=n_lat_tiles, n_tiles=n_tiles, n_batch=nb)

    def cur(s):
        t = jnp.minimum(s, n_steps - 1)
        return t // n_tiles, t % n_tiles

    def prev(s):
        t = jnp.maximum(s - 1, 0)
        return t // n_tiles, t % n_tiles

    tile = pl.BlockSpec((1, TOKEN_TILE, d), lambda s: (*cur(s), 0))
    whole = lambda arr: pl.BlockSpec(arr.shape, lambda s: (0,) * arr.ndim)
    small = pl.BlockSpec((1, 2, TOKEN_TILE), lambda s: (prev(s)[0], 0, prev(s)[1]))
    return pl.pallas_call(
        kern,
        out_shape=[jax.ShapeDtypeStruct((nb, nt, d), F32),
                   jax.ShapeDtypeStruct((nb, rows * SUBLANES, LANES), F32),
                   jax.ShapeDtypeStruct((nb, 2, rows), jnp.int32),
                   jax.ShapeDtypeStruct((nb, 2, rows), jnp.int32),
                   jax.ShapeDtypeStruct((nb, 2, rows), F32),
                   jax.ShapeDtypeStruct((N_EXPERTS, LANES), F32)],
        grid=(n_steps + 1,),
        in_specs=[tile, tile, tile, tile, tile, tile, tile, whole(mod), whole(wbr), whole(wout),
                  pl.BlockSpec((1, d), lambda s: (0, 0)), whole(wrt), whole(br)],
        out_specs=[tile,
                   pl.BlockSpec((1, TOKEN_TILE * SUBLANES, LANES), lambda s: (*cur(s), 0)),
                   small, small, small,
                   pl.BlockSpec((N_EXPERTS, LANES), lambda s: (0, 0))],
        scratch_shapes=[pltpu.VMEM((TOKEN_TILE, d), F32), pltpu.VMEM((N_EXPERTS, LANES), F32)],
        input_output_aliases={6: 0},
        compiler_params=_params(dimension_semantics=("arbitrary",)),
        name="merge_route",
    )(oret, oswa, oga, ar, as_, aa, xa, mod, wbr, wout, g2.reshape(1, d), wrt, br)


def _row_copy(src, dst, sem):
    return pltpu.make_async_copy(src, dst, sem)


def _tile_rows(r):
    return pl.ds(pl.multiple_of(r * SUBLANES, SUBLANES), SUBLANES)


def _dispatch_kernel(tail_ref, nv_ref, dest_ref, h2t_ref, buf_out, zeros_ref, sem, tail_sem):
    @pl.when((pl.program_id(0) == 0) & (pl.program_id(1) == 0))
    def _zero_tails():
        zeros_ref[...] = jnp.zeros_like(zeros_ref)

        def zero_block(first_slot):
            rows = pl.ds(pl.multiple_of(first_slot * SUBLANES, SUBLANES), MOE_ROWS * SUBLANES)
            copy = pltpu.make_async_copy(zeros_ref, buf_out.at[rows, :], tail_sem)
            copy.start()
            copy.wait()

        for e in range(N_EXPERTS):
            @pl.when(tail_ref[e] >= 0)
            def _():
                zero_block(tail_ref[e])

        def unused(blk, carry):
            zero_block(blk * MOE_ROWS)
            return carry

        lax.fori_loop(nv_ref[0], buf_out.shape[0] // (MOE_ROWS * SUBLANES), unused, 0)

    def issue(r, carry):
        for k in range(2):
            d = dest_ref[0, 0, k * TOKEN_TILE + r]
            _row_copy(h2t_ref.at[0, _tile_rows(r), :], buf_out.at[_tile_rows(d), :], sem).start(priority=k)
        return carry

    lax.fori_loop(0, TOKEN_TILE, issue, 0, unroll=8)

    def drain(r, carry):
        for k in range(2):
            _row_copy(h2t_ref.at[0, _tile_rows(0), :], buf_out.at[_tile_rows(0), :], sem).wait()
        return carry

    lax.fori_loop(0, TOKEN_TILE, drain, 0, unroll=8)


def _dispatch(tail_start, n_valid, dest_tiles, h2t, n_slots):
    nb, rows8, _ = h2t.shape
    n_tiles = rows8 // (TOKEN_TILE * SUBLANES)
    grid_spec = pltpu.PrefetchScalarGridSpec(
        num_scalar_prefetch=2,
        grid=(nb, n_tiles),
        in_specs=[pl.BlockSpec((1, 1, 2 * TOKEN_TILE), lambda b, i, tail, nv: (b * n_tiles + i, 0, 0),
                               memory_space=pltpu.SMEM),
                  pl.BlockSpec((1, TOKEN_TILE * SUBLANES, LANES), lambda b, i, tail, nv: (b, i, 0))],
        out_specs=pl.BlockSpec(memory_space=pl.ANY),
        scratch_shapes=[pltpu.VMEM((MOE_ROWS * SUBLANES, LANES), F32),
                        pltpu.SemaphoreType.DMA(()), pltpu.SemaphoreType.DMA(())],
    )
    return pl.pallas_call(
        _dispatch_kernel,
        out_shape=jax.ShapeDtypeStruct((n_slots * SUBLANES, LANES), F32),
        grid_spec=grid_spec,
        compiler_params=_params(has_side_effects=True, dimension_semantics=("arbitrary", "arbitrary")),
        name="moe_dispatch",
    )(tail_start, n_valid, dest_tiles, h2t)


def _expert_kernel(be_ref, nv_ref, x_ref, wg_ref, wu_ref, wd_ref, y_ref):
    del be_ref

    @pl.when(pl.program_id(0) < nv_ref[0])
    def _():
        x = _load_row_tiles(x_ref, MOE_ROWS).astype(BF16)
        hid = _silu(_dot(x, wg_ref[0])) * _dot(x, wu_ref[0])
        _store_row_tiles(y_ref, _dot(hid.astype(BF16), wd_ref[0]))

    @pl.when(pl.program_id(0) >= nv_ref[0])
    def _():
        y_ref[...] = jnp.zeros_like(y_ref)


def _experts(block_e, n_valid, buf, wg, wu, wd):
    n_slots = buf.shape[0] // SUBLANES
    d = wg.shape[1]
    used = lambda i, nv: jnp.minimum(i, nv[0] - 1)
    rows = pl.BlockSpec((MOE_ROWS * SUBLANES, LANES), lambda i, be, nv: (used(i, nv), 0))
    grid_spec = pltpu.PrefetchScalarGridSpec(
        num_scalar_prefetch=2,
        grid=(n_slots // MOE_ROWS,),
        in_specs=[rows,
                  pl.BlockSpec((1, d, D_EXPERT), lambda i, be, nv: (be[used(i, nv)], 0, 0)),
                  pl.BlockSpec((1, d, D_EXPERT), lambda i, be, nv: (be[used(i, nv)], 0, 0)),
                  pl.BlockSpec((1, D_EXPERT, d), lambda i, be, nv: (be[used(i, nv)], 0, 0))],
        out_specs=pl.BlockSpec((MOE_ROWS * SUBLANES, LANES), lambda i, be, nv: (i, 0)),
    )
    return pl.pallas_call(
        _expert_kernel,
        out_shape=jax.ShapeDtypeStruct((n_slots * SUBLANES, LANES), F32),
        grid_spec=grid_spec,
        compiler_params=_params(dimension_semantics=("arbitrary",)),
        name="moe_experts",
    )(block_e, n_valid, buf, wg, wu, wd)


def _combine_kernel(dest_ref, y_hbm, wt_ref, x_ref, mod_ref, g_ref, *rest,
                    n_lat_tiles, n_batch, last):
    if last:
        out_ref, gbuf, sem = rest
    else:
        xo_ref, h_ref, gbuf, sem = rest

    def issue(r, carry):
        for k in range(2):
            d = dest_ref[0, 0, k * TOKEN_TILE + r]
            _row_copy(y_hbm.at[_tile_rows(d), :], gbuf.at[k, _tile_rows(r), :], sem).start(priority=k)
        return carry

    lax.fori_loop(0, TOKEN_TILE, issue, 0, unroll=8)

    def drain(r, carry):
        for k in range(2):
            _row_copy(y_hbm.at[_tile_rows(0), :], gbuf.at[0, _tile_rows(0), :], sem).wait()
        return carry

    lax.fori_loop(0, TOKEN_TILE, drain, 0, unroll=8)

    m = _mod_row(mod_ref, n_lat_tiles, n_batch)
    wt = wt_ref[0]
    moe = (_load_row_tiles(gbuf.at[0], TOKEN_TILE) * wt[:, 0:1]
           + _load_row_tiles(gbuf.at[1], TOKEN_TILE) * wt[:, 1:2])
    x = x_ref[0] + _mod_chunk(m, 5) * moe
    if last:
        out_ref[0] = _rms_normalize(x) * g_ref[...]
    else:
        xo_ref[0] = x
        h = _rms_normalize(x) * g_ref[...] * (1.0 + _mod_chunk(m, 7)) + _mod_chunk(m, 6)
        h_ref[0] = h.astype(BF16)


def _combine(dest_tiles, y, wt, xa, mod, g, n_lat_tiles, n_tiles, last):
    nb, nt, d = xa.shape
    rows = n_tiles * TOKEN_TILE
    kern = functools.partial(_combine_kernel, n_lat_tiles=n_lat_tiles, n_batch=nb, last=last)
    tile = pl.BlockSpec((1, TOKEN_TILE, d), lambda b, i: (b, i, 0))
    if last:
        out_shape = jax.ShapeDtypeStruct((nb, rows, d), F32)
        out_specs = tile
        aliases = {}
    else:
        out_shape = [jax.ShapeDtypeStruct((nb, nt, d), F32), jax.ShapeDtypeStruct((nb, nt, d), BF16)]
        out_specs = [tile, tile]
        aliases = {3: 0}
    return pl.pallas_call(
        kern,
        out_shape=out_shape,
        grid=(nb, n_tiles),
        in_specs=[pl.BlockSpec((1, 1, 2 * TOKEN_TILE), lambda b, i: (b * n_tiles + i, 0, 0),
                               memory_space=pltpu.SMEM),
                  pl.BlockSpec(memory_space=pl.ANY),
                  pl.BlockSpec((1, TOKEN_TILE, 2), lambda b, i: (b, i, 0)),
                  tile,
                  pl.BlockSpec(mod.shape, lambda b, i: (0, 0)),
                  pl.BlockSpec((1, d), lambda b, i: (0, 0))],
        out_specs=out_specs,
        scratch_shapes=[pltpu.VMEM((2, TOKEN_TILE * SUBLANES, LANES), F32), pltpu.SemaphoreType.DMA(())],
        input_output_aliases=aliases,
        compiler_params=_params(),
        name="moe_combine",
    )(dest_tiles, y, wt, xa, mod, g.reshape(1, d))


def _slot_plan(experts, ranks, counts):
    nb, _, rows = experts.shape
    padded = (counts + MOE_ROWS - 1) // MOE_ROWS * MOE_ROWS
    pad_ends = jnp.cumsum(padded)
    pad_starts = pad_ends - padded
    one_hot = experts[..., None] == jnp.arange(N_EXPERTS, dtype=jnp.int32)
    dest = jnp.sum(jnp.where(one_hot, pad_starts, 0), axis=-1) + ranks
    n_assign = nb * 2 * rows
    n_blocks = (n_assign + N_EXPERTS * (MOE_ROWS - 1) + MOE_ROWS - 1) // MOE_ROWS
    block_e = jnp.minimum(
        jnp.searchsorted(pad_ends, jnp.arange(n_blocks, dtype=jnp.int32) * MOE_ROWS, side='right'),
        N_EXPERTS - 1).astype(jnp.int32)
    n_valid = (pad_ends[-1:] // MOE_ROWS).astype(jnp.int32)
    tail_start = jnp.where(padded > 0, pad_ends - MOE_ROWS, -1).astype(jnp.int32)
    n_tiles = rows // TOKEN_TILE
    dest_tiles = (dest.astype(jnp.int32).reshape(nb, 2, n_tiles, TOKEN_TILE)
                  .transpose(0, 2, 1, 3).reshape(nb * n_tiles, 1, 2 * TOKEN_TILE))
    return dest_tiles, block_e, n_valid, tail_start, n_blocks * MOE_ROWS


def _rope_tables(length, n_ctx):
    rows = length // GRID_W
    row = jnp.repeat(jnp.arange(rows, dtype=jnp.int32), GRID_W).astype(F32)
    col = jnp.tile(jnp.arange(GRID_W, dtype=jnp.int32), rows).astype(F32)

    def table(hd):
        quarter = hd // 4
        freqs = ROPE_THETA ** (-jnp.arange(quarter, dtype=F32) / quarter)
        ang_r = row[:, None] * freqs[None, :]
        ang_c = col[:, None] * freqs[None, :]
        cos = jnp.concatenate([jnp.cos(ang_r), jnp.cos(ang_r), jnp.cos(ang_c), jnp.cos(ang_c)], axis=-1)
        sin = jnp.concatenate([-jnp.sin(ang_r), jnp.sin(ang_r), -jnp.sin(ang_c), jnp.sin(ang_c)], axis=-1)
        cos = jnp.concatenate([cos, jnp.ones((n_ctx, hd), F32)], axis=0)
        sin = jnp.concatenate([sin, jnp.zeros((n_ctx, hd), F32)], axis=0)
        reps = LANES // hd
        return jnp.tile(cos, (1, reps)), jnp.tile(sin, (1, reps))

    c128, s128 = table(RET_DK)
    c64, s64 = table(SWA_HD)
    return c128, s128, c64, s64


def _relayout_w_in(w):
    parts = []
    start = 0
    for width in SPLITS:
        parts.append(w[:, start:start + width])
        start += width

    def duplicate_heads(p):
        return jnp.concatenate([p[:, h * SWA_HD:(h + 1) * SWA_HD] for h in range(SWA_KV_HEADS) for _ in range(2)],
                               axis=1)

    parts[5] = duplicate_heads(parts[5])
    parts[6] = duplicate_heads(parts[6])
    return jnp.concatenate(parts, axis=1).astype(BF16)


def kernel(x, c, ctx, c_ctx, w_mod, b_mod, g_norm1, g_norm2, w_in, ret_decay_logit, swa_sink, g_qnorm, g_knorm,
           w_br_ret, w_br_swa, w_br_ga, w_out, w_router, b_router, w_gate, w_up, w_down, g_final):
    nb, length, d = x.shape
    n_ctx = ctx.shape[1]
    depth = w_mod.shape[0]
    nt = length + n_ctx
    n_lat_tiles = length // TOKEN_TILE
    n_all_tiles = nt // TOKEN_TILE
    assert GA_HD == RET_DK == LANES and 2 * SWA_HD == LANES
    assert length % TOKEN_TILE == 0 and n_ctx == TOKEN_TILE and length % n_ctx == 0

    mod_rows = 8
    c_rows = jnp.concatenate([c, c_ctx[None, :], jnp.zeros((mod_rows - nb - 1, d), F32)], axis=0)
    mods = _modulation(c_rows, w_mod, b_mod)
    tables = _rope_tables(length, n_ctx)
    wrt = w_router.astype(F32).T
    br = b_router.astype(F32).reshape(N_EXPERTS, 1)

    xa = jnp.concatenate([x, ctx], axis=1)
    h = _prenorm(xa, mods[0], g_norm1[0], n_lat_tiles)
    out = None
    for l in range(depth):
        need_ctx = l < depth - 1
        n_tiles = n_all_tiles if need_ctx else n_lat_tiles
        (rq, rk, rv, ru, sq, sk, sv, gq, gk, gv, ar, as_, aa) = _inproj(
            h, _relayout_w_in(w_in[l]), tables, g_qnorm[l], g_knorm[l])
        o_ret = _retention(ret_decay_logit[l].astype(F32), rq, rk, rv, ru, length, n_ctx, need_ctx)
        o_swa = _window_attention(swa_sink[l].astype(F32), sq, sk, sv, length, n_ctx, need_ctx)
        o_ga = _global_attention(gq, gk, gv, length, n_ctx, need_ctx)
        wbr = jnp.stack([w_br_ret[l], w_br_swa[l], w_br_ga[l]]).astype(BF16)
        xa, h2t, experts, ranks, weights, counts = _merge(
            o_ret, o_swa, o_ga, ar, as_, aa, xa, mods[l], wbr, w_out[l].astype(BF16), g_norm2[l], wrt, br,
            n_lat_tiles, n_tiles)
        dest_tiles, block_e, n_valid, tail_start, n_slots = _slot_plan(
            experts, ranks, counts[:, 0].astype(jnp.int32))
        buf = _dispatch(tail_start, n_valid, dest_tiles, h2t, n_slots)
        y = _experts(block_e, n_valid, buf, w_gate[l].astype(BF16), w_up[l].astype(BF16),
                     w_down[l].astype(BF16))
        wt = weights.transpose(0, 2, 1)
        if need_ctx:
            mod_pair = jnp.concatenate([mods[l], mods[l + 1][:, :2 * d]], axis=1)
            xa, h = _combine(dest_tiles, y, wt, xa, mod_pair, g_norm1[l + 1], n_lat_tiles, n_tiles, last=False)
        else:
            out = _combine(dest_tiles, y, wt, xa, mods[l], g_final, n_lat_tiles, n_tiles, last=True)
    return out
```

```python
import functools

import jax
import jax.numpy as jnp
from jax import lax
from jax.experimental import pallas as pl
from jax.experimental.pallas import tpu as pltpu

F32 = jnp.float32
BF16 = jnp.bfloat16

D_MODEL = 1024
GRID_W = 64
NORM_EPS = 1e-6
ROPE_THETA = 10000.0
RET_HEADS, RET_DK, RET_DV, RET_CHUNK = 4, 128, 256, 128
SWA_HEADS, SWA_KV_HEADS, SWA_HD, WINDOW = 16, 2, 64, 128
GA_HEADS, GA_KV_HEADS, GA_HD = 8, 2, 128
N_EXPERTS, N_GROUPS, EXPERTS_PER_GROUP, D_EXPERT = 16, 4, 4, 512

RET_QK = RET_HEADS * RET_DK
RET_V = RET_HEADS * RET_DV
SWA_Q = SWA_HEADS * SWA_HD
SWA_KV = SWA_KV_HEADS * SWA_HD
GA_Q = GA_HEADS * GA_HD
GA_KV = GA_KV_HEADS * GA_HD
SPLITS = (RET_QK, RET_QK, RET_V, RET_V, SWA_Q, SWA_KV, SWA_KV, GA_Q, GA_KV, GA_KV,
          D_MODEL, D_MODEL, D_MODEL)

LANES = 128
SUBLANES = 8
TOKEN_TILE = 256
QUERY_TILE = 128
MOE_ROWS = 256
GA_KEY_CHUNK = 256
RET_HEADS_PER_STEP = 2
SWA_TILES_PER_CHUNK = 2
MOD_COLS = 1536
VMEM_LIMIT = 56 * 1024 * 1024
MASKED = -1e30
LOG2E = 1.4426950408889634

_W_WIDTHS = (RET_QK, RET_QK, RET_V, RET_V, SWA_Q, 2 * SWA_KV, 2 * SWA_KV, GA_Q, GA_KV, GA_KV,
             D_MODEL, D_MODEL, D_MODEL)
_IN_OFFS = tuple(sum(SPLITS[:i]) for i in range(len(SPLITS)))


def _dot(a, b):
    return jnp.dot(a, b, preferred_element_type=F32)


def _dot_nt(a, b):
    return lax.dot_general(a, b, (((1,), (1,)), ((), ())), preferred_element_type=F32)


def _dot_tn(a, b):
    return lax.dot_general(a, b, (((0,), (0,)), ((), ())), preferred_element_type=F32)


def _silu(x):
    return x * jax.nn.sigmoid(x)


def _rms_normalize(x):
    return x * lax.rsqrt(jnp.mean(x * x, axis=-1, keepdims=True) + NORM_EPS)


def _params(**kw):
    return pltpu.CompilerParams(vmem_limit_bytes=VMEM_LIMIT, **kw)


def _mod_kernel(c_ref, w_ref, b_ref, o_ref):
    a = _silu(c_ref[...])
    o_ref[0] = _dot(a.astype(BF16), w_ref[0].astype(BF16)) + b_ref[0]


def _modulation(c_rows, w_mod, b_mod):
    depth, d, n = w_mod.shape
    rows = c_rows.shape[0]
    return pl.pallas_call(
        _mod_kernel,
        out_shape=jax.ShapeDtypeStruct((depth, rows, n), F32),
        grid=(depth, n // MOD_COLS),
        in_specs=[pl.BlockSpec((rows, d), lambda l, j: (0, 0)),
                  pl.BlockSpec((1, d, MOD_COLS), lambda l, j: (l, 0, j)),
                  pl.BlockSpec((1, 1, MOD_COLS), lambda l, j: (l, 0, j))],
        out_specs=pl.BlockSpec((1, rows, MOD_COLS), lambda l, j: (l, 0, j)),
        compiler_params=_params(),
        name="modulation",
    )(c_rows, w_mod, b_mod.reshape(depth, 1, n))


def _mod_row(mod_ref, n_lat_tiles, n_batch):
    b = pl.program_id(0)
    i = pl.program_id(1)
    r = jnp.where(i < n_lat_tiles, b, n_batch)
    return mod_ref[pl.ds(r, 1), :]


def _mod_chunk(m, k):
    return m[:, k * D_MODEL:(k + 1) * D_MODEL]


def _prenorm_kernel(x_ref, c_ref, mod_ref, g_ref, xa_ref, h_ref, *, n_lat_tiles, n_batch):
    m = _mod_row(mod_ref, n_lat_tiles, n_batch)
    x = jnp.where(pl.program_id(1) < n_lat_tiles, x_ref[0], c_ref[0])
    xa_ref[0] = x
    h = _rms_normalize(x) * g_ref[...] * (1.0 + _mod_chunk(m, 1)) + _mod_chunk(m, 0)
    h_ref[0] = h.astype(BF16)


def _prenorm(x, ctx, mod, g):
    nb, length, d = x.shape
    n_lat_tiles = length // TOKEN_TILE
    nt = length + ctx.shape[1]
    kern = functools.partial(_prenorm_kernel, n_lat_tiles=n_lat_tiles, n_batch=nb)
    tile = pl.BlockSpec((1, TOKEN_TILE, d), lambda b, i: (b, i, 0))
    return pl.pallas_call(
        kern,
        out_shape=[jax.ShapeDtypeStruct((nb, nt, d), F32), jax.ShapeDtypeStruct((nb, nt, d), BF16)],
        grid=(nb, nt // TOKEN_TILE),
        in_specs=[pl.BlockSpec((1, TOKEN_TILE, d), lambda b, i: (b, jnp.minimum(i, n_lat_tiles - 1), 0)),
                  pl.BlockSpec((1, TOKEN_TILE, d), lambda b, i: (b, 0, 0)),
                  pl.BlockSpec(mod.shape, lambda b, i: (0, 0)),
                  pl.BlockSpec((1, d), lambda b, i: (0, 0))],
        out_specs=[tile, tile],
        compiler_params=_params(),
        name="prenorm",
    )(x, ctx, mod, g.reshape(1, d))


def _rope(x, cos, sin, quarter):
    lane = lax.broadcasted_iota(jnp.int32, x.shape, 1)
    first = (lane % (2 * quarter)) < quarter
    partner = jnp.where(first, pltpu.roll(x, LANES - quarter, 1), pltpu.roll(x, quarter, 1))
    return x * cos + partner * sin


def _inproj_kernel(h_ref, w_ref, c128_ref, s128_ref, c64_ref, s64_ref, gq_ref, gk_ref,
                   rq, rk, rv, ru, sq, sk, sv, gq, gk, gv, ar, as_, aa):
    h = h_ref[0]
    c128, s128 = c128_ref[...], s128_ref[...]
    c64, s64 = c64_ref[...], s64_ref[...]

    def proj(idx):
        return _dot(h, w_ref[0, :, _IN_OFFS[idx]:_IN_OFFS[idx] + SPLITS[idx]])

    def slabs(acc):
        return [acc[:, s * LANES:(s + 1) * LANES] for s in range(acc.shape[1] // LANES)]

    def store(ref, s, val):
        ref[0, :, s * LANES:(s + 1) * LANES] = val.astype(ref.dtype)

    def head_per_slab(acc):
        low = lax.broadcasted_iota(jnp.int32, acc.shape, 1) < SWA_HD
        other = pltpu.roll(acc, SWA_HD, 1)
        return [jnp.where(low, acc, other), jnp.where(low, other, acc)]

    for s, xs in enumerate(slabs(proj(0))):
        store(rq, s, _rope(xs, c128, s128, RET_DK // 4))
    for s, xs in enumerate(slabs(proj(1))):
        store(rk, s, _rope(xs, c128, s128, RET_DK // 4) * (RET_DK ** -0.5))
    rv[0] = proj(2).astype(rv.dtype)
    ru[0] = proj(3).astype(ru.dtype)
    for s, xs in enumerate(slabs(proj(4))):
        store(sq, s, _rope(xs, c64, s64, SWA_HD // 4) * (SWA_HD ** -0.5 * LOG2E))
    for s, xs in enumerate(head_per_slab(proj(5))):
        store(sk, s, _rope(xs, c64, s64, SWA_HD // 4))
    for s, xs in enumerate(head_per_slab(proj(6))):
        store(sv, s, xs)
    for s, xs in enumerate(slabs(proj(7))):
        xn = _rms_normalize(xs) * gq_ref[...]
        store(gq, s, _rope(xn, c128, s128, GA_HD // 4) * (GA_HD ** -0.5 * LOG2E))
    for s, xs in enumerate(slabs(proj(8))):
        xn = _rms_normalize(xs) * gk_ref[...]
        store(gk, s, _rope(xn, c128, s128, GA_HD // 4))
    gv[0] = proj(9).astype(gv.dtype)
    ar[0] = proj(10).astype(ar.dtype)
    as_[0] = proj(11).astype(as_.dtype)
    aa[0] = proj(12).astype(aa.dtype)


def _inproj(h, w_in, layer, tables, g_q, g_k):
    nb, nt, d = h.shape
    tile = lambda width: pl.BlockSpec((1, TOKEN_TILE, width), lambda b, i: (b, i, 0))
    tab = pl.BlockSpec((TOKEN_TILE, LANES), lambda b, i: (i, 0))
    vec = pl.BlockSpec((1, LANES), lambda b, i: (0, 0))
    outs = [jax.ShapeDtypeStruct((nb, nt, w), BF16) for w in _W_WIDTHS]
    return pl.pallas_call(
        _inproj_kernel,
        out_shape=outs,
        grid=(nb, nt // TOKEN_TILE),
        in_specs=[tile(d),
                  pl.BlockSpec((1, d, w_in.shape[2]), lambda b, i: (layer, 0, 0), pipeline_mode=pl.Buffered(1)),
                  tab, tab, tab, tab, vec, vec],
        out_specs=[tile(w) for w in _W_WIDTHS],
        compiler_params=_params(),
        name="inproj",
    )(h, w_in, *tables, g_q.reshape(1, LANES), g_k.reshape(1, LANES))


def _log_sigmoid(x):
    return jnp.minimum(x, 0.0) - jnp.log1p(jnp.exp(-jnp.abs(x)))


def _ret_kernel(logit_ref, q_ref, k_ref, v_ref, u_ref, o_ref, acc_ref, sf_ref, sb_ref,
                *, length, n_ctx, need_ctx):
    chunk = RET_CHUNK
    n_chunks = length // chunk
    half = n_chunks // 2

    def rows_f32(shape):
        return lax.broadcasted_iota(jnp.int32, shape, 0).astype(F32)

    def finish(o, u):
        return (_rms_normalize(o) * _silu(u.astype(F32))).astype(o_ref.dtype)

    ci = rows_f32((chunk, RET_DK))
    heads = []
    for hi in range(RET_HEADS_PER_STEP):
        hh = pl.program_id(1) * RET_HEADS_PER_STEP + hi
        lgf = _log_sigmoid(jnp.full((1, 1), logit_ref[0, hh], F32))
        lgb = _log_sigmoid(jnp.full((1, 1), logit_ref[1, hh], F32))
        qk = slice(hi * RET_DK, (hi + 1) * RET_DK)
        vu = slice(hi * RET_DV, (hi + 1) * RET_DV)

        def both_ways_decay(n, lgf=lgf, lgb=lgb):
            diff = rows_f32((n, n)) - lax.broadcasted_iota(jnp.int32, (n, n), 1).astype(F32)
            return jnp.where(diff >= 0, jnp.exp(jnp.maximum(diff, 0.0) * lgf),
                             jnp.exp(jnp.maximum(-diff, 0.0) * lgb))

        kx = k_ref[0, length:length + n_ctx, qk].astype(F32)
        vx = v_ref[0, length:length + n_ctx, vu]
        lx = rows_f32((n_ctx, RET_DK))
        sf_ref[hi] = _dot_tn((kx * jnp.exp((n_ctx - 1.0 - lx) * lgf)).astype(BF16), vx)
        sb_ref[hi] = _dot_tn((kx * jnp.exp(lx * lgb)).astype(BF16), vx)
        if need_ctx:
            qx = q_ref[0, length:length + n_ctx, qk]
            sx = _dot_nt(qx, kx.astype(BF16)) * both_ways_decay(n_ctx)
            o_ref[0, length:length + n_ctx, vu] = finish(_dot(sx.astype(BF16), vx),
                                                         u_ref[0, length:length + n_ctx, vu])
        heads.append(dict(
            qk=qk, vu=vu, intra=both_ways_decay(chunk),
            q_dec_f=jnp.exp((ci + 1.0) * lgf), k_dec_f=jnp.exp((chunk - 1.0 - ci) * lgf),
            q_dec_b=jnp.exp((chunk - ci) * lgb), k_dec_b=jnp.exp(ci * lgb),
            chunk_dec_f=jnp.exp(chunk * lgf), chunk_dec_b=jnp.exp(chunk * lgb)))

    def sweep(s, first_touch):
        rf = pl.ds(pl.multiple_of(s * chunk, chunk), chunk)
        rb = pl.ds(pl.multiple_of((n_chunks - 1 - s) * chunk, chunk), chunk)
        for hi, hd in enumerate(heads):
            qk, vu = hd["qk"], hd["vu"]
            qf, kf, vf = q_ref[0, rf, qk], k_ref[0, rf, qk], v_ref[0, rf, vu]
            sc = _dot_nt(qf, kf) * hd["intra"]
            state_f = sf_ref[hi]
            o_f = (_dot(sc.astype(BF16), vf)
                   + _dot((qf.astype(F32) * hd["q_dec_f"]).astype(BF16), state_f.astype(BF16)))
            sf_ref[hi] = state_f * hd["chunk_dec_f"] + _dot_tn((kf.astype(F32) * hd["k_dec_f"]).astype(BF16), vf)
            qb, kb, vb = q_ref[0, rb, qk], k_ref[0, rb, qk], v_ref[0, rb, vu]
            state_b = sb_ref[hi]
            o_b = _dot((qb.astype(F32) * hd["q_dec_b"]).astype(BF16), state_b.astype(BF16))
            sb_ref[hi] = state_b * hd["chunk_dec_b"] + _dot_tn((kb.astype(F32) * hd["k_dec_b"]).astype(BF16), vb)
            if first_touch:
                acc_ref[hi, rf, :] = o_f
                acc_ref[hi, rb, :] = o_b
            else:
                o_ref[0, rf, vu] = finish(acc_ref[hi, rf, :] + o_f, u_ref[0, rf, vu])
                o_ref[0, rb, vu] = finish(acc_ref[hi, rb, :] + o_b, u_ref[0, rb, vu])

    def first_half(s, carry):
        sweep(s, True)
        return carry

    def second_half(s, carry):
        sweep(s, False)
        return carry

    lax.fori_loop(0, half, first_half, 0)
    lax.fori_loop(half, n_chunks, second_half, 0)


def _retention(logit, rq, rk, rv, ru, length, n_ctx, need_ctx):
    nb, nt, _ = rq.shape
    assert (length // RET_CHUNK) % 2 == 0 and RET_HEADS % RET_HEADS_PER_STEP == 0
    out_rows = nt if need_ctx else length
    kern = functools.partial(_ret_kernel, length=length, n_ctx=n_ctx, need_ctx=need_ctx)
    head = lambda width: pl.BlockSpec((1, nt, RET_HEADS_PER_STEP * width), lambda b, h: (b, 0, h))
    return pl.pallas_call(
        kern,
        out_shape=jax.ShapeDtypeStruct((nb, out_rows, RET_V), BF16),
        grid=(nb, RET_HEADS // RET_HEADS_PER_STEP),
        in_specs=[pl.BlockSpec(memory_space=pltpu.SMEM),
                  head(RET_DK), head(RET_DK), head(RET_DV), head(RET_DV)],
        out_specs=pl.BlockSpec((1, out_rows, RET_HEADS_PER_STEP * RET_DV), lambda b, h: (b, 0, h)),
        scratch_shapes=[pltpu.VMEM((RET_HEADS_PER_STEP, length, RET_DV), F32),
                        pltpu.VMEM((RET_HEADS_PER_STEP, RET_DK, RET_DV), F32),
                        pltpu.VMEM((RET_HEADS_PER_STEP, RET_DK, RET_DV), F32)],
        compiler_params=_params(),
        name="retention",
    )(logit, rq, rk, rv, ru)


def _swa_group(q_ref, g, kcat, vcat, biases, sink_ref, o_ref):
    tq = q_ref.shape[1]
    heads_per_group = SWA_HEADS // SWA_KV_HEADS
    slabs_per_group = heads_per_group // 2
    low = lax.broadcasted_iota(jnp.int32, (tq, LANES), 1) < SWA_HD
    zero = jnp.zeros((tq, LANES), q_ref.dtype)
    stacked = []
    for a in range(slabs_per_group):
        s0 = (g * slabs_per_group + a) * LANES
        slab = q_ref[0, :, s0:s0 + LANES]
        stacked.append(jnp.where(low, slab, zero))
        stacked.append(jnp.where(low, zero, slab))
    heads_per_part = heads_per_group // 2
    rows = heads_per_part * tq
    n_key_tiles = kcat.shape[0] // LANES
    for part in range(2):
        q = jnp.concatenate(stacked[part * heads_per_part:(part + 1) * heads_per_part], axis=0)
        sink = jnp.concatenate(
            [jnp.full((tq, LANES), sink_ref[g * heads_per_group + part * heads_per_part + h] * LOG2E, F32)
             for h in range(heads_per_part)], axis=0)
        m = l = acc = None
        for t0 in range(0, n_key_tiles, SWA_TILES_PER_CHUNK):
            t1 = min(t0 + SWA_TILES_PER_CHUNK, n_key_tiles)
            s = _dot_nt(q, kcat[t0 * LANES:t1 * LANES])
            tiles = [s[:, t * LANES:(t + 1) * LANES] for t in range(t1 - t0)]
            for t in range(t0, t1):
                if t in biases:
                    tiles[t - t0] = tiles[t - t0] + biases[t][:rows]
            mc = jnp.broadcast_to(jnp.max(functools.reduce(jnp.maximum, tiles), axis=-1, keepdims=True),
                                  (rows, LANES))
            m_new = jnp.maximum(mc, sink if m is None else m)
            ps = [jnp.exp2(tile - m_new) for tile in tiles]
            pv = _dot(jnp.concatenate(ps, axis=1).astype(BF16), vcat[t0 * LANES:t1 * LANES])
            psum = functools.reduce(jnp.add, ps)
            if m is None:
                l, acc = psum, pv
            else:
                alpha = jnp.exp2(m - m_new)
                l, acc = alpha * l + psum, alpha * acc + pv
            m = m_new
        den = jnp.broadcast_to(jnp.sum(l, axis=-1, keepdims=True), (rows, LANES)) + jnp.exp2(sink - m)
        o = acc / den
        for a in range(heads_per_part // 2):
            s0 = (g * slabs_per_group + part * (heads_per_part // 2) + a) * LANES
            even = o[(2 * a) * tq:(2 * a + 1) * tq]
            odd = o[(2 * a + 1) * tq:(2 * a + 2) * tq]
            o_ref[0, :, s0:s0 + LANES] = jnp.where(low, even, odd).astype(o_ref.dtype)


def _swa_kernel(sink_ref, q_ref, kp_ref, kc_ref, kn_ref, kx_ref, vp_ref, vc_ref, vn_ref, vx_ref, o_ref,
                *, n_lat_tiles, need_ctx):
    j = pl.program_id(1)
    tq = q_ref.shape[1]
    n_ctx = kx_ref.shape[1]

    def lanes_of(ref, g):
        return ref[0, :, g * LANES:(g + 1) * LANES]

    @pl.when(j < n_lat_tiles)
    def _latent():
        ci = lax.broadcasted_iota(jnp.int32, (tq, tq), 1)
        ri = lax.broadcasted_iota(jnp.int32, (tq, tq), 0)
        far = 4 * tq
        ri_prev = ri + jnp.where(j > 0, 0, far)
        ri_next = ri - jnp.where(j < n_lat_tiles - 1, 0, far)
        zero = jnp.zeros((tq, tq), F32)
        masked = jnp.full((tq, tq), MASKED, F32)
        heads_per_part = SWA_HEADS // SWA_KV_HEADS // 2
        bias = {0: jnp.concatenate([jnp.where(ci >= ri_prev, zero, masked)] * heads_per_part, axis=0),
                2: jnp.concatenate([jnp.where(ci <= ri_next, zero, masked)] * heads_per_part, axis=0)}
        for g in range(SWA_KV_HEADS):
            kcat = jnp.concatenate([lanes_of(kp_ref, g), lanes_of(kc_ref, g), lanes_of(kn_ref, g),
                                    lanes_of(kx_ref, g)], axis=0)
            vcat = jnp.concatenate([lanes_of(vp_ref, g), lanes_of(vc_ref, g), lanes_of(vn_ref, g),
                                    lanes_of(vx_ref, g)], axis=0)
            _swa_group(q_ref, g, kcat, vcat, bias, sink_ref, o_ref)

    if need_ctx:
        @pl.when(j >= n_lat_tiles)
        def _context():
            for g in range(SWA_KV_HEADS):
                _swa_group(q_ref, g, lanes_of(kx_ref, g), lanes_of(vx_ref, g), {}, sink_ref, o_ref)


def _window_attention(sink, sq, sk, sv, length, n_ctx, need_ctx):
    nb, nt, _ = sq.shape
    tq = QUERY_TILE
    n_lat = length // tq
    out_rows = nt if need_ctx else length
    kvw = 2 * SWA_KV
    kern = functools.partial(_swa_kernel, n_lat_tiles=n_lat, need_ctx=need_ctx)
    prev = pl.BlockSpec((1, tq, kvw), lambda b, j: (b, jnp.clip(j - 1, 0, n_lat - 1), 0))
    cur = pl.BlockSpec((1, tq, kvw), lambda b, j: (b, jnp.minimum(j, n_lat - 1), 0))
    nxt = pl.BlockSpec((1, tq, kvw), lambda b, j: (b, jnp.minimum(j + 1, n_lat - 1), 0))
    ctx = pl.BlockSpec((1, n_ctx, kvw), lambda b, j: (b, length // n_ctx, 0))
    return pl.pallas_call(
        kern,
        out_shape=jax.ShapeDtypeStruct((nb, out_rows, SWA_Q), BF16),
        grid=(nb, out_rows // tq),
        in_specs=[pl.BlockSpec(memory_space=pltpu.SMEM),
                  pl.BlockSpec((1, tq, SWA_Q), lambda b, j: (b, j, 0)),
                  prev, cur, nxt, ctx, prev, cur, nxt, ctx],
        out_specs=pl.BlockSpec((1, tq, SWA_Q), lambda b, j: (b, j, 0)),
        compiler_params=_params(),
        name="window_attention",
    )(sink, sq, sk, sk, sk, sk, sv, sv, sv, sv)


def _ga_kernel(q_ref, k_ref, v_ref, o_ref, *, length, n_ctx, n_lat_tiles, need_ctx):
    j = pl.program_id(2)
    tq = q_ref.shape[1]
    heads_per_group = GA_HEADS // GA_KV_HEADS
    q = jnp.concatenate([q_ref[0, :, h * LANES:(h + 1) * LANES] for h in range(heads_per_group)], axis=0)

    rows = heads_per_group * tq

    def attend(key_lo, key_hi):
        m = l = acc = None
        for c0 in range(key_lo, key_hi, GA_KEY_CHUNK):
            s = _dot_nt(q, k_ref[0, c0:c0 + GA_KEY_CHUNK, :])
            parts = [s[:, t * LANES:(t + 1) * LANES] for t in range(GA_KEY_CHUNK // LANES)]
            lane_max = functools.reduce(jnp.maximum, parts)
            mc = jnp.broadcast_to(jnp.max(lane_max, axis=-1, keepdims=True), (rows, LANES))
            m_new = mc if m is None else jnp.maximum(m, mc)
            ps = [jnp.exp2(part - m_new) for part in parts]
            pv = _dot(jnp.concatenate(ps, axis=1).astype(BF16), v_ref[0, c0:c0 + GA_KEY_CHUNK, :])
            psum = functools.reduce(jnp.add, ps)
            if m is None:
                l, acc = psum, pv
            else:
                alpha = jnp.exp2(m - m_new)
                l, acc = alpha * l + psum, alpha * acc + pv
            m = m_new
        o = acc / jnp.sum(l, axis=-1, keepdims=True)
        for h in range(heads_per_group):
            o_ref[0, :, h * LANES:(h + 1) * LANES] = o[h * tq:(h + 1) * tq].astype(o_ref.dtype)

    @pl.when(j < n_lat_tiles)
    def _latent():
        attend(0, length + n_ctx)

    if need_ctx:
        @pl.when(j >= n_lat_tiles)
        def _context():
            attend(length, length + n_ctx)


def _global_attention(gq, gk, gv, length, n_ctx, need_ctx):
    nb, nt, _ = gq.shape
    tq = QUERY_TILE
    out_rows = nt if need_ctx else length
    group_w = GA_Q // GA_KV_HEADS
    kern = functools.partial(_ga_kernel, length=length, n_ctx=n_ctx, n_lat_tiles=length // tq,
                             need_ctx=need_ctx)
    kv = pl.BlockSpec((1, nt, GA_HD), lambda b, g, j: (b, 0, g))
    return pl.pallas_call(
        kern,
        out_shape=jax.ShapeDtypeStruct((nb, out_rows, GA_Q), BF16),
        grid=(nb, GA_KV_HEADS, out_rows // tq),
        in_specs=[pl.BlockSpec((1, tq, group_w), lambda b, g, j: (b, j, g)), kv, kv],
        out_specs=pl.BlockSpec((1, tq, group_w), lambda b, g, j: (b, j, g)),
        compiler_params=_params(),
        name="global_attention",
    )(gq, gk, gv)


def _route(scores, biased):
    rows = [biased[r:r + 1, :] for r in range(N_EXPERTS)]
    raw = [scores[r:r + 1, :] for r in range(N_EXPERTS)]

    def top2_sum(vals):
        best = None
        for a in range(len(vals)):
            for b in range(a + 1, len(vals)):
                pair = vals[a] + vals[b]
                best = pair if best is None else jnp.maximum(best, pair)
        return best

    group_scores = [top2_sum(rows[g * EXPERTS_PER_GROUP:(g + 1) * EXPERTS_PER_GROUP]) for g in range(N_GROUPS)]
    group = jnp.zeros_like(group_scores[0], dtype=jnp.int32)
    best = group_scores[0]
    for g in range(1, N_GROUPS):
        better = group_scores[g] > best
        group = jnp.where(better, g, group)
        best = jnp.where(better, group_scores[g], best)

    def in_group(table, k):
        val = table[k]
        for g in range(1, N_GROUPS):
            val = jnp.where(group == g, table[g * EXPERTS_PER_GROUP + k], val)
        return val

    vals = [in_group(rows, k) for k in range(EXPERTS_PER_GROUP)]
    unbiased = [in_group(raw, k) for k in range(EXPERTS_PER_GROUP)]

    def first_argmax(cands):
        idx = jnp.zeros_like(group)
        top = cands[0]
        for k in range(1, len(cands)):
            better = cands[k] > top
            idx = jnp.where(better, k, idx)
            top = jnp.where(better, cands[k], top)
        return idx

    i1 = first_argmax(vals)
    i2 = first_argmax([jnp.where(i1 == k, -jnp.inf, vals[k]) for k in range(EXPERTS_PER_GROUP)])

    def pick(idx):
        val = unbiased[0]
        for k in range(1, EXPERTS_PER_GROUP):
            val = jnp.where(idx == k, unbiased[k], val)
        return val

    s1, s2 = pick(i1), pick(i2)
    total = s1 + s2
    experts = jnp.concatenate([group * EXPERTS_PER_GROUP + i1, group * EXPERTS_PER_GROUP + i2], axis=0)
    weights = jnp.concatenate([s1 / total, s2 / total], axis=0)
    return experts, weights


def _slot_ranks(experts, carry):
    tm = experts.shape[1]
    expert_id = lax.broadcasted_iota(jnp.int32, (N_EXPERTS, tm), 0)
    upper = jnp.where(lax.broadcasted_iota(jnp.int32, (tm, tm), 0) <= lax.broadcasted_iota(jnp.int32, (tm, tm), 1),
                      1.0, 0.0).astype(BF16)
    ranks = []
    for k in range(2):
        hit = jnp.where(expert_id == experts[k:k + 1, :], 1.0, 0.0)
        inclusive = _dot(hit.astype(BF16), upper)
        ranks.append(jnp.sum(hit * (carry + inclusive - 1.0), axis=0, keepdims=True))
        carry = carry + jnp.sum(hit, axis=1, keepdims=True)
    return jnp.concatenate(ranks, axis=0).astype(jnp.int32), carry


def _store_row_tiles(ref, val):
    rows = val.shape[0]
    for j in range(val.shape[1] // LANES):
        ref[pl.ds(j, rows, stride=SUBLANES), :] = val[:, j * LANES:(j + 1) * LANES]


def _load_row_tiles(ref, rows):
    return jnp.concatenate([ref[pl.ds(j, rows, stride=SUBLANES), :] for j in range(ref.shape[0] // rows)], axis=1)


def _merge_kernel(oret, oswa, oga, ar, as_, aa, x_ref, mod_ref, wr_ref, ws_ref, wa_ref, wout_ref, g2_ref, wrt_ref, br_ref,
                  xo_ref, h2t_ref, e_ref, r_ref, w_ref, cnt_ref, h2_prev, carry_ref,
                  *, n_lat_tiles, n_tiles, n_batch):
    s = pl.program_id(0)
    n_steps = n_batch * n_tiles

    def merge_tile():
        b = s // n_tiles
        i = s - b * n_tiles
        m = mod_ref[pl.ds(jnp.where(i < n_lat_tiles, b, n_batch), 1), :]

        def gate(a_ref):
            return jax.nn.sigmoid(a_ref[0].astype(F32))

        y = (gate(ar) * _dot(oret[0], wr_ref[0]) + gate(as_) * _dot(oswa[0], ws_ref[0])
             + gate(aa) * _dot(oga[0], wa_ref[0]))
        x = x_ref[0] + _mod_chunk(m, 2) * _dot(y.astype(BF16), wout_ref[0])
        xo_ref[0] = x
        h2 = _rms_normalize(x) * g2_ref[...] * (1.0 + _mod_chunk(m, 4)) + _mod_chunk(m, 3)
        _store_row_tiles(h2t_ref.at[0], h2)
        h2_prev[...] = h2

    def route_previous_tile():
        h2 = h2_prev[...]
        h_hi = h2.astype(BF16)
        h_lo = (h2 - h_hi.astype(F32)).astype(BF16)
        w = wrt_ref[...]
        w_hi = w.astype(BF16)
        w_lo = (w - w_hi.astype(F32)).astype(BF16)
        logits = _dot_nt(w_hi, h_hi) + _dot_nt(w_hi, h_lo) + _dot_nt(w_lo, h_hi)
        scores = jax.nn.sigmoid(logits)
        experts, weights = _route(scores, scores + br_ref[...])
        ranks, carry = _slot_ranks(experts, carry_ref[:, 0:1])
        e_ref[0] = experts
        r_ref[0] = ranks
        w_ref[0] = weights
        carry_ref[...] = jnp.broadcast_to(carry, carry_ref.shape)
        cnt_ref[...] = jnp.broadcast_to(carry, cnt_ref.shape)

    @pl.when(s == 0)
    def _init():
        carry_ref[...] = jnp.zeros_like(carry_ref)

    @pl.when(s > 0)
    def _routing_step():
        route_previous_tile()

    @pl.when(s < n_steps)
    def _merge_step():
        merge_tile()


def _merge(oret, oswa, oga, ar, as_, aa, xa, mod, w_ret, w_swa, w_ga, wout, layer, g2, wrt, br,
           n_lat_tiles, n_tiles):
    nb, nt, d = xa.shape
    rows = n_tiles * TOKEN_TILE
    n_steps = nb * n_tiles
    kern = functools.partial(_merge_kernel, n_lat_tiles=n_lat_tiles, n_tiles=n_tiles, n_batch=nb)

    def cur(s):
        t = jnp.minimum(s, n_steps - 1)
        return t // n_tiles, t % n_tiles

    def prev(s):
        t = jnp.maximum(s - 1, 0)
        return t // n_tiles, t % n_tiles

    tile = pl.BlockSpec((1, TOKEN_TILE, d), lambda s: (*cur(s), 0))
    whole = lambda arr: pl.BlockSpec(arr.shape, lambda s: (0,) * arr.ndim)
    square = pl.BlockSpec((1, d, d), lambda s: (layer, 0, 0))
    small = pl.BlockSpec((1, 2, TOKEN_TILE), lambda s: (prev(s)[0], 0, prev(s)[1]))
    return pl.pallas_call(
        kern,
        out_shape=[jax.ShapeDtypeStruct((nb, nt, d), F32),
                   jax.ShapeDtypeStruct((nb, rows * SUBLANES, LANES), F32),
                   jax.ShapeDtypeStruct((nb, 2, rows), jnp.int32),
                   jax.ShapeDtypeStruct((nb, 2, rows), jnp.int32),
                   jax.ShapeDtypeStruct((nb, 2, rows), F32),
                   jax.ShapeDtypeStruct((N_EXPERTS, LANES), F32)],
        grid=(n_steps + 1,),
        in_specs=[tile, tile, tile, tile, tile, tile, tile, whole(mod), square, square, square, square,
                  pl.BlockSpec((1, d), lambda s: (0, 0)), whole(wrt), whole(br)],
        out_specs=[tile,
                   pl.BlockSpec((1, TOKEN_TILE * SUBLANES, LANES), lambda s: (*cur(s), 0)),
                   small, small, small,
                   pl.BlockSpec((N_EXPERTS, LANES), lambda s: (0, 0))],
        scratch_shapes=[pltpu.VMEM((TOKEN_TILE, d), F32), pltpu.VMEM((N_EXPERTS, LANES), F32)],
        input_output_aliases={6: 0},
        compiler_params=_params(dimension_semantics=("arbitrary",)),
        name="merge_route",
    )(oret, oswa, oga, ar, as_, aa, xa, mod, w_ret, w_swa, w_ga, wout, g2.reshape(1, d), wrt, br)


def _row_copy(src, dst, sem):
    return pltpu.make_async_copy(src, dst, sem)


def _tile_rows(r):
    return pl.ds(pl.multiple_of(r * SUBLANES, SUBLANES), SUBLANES)


def _dispatch_kernel(tail_ref, nv_ref, dest_ref, h2t_ref, buf_out, zeros_ref, sem, tail_sem):
    @pl.when((pl.program_id(0) == 0) & (pl.program_id(1) == 0))
    def _zero_tails():
        zeros_ref[...] = jnp.zeros_like(zeros_ref)

        def zero_block(first_slot):
            rows = pl.ds(pl.multiple_of(first_slot * SUBLANES, SUBLANES), MOE_ROWS * SUBLANES)
            copy = pltpu.make_async_copy(zeros_ref, buf_out.at[rows, :], tail_sem)
            copy.start()
            copy.wait()

        for e in range(N_EXPERTS):
            @pl.when(tail_ref[e] >= 0)
            def _():
                zero_block(tail_ref[e])

        def unused(blk, carry):
            zero_block(blk * MOE_ROWS)
            return carry

        lax.fori_loop(nv_ref[0], buf_out.shape[0] // (MOE_ROWS * SUBLANES), unused, 0)

    def issue(r, carry):
        for k in range(2):
            d = dest_ref[0, 0, k * TOKEN_TILE + r]
            _row_copy(h2t_ref.at[0, _tile_rows(r), :], buf_out.at[_tile_rows(d), :], sem).start(priority=k)
        return carry

    lax.fori_loop(0, TOKEN_TILE, issue, 0, unroll=8)

    def drain(r, carry):
        for k in range(2):
            _row_copy(h2t_ref.at[0, _tile_rows(0), :], buf_out.at[_tile_rows(0), :], sem).wait()
        return carry

    lax.fori_loop(0, TOKEN_TILE, drain, 0, unroll=8)


def _dispatch(tail_start, n_valid, dest_tiles, h2t, n_slots):
    nb, rows8, _ = h2t.shape
    n_tiles = rows8 // (TOKEN_TILE * SUBLANES)
    grid_spec = pltpu.PrefetchScalarGridSpec(
        num_scalar_prefetch=2,
        grid=(nb, n_tiles),
        in_specs=[pl.BlockSpec((1, 1, 2 * TOKEN_TILE), lambda b, i, tail, nv: (b * n_tiles + i, 0, 0),
                               memory_space=pltpu.SMEM),
                  pl.BlockSpec((1, TOKEN_TILE * SUBLANES, LANES), lambda b, i, tail, nv: (b, i, 0))],
        out_specs=pl.BlockSpec(memory_space=pl.ANY),
        scratch_shapes=[pltpu.VMEM((MOE_ROWS * SUBLANES, LANES), F32),
                        pltpu.SemaphoreType.DMA(()), pltpu.SemaphoreType.DMA(())],
    )
    return pl.pallas_call(
        _dispatch_kernel,
        out_shape=jax.ShapeDtypeStruct((n_slots * SUBLANES, LANES), F32),
        grid_spec=grid_spec,
        compiler_params=_params(has_side_effects=True, dimension_semantics=("arbitrary", "arbitrary")),
        name="moe_dispatch",
    )(tail_start, n_valid, dest_tiles, h2t)


def _expert_kernel(be_ref, nv_ref, x_ref, wg_ref, wu_ref, wd_ref, y_ref, wg_bf, wu_bf, wd_bf):
    i = pl.program_id(0)

    @pl.when(i < nv_ref[0])
    def _():
        @pl.when((i == 0) | (be_ref[i] != be_ref[jnp.maximum(i - 1, 0)]))
        def _():
            wg_bf[...] = wg_ref[0, 0].astype(BF16)
            wu_bf[...] = wu_ref[0, 0].astype(BF16)
            wd_bf[...] = wd_ref[0, 0].astype(BF16)

        x = _load_row_tiles(x_ref, MOE_ROWS).astype(BF16)
        hid = _silu(_dot(x, wg_bf[...])) * _dot(x, wu_bf[...])
        _store_row_tiles(y_ref, _dot(hid.astype(BF16), wd_bf[...]))

    @pl.when(i >= nv_ref[0])
    def _():
        y_ref[...] = jnp.zeros_like(y_ref)


def _experts(block_e, n_valid, buf, wg, wu, wd, layer):
    n_slots = buf.shape[0] // SUBLANES
    d = wg.shape[2]
    used = lambda i, nv: jnp.maximum(jnp.minimum(i, nv[0] - 1), 0)
    rows = pl.BlockSpec((MOE_ROWS * SUBLANES, LANES), lambda i, be, nv: (used(i, nv), 0))
    grid_spec = pltpu.PrefetchScalarGridSpec(
        num_scalar_prefetch=2,
        grid=(n_slots // MOE_ROWS,),
        in_specs=[rows,
                  pl.BlockSpec((1, 1, d, D_EXPERT), lambda i, be, nv: (layer, be[used(i, nv)], 0, 0)),
                  pl.BlockSpec((1, 1, d, D_EXPERT), lambda i, be, nv: (layer, be[used(i, nv)], 0, 0)),
                  pl.BlockSpec((1, 1, D_EXPERT, d), lambda i, be, nv: (layer, be[used(i, nv)], 0, 0))],
        out_specs=pl.BlockSpec((MOE_ROWS * SUBLANES, LANES), lambda i, be, nv: (i, 0)),
        scratch_shapes=[pltpu.VMEM((d, D_EXPERT), BF16), pltpu.VMEM((d, D_EXPERT), BF16),
                        pltpu.VMEM((D_EXPERT, d), BF16)],
    )
    return pl.pallas_call(
        _expert_kernel,
        out_shape=jax.ShapeDtypeStruct((n_slots * SUBLANES, LANES), F32),
        grid_spec=grid_spec,
        compiler_params=_params(dimension_semantics=("arbitrary",)),
        name="moe_experts",
    )(block_e, n_valid, buf, wg, wu, wd)


def _combine_kernel(dest_ref, next_ref, y_hbm, wt_ref, x_ref, mod_ref, g_ref, *rest,
                    n_lat_tiles, n_tiles, n_batch, last):
    if last:
        out_ref, gbuf, sems = rest
    else:
        xo_ref, h_ref, gbuf, sems = rest
    t = pl.program_id(0)
    slot = lax.rem(t, 2)

    def gather(idx_ref, into):
        def issue(r, carry):
            for k in range(2):
                d = idx_ref[0, 0, k * TOKEN_TILE + r]
                _row_copy(y_hbm.at[_tile_rows(d), :], gbuf.at[into, k, _tile_rows(r), :],
                          sems.at[into]).start(priority=k)
            return carry

        lax.fori_loop(0, TOKEN_TILE, issue, 0, unroll=8)

    @pl.when(t == 0)
    def _():
        gather(dest_ref, 0)

    @pl.when(t + 1 < n_batch * n_tiles)
    def _():
        gather(next_ref, 1 - slot)

    def drain(r, carry):
        for k in range(2):
            _row_copy(y_hbm.at[_tile_rows(0), :], gbuf.at[slot, 0, _tile_rows(0), :], sems.at[slot]).wait()
        return carry

    lax.fori_loop(0, TOKEN_TILE, drain, 0, unroll=8)

    b = t // n_tiles
    m = mod_ref[pl.ds(jnp.where(t - b * n_tiles < n_lat_tiles, b, n_batch), 1), :]
    wt = wt_ref[0]
    moe = (_load_row_tiles(gbuf.at[slot, 0], TOKEN_TILE) * wt[:, 0:1]
           + _load_row_tiles(gbuf.at[slot, 1], TOKEN_TILE) * wt[:, 1:2])
    x = x_ref[0] + _mod_chunk(m, 5) * moe
    if last:
        out_ref[0] = _rms_normalize(x) * g_ref[...]
    else:
        xo_ref[0] = x
        h = _rms_normalize(x) * g_ref[...] * (1.0 + _mod_chunk(m, 7)) + _mod_chunk(m, 6)
        h_ref[0] = h.astype(BF16)


def _combine(dest_tiles, y, wt, xa, mod, g, n_lat_tiles, n_tiles, last):
    nb, nt, d = xa.shape
    rows = n_tiles * TOKEN_TILE
    n_steps = nb * n_tiles
    kern = functools.partial(_combine_kernel, n_lat_tiles=n_lat_tiles, n_tiles=n_tiles, n_batch=nb, last=last)
    tile = pl.BlockSpec((1, TOKEN_TILE, d), lambda t: (t // n_tiles, t % n_tiles, 0))
    if last:
        out_shape = jax.ShapeDtypeStruct((nb, rows, d), F32)
        out_specs = tile
        aliases = {}
    else:
        out_shape = [jax.ShapeDtypeStruct((nb, nt, d), F32), jax.ShapeDtypeStruct((nb, nt, d), BF16)]
        out_specs = [tile, tile]
        aliases = {4: 0}
    return pl.pallas_call(
        kern,
        out_shape=out_shape,
        grid=(n_steps,),
        in_specs=[pl.BlockSpec((1, 1, 2 * TOKEN_TILE), lambda t: (t, 0, 0), memory_space=pltpu.SMEM),
                  pl.BlockSpec((1, 1, 2 * TOKEN_TILE), lambda t: (jnp.minimum(t + 1, n_steps - 1), 0, 0),
                               memory_space=pltpu.SMEM),
                  pl.BlockSpec(memory_space=pl.ANY),
                  pl.BlockSpec((1, TOKEN_TILE, 2), lambda t: (t // n_tiles, t % n_tiles, 0)),
                  tile,
                  pl.BlockSpec(mod.shape, lambda t: (0, 0)),
                  pl.BlockSpec((1, d), lambda t: (0, 0))],
        out_specs=out_specs,
        scratch_shapes=[pltpu.VMEM((2, 2, TOKEN_TILE * SUBLANES, LANES), F32), pltpu.SemaphoreType.DMA((2,))],
        input_output_aliases=aliases,
        compiler_params=_params(dimension_semantics=("arbitrary",)),
        name="moe_combine",
    )(dest_tiles, dest_tiles, y, wt, xa, mod, g.reshape(1, d))


def _slot_plan(experts, ranks, counts):
    nb, _, rows = experts.shape
    padded = (counts + MOE_ROWS - 1) // MOE_ROWS * MOE_ROWS
    pad_ends = jnp.cumsum(padded)
    pad_starts = pad_ends - padded
    one_hot = experts[..., None] == jnp.arange(N_EXPERTS, dtype=jnp.int32)
    dest = jnp.sum(jnp.where(one_hot, pad_starts, 0), axis=-1) + ranks
    n_assign = nb * 2 * rows
    n_blocks = (n_assign + N_EXPERTS * (MOE_ROWS - 1) + MOE_ROWS - 1) // MOE_ROWS
    first_slot = jnp.arange(n_blocks, dtype=jnp.int32) * MOE_ROWS
    block_e = jnp.minimum(jnp.sum((pad_ends[None, :] <= first_slot[:, None]).astype(jnp.int32), axis=1),
                          N_EXPERTS - 1).astype(jnp.int32)
    n_valid = (pad_ends[-1:] // MOE_ROWS).astype(jnp.int32)
    tail_start = jnp.where(padded > 0, pad_ends - MOE_ROWS, -1).astype(jnp.int32)
    n_tiles = rows // TOKEN_TILE
    dest_tiles = (dest.astype(jnp.int32).reshape(nb, 2, n_tiles, TOKEN_TILE)
                  .transpose(0, 2, 1, 3).reshape(nb * n_tiles, 1, 2 * TOKEN_TILE))
    return dest_tiles, block_e, n_valid, tail_start, n_blocks * MOE_ROWS


def _rope_tables(length, n_ctx):
    rows = length // GRID_W
    row = jnp.repeat(jnp.arange(rows, dtype=jnp.int32), GRID_W).astype(F32)
    col = jnp.tile(jnp.arange(GRID_W, dtype=jnp.int32), rows).astype(F32)

    def table(hd):
        quarter = hd // 4
        freqs = ROPE_THETA ** (-jnp.arange(quarter, dtype=F32) / quarter)
        ang_r = row[:, None] * freqs[None, :]
        ang_c = col[:, None] * freqs[None, :]
        cos = jnp.concatenate([jnp.cos(ang_r), jnp.cos(ang_r), jnp.cos(ang_c), jnp.cos(ang_c)], axis=-1)
        sin = jnp.concatenate([-jnp.sin(ang_r), jnp.sin(ang_r), -jnp.sin(ang_c), jnp.sin(ang_c)], axis=-1)
        cos = jnp.concatenate([cos, jnp.ones((n_ctx, hd), F32)], axis=0)
        sin = jnp.concatenate([sin, jnp.zeros((n_ctx, hd), F32)], axis=0)
        reps = LANES // hd
        return jnp.tile(cos, (1, reps)), jnp.tile(sin, (1, reps))

    c128, s128 = table(RET_DK)
    c64, s64 = table(SWA_HD)
    return c128, s128, c64, s64


def kernel(x, c, ctx, c_ctx, w_mod, b_mod, g_norm1, g_norm2, w_in, ret_decay_logit, swa_sink, g_qnorm, g_knorm,
           w_br_ret, w_br_swa, w_br_ga, w_out, w_router, b_router, w_gate, w_up, w_down, g_final):
    nb, length, d = x.shape
    n_ctx = ctx.shape[1]
    depth = w_mod.shape[0]
    nt = length + n_ctx
    n_lat_tiles = length // TOKEN_TILE
    n_all_tiles = nt // TOKEN_TILE
    assert GA_HD == RET_DK == LANES and 2 * SWA_HD == LANES
    assert length % TOKEN_TILE == 0 and n_ctx == TOKEN_TILE and length % n_ctx == 0

    mod_rows = 8
    c_rows = jnp.concatenate([c, c_ctx[None, :], jnp.zeros((mod_rows - nb - 1, d), F32)], axis=0)
    mods = _modulation(c_rows, w_mod, b_mod)
    tables = _rope_tables(length, n_ctx)
    wrt = w_router.astype(F32).T
    br = b_router.astype(F32).reshape(N_EXPERTS, 1)

    xa, h = _prenorm(x, ctx, mods[0], g_norm1[0])
    w_in_bf, w_ret_bf, w_swa_bf, w_ga_bf, w_out_bf = (
        w.astype(BF16) for w in (w_in, w_br_ret, w_br_swa, w_br_ga, w_out))
    out = None
    for l in range(depth):
        need_ctx = l < depth - 1
        n_tiles = n_all_tiles if need_ctx else n_lat_tiles
        (rq, rk, rv, ru, sq, sk, sv, gq, gk, gv, ar, as_, aa) = _inproj(
            h, w_in_bf, l, tables, g_qnorm[l], g_knorm[l])
        o_ret = _retention(ret_decay_logit[l].astype(F32), rq, rk, rv, ru, length, n_ctx, need_ctx)
        o_swa = _window_attention(swa_sink[l].astype(F32), sq, sk, sv, length, n_ctx, need_ctx)
        o_ga = _global_attention(gq, gk, gv, length, n_ctx, need_ctx)
        xa, h2t, experts, ranks, weights, counts = _merge(
            o_ret, o_swa, o_ga, ar, as_, aa, xa, mods[l], w_ret_bf, w_swa_bf, w_ga_bf, w_out_bf, l,
            g_norm2[l], wrt, br, n_lat_tiles, n_tiles)
        dest_tiles, block_e, n_valid, tail_start, n_slots = _slot_plan(
            experts, ranks, counts[:, 0].astype(jnp.int32))
        buf = _dispatch(tail_start, n_valid, dest_tiles, h2t, n_slots)
        y = _experts(block_e, n_valid, buf, w_gate, w_up, w_down, l)
        wt = weights.transpose(0, 2, 1)
        if need_ctx:
            mod_pair = jnp.concatenate([mods[l], mods[l + 1][:, :2 * d]], axis=1)
            xa, h = _combine(dest_tiles, y, wt, xa, mod_pair, g_norm1[l + 1], n_lat_tiles, n_tiles, last=False)
        else:
            out = _combine(dest_tiles, y, wt, xa, mods[l], g_final, n_lat_tiles, n_tiles, last=True)
    return out
```

```python
import functools

import jax
import jax.numpy as jnp
from jax import lax
from jax.experimental import pallas as pl
from jax.experimental.pallas import tpu as pltpu

F32 = jnp.float32
BF16 = jnp.bfloat16

D_MODEL = 1024
GRID_W = 64
NORM_EPS = 1e-6
ROPE_THETA = 10000.0
RET_HEADS, RET_DK, RET_DV, RET_CHUNK = 4, 128, 256, 128
SWA_HEADS, SWA_KV_HEADS, SWA_HD, WINDOW = 16, 2, 64, 128
GA_HEADS, GA_KV_HEADS, GA_HD = 8, 2, 128
N_EXPERTS, N_GROUPS, EXPERTS_PER_GROUP, D_EXPERT = 16, 4, 4, 512

RET_QK = RET_HEADS * RET_DK
RET_V = RET_HEADS * RET_DV
SWA_Q = SWA_HEADS * SWA_HD
SWA_KV = SWA_KV_HEADS * SWA_HD
GA_Q = GA_HEADS * GA_HD
GA_KV = GA_KV_HEADS * GA_HD
SPLITS = (RET_QK, RET_QK, RET_V, RET_V, SWA_Q, SWA_KV, SWA_KV, GA_Q, GA_KV, GA_KV,
          D_MODEL, D_MODEL, D_MODEL)

LANES = 128
SUBLANES = 8
TOKEN_TILE = 256
QUERY_TILE = 128
GA_QUERY_TILE = 256
MOE_ROWS = 256
GA_KEY_CHUNK = 256
RET_HEADS_PER_STEP = 2
SWA_TILES_PER_CHUNK = 2
MOD_COLS = 1536
VMEM_LIMIT = 56 * 1024 * 1024
MASKED = -1e30
LOG2E = 1.4426950408889634

_W_WIDTHS = (RET_QK, RET_QK, RET_V, RET_V, SWA_Q, 2 * SWA_KV, 2 * SWA_KV, GA_Q, GA_KV, GA_KV,
             D_MODEL, D_MODEL, D_MODEL)
_IN_OFFS = tuple(sum(SPLITS[:i]) for i in range(len(SPLITS)))


def _dot(a, b):
    return jnp.dot(a, b, preferred_element_type=F32)


def _dot_nt(a, b):
    return lax.dot_general(a, b, (((1,), (1,)), ((), ())), preferred_element_type=F32)


def _dot_tn(a, b):
    return lax.dot_general(a, b, (((0,), (0,)), ((), ())), preferred_element_type=F32)


def _silu(x):
    return x * jax.nn.sigmoid(x)


def _rms_normalize(x):
    return x * lax.rsqrt(jnp.mean(x * x, axis=-1, keepdims=True) + NORM_EPS)


def _params(**kw):
    return pltpu.CompilerParams(vmem_limit_bytes=VMEM_LIMIT, **kw)


def _mod_kernel(c_ref, w_ref, b_ref, o_ref):
    a = _silu(c_ref[...])
    o_ref[0] = _dot(a.astype(BF16), w_ref[0].astype(BF16)) + b_ref[0]


def _modulation(c_rows, w_mod, b_mod):
    depth, d, n = w_mod.shape
    rows = c_rows.shape[0]
    return pl.pallas_call(
        _mod_kernel,
        out_shape=jax.ShapeDtypeStruct((depth, rows, n), F32),
        grid=(depth, n // MOD_COLS),
        in_specs=[pl.BlockSpec((rows, d), lambda l, j: (0, 0)),
                  pl.BlockSpec((1, d, MOD_COLS), lambda l, j: (l, 0, j)),
                  pl.BlockSpec((1, 1, MOD_COLS), lambda l, j: (l, 0, j))],
        out_specs=pl.BlockSpec((1, rows, MOD_COLS), lambda l, j: (l, 0, j)),
        compiler_params=_params(),
        name="modulation",
    )(c_rows, w_mod, b_mod.reshape(depth, 1, n))


def _mod_row(mod_ref, n_lat_tiles, n_batch):
    b = pl.program_id(0)
    i = pl.program_id(1)
    r = jnp.where(i < n_lat_tiles, b, n_batch)
    return mod_ref[pl.ds(r, 1), :]


def _mod_chunk(m, k):
    return m[:, k * D_MODEL:(k + 1) * D_MODEL]


def _prenorm_kernel(x_ref, c_ref, mod_ref, g_ref, xa_ref, h_ref, *, n_lat_tiles, n_batch):
    m = _mod_row(mod_ref, n_lat_tiles, n_batch)
    x = jnp.where(pl.program_id(1) < n_lat_tiles, x_ref[0], c_ref[0])
    xa_ref[0] = x
    h = _rms_normalize(x) * g_ref[...] * (1.0 + _mod_chunk(m, 1)) + _mod_chunk(m, 0)
    h_ref[0] = h.astype(BF16)


def _prenorm(x, ctx, mod, g):
    nb, length, d = x.shape
    n_lat_tiles = length // TOKEN_TILE
    nt = length + ctx.shape[1]
    kern = functools.partial(_prenorm_kernel, n_lat_tiles=n_lat_tiles, n_batch=nb)
    tile = pl.BlockSpec((1, TOKEN_TILE, d), lambda b, i: (b, i, 0))
    return pl.pallas_call(
        kern,
        out_shape=[jax.ShapeDtypeStruct((nb, nt, d), F32), jax.ShapeDtypeStruct((nb, nt, d), BF16)],
        grid=(nb, nt // TOKEN_TILE),
        in_specs=[pl.BlockSpec((1, TOKEN_TILE, d), lambda b, i: (b, jnp.minimum(i, n_lat_tiles - 1), 0)),
                  pl.BlockSpec((1, TOKEN_TILE, d), lambda b, i: (b, 0, 0)),
                  pl.BlockSpec(mod.shape, lambda b, i: (0, 0)),
                  pl.BlockSpec((1, d), lambda b, i: (0, 0))],
        out_specs=[tile, tile],
        compiler_params=_params(),
        name="prenorm",
    )(x, ctx, mod, g.reshape(1, d))


def _rope(x, cos, sin, quarter):
    lane = lax.broadcasted_iota(jnp.int32, x.shape, 1)
    first = (lane % (2 * quarter)) < quarter
    partner = jnp.where(first, pltpu.roll(x, LANES - quarter, 1), pltpu.roll(x, quarter, 1))
    return x * cos + partner * sin


def _inproj_kernel(h_ref, w_ref, c128_ref, s128_ref, c64_ref, s64_ref, gq_ref, gk_ref,
                   rq, rk, rv, ru, sq, sk, sv, gq, gk, gv, ar, as_, aa):
    h = h_ref[0]
    c128, s128 = c128_ref[...], s128_ref[...]
    c64, s64 = c64_ref[...], s64_ref[...]

    def proj(idx):
        return _dot(h, w_ref[0, :, _IN_OFFS[idx]:_IN_OFFS[idx] + SPLITS[idx]])

    def slabs(acc):
        return [acc[:, s * LANES:(s + 1) * LANES] for s in range(acc.shape[1] // LANES)]

    def store(ref, s, val):
        ref[0, :, s * LANES:(s + 1) * LANES] = val.astype(ref.dtype)

    def head_per_slab(acc):
        low = lax.broadcasted_iota(jnp.int32, acc.shape, 1) < SWA_HD
        other = pltpu.roll(acc, SWA_HD, 1)
        return [jnp.where(low, acc, other), jnp.where(low, other, acc)]

    for s, xs in enumerate(slabs(proj(0))):
        store(rq, s, _rope(xs, c128, s128, RET_DK // 4))
    for s, xs in enumerate(slabs(proj(1))):
        store(rk, s, _rope(xs, c128, s128, RET_DK // 4) * (RET_DK ** -0.5))
    rv[0] = proj(2).astype(rv.dtype)
    ru[0] = proj(3).astype(ru.dtype)
    for s, xs in enumerate(slabs(proj(4))):
        store(sq, s, _rope(xs, c64, s64, SWA_HD // 4) * (SWA_HD ** -0.5 * LOG2E))
    for s, xs in enumerate(head_per_slab(proj(5))):
        store(sk, s, _rope(xs, c64, s64, SWA_HD // 4))
    for s, xs in enumerate(head_per_slab(proj(6))):
        store(sv, s, xs)
    for s, xs in enumerate(slabs(proj(7))):
        xn = _rms_normalize(xs) * gq_ref[...]
        store(gq, s, _rope(xn, c128, s128, GA_HD // 4) * (GA_HD ** -0.5 * LOG2E))
    for s, xs in enumerate(slabs(proj(8))):
        xn = _rms_normalize(xs) * gk_ref[...]
        store(gk, s, _rope(xn, c128, s128, GA_HD // 4))
    gv[0] = proj(9).astype(gv.dtype)
    ar[0] = proj(10).astype(ar.dtype)
    as_[0] = proj(11).astype(as_.dtype)
    aa[0] = proj(12).astype(aa.dtype)


def _inproj(h, w_in, layer, tables, g_q, g_k):
    nb, nt, d = h.shape
    tile = lambda width: pl.BlockSpec((1, TOKEN_TILE, width), lambda b, i: (b, i, 0))
    tab = pl.BlockSpec((TOKEN_TILE, LANES), lambda b, i: (i, 0))
    vec = pl.BlockSpec((1, LANES), lambda b, i: (0, 0))
    outs = [jax.ShapeDtypeStruct((nb, nt, w), BF16) for w in _W_WIDTHS]
    return pl.pallas_call(
        _inproj_kernel,
        out_shape=outs,
        grid=(nb, nt // TOKEN_TILE),
        in_specs=[tile(d),
                  pl.BlockSpec((1, d, w_in.shape[2]), lambda b, i: (layer, 0, 0), pipeline_mode=pl.Buffered(1)),
                  tab, tab, tab, tab, vec, vec],
        out_specs=[tile(w) for w in _W_WIDTHS],
        compiler_params=_params(),
        name="inproj",
    )(h, w_in, *tables, g_q.reshape(1, LANES), g_k.reshape(1, LANES))


def _log_sigmoid(x):
    return jnp.minimum(x, 0.0) - jnp.log1p(jnp.exp(-jnp.abs(x)))


def _ret_kernel(logit_ref, q_ref, k_ref, v_ref, u_ref, o_ref, acc_ref, sf_ref, sb_ref,
                *, length, n_ctx, need_ctx):
    chunk = RET_CHUNK
    n_chunks = length // chunk
    half = n_chunks // 2

    def rows_f32(shape):
        return lax.broadcasted_iota(jnp.int32, shape, 0).astype(F32)

    def finish(o, u):
        return (_rms_normalize(o) * _silu(u.astype(F32))).astype(o_ref.dtype)

    ci = rows_f32((chunk, RET_DK))
    heads = []
    for hi in range(RET_HEADS_PER_STEP):
        hh = pl.program_id(1) * RET_HEADS_PER_STEP + hi
        lgf = _log_sigmoid(jnp.full((1, 1), logit_ref[0, hh], F32))
        lgb = _log_sigmoid(jnp.full((1, 1), logit_ref[1, hh], F32))
        qk = slice(hi * RET_DK, (hi + 1) * RET_DK)
        vu = slice(hi * RET_DV, (hi + 1) * RET_DV)

        def both_ways_decay(n, lgf=lgf, lgb=lgb):
            diff = rows_f32((n, n)) - lax.broadcasted_iota(jnp.int32, (n, n), 1).astype(F32)
            return jnp.where(diff >= 0, jnp.exp(jnp.maximum(diff, 0.0) * lgf),
                             jnp.exp(jnp.maximum(-diff, 0.0) * lgb))

        kx = k_ref[0, length:length + n_ctx, qk].astype(F32)
        vx = v_ref[0, length:length + n_ctx, vu]
        lx = rows_f32((n_ctx, RET_DK))
        sf_ref[hi] = _dot_tn((kx * jnp.exp((n_ctx - 1.0 - lx) * lgf)).astype(BF16), vx)
        sb_ref[hi] = _dot_tn((kx * jnp.exp(lx * lgb)).astype(BF16), vx)
        if need_ctx:
            qx = q_ref[0, length:length + n_ctx, qk]
            sx = _dot_nt(qx, kx.astype(BF16)) * both_ways_decay(n_ctx)
            o_ref[0, length:length + n_ctx, vu] = finish(_dot(sx.astype(BF16), vx),
                                                         u_ref[0, length:length + n_ctx, vu])
        heads.append(dict(
            qk=qk, vu=vu, intra=both_ways_decay(chunk),
            q_dec_f=jnp.exp((ci + 1.0) * lgf), k_dec_f=jnp.exp((chunk - 1.0 - ci) * lgf),
            q_dec_b=jnp.exp((chunk - ci) * lgb), k_dec_b=jnp.exp(ci * lgb),
            chunk_dec_f=jnp.exp(chunk * lgf), chunk_dec_b=jnp.exp(chunk * lgb)))

    def sweep(s, first_touch):
        rf = pl.ds(pl.multiple_of(s * chunk, chunk), chunk)
        rb = pl.ds(pl.multiple_of((n_chunks - 1 - s) * chunk, chunk), chunk)
        for hi, hd in enumerate(heads):
            qk, vu = hd["qk"], hd["vu"]
            qf, kf, vf = q_ref[0, rf, qk], k_ref[0, rf, qk], v_ref[0, rf, vu]
            sc = _dot_nt(qf, kf) * hd["intra"]
            state_f = sf_ref[hi]
            o_f = (_dot(sc.astype(BF16), vf)
                   + _dot((qf.astype(F32) * hd["q_dec_f"]).astype(BF16), state_f.astype(BF16)))
            sf_ref[hi] = state_f * hd["chunk_dec_f"] + _dot_tn((kf.astype(F32) * hd["k_dec_f"]).astype(BF16), vf)
            qb, kb, vb = q_ref[0, rb, qk], k_ref[0, rb, qk], v_ref[0, rb, vu]
            state_b = sb_ref[hi]
            o_b = _dot((qb.astype(F32) * hd["q_dec_b"]).astype(BF16), state_b.astype(BF16))
            sb_ref[hi] = state_b * hd["chunk_dec_b"] + _dot_tn((kb.astype(F32) * hd["k_dec_b"]).astype(BF16), vb)
            if first_touch:
                acc_ref[hi, rf, :] = o_f
                acc_ref[hi, rb, :] = o_b
            else:
                o_ref[0, rf, vu] = finish(acc_ref[hi, rf, :] + o_f, u_ref[0, rf, vu])
                o_ref[0, rb, vu] = finish(acc_ref[hi, rb, :] + o_b, u_ref[0, rb, vu])

    def first_half(s, carry):
        sweep(s, True)
        return carry

    def second_half(s, carry):
        sweep(s, False)
        return carry

    lax.fori_loop(0, half, first_half, 0)
    lax.fori_loop(half, n_chunks, second_half, 0)


def _retention(logit, rq, rk, rv, ru, length, n_ctx, need_ctx):
    nb, nt, _ = rq.shape
    assert (length // RET_CHUNK) % 2 == 0 and RET_HEADS % RET_HEADS_PER_STEP == 0
    out_rows = nt if need_ctx else length
    kern = functools.partial(_ret_kernel, length=length, n_ctx=n_ctx, need_ctx=need_ctx)
    head = lambda width: pl.BlockSpec((1, nt, RET_HEADS_PER_STEP * width), lambda b, h: (b, 0, h))
    return pl.pallas_call(
        kern,
        out_shape=jax.ShapeDtypeStruct((nb, out_rows, RET_V), BF16),
        grid=(nb, RET_HEADS // RET_HEADS_PER_STEP),
        in_specs=[pl.BlockSpec(memory_space=pltpu.SMEM),
                  head(RET_DK), head(RET_DK), head(RET_DV), head(RET_DV)],
        out_specs=pl.BlockSpec((1, out_rows, RET_HEADS_PER_STEP * RET_DV), lambda b, h: (b, 0, h)),
        scratch_shapes=[pltpu.VMEM((RET_HEADS_PER_STEP, length, RET_DV), F32),
                        pltpu.VMEM((RET_HEADS_PER_STEP, RET_DK, RET_DV), F32),
                        pltpu.VMEM((RET_HEADS_PER_STEP, RET_DK, RET_DV), F32)],
        compiler_params=_params(),
        name="retention",
    )(logit, rq, rk, rv, ru)


def _swa_group(q_ref, g, kcat, vcat, biases, sink_ref, o_ref):
    tq = q_ref.shape[1]
    heads_per_group = SWA_HEADS // SWA_KV_HEADS
    slabs_per_group = heads_per_group // 2
    low = lax.broadcasted_iota(jnp.int32, (tq, LANES), 1) < SWA_HD
    zero = jnp.zeros((tq, LANES), q_ref.dtype)
    stacked = []
    for a in range(slabs_per_group):
        s0 = (g * slabs_per_group + a) * LANES
        slab = q_ref[0, :, s0:s0 + LANES]
        stacked.append(jnp.where(low, slab, zero))
        stacked.append(jnp.where(low, zero, slab))
    heads_per_part = heads_per_group // 2
    rows = heads_per_part * tq
    n_key_tiles = kcat.shape[0] // LANES
    for part in range(2):
        q = jnp.concatenate(stacked[part * heads_per_part:(part + 1) * heads_per_part], axis=0)
        sink = jnp.concatenate(
            [jnp.full((tq, LANES), sink_ref[g * heads_per_group + part * heads_per_part + h] * LOG2E, F32)
             for h in range(heads_per_part)], axis=0)
        m = l = acc = None
        for t0 in range(0, n_key_tiles, SWA_TILES_PER_CHUNK):
            t1 = min(t0 + SWA_TILES_PER_CHUNK, n_key_tiles)
            s = _dot_nt(q, kcat[t0 * LANES:t1 * LANES])
            tiles = [s[:, t * LANES:(t + 1) * LANES] for t in range(t1 - t0)]
            for t in range(t0, t1):
                if t in biases:
                    tiles[t - t0] = tiles[t - t0] + biases[t][:rows]
            mc = jnp.broadcast_to(jnp.max(functools.reduce(jnp.maximum, tiles), axis=-1, keepdims=True),
                                  (rows, LANES))
            m_new = jnp.maximum(mc, sink if m is None else m)
            ps = [jnp.exp2(tile - m_new) for tile in tiles]
            pv = _dot(jnp.concatenate(ps, axis=1).astype(BF16), vcat[t0 * LANES:t1 * LANES])
            psum = functools.reduce(jnp.add, ps)
            if m is None:
                l, acc = psum, pv
            else:
                alpha = jnp.exp2(m - m_new)
                l, acc = alpha * l + psum, alpha * acc + pv
            m = m_new
        den = jnp.broadcast_to(jnp.sum(l, axis=-1, keepdims=True), (rows, LANES)) + jnp.exp2(sink - m)
        o = acc / den
        for a in range(heads_per_part // 2):
            s0 = (g * slabs_per_group + part * (heads_per_part // 2) + a) * LANES
            even = o[(2 * a) * tq:(2 * a + 1) * tq]
            odd = o[(2 * a + 1) * tq:(2 * a + 2) * tq]
            o_ref[0, :, s0:s0 + LANES] = jnp.where(low, even, odd).astype(o_ref.dtype)


def _swa_kernel(sink_ref, q_ref, kp_ref, kc_ref, kn_ref, kx_ref, vp_ref, vc_ref, vn_ref, vx_ref, o_ref,
                *, n_lat_tiles, need_ctx):
    j = pl.program_id(1)
    tq = q_ref.shape[1]
    n_ctx = kx_ref.shape[1]

    def lanes_of(ref, g):
        return ref[0, :, g * LANES:(g + 1) * LANES]

    @pl.when(j < n_lat_tiles)
    def _latent():
        ci = lax.broadcasted_iota(jnp.int32, (tq, tq), 1)
        ri = lax.broadcasted_iota(jnp.int32, (tq, tq), 0)
        far = 4 * tq
        ri_prev = ri + jnp.where(j > 0, 0, far)
        ri_next = ri - jnp.where(j < n_lat_tiles - 1, 0, far)
        zero = jnp.zeros((tq, tq), F32)
        masked = jnp.full((tq, tq), MASKED, F32)
        heads_per_part = SWA_HEADS // SWA_KV_HEADS // 2
        bias = {0: jnp.concatenate([jnp.where(ci >= ri_prev, zero, masked)] * heads_per_part, axis=0),
                2: jnp.concatenate([jnp.where(ci <= ri_next, zero, masked)] * heads_per_part, axis=0)}
        for g in range(SWA_KV_HEADS):
            kcat = jnp.concatenate([lanes_of(kp_ref, g), lanes_of(kc_ref, g), lanes_of(kn_ref, g),
                                    lanes_of(kx_ref, g)], axis=0)
            vcat = jnp.concatenate([lanes_of(vp_ref, g), lanes_of(vc_ref, g), lanes_of(vn_ref, g),
                                    lanes_of(vx_ref, g)], axis=0)
            _swa_group(q_ref, g, kcat, vcat, bias, sink_ref, o_ref)

    if need_ctx:
        @pl.when(j >= n_lat_tiles)
        def _context():
            for g in range(SWA_KV_HEADS):
                _swa_group(q_ref, g, lanes_of(kx_ref, g), lanes_of(vx_ref, g), {}, sink_ref, o_ref)


def _window_attention(sink, sq, sk, sv, length, n_ctx, need_ctx):
    nb, nt, _ = sq.shape
    tq = QUERY_TILE
    n_lat = length // tq
    out_rows = nt if need_ctx else length
    kvw = 2 * SWA_KV
    kern = functools.partial(_swa_kernel, n_lat_tiles=n_lat, need_ctx=need_ctx)
    prev = pl.BlockSpec((1, tq, kvw), lambda b, j: (b, jnp.clip(j - 1, 0, n_lat - 1), 0))
    cur = pl.BlockSpec((1, tq, kvw), lambda b, j: (b, jnp.minimum(j, n_lat - 1), 0))
    nxt = pl.BlockSpec((1, tq, kvw), lambda b, j: (b, jnp.minimum(j + 1, n_lat - 1), 0))
    ctx = pl.BlockSpec((1, n_ctx, kvw), lambda b, j: (b, length // n_ctx, 0))
    return pl.pallas_call(
        kern,
        out_shape=jax.ShapeDtypeStruct((nb, out_rows, SWA_Q), BF16),
        grid=(nb, out_rows // tq),
        in_specs=[pl.BlockSpec(memory_space=pltpu.SMEM),
                  pl.BlockSpec((1, tq, SWA_Q), lambda b, j: (b, j, 0)),
                  prev, cur, nxt, ctx, prev, cur, nxt, ctx],
        out_specs=pl.BlockSpec((1, tq, SWA_Q), lambda b, j: (b, j, 0)),
        compiler_params=_params(),
        name="window_attention",
    )(sink, sq, sk, sk, sk, sk, sv, sv, sv, sv)


def _ga_kernel(q_ref, k_ref, v_ref, o_ref, *, length, n_ctx, n_lat_tiles, need_ctx):
    j = pl.program_id(2)
    tq = q_ref.shape[1]
    heads_per_group = GA_HEADS // GA_KV_HEADS
    q = jnp.concatenate([q_ref[0, :, h * LANES:(h + 1) * LANES] for h in range(heads_per_group)], axis=0)

    rows = heads_per_group * tq

    def attend(key_lo, key_hi):
        m = l = acc = None
        for c0 in range(key_lo, key_hi, GA_KEY_CHUNK):
            c1 = min(c0 + GA_KEY_CHUNK, key_hi)
            s = _dot_nt(q, k_ref[0, c0:c1, :])
            parts = [s[:, t * LANES:(t + 1) * LANES] for t in range((c1 - c0) // LANES)]
            lane_max = functools.reduce(jnp.maximum, parts)
            mc = jnp.broadcast_to(jnp.max(lane_max, axis=-1, keepdims=True), (rows, LANES))
            m_new = mc if m is None else jnp.maximum(m, mc)
            ps = [jnp.exp2(part - m_new) for part in parts]
            pv = _dot(jnp.concatenate(ps, axis=1).astype(BF16), v_ref[0, c0:c1, :])
            psum = functools.reduce(jnp.add, ps)
            if m is None:
                l, acc = psum, pv
            else:
                alpha = jnp.exp2(m - m_new)
                l, acc = alpha * l + psum, alpha * acc + pv
            m = m_new
        o = acc / jnp.sum(l, axis=-1, keepdims=True)
        for h in range(heads_per_group):
            o_ref[0, :, h * LANES:(h + 1) * LANES] = o[h * tq:(h + 1) * tq].astype(o_ref.dtype)

    @pl.when(j < n_lat_tiles)
    def _latent():
        attend(0, length + n_ctx)

    if need_ctx:
        @pl.when(j >= n_lat_tiles)
        def _context():
            attend(length, length + n_ctx)


def _global_attention(gq, gk, gv, length, n_ctx, need_ctx):
    nb, nt, _ = gq.shape
    tq = GA_QUERY_TILE
    out_rows = nt if need_ctx else length
    group_w = GA_Q // GA_KV_HEADS
    kern = functools.partial(_ga_kernel, length=length, n_ctx=n_ctx, n_lat_tiles=length // tq,
                             need_ctx=need_ctx)
    kv = pl.BlockSpec((1, nt, GA_HD), lambda b, g, j: (b, 0, g))
    return pl.pallas_call(
        kern,
        out_shape=jax.ShapeDtypeStruct((nb, out_rows, GA_Q), BF16),
        grid=(nb, GA_KV_HEADS, out_rows // tq),
        in_specs=[pl.BlockSpec((1, tq, group_w), lambda b, g, j: (b, j, g)), kv, kv],
        out_specs=pl.BlockSpec((1, tq, group_w), lambda b, g, j: (b, j, g)),
        compiler_params=_params(),
        name="global_attention",
    )(gq, gk, gv)


def _route(scores, biased):
    rows = [biased[r:r + 1, :] for r in range(N_EXPERTS)]
    raw = [scores[r:r + 1, :] for r in range(N_EXPERTS)]

    def top2_sum(vals):
        best = None
        for a in range(len(vals)):
            for b in range(a + 1, len(vals)):
                pair = vals[a] + vals[b]
                best = pair if best is None else jnp.maximum(best, pair)
        return best

    group_scores = [top2_sum(rows[g * EXPERTS_PER_GROUP:(g + 1) * EXPERTS_PER_GROUP]) for g in range(N_GROUPS)]
    group = jnp.zeros_like(group_scores[0], dtype=jnp.int32)
    best = group_scores[0]
    for g in range(1, N_GROUPS):
        better = group_scores[g] > best
        group = jnp.where(better, g, group)
        best = jnp.where(better, group_scores[g], best)

    def in_group(table, k):
        val = table[k]
        for g in range(1, N_GROUPS):
            val = jnp.where(group == g, table[g * EXPERTS_PER_GROUP + k], val)
        return val

    vals = [in_group(rows, k) for k in range(EXPERTS_PER_GROUP)]
    unbiased = [in_group(raw, k) for k in range(EXPERTS_PER_GROUP)]

    def first_argmax(cands):
        idx = jnp.zeros_like(group)
        top = cands[0]
        for k in range(1, len(cands)):
            better = cands[k] > top
            idx = jnp.where(better, k, idx)
            top = jnp.where(better, cands[k], top)
        return idx

    i1 = first_argmax(vals)
    i2 = first_argmax([jnp.where(i1 == k, -jnp.inf, vals[k]) for k in range(EXPERTS_PER_GROUP)])

    def pick(idx):
        val = unbiased[0]
        for k in range(1, EXPERTS_PER_GROUP):
            val = jnp.where(idx == k, unbiased[k], val)
        return val

    s1, s2 = pick(i1), pick(i2)
    total = s1 + s2
    experts = jnp.concatenate([group * EXPERTS_PER_GROUP + i1, group * EXPERTS_PER_GROUP + i2], axis=0)
    weights = jnp.concatenate([s1 / total, s2 / total], axis=0)
    return experts, weights


def _slot_ranks(experts, carry):
    tm = experts.shape[1]
    expert_id = lax.broadcasted_iota(jnp.int32, (N_EXPERTS, tm), 0)
    upper = jnp.where(lax.broadcasted_iota(jnp.int32, (tm, tm), 0) <= lax.broadcasted_iota(jnp.int32, (tm, tm), 1),
                      1.0, 0.0).astype(BF16)
    ranks = []
    for k in range(2):
        hit = jnp.where(expert_id == experts[k:k + 1, :], 1.0, 0.0)
        inclusive = _dot(hit.astype(BF16), upper)
        ranks.append(jnp.sum(hit * (carry + inclusive - 1.0), axis=0, keepdims=True))
        carry = carry + jnp.sum(hit, axis=1, keepdims=True)
    return jnp.concatenate(ranks, axis=0).astype(jnp.int32), carry


def _store_row_tiles(ref, val):
    rows = val.shape[0]
    for j in range(val.shape[1] // LANES):
        ref[pl.ds(j, rows, stride=SUBLANES), :] = val[:, j * LANES:(j + 1) * LANES]


def _load_row_tiles(ref, rows):
    return jnp.concatenate([ref[pl.ds(j, rows, stride=SUBLANES), :] for j in range(ref.shape[0] // rows)], axis=1)


def _merge_kernel(oret, oswa, oga, ar, as_, aa, x_ref, mod_ref, wr_ref, ws_ref, wa_ref, wout_ref, g2_ref, wrt_ref, br_ref,
                  xo_ref, h2t_ref, e_ref, r_ref, w_ref, cnt_ref, h2_prev, carry_ref,
                  *, n_lat_tiles, n_tiles, n_batch):
    s = pl.program_id(0)
    n_steps = n_batch * n_tiles

    def merge_tile():
        b = s // n_tiles
        i = s - b * n_tiles
        m = mod_ref[pl.ds(jnp.where(i < n_lat_tiles, b, n_batch), 1), :]

        def gate(a_ref):
            return jax.nn.sigmoid(a_ref[0].astype(F32))

        y = (gate(ar) * _dot(oret[0], wr_ref[0]) + gate(as_) * _dot(oswa[0], ws_ref[0])
             + gate(aa) * _dot(oga[0], wa_ref[0]))
        x = x_ref[0] + _mod_chunk(m, 2) * _dot(y.astype(BF16), wout_ref[0])
        xo_ref[0] = x
        h2 = _rms_normalize(x) * g2_ref[...] * (1.0 + _mod_chunk(m, 4)) + _mod_chunk(m, 3)
        _store_row_tiles(h2t_ref.at[0], h2)
        h2_prev[...] = h2

    def route_previous_tile():
        h2 = h2_prev[...]
        h_hi = h2.astype(BF16)
        h_lo = (h2 - h_hi.astype(F32)).astype(BF16)
        w = wrt_ref[...]
        w_hi = w.astype(BF16)
        w_lo = (w - w_hi.astype(F32)).astype(BF16)
        logits = _dot_nt(w_hi, h_hi) + _dot_nt(w_hi, h_lo) + _dot_nt(w_lo, h_hi)
        scores = jax.nn.sigmoid(logits)
        experts, weights = _route(scores, scores + br_ref[...])
        ranks, carry = _slot_ranks(experts, carry_ref[:, 0:1])
        e_ref[0] = experts
        r_ref[0] = ranks
        w_ref[0] = weights
        carry_ref[...] = jnp.broadcast_to(carry, carry_ref.shape)
        cnt_ref[...] = jnp.broadcast_to(carry, cnt_ref.shape)

    @pl.when(s == 0)
    def _init():
        carry_ref[...] = jnp.zeros_like(carry_ref)

    @pl.when(s > 0)
    def _routing_step():
        route_previous_tile()

    @pl.when(s < n_steps)
    def _merge_step():
        merge_tile()


def _merge(oret, oswa, oga, ar, as_, aa, xa, mod, w_ret, w_swa, w_ga, wout, layer, g2, wrt, br,
           n_lat_tiles, n_tiles):
    nb, nt, d = xa.shape
    rows = n_tiles * TOKEN_TILE
    n_steps = nb * n_tiles
    kern = functools.partial(_merge_kernel, n_lat_tiles=n_lat_tiles, n_tiles=n_tiles, n_batch=nb)

    def cur(s):
        t = jnp.minimum(s, n_steps - 1)
        return t // n_tiles, t % n_tiles

    def prev(s):
        t = jnp.maximum(s - 1, 0)
        return t // n_tiles, t % n_tiles

    tile = pl.BlockSpec((1, TOKEN_TILE, d), lambda s: (*cur(s), 0))
    whole = lambda arr: pl.BlockSpec(arr.shape, lambda s: (0,) * arr.ndim)
    square = pl.BlockSpec((1, d, d), lambda s: (layer, 0, 0))
    small = pl.BlockSpec((1, 2, TOKEN_TILE), lambda s: (prev(s)[0], 0, prev(s)[1]))
    return pl.pallas_call(
        kern,
        out_shape=[jax.ShapeDtypeStruct((nb, nt, d), F32),
                   jax.ShapeDtypeStruct((nb, rows * SUBLANES, LANES), F32),
                   jax.ShapeDtypeStruct((nb, 2, rows), jnp.int32),
                   jax.ShapeDtypeStruct((nb, 2, rows), jnp.int32),
                   jax.ShapeDtypeStruct((nb, 2, rows), F32),
                   jax.ShapeDtypeStruct((N_EXPERTS, LANES), F32)],
        grid=(n_steps + 1,),
        in_specs=[tile, tile, tile, tile, tile, tile, tile, whole(mod), square, square, square, square,
                  pl.BlockSpec((1, d), lambda s: (0, 0)), whole(wrt), whole(br)],
        out_specs=[tile,
                   pl.BlockSpec((1, TOKEN_TILE * SUBLANES, LANES), lambda s: (*cur(s), 0)),
                   small, small, small,
                   pl.BlockSpec((N_EXPERTS, LANES), lambda s: (0, 0))],
        scratch_shapes=[pltpu.VMEM((TOKEN_TILE, d), F32), pltpu.VMEM((N_EXPERTS, LANES), F32)],
        input_output_aliases={6: 0},
        compiler_params=_params(dimension_semantics=("arbitrary",)),
        name="merge_route",
    )(oret, oswa, oga, ar, as_, aa, xa, mod, w_ret, w_swa, w_ga, wout, g2.reshape(1, d), wrt, br)


def _row_copy(src, dst, sem):
    return pltpu.make_async_copy(src, dst, sem)


def _tile_rows(r):
    return pl.ds(pl.multiple_of(r * SUBLANES, SUBLANES), SUBLANES)


def _expert_kernel(be_ref, nv_ref, src_ref, next_ref, h2t_hbm, wg_ref, wu_ref, wd_ref, y_ref,
                   xbuf, sems, wg_bf, wu_bf, wd_bf):
    i = pl.program_id(0)
    n_used = nv_ref[0]
    slot = lax.rem(i, 2)

    def gather(idx_ref, into):
        def issue(r2, carry):
            for p in range(2):
                r = 2 * r2 + p
                tok = idx_ref[0, 0, r]
                _row_copy(h2t_hbm.at[_tile_rows(tok), :], xbuf.at[into, _tile_rows(r), :],
                          sems.at[into]).start(priority=p)
            return carry

        lax.fori_loop(0, MOE_ROWS // 2, issue, 0, unroll=8)

    @pl.when(i == 0)
    def _():
        gather(src_ref, 0)

    @pl.when(i + 1 < n_used)
    def _():
        gather(next_ref, 1 - slot)

    @pl.when(i < n_used)
    def _():
        def drain(r, carry):
            _row_copy(h2t_hbm.at[_tile_rows(0), :], xbuf.at[slot, _tile_rows(0), :], sems.at[slot]).wait()
            return carry

        lax.fori_loop(0, MOE_ROWS, drain, 0, unroll=8)

        @pl.when((i == 0) | (be_ref[i] != be_ref[jnp.maximum(i - 1, 0)]))
        def _():
            wg_bf[...] = wg_ref[0, 0].astype(BF16)
            wu_bf[...] = wu_ref[0, 0].astype(BF16)
            wd_bf[...] = wd_ref[0, 0].astype(BF16)

        x = _load_row_tiles(xbuf.at[slot], MOE_ROWS).astype(BF16)
        hid = _silu(_dot(x, wg_bf[...])) * _dot(x, wu_bf[...])
        _store_row_tiles(y_ref, _dot(hid.astype(BF16), wd_bf[...]))

    @pl.when(i >= n_used)
    def _():
        y_ref[...] = jnp.zeros_like(y_ref)


def _experts(block_e, n_valid, src_blocks, h2t, wg, wu, wd, layer):
    n_blocks = src_blocks.shape[0]
    d = wg.shape[2]
    used = lambda i, nv: jnp.maximum(jnp.minimum(i, nv[0] - 1), 0)
    idx = lambda shift: pl.BlockSpec((1, 1, MOE_ROWS), lambda i, be, nv: (used(i + shift, nv), 0, 0),
                                     memory_space=pltpu.SMEM)
    grid_spec = pltpu.PrefetchScalarGridSpec(
        num_scalar_prefetch=2,
        grid=(n_blocks,),
        in_specs=[idx(0), idx(1),
                  pl.BlockSpec(memory_space=pl.ANY),
                  pl.BlockSpec((1, 1, d, D_EXPERT), lambda i, be, nv: (layer, be[used(i, nv)], 0, 0)),
                  pl.BlockSpec((1, 1, d, D_EXPERT), lambda i, be, nv: (layer, be[used(i, nv)], 0, 0)),
                  pl.BlockSpec((1, 1, D_EXPERT, d), lambda i, be, nv: (layer, be[used(i, nv)], 0, 0))],
        out_specs=pl.BlockSpec((MOE_ROWS * SUBLANES, LANES), lambda i, be, nv: (i, 0)),
        scratch_shapes=[pltpu.VMEM((2, MOE_ROWS * SUBLANES, LANES), F32), pltpu.SemaphoreType.DMA((2,)),
                        pltpu.VMEM((d, D_EXPERT), BF16), pltpu.VMEM((d, D_EXPERT), BF16),
                        pltpu.VMEM((D_EXPERT, d), BF16)],
    )
    return pl.pallas_call(
        _expert_kernel,
        out_shape=jax.ShapeDtypeStruct((n_blocks * MOE_ROWS * SUBLANES, LANES), F32),
        grid_spec=grid_spec,
        compiler_params=_params(dimension_semantics=("arbitrary",)),
        name="moe_experts",
    )(block_e, n_valid, src_blocks, src_blocks, h2t, wg, wu, wd)


def _combine_kernel(dest_ref, next_ref, y_hbm, wt_ref, x_ref, mod_ref, g_ref, *rest,
                    n_lat_tiles, n_tiles, n_batch, last):
    if last:
        out_ref, gbuf, sems = rest
    else:
        xo_ref, h_ref, gbuf, sems = rest
    t = pl.program_id(0)
    slot = lax.rem(t, 2)

    def gather(idx_ref, into):
        def issue(r, carry):
            for k in range(2):
                d = idx_ref[0, 0, k * TOKEN_TILE + r]
                _row_copy(y_hbm.at[_tile_rows(d), :], gbuf.at[into, k, _tile_rows(r), :],
                          sems.at[into]).start(priority=k)
            return carry

        lax.fori_loop(0, TOKEN_TILE, issue, 0, unroll=8)

    @pl.when(t == 0)
    def _():
        gather(dest_ref, 0)

    @pl.when(t + 1 < n_batch * n_tiles)
    def _():
        gather(next_ref, 1 - slot)

    def drain(r, carry):
        for k in range(2):
            _row_copy(y_hbm.at[_tile_rows(0), :], gbuf.at[slot, 0, _tile_rows(0), :], sems.at[slot]).wait()
        return carry

    lax.fori_loop(0, TOKEN_TILE, drain, 0, unroll=8)

    b = t // n_tiles
    m = mod_ref[pl.ds(jnp.where(t - b * n_tiles < n_lat_tiles, b, n_batch), 1), :]
    wt = wt_ref[0]
    moe = (_load_row_tiles(gbuf.at[slot, 0], TOKEN_TILE) * wt[:, 0:1]
           + _load_row_tiles(gbuf.at[slot, 1], TOKEN_TILE) * wt[:, 1:2])
    x = x_ref[0] + _mod_chunk(m, 5) * moe
    if last:
        out_ref[0] = _rms_normalize(x) * g_ref[...]
    else:
        xo_ref[0] = x
        h = _rms_normalize(x) * g_ref[...] * (1.0 + _mod_chunk(m, 7)) + _mod_chunk(m, 6)
        h_ref[0] = h.astype(BF16)


def _combine(dest_tiles, y, wt, xa, mod, g, n_lat_tiles, n_tiles, last):
    nb, nt, d = xa.shape
    rows = n_tiles * TOKEN_TILE
    n_steps = nb * n_tiles
    kern = functools.partial(_combine_kernel, n_lat_tiles=n_lat_tiles, n_tiles=n_tiles, n_batch=nb, last=last)
    tile = pl.BlockSpec((1, TOKEN_TILE, d), lambda t: (t // n_tiles, t % n_tiles, 0))
    if last:
        out_shape = jax.ShapeDtypeStruct((nb, rows, d), F32)
        out_specs = tile
        aliases = {}
    else:
        out_shape = [jax.ShapeDtypeStruct((nb, nt, d), F32), jax.ShapeDtypeStruct((nb, nt, d), BF16)]
        out_specs = [tile, tile]
        aliases = {4: 0}
    return pl.pallas_call(
        kern,
        out_shape=out_shape,
        grid=(n_steps,),
        in_specs=[pl.BlockSpec((1, 1, 2 * TOKEN_TILE), lambda t: (t, 0, 0), memory_space=pltpu.SMEM),
                  pl.BlockSpec((1, 1, 2 * TOKEN_TILE), lambda t: (jnp.minimum(t + 1, n_steps - 1), 0, 0),
                               memory_space=pltpu.SMEM),
                  pl.BlockSpec(memory_space=pl.ANY),
                  pl.BlockSpec((1, TOKEN_TILE, 2), lambda t: (t // n_tiles, t % n_tiles, 0)),
                  tile,
                  pl.BlockSpec(mod.shape, lambda t: (0, 0)),
                  pl.BlockSpec((1, d), lambda t: (0, 0))],
        out_specs=out_specs,
        scratch_shapes=[pltpu.VMEM((2, 2, TOKEN_TILE * SUBLANES, LANES), F32), pltpu.SemaphoreType.DMA((2,))],
        input_output_aliases=aliases,
        compiler_params=_params(dimension_semantics=("arbitrary",)),
        name="moe_combine",
    )(dest_tiles, dest_tiles, y, wt, xa, mod, g.reshape(1, d))


def _slot_plan(experts, ranks, counts):
    nb, _, rows = experts.shape
    padded = (counts + MOE_ROWS - 1) // MOE_ROWS * MOE_ROWS
    pad_ends = jnp.cumsum(padded)
    pad_starts = pad_ends - padded
    one_hot = experts[..., None] == jnp.arange(N_EXPERTS, dtype=jnp.int32)
    dest = jnp.sum(jnp.where(one_hot, pad_starts, 0), axis=-1) + ranks
    n_assign = nb * 2 * rows
    n_blocks = (n_assign + N_EXPERTS * (MOE_ROWS - 1) + MOE_ROWS - 1) // MOE_ROWS
    first_slot = jnp.arange(n_blocks, dtype=jnp.int32) * MOE_ROWS
    block_e = jnp.minimum(jnp.sum((pad_ends[None, :] <= first_slot[:, None]).astype(jnp.int32), axis=1),
                          N_EXPERTS - 1).astype(jnp.int32)
    n_valid = (pad_ends[-1:] // MOE_ROWS).astype(jnp.int32)
    n_tiles = rows // TOKEN_TILE
    dest = dest.astype(jnp.int32)
    dest_tiles = (dest.reshape(nb, 2, n_tiles, TOKEN_TILE)
                  .transpose(0, 2, 1, 3).reshape(nb * n_tiles, 1, 2 * TOKEN_TILE))
    token = jnp.broadcast_to((jnp.arange(nb, dtype=jnp.int32)[:, None, None] * rows
                              + jnp.arange(rows, dtype=jnp.int32)[None, None, :]), dest.shape)
    src = jnp.zeros((n_blocks * MOE_ROWS,), jnp.int32).at[dest.reshape(-1)].set(
        token.reshape(-1), unique_indices=True)
    return dest_tiles, block_e, n_valid, src.reshape(n_blocks, 1, MOE_ROWS)


def _rope_tables(length, n_ctx):
    rows = length // GRID_W
    row = jnp.repeat(jnp.arange(rows, dtype=jnp.int32), GRID_W).astype(F32)
    col = jnp.tile(jnp.arange(GRID_W, dtype=jnp.int32), rows).astype(F32)

    def table(hd):
        quarter = hd // 4
        freqs = ROPE_THETA ** (-jnp.arange(quarter, dtype=F32) / quarter)
        ang_r = row[:, None] * freqs[None, :]
        ang_c = col[:, None] * freqs[None, :]
        cos = jnp.concatenate([jnp.cos(ang_r), jnp.cos(ang_r), jnp.cos(ang_c), jnp.cos(ang_c)], axis=-1)
        sin = jnp.concatenate([-jnp.sin(ang_r), jnp.sin(ang_r), -jnp.sin(ang_c), jnp.sin(ang_c)], axis=-1)
        cos = jnp.concatenate([cos, jnp.ones((n_ctx, hd), F32)], axis=0)
        sin = jnp.concatenate([sin, jnp.zeros((n_ctx, hd), F32)], axis=0)
        reps = LANES // hd
        return jnp.tile(cos, (1, reps)), jnp.tile(sin, (1, reps))

    c128, s128 = table(RET_DK)
    c64, s64 = table(SWA_HD)
    return c128, s128, c64, s64


def kernel(x, c, ctx, c_ctx, w_mod, b_mod, g_norm1, g_norm2, w_in, ret_decay_logit, swa_sink, g_qnorm, g_knorm,
           w_br_ret, w_br_swa, w_br_ga, w_out, w_router, b_router, w_gate, w_up, w_down, g_final):
    nb, length, d = x.shape
    n_ctx = ctx.shape[1]
    depth = w_mod.shape[0]
    nt = length + n_ctx
    n_lat_tiles = length // TOKEN_TILE
    n_all_tiles = nt // TOKEN_TILE
    assert GA_HD == RET_DK == LANES and 2 * SWA_HD == LANES
    assert length % TOKEN_TILE == 0 and n_ctx == TOKEN_TILE and length % n_ctx == 0

    mod_rows = 8
    c_rows = jnp.concatenate([c, c_ctx[None, :], jnp.zeros((mod_rows - nb - 1, d), F32)], axis=0)
    mods = _modulation(c_rows, w_mod, b_mod)
    tables = _rope_tables(length, n_ctx)
    wrt = w_router.astype(F32).T
    br = b_router.astype(F32).reshape(N_EXPERTS, 1)

    xa, h = _prenorm(x, ctx, mods[0], g_norm1[0])
    w_in_bf, w_ret_bf, w_swa_bf, w_ga_bf, w_out_bf = (
        w.astype(BF16) for w in (w_in, w_br_ret, w_br_swa, w_br_ga, w_out))
    out = None
    for l in range(depth):
        need_ctx = l < depth - 1
        n_tiles = n_all_tiles if need_ctx else n_lat_tiles
        (rq, rk, rv, ru, sq, sk, sv, gq, gk, gv, ar, as_, aa) = _inproj(
            h, w_in_bf, l, tables, g_qnorm[l], g_knorm[l])
        o_ret = _retention(ret_decay_logit[l].astype(F32), rq, rk, rv, ru, length, n_ctx, need_ctx)
        o_swa = _window_attention(swa_sink[l].astype(F32), sq, sk, sv, length, n_ctx, need_ctx)
        o_ga = _global_attention(gq, gk, gv, length, n_ctx, need_ctx)
        xa, h2t, experts, ranks, weights, counts = _merge(
            o_ret, o_swa, o_ga, ar, as_, aa, xa, mods[l], w_ret_bf, w_swa_bf, w_ga_bf, w_out_bf, l,
            g_norm2[l], wrt, br, n_lat_tiles, n_tiles)
        dest_tiles, block_e, n_valid, src_blocks = _slot_plan(experts, ranks, counts[:, 0].astype(jnp.int32))
        y = _experts(block_e, n_valid, src_blocks, h2t.reshape(-1, LANES), w_gate, w_up, w_down, l)
        wt = weights.transpose(0, 2, 1)
        if need_ctx:
            mod_pair = jnp.concatenate([mods[l], mods[l + 1][:, :2 * d]], axis=1)
            xa, h = _combine(dest_tiles, y, wt, xa, mod_pair, g_norm1[l + 1], n_lat_tiles, n_tiles, last=False)
        else:
            out = _combine(dest_tiles, y, wt, xa, mods[l], g_final, n_lat_tiles, n_tiles, last=True)
    return out
```

```python
import functools

import jax
import jax.numpy as jnp
from jax import lax
from jax.experimental import pallas as pl
from jax.experimental.pallas import tpu as pltpu

F32 = jnp.float32
BF16 = jnp.bfloat16

D_MODEL = 1024
GRID_W = 64
NORM_EPS = 1e-6
ROPE_THETA = 10000.0
RET_HEADS, RET_DK, RET_DV, RET_CHUNK = 4, 128, 256, 256
SWA_HEADS, SWA_KV_HEADS, SWA_HD, WINDOW = 16, 2, 64, 128
GA_HEADS, GA_KV_HEADS, GA_HD = 8, 2, 128
N_EXPERTS, N_GROUPS, EXPERTS_PER_GROUP, D_EXPERT = 16, 4, 4, 512

RET_QK = RET_HEADS * RET_DK
RET_V = RET_HEADS * RET_DV
SWA_Q = SWA_HEADS * SWA_HD
SWA_KV = SWA_KV_HEADS * SWA_HD
GA_Q = GA_HEADS * GA_HD
GA_KV = GA_KV_HEADS * GA_HD
SPLITS = (RET_QK, RET_QK, RET_V, RET_V, SWA_Q, SWA_KV, SWA_KV, GA_Q, GA_KV, GA_KV,
          D_MODEL, D_MODEL, D_MODEL)

LANES = 128
SUBLANES = 8
TOKEN_TILE = 256
QUERY_TILE = 128
GA_QUERY_TILE = 256
MOE_ROWS = 512
GA_KEY_CHUNK = 256
RET_HEADS_PER_STEP = 2
SWA_TILES_PER_CHUNK = 2
MOD_COLS = 1536
VMEM_LIMIT = 56 * 1024 * 1024
MASKED = -1e30
LOG2E = 1.4426950408889634

_W_WIDTHS = (RET_QK, RET_QK, RET_V, RET_V, SWA_Q, 2 * SWA_KV, 2 * SWA_KV, GA_Q, GA_KV, GA_KV,
             D_MODEL, D_MODEL, D_MODEL)
_IN_OFFS = tuple(sum(SPLITS[:i]) for i in range(len(SPLITS)))


def _dot(a, b):
    return jnp.dot(a, b, preferred_element_type=F32)


def _dot_nt(a, b):
    return lax.dot_general(a, b, (((1,), (1,)), ((), ())), preferred_element_type=F32)


def _dot_tn(a, b):
    return lax.dot_general(a, b, (((0,), (0,)), ((), ())), preferred_element_type=F32)


def _silu(x):
    return x * jax.nn.sigmoid(x)


def _rms_normalize(x):
    return x * lax.rsqrt(jnp.mean(x * x, axis=-1, keepdims=True) + NORM_EPS)


def _params(**kw):
    return pltpu.CompilerParams(vmem_limit_bytes=VMEM_LIMIT, **kw)


def _mod_kernel(c_ref, w_ref, b_ref, o_ref):
    a = _silu(c_ref[...])
    o_ref[0] = _dot(a.astype(BF16), w_ref[0].astype(BF16)) + b_ref[0]


def _modulation(c_rows, w_mod, b_mod):
    depth, d, n = w_mod.shape
    rows = c_rows.shape[0]
    return pl.pallas_call(
        _mod_kernel,
        out_shape=jax.ShapeDtypeStruct((depth, rows, n), F32),
        grid=(depth, n // MOD_COLS),
        in_specs=[pl.BlockSpec((rows, d), lambda l, j: (0, 0)),
                  pl.BlockSpec((1, d, MOD_COLS), lambda l, j: (l, 0, j)),
                  pl.BlockSpec((1, 1, MOD_COLS), lambda l, j: (l, 0, j))],
        out_specs=pl.BlockSpec((1, rows, MOD_COLS), lambda l, j: (l, 0, j)),
        compiler_params=_params(),
        name="modulation",
    )(c_rows, w_mod, b_mod.reshape(depth, 1, n))


def _mod_row(mod_ref, n_lat_tiles, n_batch):
    b = pl.program_id(0)
    i = pl.program_id(1)
    r = jnp.where(i < n_lat_tiles, b, n_batch)
    return mod_ref[pl.ds(r, 1), :]


def _mod_chunk(m, k):
    return m[:, k * D_MODEL:(k + 1) * D_MODEL]


def _prenorm_kernel(x_ref, c_ref, mod_ref, g_ref, xa_ref, h_ref, *, n_lat_tiles, n_batch):
    m = _mod_row(mod_ref, n_lat_tiles, n_batch)
    x = jnp.where(pl.program_id(1) < n_lat_tiles, x_ref[0], c_ref[0])
    xa_ref[0] = x
    h = _rms_normalize(x) * g_ref[...] * (1.0 + _mod_chunk(m, 1)) + _mod_chunk(m, 0)
    h_ref[0] = h.astype(BF16)


def _prenorm(x, ctx, mod, g):
    nb, length, d = x.shape
    n_lat_tiles = length // TOKEN_TILE
    nt = length + ctx.shape[1]
    kern = functools.partial(_prenorm_kernel, n_lat_tiles=n_lat_tiles, n_batch=nb)
    tile = pl.BlockSpec((1, TOKEN_TILE, d), lambda b, i: (b, i, 0))
    return pl.pallas_call(
        kern,
        out_shape=[jax.ShapeDtypeStruct((nb, nt, d), F32), jax.ShapeDtypeStruct((nb, nt, d), BF16)],
        grid=(nb, nt // TOKEN_TILE),
        in_specs=[pl.BlockSpec((1, TOKEN_TILE, d), lambda b, i: (b, jnp.minimum(i, n_lat_tiles - 1), 0)),
                  pl.BlockSpec((1, TOKEN_TILE, d), lambda b, i: (b, 0, 0)),
                  pl.BlockSpec(mod.shape, lambda b, i: (0, 0)),
                  pl.BlockSpec((1, d), lambda b, i: (0, 0))],
        out_specs=[tile, tile],
        compiler_params=_params(),
        name="prenorm",
    )(x, ctx, mod, g.reshape(1, d))


def _rope(x, cos, sin, quarter):
    lane = lax.broadcasted_iota(jnp.int32, x.shape, 1)
    first = (lane % (2 * quarter)) < quarter
    partner = jnp.where(first, pltpu.roll(x, LANES - quarter, 1), pltpu.roll(x, quarter, 1))
    return x * cos + partner * sin


def _inproj_kernel(h_ref, w_ref, c128_ref, s128_ref, c64_ref, s64_ref, gq_ref, gk_ref,
                   rq, rk, rv, ru, sq, sk, sv, gq, gk, gv, ar, as_, aa):
    h = h_ref[0]
    c128, s128 = c128_ref[...], s128_ref[...]
    c64, s64 = c64_ref[...], s64_ref[...]

    def proj(idx):
        return _dot(h, w_ref[0, :, _IN_OFFS[idx]:_IN_OFFS[idx] + SPLITS[idx]])

    def slabs(acc):
        return [acc[:, s * LANES:(s + 1) * LANES] for s in range(acc.shape[1] // LANES)]

    def store(ref, s, val):
        ref[0, :, s * LANES:(s + 1) * LANES] = val.astype(ref.dtype)

    def head_per_slab(acc):
        low = lax.broadcasted_iota(jnp.int32, acc.shape, 1) < SWA_HD
        other = pltpu.roll(acc, SWA_HD, 1)
        return [jnp.where(low, acc, other), jnp.where(low, other, acc)]

    for s, xs in enumerate(slabs(proj(0))):
        store(rq, s, _rope(xs, c128, s128, RET_DK // 4))
    for s, xs in enumerate(slabs(proj(1))):
        store(rk, s, _rope(xs, c128, s128, RET_DK // 4) * (RET_DK ** -0.5))
    rv[0] = proj(2).astype(rv.dtype)
    ru[0] = proj(3).astype(ru.dtype)
    for s, xs in enumerate(slabs(proj(4))):
        store(sq, s, _rope(xs, c64, s64, SWA_HD // 4) * (SWA_HD ** -0.5 * LOG2E))
    for s, xs in enumerate(head_per_slab(proj(5))):
        store(sk, s, _rope(xs, c64, s64, SWA_HD // 4))
    for s, xs in enumerate(head_per_slab(proj(6))):
        store(sv, s, xs)
    for s, xs in enumerate(slabs(proj(7))):
        xn = _rms_normalize(xs) * gq_ref[...]
        store(gq, s, _rope(xn, c128, s128, GA_HD // 4) * (GA_HD ** -0.5 * LOG2E))
    for s, xs in enumerate(slabs(proj(8))):
        xn = _rms_normalize(xs) * gk_ref[...]
        store(gk, s, _rope(xn, c128, s128, GA_HD // 4))
    gv[0] = proj(9).astype(gv.dtype)
    ar[0] = proj(10).astype(ar.dtype)
    as_[0] = proj(11).astype(as_.dtype)
    aa[0] = proj(12).astype(aa.dtype)


def _inproj(h, w_in, layer, tables, g_q, g_k):
    nb, nt, d = h.shape
    tile = lambda width: pl.BlockSpec((1, TOKEN_TILE, width), lambda b, i: (b, i, 0))
    tab = pl.BlockSpec((TOKEN_TILE, LANES), lambda b, i: (i, 0))
    vec = pl.BlockSpec((1, LANES), lambda b, i: (0, 0))
    outs = [jax.ShapeDtypeStruct((nb, nt, w), BF16) for w in _W_WIDTHS]
    return pl.pallas_call(
        _inproj_kernel,
        out_shape=outs,
        grid=(nb, nt // TOKEN_TILE),
        in_specs=[tile(d),
                  pl.BlockSpec((1, d, w_in.shape[2]), lambda b, i: (layer, 0, 0), pipeline_mode=pl.Buffered(1)),
                  tab, tab, tab, tab, vec, vec],
        out_specs=[tile(w) for w in _W_WIDTHS],
        compiler_params=_params(),
        name="inproj",
    )(h, w_in, *tables, g_q.reshape(1, LANES), g_k.reshape(1, LANES))


def _log_sigmoid(x):
    return jnp.minimum(x, 0.0) - jnp.log1p(jnp.exp(-jnp.abs(x)))


def _ret_kernel(logit_ref, q_ref, k_ref, v_ref, u_ref, o_ref, acc_ref, sf_ref, sb_ref,
                *, length, n_ctx, need_ctx):
    chunk = RET_CHUNK
    n_chunks = length // chunk
    half = n_chunks // 2

    def rows_f32(shape):
        return lax.broadcasted_iota(jnp.int32, shape, 0).astype(F32)

    def finish(o, u):
        return (_rms_normalize(o) * _silu(u.astype(F32))).astype(o_ref.dtype)

    ci = rows_f32((chunk, RET_DK))
    heads = []
    for hi in range(RET_HEADS_PER_STEP):
        hh = pl.program_id(1) * RET_HEADS_PER_STEP + hi
        lgf = _log_sigmoid(jnp.full((1, 1), logit_ref[0, hh], F32))
        lgb = _log_sigmoid(jnp.full((1, 1), logit_ref[1, hh], F32))
        qk = slice(hi * RET_DK, (hi + 1) * RET_DK)
        vu = slice(hi * RET_DV, (hi + 1) * RET_DV)

        def both_ways_decay(n, lgf=lgf, lgb=lgb):
            diff = rows_f32((n, n)) - lax.broadcasted_iota(jnp.int32, (n, n), 1).astype(F32)
            return jnp.where(diff >= 0, jnp.exp(jnp.maximum(diff, 0.0) * lgf),
                             jnp.exp(jnp.maximum(-diff, 0.0) * lgb))

        kx = k_ref[0, length:length + n_ctx, qk].astype(F32)
        vx = v_ref[0, length:length + n_ctx, vu]
        lx = rows_f32((n_ctx, RET_DK))
        sf_ref[hi] = _dot_tn((kx * jnp.exp((n_ctx - 1.0 - lx) * lgf)).astype(BF16), vx)
        sb_ref[hi] = _dot_tn((kx * jnp.exp(lx * lgb)).astype(BF16), vx)
        if need_ctx:
            qx = q_ref[0, length:length + n_ctx, qk]
            sx = _dot_nt(qx, kx.astype(BF16)) * both_ways_decay(n_ctx)
            o_ref[0, length:length + n_ctx, vu] = finish(_dot(sx.astype(BF16), vx),
                                                         u_ref[0, length:length + n_ctx, vu])
        heads.append(dict(
            qk=qk, vu=vu, intra=both_ways_decay(chunk),
            q_dec_f=jnp.exp((ci + 1.0) * lgf), k_dec_f=jnp.exp((chunk - 1.0 - ci) * lgf),
            q_dec_b=jnp.exp((chunk - ci) * lgb), k_dec_b=jnp.exp(ci * lgb),
            chunk_dec_f=jnp.exp(chunk * lgf), chunk_dec_b=jnp.exp(chunk * lgb)))

    def sweep(s, first_touch):
        rf = pl.ds(pl.multiple_of(s * chunk, chunk), chunk)
        rb = pl.ds(pl.multiple_of((n_chunks - 1 - s) * chunk, chunk), chunk)
        for hi, hd in enumerate(heads):
            qk, vu = hd["qk"], hd["vu"]
            qf, kf, vf = q_ref[0, rf, qk], k_ref[0, rf, qk], v_ref[0, rf, vu]
            sc = _dot_nt(qf, kf) * hd["intra"]
            state_f = sf_ref[hi]
            o_f = (_dot(sc.astype(BF16), vf)
                   + _dot((qf.astype(F32) * hd["q_dec_f"]).astype(BF16), state_f.astype(BF16)))
            sf_ref[hi] = state_f * hd["chunk_dec_f"] + _dot_tn((kf.astype(F32) * hd["k_dec_f"]).astype(BF16), vf)
            qb, kb, vb = q_ref[0, rb, qk], k_ref[0, rb, qk], v_ref[0, rb, vu]
            state_b = sb_ref[hi]
            o_b = _dot((qb.astype(F32) * hd["q_dec_b"]).astype(BF16), state_b.astype(BF16))
            sb_ref[hi] = state_b * hd["chunk_dec_b"] + _dot_tn((kb.astype(F32) * hd["k_dec_b"]).astype(BF16), vb)
            if first_touch:
                acc_ref[hi, rf, :] = o_f
                acc_ref[hi, rb, :] = o_b
            else:
                o_ref[0, rf, vu] = finish(acc_ref[hi, rf, :] + o_f, u_ref[0, rf, vu])
                o_ref[0, rb, vu] = finish(acc_ref[hi, rb, :] + o_b, u_ref[0, rb, vu])

    def first_half(s, carry):
        sweep(s, True)
        return carry

    def second_half(s, carry):
        sweep(s, False)
        return carry

    lax.fori_loop(0, half, first_half, 0)
    lax.fori_loop(half, n_chunks, second_half, 0)


def _retention(logit, rq, rk, rv, ru, length, n_ctx, need_ctx):
    nb, nt, _ = rq.shape
    assert (length // RET_CHUNK) % 2 == 0 and RET_HEADS % RET_HEADS_PER_STEP == 0
    out_rows = nt if need_ctx else length
    kern = functools.partial(_ret_kernel, length=length, n_ctx=n_ctx, need_ctx=need_ctx)
    head = lambda width: pl.BlockSpec((1, nt, RET_HEADS_PER_STEP * width), lambda b, h: (b, 0, h))
    return pl.pallas_call(
        kern,
        out_shape=jax.ShapeDtypeStruct((nb, out_rows, RET_V), BF16),
        grid=(nb, RET_HEADS // RET_HEADS_PER_STEP),
        in_specs=[pl.BlockSpec(memory_space=pltpu.SMEM),
                  head(RET_DK), head(RET_DK), head(RET_DV), head(RET_DV)],
        out_specs=pl.BlockSpec((1, out_rows, RET_HEADS_PER_STEP * RET_DV), lambda b, h: (b, 0, h)),
        scratch_shapes=[pltpu.VMEM((RET_HEADS_PER_STEP, length, RET_DV), F32),
                        pltpu.VMEM((RET_HEADS_PER_STEP, RET_DK, RET_DV), F32),
                        pltpu.VMEM((RET_HEADS_PER_STEP, RET_DK, RET_DV), F32)],
        compiler_params=_params(),
        name="retention",
    )(logit, rq, rk, rv, ru)


def _swa_group(q_ref, g, kcat, vcat, biases, sink_ref, o_ref):
    tq = q_ref.shape[1]
    heads_per_group = SWA_HEADS // SWA_KV_HEADS
    slabs_per_group = heads_per_group // 2
    low = lax.broadcasted_iota(jnp.int32, (tq, LANES), 1) < SWA_HD
    zero = jnp.zeros((tq, LANES), q_ref.dtype)
    stacked = []
    for a in range(slabs_per_group):
        s0 = (g * slabs_per_group + a) * LANES
        slab = q_ref[0, :, s0:s0 + LANES]
        stacked.append(jnp.where(low, slab, zero))
        stacked.append(jnp.where(low, zero, slab))
    heads_per_part = heads_per_group // 2
    rows = heads_per_part * tq
    n_key_tiles = kcat.shape[0] // LANES
    for part in range(2):
        q = jnp.concatenate(stacked[part * heads_per_part:(part + 1) * heads_per_part], axis=0)
        sink = jnp.concatenate(
            [jnp.full((tq, LANES), sink_ref[g * heads_per_group + part * heads_per_part + h] * LOG2E, F32)
             for h in range(heads_per_part)], axis=0)
        m = l = acc = None
        for t0 in range(0, n_key_tiles, SWA_TILES_PER_CHUNK):
            t1 = min(t0 + SWA_TILES_PER_CHUNK, n_key_tiles)
            s = _dot_nt(q, kcat[t0 * LANES:t1 * LANES])
            tiles = [s[:, t * LANES:(t + 1) * LANES] for t in range(t1 - t0)]
            for t in range(t0, t1):
                if t in biases:
                    tiles[t - t0] = tiles[t - t0] + biases[t][:rows]
            mc = jnp.broadcast_to(jnp.max(functools.reduce(jnp.maximum, tiles), axis=-1, keepdims=True),
                                  (rows, LANES))
            m_new = jnp.maximum(mc, sink if m is None else m)
            ps = [jnp.exp2(tile - m_new) for tile in tiles]
            pv = _dot(jnp.concatenate(ps, axis=1).astype(BF16), vcat[t0 * LANES:t1 * LANES])
            psum = functools.reduce(jnp.add, ps)
            if m is None:
                l, acc = psum, pv
            else:
                alpha = jnp.exp2(m - m_new)
                l, acc = alpha * l + psum, alpha * acc + pv
            m = m_new
        den = jnp.broadcast_to(jnp.sum(l, axis=-1, keepdims=True), (rows, LANES)) + jnp.exp2(sink - m)
        o = acc / den
        for a in range(heads_per_part // 2):
            s0 = (g * slabs_per_group + part * (heads_per_part // 2) + a) * LANES
            even = o[(2 * a) * tq:(2 * a + 1) * tq]
            odd = o[(2 * a + 1) * tq:(2 * a + 2) * tq]
            o_ref[0, :, s0:s0 + LANES] = jnp.where(low, even, odd).astype(o_ref.dtype)


def _swa_kernel(sink_ref, q_ref, kp_ref, kc_ref, kn_ref, kx_ref, vp_ref, vc_ref, vn_ref, vx_ref, o_ref,
                *, n_lat_tiles, need_ctx):
    j = pl.program_id(1)
    tq = q_ref.shape[1]
    n_ctx = kx_ref.shape[1]

    def lanes_of(ref, g):
        return ref[0, :, g * LANES:(g + 1) * LANES]

    @pl.when(j < n_lat_tiles)
    def _latent():
        ci = lax.broadcasted_iota(jnp.int32, (tq, tq), 1)
        ri = lax.broadcasted_iota(jnp.int32, (tq, tq), 0)
        far = 4 * tq
        ri_prev = ri + jnp.where(j > 0, 0, far)
        ri_next = ri - jnp.where(j < n_lat_tiles - 1, 0, far)
        zero = jnp.zeros((tq, tq), F32)
        masked = jnp.full((tq, tq), MASKED, F32)
        heads_per_part = SWA_HEADS // SWA_KV_HEADS // 2
        bias = {0: jnp.concatenate([jnp.where(ci >= ri_prev, zero, masked)] * heads_per_part, axis=0),
                2: jnp.concatenate([jnp.where(ci <= ri_next, zero, masked)] * heads_per_part, axis=0)}
        for g in range(SWA_KV_HEADS):
            kcat = jnp.concatenate([lanes_of(kp_ref, g), lanes_of(kc_ref, g), lanes_of(kn_ref, g),
                                    lanes_of(kx_ref, g)], axis=0)
            vcat = jnp.concatenate([lanes_of(vp_ref, g), lanes_of(vc_ref, g), lanes_of(vn_ref, g),
                                    lanes_of(vx_ref, g)], axis=0)
            _swa_group(q_ref, g, kcat, vcat, bias, sink_ref, o_ref)

    if need_ctx:
        @pl.when(j >= n_lat_tiles)
        def _context():
            for g in range(SWA_KV_HEADS):
                _swa_group(q_ref, g, lanes_of(kx_ref, g), lanes_of(vx_ref, g), {}, sink_ref, o_ref)


def _window_attention(sink, sq, sk, sv, length, n_ctx, need_ctx):
    nb, nt, _ = sq.shape
    tq = QUERY_TILE
    n_lat = length // tq
    out_rows = nt if need_ctx else length
    kvw = 2 * SWA_KV
    kern = functools.partial(_swa_kernel, n_lat_tiles=n_lat, need_ctx=need_ctx)
    prev = pl.BlockSpec((1, tq, kvw), lambda b, j: (b, jnp.clip(j - 1, 0, n_lat - 1), 0))
    cur = pl.BlockSpec((1, tq, kvw), lambda b, j: (b, jnp.minimum(j, n_lat - 1), 0))
    nxt = pl.BlockSpec((1, tq, kvw), lambda b, j: (b, jnp.minimum(j + 1, n_lat - 1), 0))
    ctx = pl.BlockSpec((1, n_ctx, kvw), lambda b, j: (b, length // n_ctx, 0))
    return pl.pallas_call(
        kern,
        out_shape=jax.ShapeDtypeStruct((nb, out_rows, SWA_Q), BF16),
        grid=(nb, out_rows // tq),
        in_specs=[pl.BlockSpec(memory_space=pltpu.SMEM),
                  pl.BlockSpec((1, tq, SWA_Q), lambda b, j: (b, j, 0)),
                  prev, cur, nxt, ctx, prev, cur, nxt, ctx],
        out_specs=pl.BlockSpec((1, tq, SWA_Q), lambda b, j: (b, j, 0)),
        compiler_params=_params(),
        name="window_attention",
    )(sink, sq, sk, sk, sk, sk, sv, sv, sv, sv)


def _ga_kernel(q_ref, k_ref, v_ref, o_ref, *, length, n_ctx, n_lat_tiles, need_ctx):
    j = pl.program_id(2)
    tq = q_ref.shape[1]
    heads_per_group = GA_HEADS // GA_KV_HEADS
    q = jnp.concatenate([q_ref[0, :, h * LANES:(h + 1) * LANES] for h in range(heads_per_group)], axis=0)

    rows = heads_per_group * tq

    def attend(key_lo, key_hi):
        m = l = acc = None
        for c0 in range(key_lo, key_hi, GA_KEY_CHUNK):
            c1 = min(c0 + GA_KEY_CHUNK, key_hi)
            s = _dot_nt(q, k_ref[0, c0:c1, :])
            parts = [s[:, t * LANES:(t + 1) * LANES] for t in range((c1 - c0) // LANES)]
            lane_max = functools.reduce(jnp.maximum, parts)
            mc = jnp.broadcast_to(jnp.max(lane_max, axis=-1, keepdims=True), (rows, LANES))
            m_new = mc if m is None else jnp.maximum(m, mc)
            ps = [jnp.exp2(part - m_new) for part in parts]
            pv = _dot(jnp.concatenate(ps, axis=1).astype(BF16), v_ref[0, c0:c1, :])
            psum = functools.reduce(jnp.add, ps)
            if m is None:
                l, acc = psum, pv
            else:
                alpha = jnp.exp2(m - m_new)
                l, acc = alpha * l + psum, alpha * acc + pv
            m = m_new
        o = acc / jnp.sum(l, axis=-1, keepdims=True)
        for h in range(heads_per_group):
            o_ref[0, :, h * LANES:(h + 1) * LANES] = o[h * tq:(h + 1) * tq].astype(o_ref.dtype)

    @pl.when(j < n_lat_tiles)
    def _latent():
        attend(0, length + n_ctx)

    if need_ctx:
        @pl.when(j >= n_lat_tiles)
        def _context():
            attend(length, length + n_ctx)


def _global_attention(gq, gk, gv, length, n_ctx, need_ctx):
    nb, nt, _ = gq.shape
    tq = GA_QUERY_TILE
    out_rows = nt if need_ctx else length
    group_w = GA_Q // GA_KV_HEADS
    kern = functools.partial(_ga_kernel, length=length, n_ctx=n_ctx, n_lat_tiles=length // tq,
                             need_ctx=need_ctx)
    kv = pl.BlockSpec((1, nt, GA_HD), lambda b, g, j: (b, 0, g))
    return pl.pallas_call(
        kern,
        out_shape=jax.ShapeDtypeStruct((nb, out_rows, GA_Q), BF16),
        grid=(nb, GA_KV_HEADS, out_rows // tq),
        in_specs=[pl.BlockSpec((1, tq, group_w), lambda b, g, j: (b, j, g)), kv, kv],
        out_specs=pl.BlockSpec((1, tq, group_w), lambda b, g, j: (b, j, g)),
        compiler_params=_params(),
        name="global_attention",
    )(gq, gk, gv)


def _route(scores, biased):
    rows = [biased[r:r + 1, :] for r in range(N_EXPERTS)]
    raw = [scores[r:r + 1, :] for r in range(N_EXPERTS)]

    def top2_sum(vals):
        best = None
        for a in range(len(vals)):
            for b in range(a + 1, len(vals)):
                pair = vals[a] + vals[b]
                best = pair if best is None else jnp.maximum(best, pair)
        return best

    group_scores = [top2_sum(rows[g * EXPERTS_PER_GROUP:(g + 1) * EXPERTS_PER_GROUP]) for g in range(N_GROUPS)]
    group = jnp.zeros_like(group_scores[0], dtype=jnp.int32)
    best = group_scores[0]
    for g in range(1, N_GROUPS):
        better = group_scores[g] > best
        group = jnp.where(better, g, group)
        best = jnp.where(better, group_scores[g], best)

    def in_group(table, k):
        val = table[k]
        for g in range(1, N_GROUPS):
            val = jnp.where(group == g, table[g * EXPERTS_PER_GROUP + k], val)
        return val

    vals = [in_group(rows, k) for k in range(EXPERTS_PER_GROUP)]
    unbiased = [in_group(raw, k) for k in range(EXPERTS_PER_GROUP)]

    def first_argmax(cands):
        idx = jnp.zeros_like(group)
        top = cands[0]
        for k in range(1, len(cands)):
            better = cands[k] > top
            idx = jnp.where(better, k, idx)
            top = jnp.where(better, cands[k], top)
        return idx

    i1 = first_argmax(vals)
    i2 = first_argmax([jnp.where(i1 == k, -jnp.inf, vals[k]) for k in range(EXPERTS_PER_GROUP)])

    def pick(idx):
        val = unbiased[0]
        for k in range(1, EXPERTS_PER_GROUP):
            val = jnp.where(idx == k, unbiased[k], val)
        return val

    s1, s2 = pick(i1), pick(i2)
    total = s1 + s2
    experts = jnp.concatenate([group * EXPERTS_PER_GROUP + i1, group * EXPERTS_PER_GROUP + i2], axis=0)
    weights = jnp.concatenate([s1 / total, s2 / total], axis=0)
    return experts, weights


def _slot_ranks(experts, carry):
    tm = experts.shape[1]
    expert_id = lax.broadcasted_iota(jnp.int32, (N_EXPERTS, tm), 0)
    upper = jnp.where(lax.broadcasted_iota(jnp.int32, (tm, tm), 0) <= lax.broadcasted_iota(jnp.int32, (tm, tm), 1),
                      1.0, 0.0).astype(BF16)
    ranks = []
    for k in range(2):
        hit = jnp.where(expert_id == experts[k:k + 1, :], 1.0, 0.0)
        inclusive = _dot(hit.astype(BF16), upper)
        ranks.append(jnp.sum(hit * (carry + inclusive - 1.0), axis=0, keepdims=True))
        carry = carry + jnp.sum(hit, axis=1, keepdims=True)
    return jnp.concatenate(ranks, axis=0).astype(jnp.int32), carry


def _store_row_tiles(ref, val):
    rows = val.shape[0]
    for j in range(val.shape[1] // LANES):
        ref[pl.ds(j, rows, stride=SUBLANES), :] = val[:, j * LANES:(j + 1) * LANES]


def _load_row_tiles(ref, rows):
    return jnp.concatenate([ref[pl.ds(j, rows, stride=SUBLANES), :] for j in range(ref.shape[0] // rows)], axis=1)


def _merge_kernel(oret, oswa, oga, ar, as_, aa, x_ref, mod_ref, wr_ref, ws_ref, wa_ref, wout_ref, g2_ref, wrt_ref, br_ref,
                  xo_ref, h2t_ref, e_ref, r_ref, w_ref, cnt_ref, h2_prev, carry_ref,
                  *, n_lat_tiles, n_tiles, n_batch):
    s = pl.program_id(0)
    n_steps = n_batch * n_tiles

    def merge_tile():
        b = s // n_tiles
        i = s - b * n_tiles
        m = mod_ref[pl.ds(jnp.where(i < n_lat_tiles, b, n_batch), 1), :]

        def gate(a_ref):
            return jax.nn.sigmoid(a_ref[0].astype(F32))

        y = (gate(ar) * _dot(oret[0], wr_ref[0]) + gate(as_) * _dot(oswa[0], ws_ref[0])
             + gate(aa) * _dot(oga[0], wa_ref[0]))
        x = x_ref[0] + _mod_chunk(m, 2) * _dot(y.astype(BF16), wout_ref[0])
        xo_ref[0] = x
        h2 = _rms_normalize(x) * g2_ref[...] * (1.0 + _mod_chunk(m, 4)) + _mod_chunk(m, 3)
        _store_row_tiles(h2t_ref.at[0], h2)
        h2_prev[...] = h2

    def route_previous_tile():
        h2 = h2_prev[...]
        h_hi = h2.astype(BF16)
        h_lo = (h2 - h_hi.astype(F32)).astype(BF16)
        w = wrt_ref[...]
        w_hi = w.astype(BF16)
        w_lo = (w - w_hi.astype(F32)).astype(BF16)
        logits = _dot_nt(w_hi, h_hi) + _dot_nt(w_hi, h_lo) + _dot_nt(w_lo, h_hi)
        scores = jax.nn.sigmoid(logits)
        experts, weights = _route(scores, scores + br_ref[...])
        ranks, carry = _slot_ranks(experts, carry_ref[:, 0:1])
        e_ref[0] = experts
        r_ref[0] = ranks
        w_ref[0] = weights
        carry_ref[...] = jnp.broadcast_to(carry, carry_ref.shape)
        cnt_ref[...] = jnp.broadcast_to(carry, cnt_ref.shape)

    @pl.when(s == 0)
    def _init():
        carry_ref[...] = jnp.zeros_like(carry_ref)

    @pl.when(s > 0)
    def _routing_step():
        route_previous_tile()

    @pl.when(s < n_steps)
    def _merge_step():
        merge_tile()


def _merge(oret, oswa, oga, ar, as_, aa, xa, mod, w_ret, w_swa, w_ga, wout, layer, g2, wrt, br,
           n_lat_tiles, n_tiles):
    nb, nt, d = xa.shape
    rows = n_tiles * TOKEN_TILE
    n_steps = nb * n_tiles
    kern = functools.partial(_merge_kernel, n_lat_tiles=n_lat_tiles, n_tiles=n_tiles, n_batch=nb)

    def cur(s):
        t = jnp.minimum(s, n_steps - 1)
        return t // n_tiles, t % n_tiles

    def prev(s):
        t = jnp.maximum(s - 1, 0)
        return t // n_tiles, t % n_tiles

    tile = pl.BlockSpec((1, TOKEN_TILE, d), lambda s: (*cur(s), 0))
    whole = lambda arr: pl.BlockSpec(arr.shape, lambda s: (0,) * arr.ndim)
    square = pl.BlockSpec((1, d, d), lambda s: (layer, 0, 0))
    small = pl.BlockSpec((1, 2, TOKEN_TILE), lambda s: (prev(s)[0], 0, prev(s)[1]))
    return pl.pallas_call(
        kern,
        out_shape=[jax.ShapeDtypeStruct((nb, nt, d), F32),
                   jax.ShapeDtypeStruct((nb, rows * SUBLANES, LANES), F32),
                   jax.ShapeDtypeStruct((nb, 2, rows), jnp.int32),
                   jax.ShapeDtypeStruct((nb, 2, rows), jnp.int32),
                   jax.ShapeDtypeStruct((nb, 2, rows), F32),
                   jax.ShapeDtypeStruct((N_EXPERTS, LANES), F32)],
        grid=(n_steps + 1,),
        in_specs=[tile, tile, tile, tile, tile, tile, tile, whole(mod), square, square, square, square,
                  pl.BlockSpec((1, d), lambda s: (0, 0)), whole(wrt), whole(br)],
        out_specs=[tile,
                   pl.BlockSpec((1, TOKEN_TILE * SUBLANES, LANES), lambda s: (*cur(s), 0)),
                   small, small, small,
                   pl.BlockSpec((N_EXPERTS, LANES), lambda s: (0, 0))],
        scratch_shapes=[pltpu.VMEM((TOKEN_TILE, d), F32), pltpu.VMEM((N_EXPERTS, LANES), F32)],
        input_output_aliases={6: 0},
        compiler_params=_params(dimension_semantics=("arbitrary",)),
        name="merge_route",
    )(oret, oswa, oga, ar, as_, aa, xa, mod, w_ret, w_swa, w_ga, wout, g2.reshape(1, d), wrt, br)


def _row_copy(src, dst, sem):
    return pltpu.make_async_copy(src, dst, sem)


def _tile_rows(r):
    return pl.ds(pl.multiple_of(r * SUBLANES, SUBLANES), SUBLANES)


def _dispatch_kernel(tail_ref, nv_ref, dest_ref, h2t_ref, buf_out, zeros_ref, sem, tail_sem):
    @pl.when((pl.program_id(0) == 0) & (pl.program_id(1) == 0))
    def _zero_tails():
        zeros_ref[...] = jnp.zeros_like(zeros_ref)

        def zero_block(first_slot):
            rows = pl.ds(pl.multiple_of(first_slot * SUBLANES, SUBLANES), MOE_ROWS * SUBLANES)
            copy = pltpu.make_async_copy(zeros_ref, buf_out.at[rows, :], tail_sem)
            copy.start()
            copy.wait()

        for e in range(N_EXPERTS):
            @pl.when(tail_ref[e] >= 0)
            def _():
                zero_block(tail_ref[e])

        def unused(blk, carry):
            zero_block(blk * MOE_ROWS)
            return carry

        lax.fori_loop(nv_ref[0], buf_out.shape[0] // (MOE_ROWS * SUBLANES), unused, 0)

    def issue(r, carry):
        for k in range(2):
            d = dest_ref[0, 0, k * TOKEN_TILE + r]
            _row_copy(h2t_ref.at[0, _tile_rows(r), :], buf_out.at[_tile_rows(d), :], sem).start(priority=k)
        return carry

    lax.fori_loop(0, TOKEN_TILE, issue, 0, unroll=8)

    def drain(r, carry):
        for k in range(2):
            _row_copy(h2t_ref.at[0, _tile_rows(0), :], buf_out.at[_tile_rows(0), :], sem).wait()
        return carry

    lax.fori_loop(0, TOKEN_TILE, drain, 0, unroll=8)


def _dispatch(tail_start, n_valid, dest_tiles, h2t, n_slots):
    nb, rows8, _ = h2t.shape
    n_tiles = rows8 // (TOKEN_TILE * SUBLANES)
    grid_spec = pltpu.PrefetchScalarGridSpec(
        num_scalar_prefetch=2,
        grid=(nb, n_tiles),
        in_specs=[pl.BlockSpec((1, 1, 2 * TOKEN_TILE), lambda b, i, tail, nv: (b * n_tiles + i, 0, 0),
                               memory_space=pltpu.SMEM),
                  pl.BlockSpec((1, TOKEN_TILE * SUBLANES, LANES), lambda b, i, tail, nv: (b, i, 0))],
        out_specs=pl.BlockSpec(memory_space=pl.ANY),
        scratch_shapes=[pltpu.VMEM((MOE_ROWS * SUBLANES, LANES), F32),
                        pltpu.SemaphoreType.DMA(()), pltpu.SemaphoreType.DMA(())],
    )
    return pl.pallas_call(
        _dispatch_kernel,
        out_shape=jax.ShapeDtypeStruct((n_slots * SUBLANES, LANES), F32),
        grid_spec=grid_spec,
        compiler_params=_params(has_side_effects=True, dimension_semantics=("arbitrary", "arbitrary")),
        name="moe_dispatch",
    )(tail_start, n_valid, dest_tiles, h2t)


def _expert_kernel(be_ref, nv_ref, x_ref, wg_ref, wu_ref, wd_ref, y_ref, wg_bf, wu_bf, wd_bf):
    i = pl.program_id(0)

    @pl.when(i < nv_ref[0])
    def _():
        @pl.when((i == 0) | (be_ref[i] != be_ref[jnp.maximum(i - 1, 0)]))
        def _():
            wg_bf[...] = wg_ref[0, 0].astype(BF16)
            wu_bf[...] = wu_ref[0, 0].astype(BF16)
            wd_bf[...] = wd_ref[0, 0].astype(BF16)

        x = _load_row_tiles(x_ref, MOE_ROWS).astype(BF16)
        hid = _silu(_dot(x, wg_bf[...])) * _dot(x, wu_bf[...])
        _store_row_tiles(y_ref, _dot(hid.astype(BF16), wd_bf[...]))

    @pl.when(i >= nv_ref[0])
    def _():
        y_ref[...] = jnp.zeros_like(y_ref)


def _experts(block_e, n_valid, buf, wg, wu, wd, layer):
    n_slots = buf.shape[0] // SUBLANES
    d = wg.shape[2]
    used = lambda i, nv: jnp.maximum(jnp.minimum(i, nv[0] - 1), 0)
    rows = pl.BlockSpec((MOE_ROWS * SUBLANES, LANES), lambda i, be, nv: (used(i, nv), 0))
    grid_spec = pltpu.PrefetchScalarGridSpec(
        num_scalar_prefetch=2,
        grid=(n_slots // MOE_ROWS,),
        in_specs=[rows,
                  pl.BlockSpec((1, 1, d, D_EXPERT), lambda i, be, nv: (layer, be[used(i, nv)], 0, 0)),
                  pl.BlockSpec((1, 1, d, D_EXPERT), lambda i, be, nv: (layer, be[used(i, nv)], 0, 0)),
                  pl.BlockSpec((1, 1, D_EXPERT, d), lambda i, be, nv: (layer, be[used(i, nv)], 0, 0))],
        out_specs=pl.BlockSpec((MOE_ROWS * SUBLANES, LANES), lambda i, be, nv: (i, 0)),
        scratch_shapes=[pltpu.VMEM((d, D_EXPERT), BF16), pltpu.VMEM((d, D_EXPERT), BF16),
                        pltpu.VMEM((D_EXPERT, d), BF16)],
    )
    return pl.pallas_call(
        _expert_kernel,
        out_shape=jax.ShapeDtypeStruct((n_slots * SUBLANES, LANES), F32),
        grid_spec=grid_spec,
        compiler_params=_params(dimension_semantics=("arbitrary",)),
        name="moe_experts",
    )(block_e, n_valid, buf, wg, wu, wd)


def _combine_kernel(dest_ref, next_ref, y_hbm, wt_ref, x_ref, mod_ref, g_ref, *rest,
                    n_lat_tiles, n_tiles, n_batch, last):
    if last:
        out_ref, gbuf, sems = rest
    else:
        xo_ref, h_ref, gbuf, sems = rest
    t = pl.program_id(0)
    slot = lax.rem(t, 2)

    def gather(idx_ref, into):
        def issue(r, carry):
            for k in range(2):
                d = idx_ref[0, 0, k * TOKEN_TILE + r]
                _row_copy(y_hbm.at[_tile_rows(d), :], gbuf.at[into, k, _tile_rows(r), :],
                          sems.at[into]).start(priority=k)
            return carry

        lax.fori_loop(0, TOKEN_TILE, issue, 0, unroll=8)

    @pl.when(t == 0)
    def _():
        gather(dest_ref, 0)

    @pl.when(t + 1 < n_batch * n_tiles)
    def _():
        gather(next_ref, 1 - slot)

    def drain(r, carry):
        for k in range(2):
            _row_copy(y_hbm.at[_tile_rows(0), :], gbuf.at[slot, 0, _tile_rows(0), :], sems.at[slot]).wait()
        return carry

    lax.fori_loop(0, TOKEN_TILE, drain, 0, unroll=8)

    b = t // n_tiles
    m = mod_ref[pl.ds(jnp.where(t - b * n_tiles < n_lat_tiles, b, n_batch), 1), :]
    wt = wt_ref[0]
    moe = (_load_row_tiles(gbuf.at[slot, 0], TOKEN_TILE) * wt[:, 0:1]
           + _load_row_tiles(gbuf.at[slot, 1], TOKEN_TILE) * wt[:, 1:2])
    x = x_ref[0] + _mod_chunk(m, 5) * moe
    if last:
        out_ref[0] = _rms_normalize(x) * g_ref[...]
    else:
        xo_ref[0] = x
        h = _rms_normalize(x) * g_ref[...] * (1.0 + _mod_chunk(m, 7)) + _mod_chunk(m, 6)
        h_ref[0] = h.astype(BF16)


def _combine(dest_tiles, y, wt, xa, mod, g, n_lat_tiles, n_tiles, last):
    nb, nt, d = xa.shape
    rows = n_tiles * TOKEN_TILE
    n_steps = nb * n_tiles
    kern = functools.partial(_combine_kernel, n_lat_tiles=n_lat_tiles, n_tiles=n_tiles, n_batch=nb, last=last)
    tile = pl.BlockSpec((1, TOKEN_TILE, d), lambda t: (t // n_tiles, t % n_tiles, 0))
    if last:
        out_shape = jax.ShapeDtypeStruct((nb, rows, d), F32)
        out_specs = tile
        aliases = {}
    else:
        out_shape = [jax.ShapeDtypeStruct((nb, nt, d), F32), jax.ShapeDtypeStruct((nb, nt, d), BF16)]
        out_specs = [tile, tile]
        aliases = {4: 0}
    return pl.pallas_call(
        kern,
        out_shape=out_shape,
        grid=(n_steps,),
        in_specs=[pl.BlockSpec((1, 1, 2 * TOKEN_TILE), lambda t: (t, 0, 0), memory_space=pltpu.SMEM),
                  pl.BlockSpec((1, 1, 2 * TOKEN_TILE), lambda t: (jnp.minimum(t + 1, n_steps - 1), 0, 0),
                               memory_space=pltpu.SMEM),
                  pl.BlockSpec(memory_space=pl.ANY),
                  pl.BlockSpec((1, TOKEN_TILE, 2), lambda t: (t // n_tiles, t % n_tiles, 0)),
                  tile,
                  pl.BlockSpec(mod.shape, lambda t: (0, 0)),
                  pl.BlockSpec((1, d), lambda t: (0, 0))],
        out_specs=out_specs,
        scratch_shapes=[pltpu.VMEM((2, 2, TOKEN_TILE * SUBLANES, LANES), F32), pltpu.SemaphoreType.DMA((2,))],
        input_output_aliases=aliases,
        compiler_params=_params(dimension_semantics=("arbitrary",)),
        name="moe_combine",
    )(dest_tiles, dest_tiles, y, wt, xa, mod, g.reshape(1, d))


def _slot_plan(experts, ranks, counts):
    nb, _, rows = experts.shape
    padded = (counts + MOE_ROWS - 1) // MOE_ROWS * MOE_ROWS
    pad_ends = jnp.cumsum(padded)
    pad_starts = pad_ends - padded
    one_hot = experts[..., None] == jnp.arange(N_EXPERTS, dtype=jnp.int32)
    dest = jnp.sum(jnp.where(one_hot, pad_starts, 0), axis=-1) + ranks
    n_assign = nb * 2 * rows
    n_blocks = (n_assign + N_EXPERTS * (MOE_ROWS - 1) + MOE_ROWS - 1) // MOE_ROWS
    first_slot = jnp.arange(n_blocks, dtype=jnp.int32) * MOE_ROWS
    block_e = jnp.minimum(jnp.sum((pad_ends[None, :] <= first_slot[:, None]).astype(jnp.int32), axis=1),
                          N_EXPERTS - 1).astype(jnp.int32)
    n_valid = (pad_ends[-1:] // MOE_ROWS).astype(jnp.int32)
    tail_start = jnp.where(padded > 0, pad_ends - MOE_ROWS, -1).astype(jnp.int32)
    n_tiles = rows // TOKEN_TILE
    dest_tiles = (dest.astype(jnp.int32).reshape(nb, 2, n_tiles, TOKEN_TILE)
                  .transpose(0, 2, 1, 3).reshape(nb * n_tiles, 1, 2 * TOKEN_TILE))
    return dest_tiles, block_e, n_valid, tail_start, n_blocks * MOE_ROWS


def _rope_tables(length, n_ctx):
    rows = length // GRID_W
    row = jnp.repeat(jnp.arange(rows, dtype=jnp.int32), GRID_W).astype(F32)
    col = jnp.tile(jnp.arange(GRID_W, dtype=jnp.int32), rows).astype(F32)

    def table(hd):
        quarter = hd // 4
        freqs = ROPE_THETA ** (-jnp.arange(quarter, dtype=F32) / quarter)
        ang_r = row[:, None] * freqs[None, :]
        ang_c = col[:, None] * freqs[None, :]
        cos = jnp.concatenate([jnp.cos(ang_r), jnp.cos(ang_r), jnp.cos(ang_c), jnp.cos(ang_c)], axis=-1)
        sin = jnp.concatenate([-jnp.sin(ang_r), jnp.sin(ang_r), -jnp.sin(ang_c), jnp.sin(ang_c)], axis=-1)
        cos = jnp.concatenate([cos, jnp.ones((n_ctx, hd), F32)], axis=0)
        sin = jnp.concatenate([sin, jnp.zeros((n_ctx, hd), F32)], axis=0)
        reps = LANES // hd
        return jnp.tile(cos, (1, reps)), jnp.tile(sin, (1, reps))

    c128, s128 = table(RET_DK)
    c64, s64 = table(SWA_HD)
    return c128, s128, c64, s64


def kernel(x, c, ctx, c_ctx, w_mod, b_mod, g_norm1, g_norm2, w_in, ret_decay_logit, swa_sink, g_qnorm, g_knorm,
           w_br_ret, w_br_swa, w_br_ga, w_out, w_router, b_router, w_gate, w_up, w_down, g_final):
    nb, length, d = x.shape
    n_ctx = ctx.shape[1]
    depth = w_mod.shape[0]
    nt = length + n_ctx
    n_lat_tiles = length // TOKEN_TILE
    n_all_tiles = nt // TOKEN_TILE
    assert GA_HD == RET_DK == LANES and 2 * SWA_HD == LANES
    assert length % TOKEN_TILE == 0 and n_ctx == TOKEN_TILE and length % n_ctx == 0

    mod_rows = 8
    c_rows = jnp.concatenate([c, c_ctx[None, :], jnp.zeros((mod_rows - nb - 1, d), F32)], axis=0)
    mods = _modulation(c_rows, w_mod, b_mod)
    tables = _rope_tables(length, n_ctx)
    wrt = w_router.astype(F32).T
    br = b_router.astype(F32).reshape(N_EXPERTS, 1)

    xa, h = _prenorm(x, ctx, mods[0], g_norm1[0])
    w_in_bf, w_ret_bf, w_swa_bf, w_ga_bf, w_out_bf = (
        w.astype(BF16) for w in (w_in, w_br_ret, w_br_swa, w_br_ga, w_out))
    out = None
    for l in range(depth):
        need_ctx = l < depth - 1
        n_tiles = n_all_tiles if need_ctx else n_lat_tiles
        (rq, rk, rv, ru, sq, sk, sv, gq, gk, gv, ar, as_, aa) = _inproj(
            h, w_in_bf, l, tables, g_qnorm[l], g_knorm[l])
        o_ret = _retention(ret_decay_logit[l].astype(F32), rq, rk, rv, ru, length, n_ctx, need_ctx)
        o_swa = _window_attention(swa_sink[l].astype(F32), sq, sk, sv, length, n_ctx, need_ctx)
        o_ga = _global_attention(gq, gk, gv, length, n_ctx, need_ctx)
        xa, h2t, experts, ranks, weights, counts = _merge(
            o_ret, o_swa, o_ga, ar, as_, aa, xa, mods[l], w_ret_bf, w_swa_bf, w_ga_bf, w_out_bf, l,
            g_norm2[l], wrt, br, n_lat_tiles, n_tiles)
        dest_tiles, block_e, n_valid, tail_start, n_slots = _slot_plan(
            experts, ranks, counts[:, 0].astype(jnp.int32))
        buf = _dispatch(tail_start, n_valid, dest_tiles, h2t, n_slots)
        y = _experts(block_e, n_valid, buf, w_gate, w_up, w_down, l)
        wt = weights.transpose(0, 2, 1)
        if need_ctx:
            mod_pair = jnp.concatenate([mods[l], mods[l + 1][:, :2 * d]], axis=1)
            xa, h = _combine(dest_tiles, y, wt, xa, mod_pair, g_norm1[l + 1], n_lat_tiles, n_tiles, last=False)
        else:
            out = _combine(dest_tiles, y, wt, xa, mods[l], g_final, n_lat_tiles, n_tiles, last=True)
    return out
```

```python
import functools

import jax
import jax.numpy as jnp
from jax import lax
from jax.experimental import pallas as pl
from jax.experimental.pallas import tpu as pltpu

F32 = jnp.float32
BF16 = jnp.bfloat16

D_MODEL = 1024
GRID_W = 64
NORM_EPS = 1e-6
ROPE_THETA = 10000.0
RET_HEADS, RET_DK, RET_DV, RET_CHUNK = 4, 128, 256, 256
SWA_HEADS, SWA_KV_HEADS, SWA_HD, WINDOW = 16, 2, 64, 128
GA_HEADS, GA_KV_HEADS, GA_HD = 8, 2, 128
N_EXPERTS, N_GROUPS, EXPERTS_PER_GROUP, D_EXPERT = 16, 4, 4, 512

RET_QK = RET_HEADS * RET_DK
RET_V = RET_HEADS * RET_DV
SWA_Q = SWA_HEADS * SWA_HD
SWA_KV = SWA_KV_HEADS * SWA_HD
GA_Q = GA_HEADS * GA_HD
GA_KV = GA_KV_HEADS * GA_HD
SPLITS = (RET_QK, RET_QK, RET_V, RET_V, SWA_Q, SWA_KV, SWA_KV, GA_Q, GA_KV, GA_KV,
          D_MODEL, D_MODEL, D_MODEL)

LANES = 128
SUBLANES = 8
TOKEN_TILE = 256
QUERY_TILE = 128
GA_QUERY_TILE = 256
MOE_ROWS = 512
GA_KEY_CHUNK = 256
RET_HEADS_PER_STEP = 2
SWA_PARTS = 2
SWA_TILES_PER_CHUNK = 2
MOD_COLS = 1536
VMEM_LIMIT = 56 * 1024 * 1024
MASKED = -1e30
LOG2E = 1.4426950408889634

_W_WIDTHS = (RET_QK, RET_QK, RET_V, RET_V, SWA_Q, 2 * SWA_KV, 2 * SWA_KV, GA_Q, GA_KV, GA_KV,
             D_MODEL, D_MODEL, D_MODEL)
_IN_OFFS = tuple(sum(SPLITS[:i]) for i in range(len(SPLITS)))


def _dot(a, b):
    return jnp.dot(a, b, preferred_element_type=F32)


def _dot_nt(a, b):
    return lax.dot_general(a, b, (((1,), (1,)), ((), ())), preferred_element_type=F32)


def _dot_tn(a, b):
    return lax.dot_general(a, b, (((0,), (0,)), ((), ())), preferred_element_type=F32)


def _silu(x):
    return x * jax.nn.sigmoid(x)


def _rms_normalize(x):
    return x * lax.rsqrt(jnp.mean(x * x, axis=-1, keepdims=True) + NORM_EPS)


def _params(**kw):
    return pltpu.CompilerParams(vmem_limit_bytes=VMEM_LIMIT, **kw)


def _mod_kernel(c_ref, w_ref, b_ref, o_ref):
    a = _silu(c_ref[...])
    o_ref[0] = _dot(a.astype(BF16), w_ref[0].astype(BF16)) + b_ref[0]


def _modulation(c_rows, w_mod, b_mod):
    depth, d, n = w_mod.shape
    rows = c_rows.shape[0]
    return pl.pallas_call(
        _mod_kernel,
        out_shape=jax.ShapeDtypeStruct((depth, rows, n), F32),
        grid=(depth, n // MOD_COLS),
        in_specs=[pl.BlockSpec((rows, d), lambda l, j: (0, 0)),
                  pl.BlockSpec((1, d, MOD_COLS), lambda l, j: (l, 0, j)),
                  pl.BlockSpec((1, 1, MOD_COLS), lambda l, j: (l, 0, j))],
        out_specs=pl.BlockSpec((1, rows, MOD_COLS), lambda l, j: (l, 0, j)),
        compiler_params=_params(),
        name="modulation",
    )(c_rows, w_mod, b_mod.reshape(depth, 1, n))


def _mod_row(mod_ref, n_lat_tiles, n_batch):
    b = pl.program_id(0)
    i = pl.program_id(1)
    r = jnp.where(i < n_lat_tiles, b, n_batch)
    return mod_ref[pl.ds(r, 1), :]


def _mod_chunk(m, k):
    return m[:, k * D_MODEL:(k + 1) * D_MODEL]


def _rope(x, cos, sin, quarter):
    lane = lax.broadcasted_iota(jnp.int32, x.shape, 1)
    first = (lane % (2 * quarter)) < quarter
    partner = jnp.where(first, pltpu.roll(x, LANES - quarter, 1), pltpu.roll(x, quarter, 1))
    return x * cos + partner * sin


def _first_layer_tokens(x_ref, c_ref, i, n_lat_tiles):
    return jnp.where(i < n_lat_tiles, x_ref[0], c_ref[0])


def _inproj_kernel(*refs, first_layer, n_lat_tiles, n_batch):
    if first_layer:
        x_ref, c_ref, mod_ref, g_ref, *refs = refs
        m = _mod_row(mod_ref, n_lat_tiles, n_batch)
        x = _first_layer_tokens(x_ref, c_ref, pl.program_id(1), n_lat_tiles)
        h = (_rms_normalize(x) * g_ref[...] * (1.0 + _mod_chunk(m, 1)) + _mod_chunk(m, 0)).astype(BF16)
    else:
        h_ref, *refs = refs
        h = h_ref[0]
    (w_ref, c128_ref, s128_ref, c64_ref, s64_ref, gq_ref, gk_ref,
     rq, rk, rv, ru, sq, sk, sv, gq, gk, gv, ar, as_, aa) = refs
    c128, s128 = c128_ref[...], s128_ref[...]
    c64, s64 = c64_ref[...], s64_ref[...]

    def proj(idx):
        return _dot(h, w_ref[0, :, _IN_OFFS[idx]:_IN_OFFS[idx] + SPLITS[idx]])

    def slabs(acc):
        return [acc[:, s * LANES:(s + 1) * LANES] for s in range(acc.shape[1] // LANES)]

    def store(ref, s, val):
        ref[0, :, s * LANES:(s + 1) * LANES] = val.astype(ref.dtype)

    def head_per_slab(acc):
        low = lax.broadcasted_iota(jnp.int32, acc.shape, 1) < SWA_HD
        other = pltpu.roll(acc, SWA_HD, 1)
        return [jnp.where(low, acc, other), jnp.where(low, other, acc)]

    for s, xs in enumerate(slabs(proj(0))):
        store(rq, s, _rope(xs, c128, s128, RET_DK // 4))
    for s, xs in enumerate(slabs(proj(1))):
        store(rk, s, _rope(xs, c128, s128, RET_DK // 4) * (RET_DK ** -0.5))
    rv[0] = proj(2).astype(rv.dtype)
    ru[0] = proj(3).astype(ru.dtype)
    for s, xs in enumerate(slabs(proj(4))):
        store(sq, s, _rope(xs, c64, s64, SWA_HD // 4) * (SWA_HD ** -0.5 * LOG2E))
    for s, xs in enumerate(head_per_slab(proj(5))):
        store(sk, s, _rope(xs, c64, s64, SWA_HD // 4))
    for s, xs in enumerate(head_per_slab(proj(6))):
        store(sv, s, xs)
    for s, xs in enumerate(slabs(proj(7))):
        xn = _rms_normalize(xs) * gq_ref[...]
        store(gq, s, _rope(xn, c128, s128, GA_HD // 4) * (GA_HD ** -0.5 * LOG2E))
    for s, xs in enumerate(slabs(proj(8))):
        xn = _rms_normalize(xs) * gk_ref[...]
        store(gk, s, _rope(xn, c128, s128, GA_HD // 4))
    gv[0] = proj(9).astype(gv.dtype)
    ar[0] = proj(10).astype(ar.dtype)
    as_[0] = proj(11).astype(as_.dtype)
    aa[0] = proj(12).astype(aa.dtype)


def _first_layer_specs(x, ctx, n_lat_tiles, index):
    d = x.shape[2]
    return [pl.BlockSpec((1, TOKEN_TILE, d), lambda *ids: (index(*ids)[0],
                                                          jnp.minimum(index(*ids)[1], n_lat_tiles - 1), 0)),
            pl.BlockSpec((1, TOKEN_TILE, d), lambda *ids: (index(*ids)[0], 0, 0))]


def _inproj(tokens, w_in, layer, tables, g_q, g_k):
    first_layer = isinstance(tokens, tuple)
    tile = lambda width: pl.BlockSpec((1, TOKEN_TILE, width), lambda b, i: (b, i, 0))
    if first_layer:
        x, ctx, mod, g = tokens
        nb, length, d = x.shape
        nt = length + ctx.shape[1]
        n_lat_tiles = length // TOKEN_TILE
        lead_specs = _first_layer_specs(x, ctx, n_lat_tiles, lambda b, i: (b, i)) + [
            pl.BlockSpec(mod.shape, lambda b, i: (0, 0)), pl.BlockSpec((1, d), lambda b, i: (0, 0))]
        lead = (x, ctx, mod, g.reshape(1, d))
    else:
        nb, nt, d = tokens.shape
        n_lat_tiles = 0
        lead_specs = [tile(d)]
        lead = (tokens,)
    tab = pl.BlockSpec((TOKEN_TILE, LANES), lambda b, i: (i, 0))
    vec = pl.BlockSpec((1, LANES), lambda b, i: (0, 0))
    outs = [jax.ShapeDtypeStruct((nb, nt, w), BF16) for w in _W_WIDTHS]
    kern = functools.partial(_inproj_kernel, first_layer=first_layer, n_lat_tiles=n_lat_tiles, n_batch=nb)
    return pl.pallas_call(
        kern,
        out_shape=outs,
        grid=(nb, nt // TOKEN_TILE),
        in_specs=lead_specs + [
            pl.BlockSpec((1, d, w_in.shape[2]), lambda b, i: (layer, 0, 0), pipeline_mode=pl.Buffered(1)),
            tab, tab, tab, tab, vec, vec],
        out_specs=[tile(w) for w in _W_WIDTHS],
        compiler_params=_params(),
        name="inproj",
    )(*lead, w_in, *tables, g_q.reshape(1, LANES), g_k.reshape(1, LANES))


def _log_sigmoid(x):
    return jnp.minimum(x, 0.0) - jnp.log1p(jnp.exp(-jnp.abs(x)))


def _ret_kernel(logit_ref, q_ref, k_ref, v_ref, u_ref, o_ref, acc_ref, sf_ref, sb_ref,
                *, length, n_ctx, need_ctx):
    chunk = RET_CHUNK
    n_chunks = length // chunk
    half = n_chunks // 2

    def rows_f32(shape):
        return lax.broadcasted_iota(jnp.int32, shape, 0).astype(F32)

    def finish(o, u):
        return (_rms_normalize(o) * _silu(u.astype(F32))).astype(o_ref.dtype)

    ci = rows_f32((chunk, RET_DK))
    heads = []
    for hi in range(RET_HEADS_PER_STEP):
        hh = pl.program_id(1) * RET_HEADS_PER_STEP + hi
        lgf = _log_sigmoid(jnp.full((1, 1), logit_ref[0, hh], F32))
        lgb = _log_sigmoid(jnp.full((1, 1), logit_ref[1, hh], F32))
        qk = slice(hi * RET_DK, (hi + 1) * RET_DK)
        vu = slice(hi * RET_DV, (hi + 1) * RET_DV)

        def both_ways_decay(n, lgf=lgf, lgb=lgb):
            diff = rows_f32((n, n)) - lax.broadcasted_iota(jnp.int32, (n, n), 1).astype(F32)
            return jnp.where(diff >= 0, jnp.exp(jnp.maximum(diff, 0.0) * lgf),
                             jnp.exp(jnp.maximum(-diff, 0.0) * lgb))

        kx = k_ref[0, length:length + n_ctx, qk].astype(F32)
        vx = v_ref[0, length:length + n_ctx, vu]
        lx = rows_f32((n_ctx, RET_DK))
        sf_ref[hi] = _dot_tn((kx * jnp.exp((n_ctx - 1.0 - lx) * lgf)).astype(BF16), vx)
        sb_ref[hi] = _dot_tn((kx * jnp.exp(lx * lgb)).astype(BF16), vx)
        if need_ctx:
            qx = q_ref[0, length:length + n_ctx, qk]
            sx = _dot_nt(qx, kx.astype(BF16)) * both_ways_decay(n_ctx)
            o_ref[0, length:length + n_ctx, vu] = finish(_dot(sx.astype(BF16), vx),
                                                         u_ref[0, length:length + n_ctx, vu])
        heads.append(dict(
            qk=qk, vu=vu, intra=both_ways_decay(chunk),
            q_dec_f=jnp.exp((ci + 1.0) * lgf), k_dec_f=jnp.exp((chunk - 1.0 - ci) * lgf),
            q_dec_b=jnp.exp((chunk - ci) * lgb), k_dec_b=jnp.exp(ci * lgb),
            chunk_dec_f=jnp.exp(chunk * lgf), chunk_dec_b=jnp.exp(chunk * lgb)))

    def sweep(s, first_touch):
        rf = pl.ds(pl.multiple_of(s * chunk, chunk), chunk)
        rb = pl.ds(pl.multiple_of((n_chunks - 1 - s) * chunk, chunk), chunk)
        for hi, hd in enumerate(heads):
            qk, vu = hd["qk"], hd["vu"]
            qf, kf, vf = q_ref[0, rf, qk], k_ref[0, rf, qk], v_ref[0, rf, vu]
            sc = _dot_nt(qf, kf) * hd["intra"]
            state_f = sf_ref[hi]
            o_f = (_dot(sc.astype(BF16), vf)
                   + _dot((qf.astype(F32) * hd["q_dec_f"]).astype(BF16), state_f.astype(BF16)))
            sf_ref[hi] = state_f * hd["chunk_dec_f"] + _dot_tn((kf.astype(F32) * hd["k_dec_f"]).astype(BF16), vf)
            qb, kb, vb = q_ref[0, rb, qk], k_ref[0, rb, qk], v_ref[0, rb, vu]
            state_b = sb_ref[hi]
            o_b = _dot((qb.astype(F32) * hd["q_dec_b"]).astype(BF16), state_b.astype(BF16))
            sb_ref[hi] = state_b * hd["chunk_dec_b"] + _dot_tn((kb.astype(F32) * hd["k_dec_b"]).astype(BF16), vb)
            if first_touch:
                acc_ref[hi, rf, :] = o_f
                acc_ref[hi, rb, :] = o_b
            else:
                o_ref[0, rf, vu] = finish(acc_ref[hi, rf, :] + o_f, u_ref[0, rf, vu])
                o_ref[0, rb, vu] = finish(acc_ref[hi, rb, :] + o_b, u_ref[0, rb, vu])

    def first_half(s, carry):
        sweep(s, True)
        return carry

    def second_half(s, carry):
        sweep(s, False)
        return carry

    lax.fori_loop(0, half, first_half, 0)
    lax.fori_loop(half, n_chunks, second_half, 0)


def _retention(logit, rq, rk, rv, ru, length, n_ctx, need_ctx):
    nb, nt, _ = rq.shape
    assert (length // RET_CHUNK) % 2 == 0 and RET_HEADS % RET_HEADS_PER_STEP == 0
    out_rows = nt if need_ctx else length
    kern = functools.partial(_ret_kernel, length=length, n_ctx=n_ctx, need_ctx=need_ctx)
    head = lambda width: pl.BlockSpec((1, nt, RET_HEADS_PER_STEP * width), lambda b, h: (b, 0, h))
    return pl.pallas_call(
        kern,
        out_shape=jax.ShapeDtypeStruct((nb, out_rows, RET_V), BF16),
        grid=(nb, RET_HEADS // RET_HEADS_PER_STEP),
        in_specs=[pl.BlockSpec(memory_space=pltpu.SMEM),
                  head(RET_DK), head(RET_DK), head(RET_DV), head(RET_DV)],
        out_specs=pl.BlockSpec((1, out_rows, RET_HEADS_PER_STEP * RET_DV), lambda b, h: (b, 0, h)),
        scratch_shapes=[pltpu.VMEM((RET_HEADS_PER_STEP, length, RET_DV), F32),
                        pltpu.VMEM((RET_HEADS_PER_STEP, RET_DK, RET_DV), F32),
                        pltpu.VMEM((RET_HEADS_PER_STEP, RET_DK, RET_DV), F32)],
        compiler_params=_params(),
        name="retention",
    )(logit, rq, rk, rv, ru)


def _swa_group(q_ref, g, kcat, vcat, biases, sink_ref, o_ref):
    tq = q_ref.shape[1]
    heads_per_group = SWA_HEADS // SWA_KV_HEADS
    slabs_per_group = heads_per_group // 2
    low = lax.broadcasted_iota(jnp.int32, (tq, LANES), 1) < SWA_HD
    zero = jnp.zeros((tq, LANES), q_ref.dtype)
    stacked = []
    for a in range(slabs_per_group):
        s0 = (g * slabs_per_group + a) * LANES
        slab = q_ref[0, :, s0:s0 + LANES]
        stacked.append(jnp.where(low, slab, zero))
        stacked.append(jnp.where(low, zero, slab))
    heads_per_part = heads_per_group // SWA_PARTS
    rows = heads_per_part * tq
    n_key_tiles = kcat.shape[0] // LANES
    for part in range(SWA_PARTS):
        q = jnp.concatenate(stacked[part * heads_per_part:(part + 1) * heads_per_part], axis=0)
        sink = jnp.concatenate(
            [jnp.full((tq, LANES), sink_ref[g * heads_per_group + part * heads_per_part + h] * LOG2E, F32)
             for h in range(heads_per_part)], axis=0)
        m = l = acc = None
        for t0 in range(0, n_key_tiles, SWA_TILES_PER_CHUNK):
            t1 = min(t0 + SWA_TILES_PER_CHUNK, n_key_tiles)
            s = _dot_nt(q, kcat[t0 * LANES:t1 * LANES])
            tiles = [s[:, t * LANES:(t + 1) * LANES] for t in range(t1 - t0)]
            for t in range(t0, t1):
                if t in biases:
                    tiles[t - t0] = tiles[t - t0] + biases[t][:rows]
            mc = jnp.broadcast_to(jnp.max(functools.reduce(jnp.maximum, tiles), axis=-1, keepdims=True),
                                  (rows, LANES))
            m_new = jnp.maximum(mc, sink if m is None else m)
            ps = [jnp.exp2(tile - m_new) for tile in tiles]
            pv = _dot(jnp.concatenate(ps, axis=1).astype(BF16), vcat[t0 * LANES:t1 * LANES])
            psum = functools.reduce(jnp.add, ps)
            if m is None:
                l, acc = psum, pv
            else:
                alpha = jnp.exp2(m - m_new)
                l, acc = alpha * l + psum, alpha * acc + pv
            m = m_new
        den = jnp.broadcast_to(jnp.sum(l, axis=-1, keepdims=True), (rows, LANES)) + jnp.exp2(sink - m)
        o = acc / den
        for a in range(heads_per_part // 2):
            s0 = (g * slabs_per_group + part * (heads_per_part // 2) + a) * LANES
            even = o[(2 * a) * tq:(2 * a + 1) * tq]
            odd = o[(2 * a + 1) * tq:(2 * a + 2) * tq]
            o_ref[0, :, s0:s0 + LANES] = jnp.where(low, even, odd).astype(o_ref.dtype)


def _swa_kernel(sink_ref, q_ref, kp_ref, kc_ref, kn_ref, kx_ref, vp_ref, vc_ref, vn_ref, vx_ref, o_ref,
                *, n_lat_tiles, need_ctx):
    j = pl.program_id(1)
    tq = q_ref.shape[1]
    n_ctx = kx_ref.shape[1]

    def lanes_of(ref, g):
        return ref[0, :, g * LANES:(g + 1) * LANES]

    @pl.when(j < n_lat_tiles)
    def _latent():
        ci = lax.broadcasted_iota(jnp.int32, (tq, tq), 1)
        ri = lax.broadcasted_iota(jnp.int32, (tq, tq), 0)
        far = 4 * tq
        ri_prev = ri + jnp.where(j > 0, 0, far)
        ri_next = ri - jnp.where(j < n_lat_tiles - 1, 0, far)
        zero = jnp.zeros((tq, tq), F32)
        masked = jnp.full((tq, tq), MASKED, F32)
        heads_per_part = SWA_HEADS // SWA_KV_HEADS // SWA_PARTS
        bias = {0: jnp.concatenate([jnp.where(ci >= ri_prev, zero, masked)] * heads_per_part, axis=0),
                2: jnp.concatenate([jnp.where(ci <= ri_next, zero, masked)] * heads_per_part, axis=0)}
        for g in range(SWA_KV_HEADS):
            kcat = jnp.concatenate([lanes_of(kp_ref, g), lanes_of(kc_ref, g), lanes_of(kn_ref, g),
                                    lanes_of(kx_ref, g)], axis=0)
            vcat = jnp.concatenate([lanes_of(vp_ref, g), lanes_of(vc_ref, g), lanes_of(vn_ref, g),
                                    lanes_of(vx_ref, g)], axis=0)
            _swa_group(q_ref, g, kcat, vcat, bias, sink_ref, o_ref)

    if need_ctx:
        @pl.when(j >= n_lat_tiles)
        def _context():
            for g in range(SWA_KV_HEADS):
                _swa_group(q_ref, g, lanes_of(kx_ref, g), lanes_of(vx_ref, g), {}, sink_ref, o_ref)


def _window_attention(sink, sq, sk, sv, length, n_ctx, need_ctx):
    nb, nt, _ = sq.shape
    tq = QUERY_TILE
    n_lat = length // tq
    out_rows = nt if need_ctx else length
    kvw = 2 * SWA_KV
    kern = functools.partial(_swa_kernel, n_lat_tiles=n_lat, need_ctx=need_ctx)
    prev = pl.BlockSpec((1, tq, kvw), lambda b, j: (b, jnp.clip(j - 1, 0, n_lat - 1), 0))
    cur = pl.BlockSpec((1, tq, kvw), lambda b, j: (b, jnp.minimum(j, n_lat - 1), 0))
    nxt = pl.BlockSpec((1, tq, kvw), lambda b, j: (b, jnp.minimum(j + 1, n_lat - 1), 0))
    ctx = pl.BlockSpec((1, n_ctx, kvw), lambda b, j: (b, length // n_ctx, 0))
    return pl.pallas_call(
        kern,
        out_shape=jax.ShapeDtypeStruct((nb, out_rows, SWA_Q), BF16),
        grid=(nb, out_rows // tq),
        in_specs=[pl.BlockSpec(memory_space=pltpu.SMEM),
                  pl.BlockSpec((1, tq, SWA_Q), lambda b, j: (b, j, 0)),
                  prev, cur, nxt, ctx, prev, cur, nxt, ctx],
        out_specs=pl.BlockSpec((1, tq, SWA_Q), lambda b, j: (b, j, 0)),
        compiler_params=_params(),
        name="window_attention",
    )(sink, sq, sk, sk, sk, sk, sv, sv, sv, sv)


def _ga_kernel(q_ref, k_ref, v_ref, o_ref, *, length, n_ctx, n_lat_tiles, need_ctx):
    j = pl.program_id(2)
    tq = q_ref.shape[1]
    heads_per_group = GA_HEADS // GA_KV_HEADS
    q = jnp.concatenate([q_ref[0, :, h * LANES:(h + 1) * LANES] for h in range(heads_per_group)], axis=0)

    rows = heads_per_group * tq

    def attend(key_lo, key_hi):
        m = l = acc = None
        for c0 in range(key_lo, key_hi, GA_KEY_CHUNK):
            c1 = min(c0 + GA_KEY_CHUNK, key_hi)
            s = _dot_nt(q, k_ref[0, c0:c1, :])
            parts = [s[:, t * LANES:(t + 1) * LANES] for t in range((c1 - c0) // LANES)]
            lane_max = functools.reduce(jnp.maximum, parts)
            mc = jnp.broadcast_to(jnp.max(lane_max, axis=-1, keepdims=True), (rows, LANES))
            m_new = mc if m is None else jnp.maximum(m, mc)
            ps = [jnp.exp2(part - m_new) for part in parts]
            pv = _dot(jnp.concatenate(ps, axis=1).astype(BF16), v_ref[0, c0:c1, :])
            psum = functools.reduce(jnp.add, ps)
            if m is None:
                l, acc = psum, pv
            else:
                alpha = jnp.exp2(m - m_new)
                l, acc = alpha * l + psum, alpha * acc + pv
            m = m_new
        o = acc / jnp.sum(l, axis=-1, keepdims=True)
        for h in range(heads_per_group):
            o_ref[0, :, h * LANES:(h + 1) * LANES] = o[h * tq:(h + 1) * tq].astype(o_ref.dtype)

    @pl.when(j < n_lat_tiles)
    def _latent():
        attend(0, length + n_ctx)

    if need_ctx:
        @pl.when(j >= n_lat_tiles)
        def _context():
            attend(length, length + n_ctx)


def _global_attention(gq, gk, gv, length, n_ctx, need_ctx):
    nb, nt, _ = gq.shape
    tq = GA_QUERY_TILE
    out_rows = nt if need_ctx else length
    group_w = GA_Q // GA_KV_HEADS
    kern = functools.partial(_ga_kernel, length=length, n_ctx=n_ctx, n_lat_tiles=length // tq,
                             need_ctx=need_ctx)
    kv = pl.BlockSpec((1, nt, GA_HD), lambda b, g, j: (b, 0, g))
    return pl.pallas_call(
        kern,
        out_shape=jax.ShapeDtypeStruct((nb, out_rows, GA_Q), BF16),
        grid=(nb, GA_KV_HEADS, out_rows // tq),
        in_specs=[pl.BlockSpec((1, tq, group_w), lambda b, g, j: (b, j, g)), kv, kv],
        out_specs=pl.BlockSpec((1, tq, group_w), lambda b, g, j: (b, j, g)),
        compiler_params=_params(),
        name="global_attention",
    )(gq, gk, gv)


def _route(scores, biased):
    rows = [biased[r:r + 1, :] for r in range(N_EXPERTS)]
    raw = [scores[r:r + 1, :] for r in range(N_EXPERTS)]

    def top2_sum(vals):
        best = None
        for a in range(len(vals)):
            for b in range(a + 1, len(vals)):
                pair = vals[a] + vals[b]
                best = pair if best is None else jnp.maximum(best, pair)
        return best

    group_scores = [top2_sum(rows[g * EXPERTS_PER_GROUP:(g + 1) * EXPERTS_PER_GROUP]) for g in range(N_GROUPS)]
    group = jnp.zeros_like(group_scores[0], dtype=jnp.int32)
    best = group_scores[0]
    for g in range(1, N_GROUPS):
        better = group_scores[g] > best
        group = jnp.where(better, g, group)
        best = jnp.where(better, group_scores[g], best)

    def in_group(table, k):
        val = table[k]
        for g in range(1, N_GROUPS):
            val = jnp.where(group == g, table[g * EXPERTS_PER_GROUP + k], val)
        return val

    vals = [in_group(rows, k) for k in range(EXPERTS_PER_GROUP)]
    unbiased = [in_group(raw, k) for k in range(EXPERTS_PER_GROUP)]

    def first_argmax(cands):
        idx = jnp.zeros_like(group)
        top = cands[0]
        for k in range(1, len(cands)):
            better = cands[k] > top
            idx = jnp.where(better, k, idx)
            top = jnp.where(better, cands[k], top)
        return idx

    i1 = first_argmax(vals)
    i2 = first_argmax([jnp.where(i1 == k, -jnp.inf, vals[k]) for k in range(EXPERTS_PER_GROUP)])

    def pick(idx):
        val = unbiased[0]
        for k in range(1, EXPERTS_PER_GROUP):
            val = jnp.where(idx == k, unbiased[k], val)
        return val

    s1, s2 = pick(i1), pick(i2)
    total = s1 + s2
    experts = jnp.concatenate([group * EXPERTS_PER_GROUP + i1, group * EXPERTS_PER_GROUP + i2], axis=0)
    weights = jnp.concatenate([s1 / total, s2 / total], axis=0)
    return experts, weights


def _slot_ranks(experts, carry):
    tm = experts.shape[1]
    expert_id = lax.broadcasted_iota(jnp.int32, (N_EXPERTS, tm), 0)
    upper = jnp.where(lax.broadcasted_iota(jnp.int32, (tm, tm), 0) <= lax.broadcasted_iota(jnp.int32, (tm, tm), 1),
                      1.0, 0.0).astype(BF16)
    ranks = []
    for k in range(2):
        hit = jnp.where(expert_id == experts[k:k + 1, :], 1.0, 0.0)
        inclusive = _dot(hit.astype(BF16), upper)
        ranks.append(jnp.sum(hit * (carry + inclusive - 1.0), axis=0, keepdims=True))
        carry = carry + jnp.sum(hit, axis=1, keepdims=True)
    return jnp.concatenate(ranks, axis=0).astype(jnp.int32), carry


def _store_row_tiles(ref, val):
    rows = val.shape[0]
    for j in range(val.shape[1] // LANES):
        ref[pl.ds(j, rows, stride=SUBLANES), :] = val[:, j * LANES:(j + 1) * LANES]


def _load_row_tiles(ref, rows):
    return jnp.concatenate([ref[pl.ds(j, rows, stride=SUBLANES), :] for j in range(ref.shape[0] // rows)], axis=1)


def _merge_kernel(oret, oswa, oga, ar, as_, aa, *refs, first_layer, n_lat_tiles, n_tiles, n_batch):
    if first_layer:
        x_ref, c_ref, *refs = refs
    else:
        x_ref, *refs = refs
    (mod_ref, wr_ref, ws_ref, wa_ref, wout_ref, g2_ref, wrt_ref, br_ref,
     xo_ref, h2t_ref, e_ref, r_ref, w_ref, cnt_ref, h2_prev, carry_ref) = refs
    s = pl.program_id(0)
    n_steps = n_batch * n_tiles

    def merge_tile():
        b = s // n_tiles
        i = s - b * n_tiles
        m = mod_ref[pl.ds(jnp.where(i < n_lat_tiles, b, n_batch), 1), :]

        def gate(a_ref):
            return jax.nn.sigmoid(a_ref[0].astype(F32))

        y = (gate(ar) * _dot(oret[0], wr_ref[0]) + gate(as_) * _dot(oswa[0], ws_ref[0])
             + gate(aa) * _dot(oga[0], wa_ref[0]))
        residual = _first_layer_tokens(x_ref, c_ref, i, n_lat_tiles) if first_layer else x_ref[0]
        x = residual + _mod_chunk(m, 2) * _dot(y.astype(BF16), wout_ref[0])
        xo_ref[0] = x
        h2 = _rms_normalize(x) * g2_ref[...] * (1.0 + _mod_chunk(m, 4)) + _mod_chunk(m, 3)
        _store_row_tiles(h2t_ref.at[0], h2)
        h2_prev[...] = h2

    def route_previous_tile():
        h2 = h2_prev[...]
        h_hi = h2.astype(BF16)
        h_lo = (h2 - h_hi.astype(F32)).astype(BF16)
        w = wrt_ref[...]
        w_hi = w.astype(BF16)
        w_lo = (w - w_hi.astype(F32)).astype(BF16)
        logits = _dot_nt(w_hi, h_hi) + _dot_nt(w_hi, h_lo) + _dot_nt(w_lo, h_hi)
        scores = jax.nn.sigmoid(logits)
        experts, weights = _route(scores, scores + br_ref[...])
        ranks, carry = _slot_ranks(experts, carry_ref[:, 0:1])
        e_ref[0] = experts
        r_ref[0] = ranks
        w_ref[0] = weights
        carry_ref[...] = jnp.broadcast_to(carry, carry_ref.shape)
        cnt_ref[...] = jnp.broadcast_to(carry, cnt_ref.shape)

    @pl.when(s == 0)
    def _init():
        carry_ref[...] = jnp.zeros_like(carry_ref)

    @pl.when(s > 0)
    def _routing_step():
        route_previous_tile()

    @pl.when(s < n_steps)
    def _merge_step():
        merge_tile()


def _merge(oret, oswa, oga, ar, as_, aa, xa, mod, w_ret, w_swa, w_ga, wout, layer, g2, wrt, br,
           n_lat_tiles, n_tiles):
    first_layer = isinstance(xa, tuple)
    if first_layer:
        x, ctx = xa
        nb, length, d = x.shape
        nt = length + ctx.shape[1]
        assert n_tiles * TOKEN_TILE == nt, "the first layer must write every row of the joined stream"
    else:
        nb, nt, d = xa.shape
    rows = n_tiles * TOKEN_TILE
    n_steps = nb * n_tiles
    kern = functools.partial(_merge_kernel, first_layer=first_layer, n_lat_tiles=n_lat_tiles, n_tiles=n_tiles,
                             n_batch=nb)

    def cur(s):
        t = jnp.minimum(s, n_steps - 1)
        return t // n_tiles, t % n_tiles

    def prev(s):
        t = jnp.maximum(s - 1, 0)
        return t // n_tiles, t % n_tiles

    tile = pl.BlockSpec((1, TOKEN_TILE, d), lambda s: (*cur(s), 0))
    whole = lambda arr: pl.BlockSpec(arr.shape, lambda s: (0,) * arr.ndim)
    square = pl.BlockSpec((1, d, d), lambda s: (layer, 0, 0))
    small = pl.BlockSpec((1, 2, TOKEN_TILE), lambda s: (prev(s)[0], 0, prev(s)[1]))
    if first_layer:
        residual_specs, residual, aliases = _first_layer_specs(x, ctx, n_lat_tiles, cur), (x, ctx), {}
    else:
        residual_specs, residual, aliases = [tile], (xa,), {6: 0}
    return pl.pallas_call(
        kern,
        out_shape=[jax.ShapeDtypeStruct((nb, nt, d), F32),
                   jax.ShapeDtypeStruct((nb, rows * SUBLANES, LANES), F32),
                   jax.ShapeDtypeStruct((nb, 2, rows), jnp.int32),
                   jax.ShapeDtypeStruct((nb, 2, rows), jnp.int32),
                   jax.ShapeDtypeStruct((nb, 2, rows), F32),
                   jax.ShapeDtypeStruct((N_EXPERTS, LANES), F32)],
        grid=(n_steps + 1,),
        in_specs=[tile, tile, tile, tile, tile, tile, *residual_specs, whole(mod), square, square, square, square,
                  pl.BlockSpec((1, d), lambda s: (0, 0)), whole(wrt), whole(br)],
        out_specs=[tile,
                   pl.BlockSpec((1, TOKEN_TILE * SUBLANES, LANES), lambda s: (*cur(s), 0)),
                   small, small, small,
                   pl.BlockSpec((N_EXPERTS, LANES), lambda s: (0, 0))],
        scratch_shapes=[pltpu.VMEM((TOKEN_TILE, d), F32), pltpu.VMEM((N_EXPERTS, LANES), F32)],
        input_output_aliases=aliases,
        compiler_params=_params(dimension_semantics=("arbitrary",)),
        name="merge_route",
    )(oret, oswa, oga, ar, as_, aa, *residual, mod, w_ret, w_swa, w_ga, wout, g2.reshape(1, d), wrt, br)


def _row_copy(src, dst, sem):
    return pltpu.make_async_copy(src, dst, sem)


def _tile_rows(r):
    return pl.ds(pl.multiple_of(r * SUBLANES, SUBLANES), SUBLANES)


def _dispatch_kernel(tail_ref, nv_ref, dest_ref, h2t_ref, buf_out, zeros_ref, sem, tail_sem):
    @pl.when((pl.program_id(0) == 0) & (pl.program_id(1) == 0))
    def _zero_tails():
        zeros_ref[...] = jnp.zeros_like(zeros_ref)

        def zero_block(first_slot):
            rows = pl.ds(pl.multiple_of(first_slot * SUBLANES, SUBLANES), MOE_ROWS * SUBLANES)
            copy = pltpu.make_async_copy(zeros_ref, buf_out.at[rows, :], tail_sem)
            copy.start()
            copy.wait()

        for e in range(N_EXPERTS):
            @pl.when(tail_ref[e] >= 0)
            def _():
                zero_block(tail_ref[e])

        def unused(blk, carry):
            zero_block(blk * MOE_ROWS)
            return carry

        lax.fori_loop(nv_ref[0], buf_out.shape[0] // (MOE_ROWS * SUBLANES), unused, 0)

    def issue(r, carry):
        for k in range(2):
            d = dest_ref[0, 0, k * TOKEN_TILE + r]
            _row_copy(h2t_ref.at[0, _tile_rows(r), :], buf_out.at[_tile_rows(d), :], sem).start(priority=k)
        return carry

    lax.fori_loop(0, TOKEN_TILE, issue, 0, unroll=8)

    def drain(r, carry):
        for k in range(2):
            _row_copy(h2t_ref.at[0, _tile_rows(0), :], buf_out.at[_tile_rows(0), :], sem).wait()
        return carry

    lax.fori_loop(0, TOKEN_TILE, drain, 0, unroll=8)


def _dispatch(tail_start, n_valid, dest_tiles, h2t, n_slots):
    nb, rows8, _ = h2t.shape
    n_tiles = rows8 // (TOKEN_TILE * SUBLANES)
    grid_spec = pltpu.PrefetchScalarGridSpec(
        num_scalar_prefetch=2,
        grid=(nb, n_tiles),
        in_specs=[pl.BlockSpec((1, 1, 2 * TOKEN_TILE), lambda b, i, tail, nv: (b * n_tiles + i, 0, 0),
                               memory_space=pltpu.SMEM),
                  pl.BlockSpec((1, TOKEN_TILE * SUBLANES, LANES), lambda b, i, tail, nv: (b, i, 0))],
        out_specs=pl.BlockSpec(memory_space=pl.ANY),
        scratch_shapes=[pltpu.VMEM((MOE_ROWS * SUBLANES, LANES), F32),
                        pltpu.SemaphoreType.DMA(()), pltpu.SemaphoreType.DMA(())],
    )
    return pl.pallas_call(
        _dispatch_kernel,
        out_shape=jax.ShapeDtypeStruct((n_slots * SUBLANES, LANES), F32),
        grid_spec=grid_spec,
        compiler_params=_params(has_side_effects=True, dimension_semantics=("arbitrary", "arbitrary")),
        name="moe_dispatch",
    )(tail_start, n_valid, dest_tiles, h2t)


def _expert_kernel(be_ref, nv_ref, x_ref, wg_ref, wu_ref, wd_ref, y_ref, wg_bf, wu_bf, wd_bf):
    i = pl.program_id(0)

    @pl.when(i < nv_ref[0])
    def _():
        @pl.when((i == 0) | (be_ref[i] != be_ref[jnp.maximum(i - 1, 0)]))
        def _():
            wg_bf[...] = wg_ref[0, 0].astype(BF16)
            wu_bf[...] = wu_ref[0, 0].astype(BF16)
            wd_bf[...] = wd_ref[0, 0].astype(BF16)

        x = _load_row_tiles(x_ref, MOE_ROWS).astype(BF16)
        hid = _silu(_dot(x, wg_bf[...])) * _dot(x, wu_bf[...])
        _store_row_tiles(y_ref, _dot(hid.astype(BF16), wd_bf[...]))

    @pl.when(i >= nv_ref[0])
    def _():
        y_ref[...] = jnp.zeros_like(y_ref)


def _experts(block_e, n_valid, buf, wg, wu, wd, layer):
    n_slots = buf.shape[0] // SUBLANES
    d = wg.shape[2]
    used = lambda i, nv: jnp.maximum(jnp.minimum(i, nv[0] - 1), 0)
    rows = pl.BlockSpec((MOE_ROWS * SUBLANES, LANES), lambda i, be, nv: (used(i, nv), 0))
    grid_spec = pltpu.PrefetchScalarGridSpec(
        num_scalar_prefetch=2,
        grid=(n_slots // MOE_ROWS,),
        in_specs=[rows,
                  pl.BlockSpec((1, 1, d, D_EXPERT), lambda i, be, nv: (layer, be[used(i, nv)], 0, 0)),
                  pl.BlockSpec((1, 1, d, D_EXPERT), lambda i, be, nv: (layer, be[used(i, nv)], 0, 0)),
                  pl.BlockSpec((1, 1, D_EXPERT, d), lambda i, be, nv: (layer, be[used(i, nv)], 0, 0))],
        out_specs=pl.BlockSpec((MOE_ROWS * SUBLANES, LANES), lambda i, be, nv: (i, 0)),
        scratch_shapes=[pltpu.VMEM((d, D_EXPERT), BF16), pltpu.VMEM((d, D_EXPERT), BF16),
                        pltpu.VMEM((D_EXPERT, d), BF16)],
    )
    return pl.pallas_call(
        _expert_kernel,
        out_shape=jax.ShapeDtypeStruct((n_slots * SUBLANES, LANES), F32),
        grid_spec=grid_spec,
        compiler_params=_params(dimension_semantics=("arbitrary",)),
        name="moe_experts",
    )(block_e, n_valid, buf, wg, wu, wd)


def _combine_kernel(dest_ref, next_ref, y_hbm, wt_ref, x_ref, mod_ref, g_ref, *rest,
                    n_lat_tiles, n_tiles, n_batch, last):
    if last:
        out_ref, gbuf, sems = rest
    else:
        xo_ref, h_ref, gbuf, sems = rest
    t = pl.program_id(0)
    slot = lax.rem(t, 2)

    def gather(idx_ref, into):
        def issue(r, carry):
            for k in range(2):
                d = idx_ref[0, 0, k * TOKEN_TILE + r]
                _row_copy(y_hbm.at[_tile_rows(d), :], gbuf.at[into, k, _tile_rows(r), :],
                          sems.at[into]).start(priority=k)
            return carry

        lax.fori_loop(0, TOKEN_TILE, issue, 0, unroll=8)

    @pl.when(t == 0)
    def _():
        gather(dest_ref, 0)

    @pl.when(t + 1 < n_batch * n_tiles)
    def _():
        gather(next_ref, 1 - slot)

    def drain(r, carry):
        for k in range(2):
            _row_copy(y_hbm.at[_tile_rows(0), :], gbuf.at[slot, 0, _tile_rows(0), :], sems.at[slot]).wait()
        return carry

    lax.fori_loop(0, TOKEN_TILE, drain, 0, unroll=8)

    b = t // n_tiles
    m = mod_ref[pl.ds(jnp.where(t - b * n_tiles < n_lat_tiles, b, n_batch), 1), :]
    wt = wt_ref[0]
    moe = (_load_row_tiles(gbuf.at[slot, 0], TOKEN_TILE) * wt[:, 0:1]
           + _load_row_tiles(gbuf.at[slot, 1], TOKEN_TILE) * wt[:, 1:2])
    x = x_ref[0] + _mod_chunk(m, 5) * moe
    if last:
        out_ref[0] = _rms_normalize(x) * g_ref[...]
    else:
        xo_ref[0] = x
        h = _rms_normalize(x) * g_ref[...] * (1.0 + _mod_chunk(m, 7)) + _mod_chunk(m, 6)
        h_ref[0] = h.astype(BF16)


def _combine(dest_tiles, y, wt, xa, mod, g, n_lat_tiles, n_tiles, last):
    nb, nt, d = xa.shape
    rows = n_tiles * TOKEN_TILE
    n_steps = nb * n_tiles
    kern = functools.partial(_combine_kernel, n_lat_tiles=n_lat_tiles, n_tiles=n_tiles, n_batch=nb, last=last)
    tile = pl.BlockSpec((1, TOKEN_TILE, d), lambda t: (t // n_tiles, t % n_tiles, 0))
    if last:
        out_shape = jax.ShapeDtypeStruct((nb, rows, d), F32)
        out_specs = tile
        aliases = {}
    else:
        out_shape = [jax.ShapeDtypeStruct((nb, nt, d), F32), jax.ShapeDtypeStruct((nb, nt, d), BF16)]
        out_specs = [tile, tile]
        aliases = {4: 0}
    return pl.pallas_call(
        kern,
        out_shape=out_shape,
        grid=(n_steps,),
        in_specs=[pl.BlockSpec((1, 1, 2 * TOKEN_TILE), lambda t: (t, 0, 0), memory_space=pltpu.SMEM),
                  pl.BlockSpec((1, 1, 2 * TOKEN_TILE), lambda t: (jnp.minimum(t + 1, n_steps - 1), 0, 0),
                               memory_space=pltpu.SMEM),
                  pl.BlockSpec(memory_space=pl.ANY),
                  pl.BlockSpec((1, TOKEN_TILE, 2), lambda t: (t // n_tiles, t % n_tiles, 0)),
                  tile,
                  pl.BlockSpec(mod.shape, lambda t: (0, 0)),
                  pl.BlockSpec((1, d), lambda t: (0, 0))],
        out_specs=out_specs,
        scratch_shapes=[pltpu.VMEM((2, 2, TOKEN_TILE * SUBLANES, LANES), F32), pltpu.SemaphoreType.DMA((2,))],
        input_output_aliases=aliases,
        compiler_params=_params(dimension_semantics=("arbitrary",)),
        name="moe_combine",
    )(dest_tiles, dest_tiles, y, wt, xa, mod, g.reshape(1, d))


def _slot_plan(experts, ranks, counts):
    nb, _, rows = experts.shape
    padded = (counts + MOE_ROWS - 1) // MOE_ROWS * MOE_ROWS
    pad_ends = jnp.cumsum(padded)
    pad_starts = pad_ends - padded
    one_hot = experts[..., None] == jnp.arange(N_EXPERTS, dtype=jnp.int32)
    dest = jnp.sum(jnp.where(one_hot, pad_starts, 0), axis=-1) + ranks
    n_assign = nb * 2 * rows
    n_blocks = (n_assign + N_EXPERTS * (MOE_ROWS - 1) + MOE_ROWS - 1) // MOE_ROWS
    first_slot = jnp.arange(n_blocks, dtype=jnp.int32) * MOE_ROWS
    block_e = jnp.minimum(jnp.sum((pad_ends[None, :] <= first_slot[:, None]).astype(jnp.int32), axis=1),
                          N_EXPERTS - 1).astype(jnp.int32)
    n_valid = (pad_ends[-1:] // MOE_ROWS).astype(jnp.int32)
    tail_start = jnp.where(padded > 0, pad_ends - MOE_ROWS, -1).astype(jnp.int32)
    n_tiles = rows // TOKEN_TILE
    dest_tiles = (dest.astype(jnp.int32).reshape(nb, 2, n_tiles, TOKEN_TILE)
                  .transpose(0, 2, 1, 3).reshape(nb * n_tiles, 1, 2 * TOKEN_TILE))
    return dest_tiles, block_e, n_valid, tail_start, n_blocks * MOE_ROWS


def _rope_tables(length, n_ctx):
    rows = length // GRID_W
    row = jnp.repeat(jnp.arange(rows, dtype=jnp.int32), GRID_W).astype(F32)
    col = jnp.tile(jnp.arange(GRID_W, dtype=jnp.int32), rows).astype(F32)

    def table(hd):
        quarter = hd // 4
        freqs = ROPE_THETA ** (-jnp.arange(quarter, dtype=F32) / quarter)
        ang_r = row[:, None] * freqs[None, :]
        ang_c = col[:, None] * freqs[None, :]
        cos = jnp.concatenate([jnp.cos(ang_r), jnp.cos(ang_r), jnp.cos(ang_c), jnp.cos(ang_c)], axis=-1)
        sin = jnp.concatenate([-jnp.sin(ang_r), jnp.sin(ang_r), -jnp.sin(ang_c), jnp.sin(ang_c)], axis=-1)
        cos = jnp.concatenate([cos, jnp.ones((n_ctx, hd), F32)], axis=0)
        sin = jnp.concatenate([sin, jnp.zeros((n_ctx, hd), F32)], axis=0)
        reps = LANES // hd
        return jnp.tile(cos, (1, reps)), jnp.tile(sin, (1, reps))

    c128, s128 = table(RET_DK)
    c64, s64 = table(SWA_HD)
    return c128, s128, c64, s64


def kernel(x, c, ctx, c_ctx, w_mod, b_mod, g_norm1, g_norm2, w_in, ret_decay_logit, swa_sink, g_qnorm, g_knorm,
           w_br_ret, w_br_swa, w_br_ga, w_out, w_router, b_router, w_gate, w_up, w_down, g_final):
    nb, length, d = x.shape
    n_ctx = ctx.shape[1]
    depth = w_mod.shape[0]
    nt = length + n_ctx
    n_lat_tiles = length // TOKEN_TILE
    n_all_tiles = nt // TOKEN_TILE
    assert GA_HD == RET_DK == LANES and 2 * SWA_HD == LANES
    assert length % TOKEN_TILE == 0 and n_ctx == TOKEN_TILE and length % n_ctx == 0

    mod_rows = 8
    c_rows = jnp.concatenate([c, c_ctx[None, :], jnp.zeros((mod_rows - nb - 1, d), F32)], axis=0)
    mods = _modulation(c_rows, w_mod, b_mod)
    tables = _rope_tables(length, n_ctx)
    wrt = w_router.astype(F32).T
    br = b_router.astype(F32).reshape(N_EXPERTS, 1)

    assert depth >= 2, "the first layer's merge kernel must also cover the context rows"
    xa = (x, ctx)
    h = (x, ctx, mods[0], g_norm1[0])
    w_in_bf, w_ret_bf, w_swa_bf, w_ga_bf, w_out_bf = (
        w.astype(BF16) for w in (w_in, w_br_ret, w_br_swa, w_br_ga, w_out))
    out = None
    for l in range(depth):
        need_ctx = l < depth - 1
        n_tiles = n_all_tiles if need_ctx else n_lat_tiles
        (rq, rk, rv, ru, sq, sk, sv, gq, gk, gv, ar, as_, aa) = _inproj(
            h, w_in_bf, l, tables, g_qnorm[l], g_knorm[l])
        o_ret = _retention(ret_decay_logit[l].astype(F32), rq, rk, rv, ru, length, n_ctx, need_ctx)
        o_swa = _window_attention(swa_sink[l].astype(F32), sq, sk, sv, length, n_ctx, need_ctx)
        o_ga = _global_attention(gq, gk, gv, length, n_ctx, need_ctx)
        xa, h2t, experts, ranks, weights, counts = _merge(
            o_ret, o_swa, o_ga, ar, as_, aa, xa, mods[l], w_ret_bf, w_swa_bf, w_ga_bf, w_out_bf, l,
            g_norm2[l], wrt, br, n_lat_tiles, n_tiles)
        dest_tiles, block_e, n_valid, tail_start, n_slots = _slot_plan(
            experts, ranks, counts[:, 0].astype(jnp.int32))
        buf = _dispatch(tail_start, n_valid, dest_tiles, h2t, n_slots)
        y = _experts(block_e, n_valid, buf, w_gate, w_up, w_down, l)
        wt = weights.transpose(0, 2, 1)
        if need_ctx:
            mod_pair = jnp.concatenate([mods[l], mods[l + 1][:, :2 * d]], axis=1)
            xa, h = _combine(dest_tiles, y, wt, xa, mod_pair, g_norm1[l + 1], n_lat_tiles, n_tiles, last=False)
        else:
            out = _combine(dest_tiles, y, wt, xa, mods[l], g_final, n_lat_tiles, n_tiles, last=True)
    return out
```

```python
import functools

import jax
import jax.numpy as jnp
from jax import lax
from jax.experimental import pallas as pl
from jax.experimental.pallas import tpu as pltpu

F32 = jnp.float32
BF16 = jnp.bfloat16

D_MODEL = 1024
GRID_W = 64
NORM_EPS = 1e-6
ROPE_THETA = 10000.0
RET_HEADS, RET_DK, RET_DV, RET_CHUNK = 4, 128, 256, 256
SWA_HEADS, SWA_KV_HEADS, SWA_HD, WINDOW = 16, 2, 64, 128
GA_HEADS, GA_KV_HEADS, GA_HD = 8, 2, 128
N_EXPERTS, N_GROUPS, EXPERTS_PER_GROUP, D_EXPERT = 16, 4, 4, 512
PAIRS_PER_GROUP = EXPERTS_PER_GROUP * (EXPERTS_PER_GROUP - 1) // 2
N_PAIRS = N_GROUPS * PAIRS_PER_GROUP

RET_QK = RET_HEADS * RET_DK
RET_V = RET_HEADS * RET_DV
SWA_Q = SWA_HEADS * SWA_HD
SWA_KV = SWA_KV_HEADS * SWA_HD
GA_Q = GA_HEADS * GA_HD
GA_KV = GA_KV_HEADS * GA_HD
SPLITS = (RET_QK, RET_QK, RET_V, RET_V, SWA_Q, SWA_KV, SWA_KV, GA_Q, GA_KV, GA_KV,
          D_MODEL, D_MODEL, D_MODEL)

LANES = 128
SUBLANES = 8
TOKEN_TILE = 256
QUERY_TILE = 128
GA_QUERY_TILE = 256
MOE_ROWS = 256
GA_KEY_CHUNK = 256
RET_HEADS_PER_STEP = 2
SWA_PARTS = 2
SWA_TILES_PER_CHUNK = 2
MOD_COLS = 1536
VMEM_LIMIT = 56 * 1024 * 1024
MASKED = -1e30
LOG2E = 1.4426950408889634

_W_WIDTHS = (RET_QK, RET_QK, RET_V, RET_V, SWA_Q, 2 * SWA_KV, 2 * SWA_KV, GA_Q, GA_KV, GA_KV,
             D_MODEL, D_MODEL, D_MODEL)
_IN_OFFS = tuple(sum(SPLITS[:i]) for i in range(len(SPLITS)))


def _dot(a, b):
    return jnp.dot(a, b, preferred_element_type=F32)


def _dot_nt(a, b):
    return lax.dot_general(a, b, (((1,), (1,)), ((), ())), preferred_element_type=F32)


def _dot_tn(a, b):
    return lax.dot_general(a, b, (((0,), (0,)), ((), ())), preferred_element_type=F32)


def _silu(x):
    return x * jax.nn.sigmoid(x)


def _rms_normalize(x):
    return x * lax.rsqrt(jnp.mean(x * x, axis=-1, keepdims=True) + NORM_EPS)


def _params(**kw):
    return pltpu.CompilerParams(vmem_limit_bytes=VMEM_LIMIT, **kw)


def _mod_kernel(c_ref, w_ref, b_ref, o_ref):
    a = _silu(c_ref[...])
    o_ref[0] = _dot(a.astype(BF16), w_ref[0].astype(BF16)) + b_ref[0]


def _modulation(c_rows, w_mod, b_mod):
    depth, d, n = w_mod.shape
    rows = c_rows.shape[0]
    return pl.pallas_call(
        _mod_kernel,
        out_shape=jax.ShapeDtypeStruct((depth, rows, n), F32),
        grid=(depth, n // MOD_COLS),
        in_specs=[pl.BlockSpec((rows, d), lambda l, j: (0, 0)),
                  pl.BlockSpec((1, d, MOD_COLS), lambda l, j: (l, 0, j)),
                  pl.BlockSpec((1, 1, MOD_COLS), lambda l, j: (l, 0, j))],
        out_specs=pl.BlockSpec((1, rows, MOD_COLS), lambda l, j: (l, 0, j)),
        compiler_params=_params(),
        name="modulation",
    )(c_rows, w_mod, b_mod.reshape(depth, 1, n))


def _mod_row(mod_ref, n_lat_tiles, n_batch):
    b = pl.program_id(0)
    i = pl.program_id(1)
    r = jnp.where(i < n_lat_tiles, b, n_batch)
    return mod_ref[pl.ds(r, 1), :]


def _mod_chunk(m, k):
    return m[:, k * D_MODEL:(k + 1) * D_MODEL]


def _rope(x, cos, sin, quarter):
    lane = lax.broadcasted_iota(jnp.int32, x.shape, 1)
    first = (lane % (2 * quarter)) < quarter
    partner = jnp.where(first, pltpu.roll(x, LANES - quarter, 1), pltpu.roll(x, quarter, 1))
    return x * cos + partner * sin


def _first_layer_tokens(x_ref, c_ref, i, n_lat_tiles):
    return jnp.where(i < n_lat_tiles, x_ref[0], c_ref[0])


def _inproj_kernel(*refs, first_layer, n_lat_tiles, n_batch):
    if first_layer:
        x_ref, c_ref, mod_ref, g_ref, *refs = refs
        m = _mod_row(mod_ref, n_lat_tiles, n_batch)
        x = _first_layer_tokens(x_ref, c_ref, pl.program_id(1), n_lat_tiles)
        h = (_rms_normalize(x) * g_ref[...] * (1.0 + _mod_chunk(m, 1)) + _mod_chunk(m, 0)).astype(BF16)
    else:
        h_ref, *refs = refs
        h = h_ref[0]
    (w_ref, c128_ref, s128_ref, c64_ref, s64_ref, gq_ref, gk_ref,
     rq, rk, rv, ru, sq, sk, sv, gq, gk, gv, ar, as_, aa) = refs
    c128, s128 = c128_ref[...], s128_ref[...]
    c64, s64 = c64_ref[...], s64_ref[...]

    def proj(idx):
        return _dot(h, w_ref[0, :, _IN_OFFS[idx]:_IN_OFFS[idx] + SPLITS[idx]])

    def slabs(acc):
        return [acc[:, s * LANES:(s + 1) * LANES] for s in range(acc.shape[1] // LANES)]

    def store(ref, s, val):
        ref[0, :, s * LANES:(s + 1) * LANES] = val.astype(ref.dtype)

    def head_per_slab(acc):
        low = lax.broadcasted_iota(jnp.int32, acc.shape, 1) < SWA_HD
        other = pltpu.roll(acc, SWA_HD, 1)
        return [jnp.where(low, acc, other), jnp.where(low, other, acc)]

    for s, xs in enumerate(slabs(proj(0))):
        store(rq, s, _rope(xs, c128, s128, RET_DK // 4))
    for s, xs in enumerate(slabs(proj(1))):
        store(rk, s, _rope(xs, c128, s128, RET_DK // 4) * (RET_DK ** -0.5))
    rv[0] = proj(2).astype(rv.dtype)
    ru[0] = proj(3).astype(ru.dtype)
    for s, xs in enumerate(slabs(proj(4))):
        store(sq, s, _rope(xs, c64, s64, SWA_HD // 4) * (SWA_HD ** -0.5 * LOG2E))
    for s, xs in enumerate(head_per_slab(proj(5))):
        store(sk, s, _rope(xs, c64, s64, SWA_HD // 4))
    for s, xs in enumerate(head_per_slab(proj(6))):
        store(sv, s, xs)
    for s, xs in enumerate(slabs(proj(7))):
        xn = _rms_normalize(xs) * gq_ref[...]
        store(gq, s, _rope(xn, c128, s128, GA_HD // 4) * (GA_HD ** -0.5 * LOG2E))
    for s, xs in enumerate(slabs(proj(8))):
        xn = _rms_normalize(xs) * gk_ref[...]
        store(gk, s, _rope(xn, c128, s128, GA_HD // 4))
    gv[0] = proj(9).astype(gv.dtype)
    ar[0] = proj(10).astype(ar.dtype)
    as_[0] = proj(11).astype(as_.dtype)
    aa[0] = proj(12).astype(aa.dtype)


def _first_layer_specs(x, ctx, n_lat_tiles, index):
    d = x.shape[2]
    return [pl.BlockSpec((1, TOKEN_TILE, d), lambda *ids: (index(*ids)[0],
                                                          jnp.minimum(index(*ids)[1], n_lat_tiles - 1), 0)),
            pl.BlockSpec((1, TOKEN_TILE, d), lambda *ids: (index(*ids)[0], 0, 0))]


def _inproj(tokens, w_in, layer, tables, g_q, g_k):
    first_layer = isinstance(tokens, tuple)
    tile = lambda width: pl.BlockSpec((1, TOKEN_TILE, width), lambda b, i: (b, i, 0))
    if first_layer:
        x, ctx, mod, g = tokens
        nb, length, d = x.shape
        nt = length + ctx.shape[1]
        n_lat_tiles = length // TOKEN_TILE
        lead_specs = _first_layer_specs(x, ctx, n_lat_tiles, lambda b, i: (b, i)) + [
            pl.BlockSpec(mod.shape, lambda b, i: (0, 0)), pl.BlockSpec((1, d), lambda b, i: (0, 0))]
        lead = (x, ctx, mod, g.reshape(1, d))
    else:
        nb, nt, d = tokens.shape
        n_lat_tiles = 0
        lead_specs = [tile(d)]
        lead = (tokens,)
    tab = pl.BlockSpec((TOKEN_TILE, LANES), lambda b, i: (i, 0))
    vec = pl.BlockSpec((1, LANES), lambda b, i: (0, 0))
    outs = [jax.ShapeDtypeStruct((nb, nt, w), BF16) for w in _W_WIDTHS]
    kern = functools.partial(_inproj_kernel, first_layer=first_layer, n_lat_tiles=n_lat_tiles, n_batch=nb)
    return pl.pallas_call(
        kern,
        out_shape=outs,
        grid=(nb, nt // TOKEN_TILE),
        in_specs=lead_specs + [
            pl.BlockSpec((1, d, w_in.shape[2]), lambda b, i: (layer, 0, 0), pipeline_mode=pl.Buffered(1)),
            tab, tab, tab, tab, vec, vec],
        out_specs=[tile(w) for w in _W_WIDTHS],
        compiler_params=_params(),
        name="inproj",
    )(*lead, w_in, *tables, g_q.reshape(1, LANES), g_k.reshape(1, LANES))


def _log_sigmoid(x):
    return jnp.minimum(x, 0.0) - jnp.log1p(jnp.exp(-jnp.abs(x)))


def _ret_kernel(logit_ref, q_ref, k_ref, v_ref, u_ref, o_ref, acc_ref, sf_ref, sb_ref,
                *, length, n_ctx, need_ctx):
    chunk = RET_CHUNK
    n_chunks = length // chunk
    half = n_chunks // 2

    def rows_f32(shape):
        return lax.broadcasted_iota(jnp.int32, shape, 0).astype(F32)

    def finish(o, u):
        return (_rms_normalize(o) * _silu(u.astype(F32))).astype(o_ref.dtype)

    ci = rows_f32((chunk, RET_DK))
    heads = []
    for hi in range(RET_HEADS_PER_STEP):
        hh = pl.program_id(1) * RET_HEADS_PER_STEP + hi
        lgf = _log_sigmoid(jnp.full((1, 1), logit_ref[0, hh], F32))
        lgb = _log_sigmoid(jnp.full((1, 1), logit_ref[1, hh], F32))
        qk = slice(hi * RET_DK, (hi + 1) * RET_DK)
        vu = slice(hi * RET_DV, (hi + 1) * RET_DV)

        def both_ways_decay(n, lgf=lgf, lgb=lgb):
            diff = rows_f32((n, n)) - lax.broadcasted_iota(jnp.int32, (n, n), 1).astype(F32)
            return jnp.where(diff >= 0, jnp.exp(jnp.maximum(diff, 0.0) * lgf),
                             jnp.exp(jnp.maximum(-diff, 0.0) * lgb))

        kx = k_ref[0, length:length + n_ctx, qk].astype(F32)
        vx = v_ref[0, length:length + n_ctx, vu]
        lx = rows_f32((n_ctx, RET_DK))
        sf_ref[hi] = _dot_tn((kx * jnp.exp((n_ctx - 1.0 - lx) * lgf)).astype(BF16), vx)
        sb_ref[hi] = _dot_tn((kx * jnp.exp(lx * lgb)).astype(BF16), vx)
        if need_ctx:
            qx = q_ref[0, length:length + n_ctx, qk]
            sx = _dot_nt(qx, kx.astype(BF16)) * both_ways_decay(n_ctx)
            o_ref[0, length:length + n_ctx, vu] = finish(_dot(sx.astype(BF16), vx),
                                                         u_ref[0, length:length + n_ctx, vu])
        heads.append(dict(
            qk=qk, vu=vu, intra=both_ways_decay(chunk),
            q_dec_f=jnp.exp((ci + 1.0) * lgf), k_dec_f=jnp.exp((chunk - 1.0 - ci) * lgf),
            q_dec_b=jnp.exp((chunk - ci) * lgb), k_dec_b=jnp.exp(ci * lgb),
            chunk_dec_f=jnp.exp(chunk * lgf), chunk_dec_b=jnp.exp(chunk * lgb)))

    def sweep(s, first_touch):
        rf = pl.ds(pl.multiple_of(s * chunk, chunk), chunk)
        rb = pl.ds(pl.multiple_of((n_chunks - 1 - s) * chunk, chunk), chunk)
        for hi, hd in enumerate(heads):
            qk, vu = hd["qk"], hd["vu"]
            qf, kf, vf = q_ref[0, rf, qk], k_ref[0, rf, qk], v_ref[0, rf, vu]
            sc = _dot_nt(qf, kf) * hd["intra"]
            state_f = sf_ref[hi]
            o_f = (_dot(sc.astype(BF16), vf)
                   + _dot((qf.astype(F32) * hd["q_dec_f"]).astype(BF16), state_f.astype(BF16)))
            sf_ref[hi] = state_f * hd["chunk_dec_f"] + _dot_tn((kf.astype(F32) * hd["k_dec_f"]).astype(BF16), vf)
            qb, kb, vb = q_ref[0, rb, qk], k_ref[0, rb, qk], v_ref[0, rb, vu]
            state_b = sb_ref[hi]
            o_b = _dot((qb.astype(F32) * hd["q_dec_b"]).astype(BF16), state_b.astype(BF16))
            sb_ref[hi] = state_b * hd["chunk_dec_b"] + _dot_tn((kb.astype(F32) * hd["k_dec_b"]).astype(BF16), vb)
            if first_touch:
                acc_ref[hi, rf, :] = o_f
                acc_ref[hi, rb, :] = o_b
            else:
                o_ref[0, rf, vu] = finish(acc_ref[hi, rf, :] + o_f, u_ref[0, rf, vu])
                o_ref[0, rb, vu] = finish(acc_ref[hi, rb, :] + o_b, u_ref[0, rb, vu])

    def first_half(s, carry):
        sweep(s, True)
        return carry

    def second_half(s, carry):
        sweep(s, False)
        return carry

    lax.fori_loop(0, half, first_half, 0)
    lax.fori_loop(half, n_chunks, second_half, 0)


def _retention(logit, rq, rk, rv, ru, length, n_ctx, need_ctx):
    nb, nt, _ = rq.shape
    assert (length // RET_CHUNK) % 2 == 0 and RET_HEADS % RET_HEADS_PER_STEP == 0
    out_rows = nt if need_ctx else length
    kern = functools.partial(_ret_kernel, length=length, n_ctx=n_ctx, need_ctx=need_ctx)
    head = lambda width: pl.BlockSpec((1, nt, RET_HEADS_PER_STEP * width), lambda b, h: (b, 0, h))
    return pl.pallas_call(
        kern,
        out_shape=jax.ShapeDtypeStruct((nb, out_rows, RET_V), BF16),
        grid=(nb, RET_HEADS // RET_HEADS_PER_STEP),
        in_specs=[pl.BlockSpec(memory_space=pltpu.SMEM),
                  head(RET_DK), head(RET_DK), head(RET_DV), head(RET_DV)],
        out_specs=pl.BlockSpec((1, out_rows, RET_HEADS_PER_STEP * RET_DV), lambda b, h: (b, 0, h)),
        scratch_shapes=[pltpu.VMEM((RET_HEADS_PER_STEP, length, RET_DV), F32),
                        pltpu.VMEM((RET_HEADS_PER_STEP, RET_DK, RET_DV), F32),
                        pltpu.VMEM((RET_HEADS_PER_STEP, RET_DK, RET_DV), F32)],
        compiler_params=_params(),
        name="retention",
    )(logit, rq, rk, rv, ru)


def _swa_group(q_ref, g, kcat, vcat, biases, sink_ref, o_ref):
    tq = q_ref.shape[1]
    heads_per_group = SWA_HEADS // SWA_KV_HEADS
    slabs_per_group = heads_per_group // 2
    low = lax.broadcasted_iota(jnp.int32, (tq, LANES), 1) < SWA_HD
    zero = jnp.zeros((tq, LANES), q_ref.dtype)
    stacked = []
    for a in range(slabs_per_group):
        s0 = (g * slabs_per_group + a) * LANES
        slab = q_ref[0, :, s0:s0 + LANES]
        stacked.append(jnp.where(low, slab, zero))
        stacked.append(jnp.where(low, zero, slab))
    heads_per_part = heads_per_group // SWA_PARTS
    rows = heads_per_part * tq
    n_key_tiles = kcat.shape[0] // LANES
    for part in range(SWA_PARTS):
        q = jnp.concatenate(stacked[part * heads_per_part:(part + 1) * heads_per_part], axis=0)
        sink = jnp.concatenate(
            [jnp.full((tq, LANES), sink_ref[g * heads_per_group + part * heads_per_part + h] * LOG2E, F32)
             for h in range(heads_per_part)], axis=0)
        m = l = acc = None
        for t0 in range(0, n_key_tiles, SWA_TILES_PER_CHUNK):
            t1 = min(t0 + SWA_TILES_PER_CHUNK, n_key_tiles)
            s = _dot_nt(q, kcat[t0 * LANES:t1 * LANES])
            tiles = [s[:, t * LANES:(t + 1) * LANES] for t in range(t1 - t0)]
            for t in range(t0, t1):
                if t in biases:
                    tiles[t - t0] = tiles[t - t0] + biases[t][:rows]
            mc = jnp.broadcast_to(jnp.max(functools.reduce(jnp.maximum, tiles), axis=-1, keepdims=True),
                                  (rows, LANES))
            m_new = jnp.maximum(mc, sink if m is None else m)
            ps = [jnp.exp2(tile - m_new) for tile in tiles]
            pv = _dot(jnp.concatenate(ps, axis=1).astype(BF16), vcat[t0 * LANES:t1 * LANES])
            psum = functools.reduce(jnp.add, ps)
            if m is None:
                l, acc = psum, pv
            else:
                alpha = jnp.exp2(m - m_new)
                l, acc = alpha * l + psum, alpha * acc + pv
            m = m_new
        den = jnp.broadcast_to(jnp.sum(l, axis=-1, keepdims=True), (rows, LANES)) + jnp.exp2(sink - m)
        o = acc / den
        for a in range(heads_per_part // 2):
            s0 = (g * slabs_per_group + part * (heads_per_part // 2) + a) * LANES
            even = o[(2 * a) * tq:(2 * a + 1) * tq]
            odd = o[(2 * a + 1) * tq:(2 * a + 2) * tq]
            o_ref[0, :, s0:s0 + LANES] = jnp.where(low, even, odd).astype(o_ref.dtype)


def _swa_kernel(sink_ref, q_ref, kp_ref, kc_ref, kn_ref, kx_ref, vp_ref, vc_ref, vn_ref, vx_ref, o_ref,
                *, n_lat_tiles, need_ctx):
    j = pl.program_id(1)
    tq = q_ref.shape[1]
    n_ctx = kx_ref.shape[1]

    def lanes_of(ref, g):
        return ref[0, :, g * LANES:(g + 1) * LANES]

    @pl.when(j < n_lat_tiles)
    def _latent():
        ci = lax.broadcasted_iota(jnp.int32, (tq, tq), 1)
        ri = lax.broadcasted_iota(jnp.int32, (tq, tq), 0)
        far = 4 * tq
        ri_prev = ri + jnp.where(j > 0, 0, far)
        ri_next = ri - jnp.where(j < n_lat_tiles - 1, 0, far)
        zero = jnp.zeros((tq, tq), F32)
        masked = jnp.full((tq, tq), MASKED, F32)
        heads_per_part = SWA_HEADS // SWA_KV_HEADS // SWA_PARTS
        bias = {0: jnp.concatenate([jnp.where(ci >= ri_prev, zero, masked)] * heads_per_part, axis=0),
                2: jnp.concatenate([jnp.where(ci <= ri_next, zero, masked)] * heads_per_part, axis=0)}
        for g in range(SWA_KV_HEADS):
            kcat = jnp.concatenate([lanes_of(kp_ref, g), lanes_of(kc_ref, g), lanes_of(kn_ref, g),
                                    lanes_of(kx_ref, g)], axis=0)
            vcat = jnp.concatenate([lanes_of(vp_ref, g), lanes_of(vc_ref, g), lanes_of(vn_ref, g),
                                    lanes_of(vx_ref, g)], axis=0)
            _swa_group(q_ref, g, kcat, vcat, bias, sink_ref, o_ref)

    if need_ctx:
        @pl.when(j >= n_lat_tiles)
        def _context():
            for g in range(SWA_KV_HEADS):
                _swa_group(q_ref, g, lanes_of(kx_ref, g), lanes_of(vx_ref, g), {}, sink_ref, o_ref)


def _window_attention(sink, sq, sk, sv, length, n_ctx, need_ctx):
    nb, nt, _ = sq.shape
    tq = QUERY_TILE
    n_lat = length // tq
    out_rows = nt if need_ctx else length
    kvw = 2 * SWA_KV
    kern = functools.partial(_swa_kernel, n_lat_tiles=n_lat, need_ctx=need_ctx)
    prev = pl.BlockSpec((1, tq, kvw), lambda b, j: (b, jnp.clip(j - 1, 0, n_lat - 1), 0))
    cur = pl.BlockSpec((1, tq, kvw), lambda b, j: (b, jnp.minimum(j, n_lat - 1), 0))
    nxt = pl.BlockSpec((1, tq, kvw), lambda b, j: (b, jnp.minimum(j + 1, n_lat - 1), 0))
    ctx = pl.BlockSpec((1, n_ctx, kvw), lambda b, j: (b, length // n_ctx, 0))
    return pl.pallas_call(
        kern,
        out_shape=jax.ShapeDtypeStruct((nb, out_rows, SWA_Q), BF16),
        grid=(nb, out_rows // tq),
        in_specs=[pl.BlockSpec(memory_space=pltpu.SMEM),
                  pl.BlockSpec((1, tq, SWA_Q), lambda b, j: (b, j, 0)),
                  prev, cur, nxt, ctx, prev, cur, nxt, ctx],
        out_specs=pl.BlockSpec((1, tq, SWA_Q), lambda b, j: (b, j, 0)),
        compiler_params=_params(),
        name="window_attention",
    )(sink, sq, sk, sk, sk, sk, sv, sv, sv, sv)


def _ga_kernel(q_ref, k_ref, v_ref, o_ref, *, length, n_ctx, n_lat_tiles, need_ctx):
    j = pl.program_id(2)
    tq = q_ref.shape[1]
    heads_per_group = GA_HEADS // GA_KV_HEADS
    q = jnp.concatenate([q_ref[0, :, h * LANES:(h + 1) * LANES] for h in range(heads_per_group)], axis=0)

    rows = heads_per_group * tq

    def attend(key_lo, key_hi):
        m = l = acc = None
        for c0 in range(key_lo, key_hi, GA_KEY_CHUNK):
            c1 = min(c0 + GA_KEY_CHUNK, key_hi)
            s = _dot_nt(q, k_ref[0, c0:c1, :])
            parts = [s[:, t * LANES:(t + 1) * LANES] for t in range((c1 - c0) // LANES)]
            lane_max = functools.reduce(jnp.maximum, parts)
            mc = jnp.broadcast_to(jnp.max(lane_max, axis=-1, keepdims=True), (rows, LANES))
            m_new = mc if m is None else jnp.maximum(m, mc)
            ps = [jnp.exp2(part - m_new) for part in parts]
            pv = _dot(jnp.concatenate(ps, axis=1).astype(BF16), v_ref[0, c0:c1, :])
            psum = functools.reduce(jnp.add, ps)
            if m is None:
                l, acc = psum, pv
            else:
                alpha = jnp.exp2(m - m_new)
                l, acc = alpha * l + psum, alpha * acc + pv
            m = m_new
        o = acc / jnp.sum(l, axis=-1, keepdims=True)
        for h in range(heads_per_group):
            o_ref[0, :, h * LANES:(h + 1) * LANES] = o[h * tq:(h + 1) * tq].astype(o_ref.dtype)

    @pl.when(j < n_lat_tiles)
    def _latent():
        attend(0, length + n_ctx)

    if need_ctx:
        @pl.when(j >= n_lat_tiles)
        def _context():
            attend(length, length + n_ctx)


def _global_attention(gq, gk, gv, length, n_ctx, need_ctx):
    nb, nt, _ = gq.shape
    tq = GA_QUERY_TILE
    out_rows = nt if need_ctx else length
    group_w = GA_Q // GA_KV_HEADS
    kern = functools.partial(_ga_kernel, length=length, n_ctx=n_ctx, n_lat_tiles=length // tq,
                             need_ctx=need_ctx)
    kv = pl.BlockSpec((1, nt, GA_HD), lambda b, g, j: (b, 0, g))
    return pl.pallas_call(
        kern,
        out_shape=jax.ShapeDtypeStruct((nb, out_rows, GA_Q), BF16),
        grid=(nb, GA_KV_HEADS, out_rows // tq),
        in_specs=[pl.BlockSpec((1, tq, group_w), lambda b, g, j: (b, j, g)), kv, kv],
        out_specs=pl.BlockSpec((1, tq, group_w), lambda b, g, j: (b, j, g)),
        compiler_params=_params(),
        name="global_attention",
    )(gq, gk, gv)


def _route(scores, biased):
    rows = [biased[r:r + 1, :] for r in range(N_EXPERTS)]
    raw = [scores[r:r + 1, :] for r in range(N_EXPERTS)]

    def top2_sum(vals):
        best = None
        for a in range(len(vals)):
            for b in range(a + 1, len(vals)):
                pair = vals[a] + vals[b]
                best = pair if best is None else jnp.maximum(best, pair)
        return best

    group_scores = [top2_sum(rows[g * EXPERTS_PER_GROUP:(g + 1) * EXPERTS_PER_GROUP]) for g in range(N_GROUPS)]
    group = jnp.zeros_like(group_scores[0], dtype=jnp.int32)
    best = group_scores[0]
    for g in range(1, N_GROUPS):
        better = group_scores[g] > best
        group = jnp.where(better, g, group)
        best = jnp.where(better, group_scores[g], best)

    def in_group(table, k):
        val = table[k]
        for g in range(1, N_GROUPS):
            val = jnp.where(group == g, table[g * EXPERTS_PER_GROUP + k], val)
        return val

    vals = [in_group(rows, k) for k in range(EXPERTS_PER_GROUP)]
    unbiased = [in_group(raw, k) for k in range(EXPERTS_PER_GROUP)]

    def first_argmax(cands):
        idx = jnp.zeros_like(group)
        top = cands[0]
        for k in range(1, len(cands)):
            better = cands[k] > top
            idx = jnp.where(better, k, idx)
            top = jnp.where(better, cands[k], top)
        return idx

    i1 = first_argmax(vals)
    i2 = first_argmax([jnp.where(i1 == k, -jnp.inf, vals[k]) for k in range(EXPERTS_PER_GROUP)])

    def pick(idx):
        val = unbiased[0]
        for k in range(1, EXPERTS_PER_GROUP):
            val = jnp.where(idx == k, unbiased[k], val)
        return val

    s1, s2 = pick(i1), pick(i2)
    total = s1 + s2
    lo = jnp.minimum(i1, i2)
    hi = jnp.maximum(i1, i2)
    first_of_lo = jnp.where(lo == 0, 0, jnp.where(lo == 1, 3, 5))
    pair = group * PAIRS_PER_GROUP + first_of_lo + (hi - lo - 1)
    first_is_lo = i1 < i2
    weights = jnp.concatenate([jnp.where(first_is_lo, s1, s2) / total, jnp.where(first_is_lo, s2, s1) / total],
                              axis=0)
    return pair, weights


def _slot_ranks(pair, carry):
    tm = pair.shape[1]
    pair_id = lax.broadcasted_iota(jnp.int32, (N_PAIRS, tm), 0)
    upper = jnp.where(lax.broadcasted_iota(jnp.int32, (tm, tm), 0) <= lax.broadcasted_iota(jnp.int32, (tm, tm), 1),
                      1.0, 0.0).astype(BF16)
    hit = jnp.where(pair_id == pair, 1.0, 0.0)
    inclusive = _dot(hit.astype(BF16), upper)
    rank = jnp.sum(hit * (carry + inclusive - 1.0), axis=0, keepdims=True)
    return rank.astype(jnp.int32), carry + jnp.sum(hit, axis=1, keepdims=True)


def _store_row_tiles(ref, val):
    rows = val.shape[0]
    for j in range(val.shape[1] // LANES):
        ref[pl.ds(j, rows, stride=SUBLANES), :] = val[:, j * LANES:(j + 1) * LANES]


def _load_row_tiles(ref, rows):
    return jnp.concatenate([ref[pl.ds(j, rows, stride=SUBLANES), :] for j in range(ref.shape[0] // rows)], axis=1)


def _merge_kernel(oret, oswa, oga, ar, as_, aa, *refs, first_layer, n_lat_tiles, n_tiles, n_batch):
    if first_layer:
        x_ref, c_ref, *refs = refs
    else:
        x_ref, *refs = refs
    (mod_ref, wr_ref, ws_ref, wa_ref, wout_ref, g2_ref, wrt_ref, br_ref,
     xo_ref, h2t_ref, slot_ref, w_ref, cnt_ref, h2_prev, carry_ref) = refs
    s = pl.program_id(0)
    n_steps = n_batch * n_tiles

    def merge_tile():
        b = s // n_tiles
        i = s - b * n_tiles
        m = mod_ref[pl.ds(jnp.where(i < n_lat_tiles, b, n_batch), 1), :]

        def gate(a_ref):
            return jax.nn.sigmoid(a_ref[0].astype(F32))

        y = (gate(ar) * _dot(oret[0], wr_ref[0]) + gate(as_) * _dot(oswa[0], ws_ref[0])
             + gate(aa) * _dot(oga[0], wa_ref[0]))
        residual = _first_layer_tokens(x_ref, c_ref, i, n_lat_tiles) if first_layer else x_ref[0]
        x = residual + _mod_chunk(m, 2) * _dot(y.astype(BF16), wout_ref[0])
        xo_ref[0] = x
        h2 = _rms_normalize(x) * g2_ref[...] * (1.0 + _mod_chunk(m, 4)) + _mod_chunk(m, 3)
        _store_row_tiles(h2t_ref.at[0], h2)
        h2_prev[...] = h2

    def route_previous_tile():
        h2 = h2_prev[...]
        h_hi = h2.astype(BF16)
        h_lo = (h2 - h_hi.astype(F32)).astype(BF16)
        w = wrt_ref[...]
        w_hi = w.astype(BF16)
        w_lo = (w - w_hi.astype(F32)).astype(BF16)
        logits = _dot_nt(w_hi, h_hi) + _dot_nt(w_hi, h_lo) + _dot_nt(w_lo, h_hi)
        scores = jax.nn.sigmoid(logits)
        pair, weights = _route(scores, scores + br_ref[...])
        rank, carry = _slot_ranks(pair, carry_ref[:, 0:1])
        slot_ref[0] = jnp.concatenate([pair, rank], axis=0)
        w_ref[0] = weights
        carry_ref[...] = jnp.broadcast_to(carry, carry_ref.shape)
        cnt_ref[...] = jnp.broadcast_to(carry, cnt_ref.shape)

    @pl.when(s == 0)
    def _init():
        carry_ref[...] = jnp.zeros_like(carry_ref)

    @pl.when(s > 0)
    def _routing_step():
        route_previous_tile()

    @pl.when(s < n_steps)
    def _merge_step():
        merge_tile()


def _merge(oret, oswa, oga, ar, as_, aa, xa, mod, w_ret, w_swa, w_ga, wout, layer, g2, wrt, br,
           n_lat_tiles, n_tiles):
    first_layer = isinstance(xa, tuple)
    if first_layer:
        x, ctx = xa
        nb, length, d = x.shape
        nt = length + ctx.shape[1]
        assert n_tiles * TOKEN_TILE == nt, "the first layer must write every row of the joined stream"
    else:
        nb, nt, d = xa.shape
    rows = n_tiles * TOKEN_TILE
    n_steps = nb * n_tiles
    kern = functools.partial(_merge_kernel, first_layer=first_layer, n_lat_tiles=n_lat_tiles, n_tiles=n_tiles,
                             n_batch=nb)

    def cur(s):
        t = jnp.minimum(s, n_steps - 1)
        return t // n_tiles, t % n_tiles

    def prev(s):
        t = jnp.maximum(s - 1, 0)
        return t // n_tiles, t % n_tiles

    tile = pl.BlockSpec((1, TOKEN_TILE, d), lambda s: (*cur(s), 0))
    whole = lambda arr: pl.BlockSpec(arr.shape, lambda s: (0,) * arr.ndim)
    square = pl.BlockSpec((1, d, d), lambda s: (layer, 0, 0))
    small = pl.BlockSpec((1, 2, TOKEN_TILE), lambda s: (prev(s)[0], 0, prev(s)[1]))
    if first_layer:
        residual_specs, residual, aliases = _first_layer_specs(x, ctx, n_lat_tiles, cur), (x, ctx), {}
    else:
        residual_specs, residual, aliases = [tile], (xa,), {6: 0}
    return pl.pallas_call(
        kern,
        out_shape=[jax.ShapeDtypeStruct((nb, nt, d), F32),
                   jax.ShapeDtypeStruct((nb, rows * SUBLANES, LANES), F32),
                   jax.ShapeDtypeStruct((nb, 2, rows), jnp.int32),
                   jax.ShapeDtypeStruct((nb, 2, rows), F32),
                   jax.ShapeDtypeStruct((N_PAIRS, LANES), F32)],
        grid=(n_steps + 1,),
        in_specs=[tile, tile, tile, tile, tile, tile, *residual_specs, whole(mod), square, square, square, square,
                  pl.BlockSpec((1, d), lambda s: (0, 0)), whole(wrt), whole(br)],
        out_specs=[tile,
                   pl.BlockSpec((1, TOKEN_TILE * SUBLANES, LANES), lambda s: (*cur(s), 0)),
                   small, small,
                   pl.BlockSpec((N_PAIRS, LANES), lambda s: (0, 0))],
        scratch_shapes=[pltpu.VMEM((TOKEN_TILE, d), F32), pltpu.VMEM((N_PAIRS, LANES), F32)],
        input_output_aliases=aliases,
        compiler_params=_params(dimension_semantics=("arbitrary",)),
        name="merge_route",
    )(oret, oswa, oga, ar, as_, aa, *residual, mod, w_ret, w_swa, w_ga, wout, g2.reshape(1, d), wrt, br)


def _row_copy(src, dst, sem):
    return pltpu.make_async_copy(src, dst, sem)


def _tile_rows(r):
    return pl.ds(pl.multiple_of(r * SUBLANES, SUBLANES), SUBLANES)


def _dispatch_kernel(tail_ref, nv_ref, dest_ref, h2t_ref, buf_out, zeros_ref, sem, tail_sem):
    @pl.when((pl.program_id(0) == 0) & (pl.program_id(1) == 0))
    def _zero_tails():
        zeros_ref[...] = jnp.zeros_like(zeros_ref)

        def zero_block(first_slot):
            rows = pl.ds(pl.multiple_of(first_slot * SUBLANES, SUBLANES), MOE_ROWS * SUBLANES)
            copy = pltpu.make_async_copy(zeros_ref, buf_out.at[rows, :], tail_sem)
            copy.start()
            copy.wait()

        for e in range(N_PAIRS):
            @pl.when(tail_ref[e] >= 0)
            def _():
                zero_block(tail_ref[e])

        def unused(blk, carry):
            zero_block(blk * MOE_ROWS)
            return carry

        lax.fori_loop(nv_ref[0], buf_out.shape[0] // (MOE_ROWS * SUBLANES), unused, 0)

    def issue(r2, carry):
        for p in range(2):
            r = 2 * r2 + p
            d = dest_ref[0, 0, r]
            _row_copy(h2t_ref.at[0, _tile_rows(r), :], buf_out.at[_tile_rows(d), :], sem).start(priority=p)
        return carry

    lax.fori_loop(0, TOKEN_TILE // 2, issue, 0, unroll=8)

    def drain(r, carry):
        _row_copy(h2t_ref.at[0, _tile_rows(0), :], buf_out.at[_tile_rows(0), :], sem).wait()
        return carry

    lax.fori_loop(0, TOKEN_TILE, drain, 0, unroll=8)


def _dispatch(tail_start, n_valid, dest_tiles, h2t, n_slots):
    nb, rows8, _ = h2t.shape
    n_tiles = rows8 // (TOKEN_TILE * SUBLANES)
    grid_spec = pltpu.PrefetchScalarGridSpec(
        num_scalar_prefetch=2,
        grid=(nb, n_tiles),
        in_specs=[pl.BlockSpec((1, 1, TOKEN_TILE), lambda b, i, tail, nv: (b * n_tiles + i, 0, 0),
                               memory_space=pltpu.SMEM),
                  pl.BlockSpec((1, TOKEN_TILE * SUBLANES, LANES), lambda b, i, tail, nv: (b, i, 0))],
        out_specs=pl.BlockSpec(memory_space=pl.ANY),
        scratch_shapes=[pltpu.VMEM((MOE_ROWS * SUBLANES, LANES), F32),
                        pltpu.SemaphoreType.DMA(()), pltpu.SemaphoreType.DMA(())],
    )
    return pl.pallas_call(
        _dispatch_kernel,
        out_shape=jax.ShapeDtypeStruct((n_slots * SUBLANES, LANES), F32),
        grid_spec=grid_spec,
        compiler_params=_params(has_side_effects=True, dimension_semantics=("arbitrary", "arbitrary")),
        name="moe_dispatch",
    )(tail_start, n_valid, dest_tiles, h2t)


def _expert_kernel(ea_ref, eb_ref, nv_ref, x_ref, *refs):
    weights = refs[:6]
    y_ref = refs[6]
    weights_bf = refs[7:]
    i = pl.program_id(0)

    @pl.when(i < nv_ref[0])
    def _():
        last = jnp.maximum(i - 1, 0)

        @pl.when((i == 0) | (ea_ref[i] != ea_ref[last]) | (eb_ref[i] != eb_ref[last]))
        def _():
            for w_ref, w_bf in zip(weights, weights_bf):
                w_bf[...] = w_ref[0, 0].astype(BF16)

        x = _load_row_tiles(x_ref, MOE_ROWS).astype(BF16)
        for which in range(2):
            wg_bf, wu_bf, wd_bf = weights_bf[3 * which:3 * which + 3]
            hid = _silu(_dot(x, wg_bf[...])) * _dot(x, wu_bf[...])
            y = _dot(hid.astype(BF16), wd_bf[...])
            for j in range(y.shape[1] // LANES):
                y_ref[pl.ds(which * SUBLANES + j, MOE_ROWS, stride=2 * SUBLANES), :] = y[:, j * LANES:(j + 1) * LANES]

    @pl.when(i >= nv_ref[0])
    def _():
        y_ref[...] = jnp.zeros_like(y_ref)


def _experts(block_a, block_b, n_valid, buf, wg, wu, wd, layer):
    n_slots = buf.shape[0] // SUBLANES
    d = wg.shape[2]
    used = lambda i, nv: jnp.maximum(jnp.minimum(i, nv[0] - 1), 0)
    rows = pl.BlockSpec((MOE_ROWS * SUBLANES, LANES), lambda i, ea, eb, nv: (used(i, nv), 0))
    up = lambda pick: pl.BlockSpec((1, 1, d, D_EXPERT),
                                   lambda i, ea, eb, nv: (layer, pick(ea, eb)[used(i, nv)], 0, 0))
    down = lambda pick: pl.BlockSpec((1, 1, D_EXPERT, d),
                                     lambda i, ea, eb, nv: (layer, pick(ea, eb)[used(i, nv)], 0, 0))
    first = lambda ea, eb: ea
    second = lambda ea, eb: eb
    grid_spec = pltpu.PrefetchScalarGridSpec(
        num_scalar_prefetch=3,
        grid=(n_slots // MOE_ROWS,),
        in_specs=[rows, up(first), up(first), down(first), up(second), up(second), down(second)],
        out_specs=pl.BlockSpec((MOE_ROWS * 2 * SUBLANES, LANES), lambda i, ea, eb, nv: (i, 0)),
        scratch_shapes=[pltpu.VMEM((d, D_EXPERT), BF16), pltpu.VMEM((d, D_EXPERT), BF16),
                        pltpu.VMEM((D_EXPERT, d), BF16)] * 2,
    )
    return pl.pallas_call(
        _expert_kernel,
        out_shape=jax.ShapeDtypeStruct((n_slots * 2 * SUBLANES, LANES), F32),
        grid_spec=grid_spec,
        compiler_params=_params(dimension_semantics=("arbitrary",)),
        name="moe_experts",
    )(block_a, block_b, n_valid, buf, wg, wu, wd, wg, wu, wd)


def _combine_kernel(dest_ref, next_ref, y_hbm, wt_ref, x_ref, mod_ref, g_ref, *rest,
                    n_lat_tiles, n_tiles, n_batch, last):
    if last:
        out_ref, gbuf, sems = rest
    else:
        xo_ref, h_ref, gbuf, sems = rest
    t = pl.program_id(0)
    slot = lax.rem(t, 2)

    def pair_rows(r):
        return pl.ds(pl.multiple_of(r * 2 * SUBLANES, 2 * SUBLANES), 2 * SUBLANES)

    def gather(idx_ref, into):
        def issue(r2, carry):
            for p in range(2):
                r = 2 * r2 + p
                d = idx_ref[0, 0, r]
                _row_copy(y_hbm.at[pair_rows(d), :], gbuf.at[into, pair_rows(r), :],
                          sems.at[into]).start(priority=p)
            return carry

        lax.fori_loop(0, TOKEN_TILE // 2, issue, 0, unroll=8)

    @pl.when(t == 0)
    def _():
        gather(dest_ref, 0)

    @pl.when(t + 1 < n_batch * n_tiles)
    def _():
        gather(next_ref, 1 - slot)

    def drain(r, carry):
        _row_copy(y_hbm.at[pair_rows(0), :], gbuf.at[slot, pair_rows(0), :], sems.at[slot]).wait()
        return carry

    lax.fori_loop(0, TOKEN_TILE, drain, 0, unroll=8)

    b = t // n_tiles
    m = mod_ref[pl.ds(jnp.where(t - b * n_tiles < n_lat_tiles, b, n_batch), 1), :]
    wt = wt_ref[0]

    def expert_output(which):
        return jnp.concatenate(
            [gbuf[slot, pl.ds(which * SUBLANES + j, TOKEN_TILE, stride=2 * SUBLANES), :]
             for j in range(SUBLANES)], axis=1)

    moe = expert_output(0) * wt[:, 0:1] + expert_output(1) * wt[:, 1:2]
    x = x_ref[0] + _mod_chunk(m, 5) * moe
    if last:
        out_ref[0] = _rms_normalize(x) * g_ref[...]
    else:
        xo_ref[0] = x
        h = _rms_normalize(x) * g_ref[...] * (1.0 + _mod_chunk(m, 7)) + _mod_chunk(m, 6)
        h_ref[0] = h.astype(BF16)


def _combine(dest_tiles, y, wt, xa, mod, g, n_lat_tiles, n_tiles, last):
    nb, nt, d = xa.shape
    rows = n_tiles * TOKEN_TILE
    n_steps = nb * n_tiles
    kern = functools.partial(_combine_kernel, n_lat_tiles=n_lat_tiles, n_tiles=n_tiles, n_batch=nb, last=last)
    tile = pl.BlockSpec((1, TOKEN_TILE, d), lambda t: (t // n_tiles, t % n_tiles, 0))
    if last:
        out_shape = jax.ShapeDtypeStruct((nb, rows, d), F32)
        out_specs = tile
        aliases = {}
    else:
        out_shape = [jax.ShapeDtypeStruct((nb, nt, d), F32), jax.ShapeDtypeStruct((nb, nt, d), BF16)]
        out_specs = [tile, tile]
        aliases = {4: 0}
    return pl.pallas_call(
        kern,
        out_shape=out_shape,
        grid=(n_steps,),
        in_specs=[pl.BlockSpec((1, 1, TOKEN_TILE), lambda t: (t, 0, 0), memory_space=pltpu.SMEM),
                  pl.BlockSpec((1, 1, TOKEN_TILE), lambda t: (jnp.minimum(t + 1, n_steps - 1), 0, 0),
                               memory_space=pltpu.SMEM),
                  pl.BlockSpec(memory_space=pl.ANY),
                  pl.BlockSpec((1, TOKEN_TILE, 2), lambda t: (t // n_tiles, t % n_tiles, 0)),
                  tile,
                  pl.BlockSpec(mod.shape, lambda t: (0, 0)),
                  pl.BlockSpec((1, d), lambda t: (0, 0))],
        out_specs=out_specs,
        scratch_shapes=[pltpu.VMEM((2, TOKEN_TILE * 2 * SUBLANES, LANES), F32), pltpu.SemaphoreType.DMA((2,))],
        input_output_aliases=aliases,
        compiler_params=_params(dimension_semantics=("arbitrary",)),
        name="moe_combine",
    )(dest_tiles, dest_tiles, y, wt, xa, mod, g.reshape(1, d))


def _slot_plan(pair, rank, counts):
    nb, rows = pair.shape
    padded = (counts + MOE_ROWS - 1) // MOE_ROWS * MOE_ROWS
    pad_ends = jnp.cumsum(padded)
    pad_starts = pad_ends - padded
    one_hot = pair[..., None] == jnp.arange(N_PAIRS, dtype=jnp.int32)
    dest = jnp.sum(jnp.where(one_hot, pad_starts, 0), axis=-1) + rank
    n_blocks = (nb * rows + N_PAIRS * (MOE_ROWS - 1) + MOE_ROWS - 1) // MOE_ROWS
    first_slot = jnp.arange(n_blocks, dtype=jnp.int32) * MOE_ROWS
    block_pair = jnp.minimum(jnp.sum((pad_ends[None, :] <= first_slot[:, None]).astype(jnp.int32), axis=1),
                             N_PAIRS - 1)
    in_group = block_pair % PAIRS_PER_GROUP
    lo = (in_group >= 3).astype(jnp.int32) + (in_group >= 5).astype(jnp.int32)
    hi = in_group - jnp.where(lo == 0, 0, jnp.where(lo == 1, 3, 5)) + lo + 1
    base = (block_pair // PAIRS_PER_GROUP) * EXPERTS_PER_GROUP
    block_a = (base + lo).astype(jnp.int32)
    block_b = (base + hi).astype(jnp.int32)
    n_valid = (pad_ends[-1:] // MOE_ROWS).astype(jnp.int32)
    tail_start = jnp.where(padded > 0, pad_ends - MOE_ROWS, -1).astype(jnp.int32)
    n_tiles = rows // TOKEN_TILE
    dest_tiles = dest.astype(jnp.int32).reshape(nb * n_tiles, 1, TOKEN_TILE)
    return dest_tiles, block_a, block_b, n_valid, tail_start, n_blocks * MOE_ROWS


def _rope_tables(length, n_ctx):
    rows = length // GRID_W
    row = jnp.repeat(jnp.arange(rows, dtype=jnp.int32), GRID_W).astype(F32)
    col = jnp.tile(jnp.arange(GRID_W, dtype=jnp.int32), rows).astype(F32)

    def table(hd):
        quarter = hd // 4
        freqs = ROPE_THETA ** (-jnp.arange(quarter, dtype=F32) / quarter)
        ang_r = row[:, None] * freqs[None, :]
        ang_c = col[:, None] * freqs[None, :]
        cos = jnp.concatenate([jnp.cos(ang_r), jnp.cos(ang_r), jnp.cos(ang_c), jnp.cos(ang_c)], axis=-1)
        sin = jnp.concatenate([-jnp.sin(ang_r), jnp.sin(ang_r), -jnp.sin(ang_c), jnp.sin(ang_c)], axis=-1)
        cos = jnp.concatenate([cos, jnp.ones((n_ctx, hd), F32)], axis=0)
        sin = jnp.concatenate([sin, jnp.zeros((n_ctx, hd), F32)], axis=0)
        reps = LANES // hd
        return jnp.tile(cos, (1, reps)), jnp.tile(sin, (1, reps))

    c128, s128 = table(RET_DK)
    c64, s64 = table(SWA_HD)
    return c128, s128, c64, s64


def kernel(x, c, ctx, c_ctx, w_mod, b_mod, g_norm1, g_norm2, w_in, ret_decay_logit, swa_sink, g_qnorm, g_knorm,
           w_br_ret, w_br_swa, w_br_ga, w_out, w_router, b_router, w_gate, w_up, w_down, g_final):
    nb, length, d = x.shape
    n_ctx = ctx.shape[1]
    depth = w_mod.shape[0]
    nt = length + n_ctx
    n_lat_tiles = length // TOKEN_TILE
    n_all_tiles = nt // TOKEN_TILE
    assert GA_HD == RET_DK == LANES and 2 * SWA_HD == LANES
    assert length % TOKEN_TILE == 0 and n_ctx == TOKEN_TILE and length % n_ctx == 0

    mod_rows = 8
    c_rows = jnp.concatenate([c, c_ctx[None, :], jnp.zeros((mod_rows - nb - 1, d), F32)], axis=0)
    mods = _modulation(c_rows, w_mod, b_mod)
    tables = _rope_tables(length, n_ctx)
    wrt = w_router.astype(F32).T
    br = b_router.astype(F32).reshape(N_EXPERTS, 1)

    assert depth >= 2, "the first layer's merge kernel must also cover the context rows"
    xa = (x, ctx)
    h = (x, ctx, mods[0], g_norm1[0])
    w_in_bf, w_ret_bf, w_swa_bf, w_ga_bf, w_out_bf = (
        w.astype(BF16) for w in (w_in, w_br_ret, w_br_swa, w_br_ga, w_out))
    out = None
    for l in range(depth):
        need_ctx = l < depth - 1
        n_tiles = n_all_tiles if need_ctx else n_lat_tiles
        (rq, rk, rv, ru, sq, sk, sv, gq, gk, gv, ar, as_, aa) = _inproj(
            h, w_in_bf, l, tables, g_qnorm[l], g_knorm[l])
        o_ret = _retention(ret_decay_logit[l].astype(F32), rq, rk, rv, ru, length, n_ctx, need_ctx)
        o_swa = _window_attention(swa_sink[l].astype(F32), sq, sk, sv, length, n_ctx, need_ctx)
        o_ga = _global_attention(gq, gk, gv, length, n_ctx, need_ctx)
        xa, h2t, slot_info, weights, counts = _merge(
            o_ret, o_swa, o_ga, ar, as_, aa, xa, mods[l], w_ret_bf, w_swa_bf, w_ga_bf, w_out_bf, l,
            g_norm2[l], wrt, br, n_lat_tiles, n_tiles)
        dest_tiles, block_a, block_b, n_valid, tail_start, n_slots = _slot_plan(
            slot_info[:, 0], slot_info[:, 1], counts[:, 0].astype(jnp.int32))
        buf = _dispatch(tail_start, n_valid, dest_tiles, h2t, n_slots)
        y = _experts(block_a, block_b, n_valid, buf, w_gate, w_up, w_down, l)
        wt = weights.transpose(0, 2, 1)
        if need_ctx:
            mod_pair = jnp.concatenate([mods[l], mods[l + 1][:, :2 * d]], axis=1)
            xa, h = _combine(dest_tiles, y, wt, xa, mod_pair, g_norm1[l + 1], n_lat_tiles, n_tiles, last=False)
        else:
            out = _combine(dest_tiles, y, wt, xa, mods[l], g_final, n_lat_tiles, n_tiles, last=True)
    return out
```

```python
import functools

import jax
import jax.numpy as jnp
from jax import lax
from jax.experimental import pallas as pl
from jax.experimental.pallas import tpu as pltpu

F32 = jnp.float32
BF16 = jnp.bfloat16

D_MODEL = 1024
GRID_W = 64
NORM_EPS = 1e-6
ROPE_THETA = 10000.0
RET_HEADS, RET_DK, RET_DV, RET_CHUNK = 4, 128, 256, 256
SWA_HEADS, SWA_KV_HEADS, SWA_HD, WINDOW = 16, 2, 64, 128
GA_HEADS, GA_KV_HEADS, GA_HD = 8, 2, 128
N_EXPERTS, N_GROUPS, EXPERTS_PER_GROUP, D_EXPERT = 16, 4, 4, 512

RET_QK = RET_HEADS * RET_DK
RET_V = RET_HEADS * RET_DV
SWA_Q = SWA_HEADS * SWA_HD
SWA_KV = SWA_KV_HEADS * SWA_HD
GA_Q = GA_HEADS * GA_HD
GA_KV = GA_KV_HEADS * GA_HD
SPLITS = (RET_QK, RET_QK, RET_V, RET_V, SWA_Q, SWA_KV, SWA_KV, GA_Q, GA_KV, GA_KV,
          D_MODEL, D_MODEL, D_MODEL)

LANES = 128
SUBLANES = 8
TOKEN_TILE = 256
QUERY_TILE = 128
GA_QUERY_TILE = 256
MOE_ROWS = 512
GA_KEY_CHUNK = 256
RET_HEADS_PER_STEP = 2
SWA_PARTS = 2
SWA_TILES_PER_CHUNK = 2
MOD_COLS = 1536
VMEM_LIMIT = 56 * 1024 * 1024
MASKED = -1e30
LOG2E = 1.4426950408889634

_W_WIDTHS = (RET_QK, RET_QK, RET_V, RET_V, SWA_Q, 2 * SWA_KV, 2 * SWA_KV, GA_Q, GA_KV, GA_KV,
             D_MODEL, D_MODEL, D_MODEL)
_IN_OFFS = tuple(sum(SPLITS[:i]) for i in range(len(SPLITS)))


def _dot(a, b):
    return jnp.dot(a, b, preferred_element_type=F32)


def _dot_nt(a, b):
    return lax.dot_general(a, b, (((1,), (1,)), ((), ())), preferred_element_type=F32)


def _dot_tn(a, b):
    return lax.dot_general(a, b, (((0,), (0,)), ((), ())), preferred_element_type=F32)


def _silu(x):
    return x * jax.nn.sigmoid(x)


def _rms_normalize(x):
    return x * lax.rsqrt(jnp.mean(x * x, axis=-1, keepdims=True) + NORM_EPS)


def _params(**kw):
    return pltpu.CompilerParams(vmem_limit_bytes=VMEM_LIMIT, **kw)


def _mod_kernel(c_ref, w_ref, b_ref, o_ref):
    a = _silu(c_ref[...])
    o_ref[0] = _dot(a.astype(BF16), w_ref[0].astype(BF16)) + b_ref[0]


def _modulation(c_rows, w_mod, b_mod):
    depth, d, n = w_mod.shape
    rows = c_rows.shape[0]
    return pl.pallas_call(
        _mod_kernel,
        out_shape=jax.ShapeDtypeStruct((depth, rows, n), F32),
        grid=(depth, n // MOD_COLS),
        in_specs=[pl.BlockSpec((rows, d), lambda l, j: (0, 0)),
                  pl.BlockSpec((1, d, MOD_COLS), lambda l, j: (l, 0, j)),
                  pl.BlockSpec((1, 1, MOD_COLS), lambda l, j: (l, 0, j))],
        out_specs=pl.BlockSpec((1, rows, MOD_COLS), lambda l, j: (l, 0, j)),
        compiler_params=_params(),
        name="modulation",
    )(c_rows, w_mod, b_mod.reshape(depth, 1, n))


def _mod_row(mod_ref, n_lat_tiles, n_batch):
    b = pl.program_id(0)
    i = pl.program_id(1)
    r = jnp.where(i < n_lat_tiles, b, n_batch)
    return mod_ref[pl.ds(r, 1), :]


def _mod_chunk(m, k):
    return m[:, k * D_MODEL:(k + 1) * D_MODEL]


def _rope(x, cos, sin, quarter):
    lane = lax.broadcasted_iota(jnp.int32, x.shape, 1)
    first = (lane % (2 * quarter)) < quarter
    partner = jnp.where(first, pltpu.roll(x, LANES - quarter, 1), pltpu.roll(x, quarter, 1))
    return x * cos + partner * sin


def _first_layer_tokens(x_ref, c_ref, i, n_lat_tiles):
    return jnp.where(i < n_lat_tiles, x_ref[0], c_ref[0])


def _inproj_kernel(*refs, first_layer, n_lat_tiles, n_batch):
    if first_layer:
        x_ref, c_ref, mod_ref, g_ref, *refs = refs
        m = _mod_row(mod_ref, n_lat_tiles, n_batch)
        x = _first_layer_tokens(x_ref, c_ref, pl.program_id(1), n_lat_tiles)
        h = (_rms_normalize(x) * g_ref[...] * (1.0 + _mod_chunk(m, 1)) + _mod_chunk(m, 0)).astype(BF16)
    else:
        h_ref, *refs = refs
        h = h_ref[0]
    (w_ref, c128_ref, s128_ref, c64_ref, s64_ref, gq_ref, gk_ref,
     rq, rk, rv, ru, sq, sk, sv, gq, gk, gv, ar, as_, aa) = refs
    c128, s128 = c128_ref[...], s128_ref[...]
    c64, s64 = c64_ref[...], s64_ref[...]

    def proj(idx):
        return _dot(h, w_ref[0, :, _IN_OFFS[idx]:_IN_OFFS[idx] + SPLITS[idx]])

    def slabs(acc):
        return [acc[:, s * LANES:(s + 1) * LANES] for s in range(acc.shape[1] // LANES)]

    def store(ref, s, val):
        ref[0, :, s * LANES:(s + 1) * LANES] = val.astype(ref.dtype)

    def head_per_slab(acc):
        low = lax.broadcasted_iota(jnp.int32, acc.shape, 1) < SWA_HD
        other = pltpu.roll(acc, SWA_HD, 1)
        return [jnp.where(low, acc, other), jnp.where(low, other, acc)]

    for s, xs in enumerate(slabs(proj(0))):
        store(rq, s, _rope(xs, c128, s128, RET_DK // 4))
    for s, xs in enumerate(slabs(proj(1))):
        store(rk, s, _rope(xs, c128, s128, RET_DK // 4) * (RET_DK ** -0.5))
    rv[0] = proj(2).astype(rv.dtype)
    ru[0] = proj(3).astype(ru.dtype)
    for s, xs in enumerate(slabs(proj(4))):
        store(sq, s, _rope(xs, c64, s64, SWA_HD // 4) * (SWA_HD ** -0.5 * LOG2E))
    for s, xs in enumerate(head_per_slab(proj(5))):
        store(sk, s, _rope(xs, c64, s64, SWA_HD // 4))
    for s, xs in enumerate(head_per_slab(proj(6))):
        store(sv, s, xs)
    for s, xs in enumerate(slabs(proj(7))):
        xn = _rms_normalize(xs) * gq_ref[...]
        store(gq, s, _rope(xn, c128, s128, GA_HD // 4) * (GA_HD ** -0.5 * LOG2E))
    for s, xs in enumerate(slabs(proj(8))):
        xn = _rms_normalize(xs) * gk_ref[...]
        store(gk, s, _rope(xn, c128, s128, GA_HD // 4))
    gv[0] = proj(9).astype(gv.dtype)
    ar[0] = proj(10).astype(ar.dtype)
    as_[0] = proj(11).astype(as_.dtype)
    aa[0] = proj(12).astype(aa.dtype)


def _first_layer_specs(x, ctx, n_lat_tiles, index):
    d = x.shape[2]
    return [pl.BlockSpec((1, TOKEN_TILE, d), lambda *ids: (index(*ids)[0],
                                                          jnp.minimum(index(*ids)[1], n_lat_tiles - 1), 0)),
            pl.BlockSpec((1, TOKEN_TILE, d), lambda *ids: (index(*ids)[0], 0, 0))]


def _inproj(tokens, w_in, layer, tables, g_q, g_k):
    first_layer = isinstance(tokens, tuple)
    tile = lambda width: pl.BlockSpec((1, TOKEN_TILE, width), lambda b, i: (b, i, 0))
    if first_layer:
        x, ctx, mod, g = tokens
        nb, length, d = x.shape
        nt = length + ctx.shape[1]
        n_lat_tiles = length // TOKEN_TILE
        lead_specs = _first_layer_specs(x, ctx, n_lat_tiles, lambda b, i: (b, i)) + [
            pl.BlockSpec(mod.shape, lambda b, i: (0, 0)), pl.BlockSpec((1, d), lambda b, i: (0, 0))]
        lead = (x, ctx, mod, g.reshape(1, d))
    else:
        nb, nt, d = tokens.shape
        n_lat_tiles = 0
        lead_specs = [tile(d)]
        lead = (tokens,)
    tab = pl.BlockSpec((TOKEN_TILE, LANES), lambda b, i: (i, 0))
    vec = pl.BlockSpec((1, LANES), lambda b, i: (0, 0))
    outs = [jax.ShapeDtypeStruct((nb, nt, w), BF16) for w in _W_WIDTHS]
    kern = functools.partial(_inproj_kernel, first_layer=first_layer, n_lat_tiles=n_lat_tiles, n_batch=nb)
    return pl.pallas_call(
        kern,
        out_shape=outs,
        grid=(nb, nt // TOKEN_TILE),
        in_specs=lead_specs + [
            pl.BlockSpec((1, d, w_in.shape[2]), lambda b, i: (layer, 0, 0), pipeline_mode=pl.Buffered(1)),
            tab, tab, tab, tab, vec, vec],
        out_specs=[tile(w) for w in _W_WIDTHS],
        compiler_params=_params(),
        name="inproj",
    )(*lead, w_in, *tables, g_q.reshape(1, LANES), g_k.reshape(1, LANES))


def _log_sigmoid(x):
    return jnp.minimum(x, 0.0) - jnp.log1p(jnp.exp(-jnp.abs(x)))


def _ret_kernel(logit_ref, q_ref, k_ref, v_ref, u_ref, o_ref, acc_ref, sf_ref, sb_ref,
                *, length, n_ctx, need_ctx):
    chunk = RET_CHUNK
    n_chunks = length // chunk
    half = n_chunks // 2

    def rows_f32(shape):
        return lax.broadcasted_iota(jnp.int32, shape, 0).astype(F32)

    def finish(o, u):
        return (_rms_normalize(o) * _silu(u.astype(F32))).astype(o_ref.dtype)

    ci = rows_f32((chunk, RET_DK))
    heads = []
    for hi in range(RET_HEADS_PER_STEP):
        hh = pl.program_id(1) * RET_HEADS_PER_STEP + hi
        lgf = _log_sigmoid(jnp.full((1, 1), logit_ref[0, hh], F32))
        lgb = _log_sigmoid(jnp.full((1, 1), logit_ref[1, hh], F32))
        qk = slice(hi * RET_DK, (hi + 1) * RET_DK)
        vu = slice(hi * RET_DV, (hi + 1) * RET_DV)

        def both_ways_decay(n, lgf=lgf, lgb=lgb):
            diff = rows_f32((n, n)) - lax.broadcasted_iota(jnp.int32, (n, n), 1).astype(F32)
            return jnp.where(diff >= 0, jnp.exp(jnp.maximum(diff, 0.0) * lgf),
                             jnp.exp(jnp.maximum(-diff, 0.0) * lgb))

        kx = k_ref[0, length:length + n_ctx, qk].astype(F32)
        vx = v_ref[0, length:length + n_ctx, vu]
        lx = rows_f32((n_ctx, RET_DK))
        sf_ref[hi] = _dot_tn((kx * jnp.exp((n_ctx - 1.0 - lx) * lgf)).astype(BF16), vx)
        sb_ref[hi] = _dot_tn((kx * jnp.exp(lx * lgb)).astype(BF16), vx)
        if need_ctx:
            qx = q_ref[0, length:length + n_ctx, qk]
            sx = _dot_nt(qx, kx.astype(BF16)) * both_ways_decay(n_ctx)
            o_ref[0, length:length + n_ctx, vu] = finish(_dot(sx.astype(BF16), vx),
                                                         u_ref[0, length:length + n_ctx, vu])
        heads.append(dict(
            qk=qk, vu=vu, intra=both_ways_decay(chunk),
            q_dec_f=jnp.exp((ci + 1.0) * lgf), k_dec_f=jnp.exp((chunk - 1.0 - ci) * lgf),
            q_dec_b=jnp.exp((chunk - ci) * lgb), k_dec_b=jnp.exp(ci * lgb),
            chunk_dec_f=jnp.exp(chunk * lgf), chunk_dec_b=jnp.exp(chunk * lgb)))

    def sweep(s, first_touch):
        rf = pl.ds(pl.multiple_of(s * chunk, chunk), chunk)
        rb = pl.ds(pl.multiple_of((n_chunks - 1 - s) * chunk, chunk), chunk)
        for hi, hd in enumerate(heads):
            qk, vu = hd["qk"], hd["vu"]
            qf, kf, vf = q_ref[0, rf, qk], k_ref[0, rf, qk], v_ref[0, rf, vu]
            sc = _dot_nt(qf, kf) * hd["intra"]
            state_f = sf_ref[hi]
            o_f = (_dot(sc.astype(BF16), vf)
                   + _dot((qf.astype(F32) * hd["q_dec_f"]).astype(BF16), state_f.astype(BF16)))
            sf_ref[hi] = state_f * hd["chunk_dec_f"] + _dot_tn((kf.astype(F32) * hd["k_dec_f"]).astype(BF16), vf)
            qb, kb, vb = q_ref[0, rb, qk], k_ref[0, rb, qk], v_ref[0, rb, vu]
            state_b = sb_ref[hi]
            o_b = _dot((qb.astype(F32) * hd["q_dec_b"]).astype(BF16), state_b.astype(BF16))
            sb_ref[hi] = state_b * hd["chunk_dec_b"] + _dot_tn((kb.astype(F32) * hd["k_dec_b"]).astype(BF16), vb)
            if first_touch:
                acc_ref[hi, rf, :] = o_f
                acc_ref[hi, rb, :] = o_b
            else:
                o_ref[0, rf, vu] = finish(acc_ref[hi, rf, :] + o_f, u_ref[0, rf, vu])
                o_ref[0, rb, vu] = finish(acc_ref[hi, rb, :] + o_b, u_ref[0, rb, vu])

    def first_half(s, carry):
        sweep(s, True)
        return carry

    def second_half(s, carry):
        sweep(s, False)
        return carry

    lax.fori_loop(0, half, first_half, 0)
    lax.fori_loop(half, n_chunks, second_half, 0)


def _retention(logit, rq, rk, rv, ru, length, n_ctx, need_ctx):
    nb, nt, _ = rq.shape
    assert (length // RET_CHUNK) % 2 == 0 and RET_HEADS % RET_HEADS_PER_STEP == 0
    out_rows = nt if need_ctx else length
    kern = functools.partial(_ret_kernel, length=length, n_ctx=n_ctx, need_ctx=need_ctx)
    head = lambda width: pl.BlockSpec((1, nt, RET_HEADS_PER_STEP * width), lambda b, h: (b, 0, h))
    return pl.pallas_call(
        kern,
        out_shape=jax.ShapeDtypeStruct((nb, out_rows, RET_V), BF16),
        grid=(nb, RET_HEADS // RET_HEADS_PER_STEP),
        in_specs=[pl.BlockSpec(memory_space=pltpu.SMEM),
                  head(RET_DK), head(RET_DK), head(RET_DV), head(RET_DV)],
        out_specs=pl.BlockSpec((1, out_rows, RET_HEADS_PER_STEP * RET_DV), lambda b, h: (b, 0, h)),
        scratch_shapes=[pltpu.VMEM((RET_HEADS_PER_STEP, length, RET_DV), F32),
                        pltpu.VMEM((RET_HEADS_PER_STEP, RET_DK, RET_DV), F32),
                        pltpu.VMEM((RET_HEADS_PER_STEP, RET_DK, RET_DV), F32)],
        compiler_params=_params(),
        name="retention",
    )(logit, rq, rk, rv, ru)


def _swa_group(q_ref, g, kcat, vcat, biases, sink_ref, o_ref):
    tq = q_ref.shape[1]
    heads_per_group = SWA_HEADS // SWA_KV_HEADS
    slabs_per_group = heads_per_group // 2
    low = lax.broadcasted_iota(jnp.int32, (tq, LANES), 1) < SWA_HD
    zero = jnp.zeros((tq, LANES), q_ref.dtype)
    stacked = []
    for a in range(slabs_per_group):
        s0 = (g * slabs_per_group + a) * LANES
        slab = q_ref[0, :, s0:s0 + LANES]
        stacked.append(jnp.where(low, slab, zero))
        stacked.append(jnp.where(low, zero, slab))
    heads_per_part = heads_per_group // SWA_PARTS
    rows = heads_per_part * tq
    n_key_tiles = kcat.shape[0] // LANES
    for part in range(SWA_PARTS):
        q = jnp.concatenate(stacked[part * heads_per_part:(part + 1) * heads_per_part], axis=0)
        sink = jnp.concatenate(
            [jnp.full((tq, LANES), sink_ref[g * heads_per_group + part * heads_per_part + h] * LOG2E, F32)
             for h in range(heads_per_part)], axis=0)
        m = l = acc = None
        for t0 in range(0, n_key_tiles, SWA_TILES_PER_CHUNK):
            t1 = min(t0 + SWA_TILES_PER_CHUNK, n_key_tiles)
            s = _dot_nt(q, kcat[t0 * LANES:t1 * LANES])
            tiles = [s[:, t * LANES:(t + 1) * LANES] for t in range(t1 - t0)]
            for t in range(t0, t1):
                if t in biases:
                    tiles[t - t0] = tiles[t - t0] + biases[t][:rows]
            mc = jnp.broadcast_to(jnp.max(functools.reduce(jnp.maximum, tiles), axis=-1, keepdims=True),
                                  (rows, LANES))
            m_new = jnp.maximum(mc, sink if m is None else m)
            ps = [jnp.exp2(tile - m_new) for tile in tiles]
            pv = _dot(jnp.concatenate(ps, axis=1).astype(BF16), vcat[t0 * LANES:t1 * LANES])
            psum = functools.reduce(jnp.add, ps)
            if m is None:
                l, acc = psum, pv
            else:
                alpha = jnp.exp2(m - m_new)
                l, acc = alpha * l + psum, alpha * acc + pv
            m = m_new
        den = jnp.broadcast_to(jnp.sum(l, axis=-1, keepdims=True), (rows, LANES)) + jnp.exp2(sink - m)
        o = acc / den
        for a in range(heads_per_part // 2):
            s0 = (g * slabs_per_group + part * (heads_per_part // 2) + a) * LANES
            even = o[(2 * a) * tq:(2 * a + 1) * tq]
            odd = o[(2 * a + 1) * tq:(2 * a + 2) * tq]
            o_ref[0, :, s0:s0 + LANES] = jnp.where(low, even, odd).astype(o_ref.dtype)


def _swa_kernel(sink_ref, q_ref, kp_ref, kc_ref, kn_ref, kx_ref, vp_ref, vc_ref, vn_ref, vx_ref, o_ref,
                *, n_lat_tiles, need_ctx):
    j = pl.program_id(1)
    tq = q_ref.shape[1]
    n_ctx = kx_ref.shape[1]

    def lanes_of(ref, g):
        return ref[0, :, g * LANES:(g + 1) * LANES]

    @pl.when(j < n_lat_tiles)
    def _latent():
        ci = lax.broadcasted_iota(jnp.int32, (tq, tq), 1)
        ri = lax.broadcasted_iota(jnp.int32, (tq, tq), 0)
        far = 4 * tq
        ri_prev = ri + jnp.where(j > 0, 0, far)
        ri_next = ri - jnp.where(j < n_lat_tiles - 1, 0, far)
        zero = jnp.zeros((tq, tq), F32)
        masked = jnp.full((tq, tq), MASKED, F32)
        heads_per_part = SWA_HEADS // SWA_KV_HEADS // SWA_PARTS
        bias = {0: jnp.concatenate([jnp.where(ci >= ri_prev, zero, masked)] * heads_per_part, axis=0),
                2: jnp.concatenate([jnp.where(ci <= ri_next, zero, masked)] * heads_per_part, axis=0)}
        for g in range(SWA_KV_HEADS):
            kcat = jnp.concatenate([lanes_of(kp_ref, g), lanes_of(kc_ref, g), lanes_of(kn_ref, g),
                                    lanes_of(kx_ref, g)], axis=0)
            vcat = jnp.concatenate([lanes_of(vp_ref, g), lanes_of(vc_ref, g), lanes_of(vn_ref, g),
                                    lanes_of(vx_ref, g)], axis=0)
            _swa_group(q_ref, g, kcat, vcat, bias, sink_ref, o_ref)

    if need_ctx:
        @pl.when(j >= n_lat_tiles)
        def _context():
            for g in range(SWA_KV_HEADS):
                _swa_group(q_ref, g, lanes_of(kx_ref, g), lanes_of(vx_ref, g), {}, sink_ref, o_ref)


def _window_attention(sink, sq, sk, sv, length, n_ctx, need_ctx):
    nb, nt, _ = sq.shape
    tq = QUERY_TILE
    n_lat = length // tq
    out_rows = nt if need_ctx else length
    kvw = 2 * SWA_KV
    kern = functools.partial(_swa_kernel, n_lat_tiles=n_lat, need_ctx=need_ctx)
    prev = pl.BlockSpec((1, tq, kvw), lambda b, j: (b, jnp.clip(j - 1, 0, n_lat - 1), 0))
    cur = pl.BlockSpec((1, tq, kvw), lambda b, j: (b, jnp.minimum(j, n_lat - 1), 0))
    nxt = pl.BlockSpec((1, tq, kvw), lambda b, j: (b, jnp.minimum(j + 1, n_lat - 1), 0))
    ctx = pl.BlockSpec((1, n_ctx, kvw), lambda b, j: (b, length // n_ctx, 0))
    return pl.pallas_call(
        kern,
        out_shape=jax.ShapeDtypeStruct((nb, out_rows, SWA_Q), BF16),
        grid=(nb, out_rows // tq),
        in_specs=[pl.BlockSpec(memory_space=pltpu.SMEM),
                  pl.BlockSpec((1, tq, SWA_Q), lambda b, j: (b, j, 0)),
                  prev, cur, nxt, ctx, prev, cur, nxt, ctx],
        out_specs=pl.BlockSpec((1, tq, SWA_Q), lambda b, j: (b, j, 0)),
        compiler_params=_params(),
        name="window_attention",
    )(sink, sq, sk, sk, sk, sk, sv, sv, sv, sv)


def _ga_kernel(q_ref, k_ref, v_ref, o_ref, *, length, n_ctx, n_lat_tiles, need_ctx):
    j = pl.program_id(2)
    tq = q_ref.shape[1]
    heads_per_group = GA_HEADS // GA_KV_HEADS
    q = jnp.concatenate([q_ref[0, :, h * LANES:(h + 1) * LANES] for h in range(heads_per_group)], axis=0)

    rows = heads_per_group * tq

    def attend(key_lo, key_hi):
        m = l = acc = None
        for c0 in range(key_lo, key_hi, GA_KEY_CHUNK):
            c1 = min(c0 + GA_KEY_CHUNK, key_hi)
            s = _dot_nt(q, k_ref[0, c0:c1, :])
            parts = [s[:, t * LANES:(t + 1) * LANES] for t in range((c1 - c0) // LANES)]
            lane_max = functools.reduce(jnp.maximum, parts)
            mc = jnp.broadcast_to(jnp.max(lane_max, axis=-1, keepdims=True), (rows, LANES))
            m_new = mc if m is None else jnp.maximum(m, mc)
            ps = [jnp.exp2(part - m_new) for part in parts]
            pv = _dot(jnp.concatenate(ps, axis=1).astype(BF16), v_ref[0, c0:c1, :])
            psum = functools.reduce(jnp.add, ps)
            if m is None:
                l, acc = psum, pv
            else:
                alpha = jnp.exp2(m - m_new)
                l, acc = alpha * l + psum, alpha * acc + pv
            m = m_new
        o = acc / jnp.sum(l, axis=-1, keepdims=True)
        for h in range(heads_per_group):
            o_ref[0, :, h * LANES:(h + 1) * LANES] = o[h * tq:(h + 1) * tq].astype(o_ref.dtype)

    @pl.when(j < n_lat_tiles)
    def _latent():
        attend(0, length + n_ctx)

    if need_ctx:
        @pl.when(j >= n_lat_tiles)
        def _context():
            attend(length, length + n_ctx)


def _global_attention(gq, gk, gv, length, n_ctx, need_ctx):
    nb, nt, _ = gq.shape
    tq = GA_QUERY_TILE
    out_rows = nt if need_ctx else length
    group_w = GA_Q // GA_KV_HEADS
    kern = functools.partial(_ga_kernel, length=length, n_ctx=n_ctx, n_lat_tiles=length // tq,
                             need_ctx=need_ctx)
    kv = pl.BlockSpec((1, nt, GA_HD), lambda b, g, j: (b, 0, g))
    return pl.pallas_call(
        kern,
        out_shape=jax.ShapeDtypeStruct((nb, out_rows, GA_Q), BF16),
        grid=(nb, GA_KV_HEADS, out_rows // tq),
        in_specs=[pl.BlockSpec((1, tq, group_w), lambda b, g, j: (b, j, g)), kv, kv],
        out_specs=pl.BlockSpec((1, tq, group_w), lambda b, g, j: (b, j, g)),
        compiler_params=_params(),
        name="global_attention",
    )(gq, gk, gv)


def _route(scores, biased):
    rows = [biased[r:r + 1, :] for r in range(N_EXPERTS)]
    raw = [scores[r:r + 1, :] for r in range(N_EXPERTS)]

    def top2_sum(vals):
        best = None
        for a in range(len(vals)):
            for b in range(a + 1, len(vals)):
                pair = vals[a] + vals[b]
                best = pair if best is None else jnp.maximum(best, pair)
        return best

    group_scores = [top2_sum(rows[g * EXPERTS_PER_GROUP:(g + 1) * EXPERTS_PER_GROUP]) for g in range(N_GROUPS)]
    group = jnp.zeros_like(group_scores[0], dtype=jnp.int32)
    best = group_scores[0]
    for g in range(1, N_GROUPS):
        better = group_scores[g] > best
        group = jnp.where(better, g, group)
        best = jnp.where(better, group_scores[g], best)

    def in_group(table, k):
        val = table[k]
        for g in range(1, N_GROUPS):
            val = jnp.where(group == g, table[g * EXPERTS_PER_GROUP + k], val)
        return val

    vals = [in_group(rows, k) for k in range(EXPERTS_PER_GROUP)]
    unbiased = [in_group(raw, k) for k in range(EXPERTS_PER_GROUP)]

    def first_argmax(cands):
        idx = jnp.zeros_like(group)
        top = cands[0]
        for k in range(1, len(cands)):
            better = cands[k] > top
            idx = jnp.where(better, k, idx)
            top = jnp.where(better, cands[k], top)
        return idx

    i1 = first_argmax(vals)
    i2 = first_argmax([jnp.where(i1 == k, -jnp.inf, vals[k]) for k in range(EXPERTS_PER_GROUP)])

    def pick(idx):
        val = unbiased[0]
        for k in range(1, EXPERTS_PER_GROUP):
            val = jnp.where(idx == k, unbiased[k], val)
        return val

    s1, s2 = pick(i1), pick(i2)
    total = s1 + s2
    experts = jnp.concatenate([group * EXPERTS_PER_GROUP + i1, group * EXPERTS_PER_GROUP + i2], axis=0)
    weights = jnp.concatenate([s1 / total, s2 / total], axis=0)
    return experts, weights


def _slot_ranks(experts, carry):
    tm = experts.shape[1]
    expert_id = lax.broadcasted_iota(jnp.int32, (N_EXPERTS, tm), 0)
    upper = jnp.where(lax.broadcasted_iota(jnp.int32, (tm, tm), 0) <= lax.broadcasted_iota(jnp.int32, (tm, tm), 1),
                      1.0, 0.0).astype(BF16)
    ranks = []
    for k in range(2):
        hit = jnp.where(expert_id == experts[k:k + 1, :], 1.0, 0.0)
        inclusive = _dot(hit.astype(BF16), upper)
        ranks.append(jnp.sum(hit * (carry + inclusive - 1.0), axis=0, keepdims=True))
        carry = carry + jnp.sum(hit, axis=1, keepdims=True)
    return jnp.concatenate(ranks, axis=0).astype(jnp.int32), carry


def _store_row_tiles(ref, val):
    rows = val.shape[0]
    for j in range(val.shape[1] // LANES):
        ref[pl.ds(j, rows, stride=SUBLANES), :] = val[:, j * LANES:(j + 1) * LANES]


def _load_row_tiles(ref, rows):
    return jnp.concatenate([ref[pl.ds(j, rows, stride=SUBLANES), :] for j in range(ref.shape[0] // rows)], axis=1)


def _merge_kernel(oret, oswa, oga, ar, as_, aa, *refs, first_layer, n_lat_tiles, n_tiles, n_batch):
    if first_layer:
        x_ref, c_ref, *refs = refs
    else:
        x_ref, *refs = refs
    (mod_ref, wr_ref, ws_ref, wa_ref, wout_ref, g2_ref, wrt_ref, br_ref,
     xo_ref, h2t_ref, e_ref, r_ref, w_ref, cnt_ref, h2_prev, carry_ref) = refs
    s = pl.program_id(0)
    n_steps = n_batch * n_tiles

    def merge_tile():
        b = s // n_tiles
        i = s - b * n_tiles
        m = mod_ref[pl.ds(jnp.where(i < n_lat_tiles, b, n_batch), 1), :]

        def gate(a_ref):
            return jax.nn.sigmoid(a_ref[0].astype(F32))

        y = (gate(ar) * _dot(oret[0], wr_ref[0]) + gate(as_) * _dot(oswa[0], ws_ref[0])
             + gate(aa) * _dot(oga[0], wa_ref[0]))
        residual = _first_layer_tokens(x_ref, c_ref, i, n_lat_tiles) if first_layer else x_ref[0]
        x = residual + _mod_chunk(m, 2) * _dot(y.astype(BF16), wout_ref[0])
        xo_ref[0] = x
        h2 = _rms_normalize(x) * g2_ref[...] * (1.0 + _mod_chunk(m, 4)) + _mod_chunk(m, 3)
        _store_row_tiles(h2t_ref.at[0], h2)
        h2_prev[...] = h2

    def route_previous_tile():
        h2 = h2_prev[...]
        h_hi = h2.astype(BF16)
        h_lo = (h2 - h_hi.astype(F32)).astype(BF16)
        w = wrt_ref[...]
        w_hi = w.astype(BF16)
        w_lo = (w - w_hi.astype(F32)).astype(BF16)
        logits = _dot_nt(w_hi, h_hi) + _dot_nt(w_hi, h_lo) + _dot_nt(w_lo, h_hi)
        scores = jax.nn.sigmoid(logits)
        experts, weights = _route(scores, scores + br_ref[...])
        ranks, carry = _slot_ranks(experts, carry_ref[:, 0:1])
        e_ref[0] = experts
        r_ref[0] = ranks
        w_ref[0] = weights
        carry_ref[...] = jnp.broadcast_to(carry, carry_ref.shape)
        cnt_ref[...] = jnp.broadcast_to(carry, cnt_ref.shape)

    @pl.when(s == 0)
    def _init():
        carry_ref[...] = jnp.zeros_like(carry_ref)

    @pl.when(s > 0)
    def _routing_step():
        route_previous_tile()

    @pl.when(s < n_steps)
    def _merge_step():
        merge_tile()


def _merge(oret, oswa, oga, ar, as_, aa, xa, mod, w_ret, w_swa, w_ga, wout, layer, g2, wrt, br,
           n_lat_tiles, n_tiles):
    first_layer = isinstance(xa, tuple)
    if first_layer:
        x, ctx = xa
        nb, length, d = x.shape
        nt = length + ctx.shape[1]
        assert n_tiles * TOKEN_TILE == nt, "the first layer must write every row of the joined stream"
    else:
        nb, nt, d = xa.shape
    rows = n_tiles * TOKEN_TILE
    n_steps = nb * n_tiles
    kern = functools.partial(_merge_kernel, first_layer=first_layer, n_lat_tiles=n_lat_tiles, n_tiles=n_tiles,
                             n_batch=nb)

    def cur(s):
        t = jnp.minimum(s, n_steps - 1)
        return t // n_tiles, t % n_tiles

    def prev(s):
        t = jnp.maximum(s - 1, 0)
        return t // n_tiles, t % n_tiles

    tile = pl.BlockSpec((1, TOKEN_TILE, d), lambda s: (*cur(s), 0))
    whole = lambda arr: pl.BlockSpec(arr.shape, lambda s: (0,) * arr.ndim)
    square = pl.BlockSpec((1, d, d), lambda s: (layer, 0, 0))
    small = pl.BlockSpec((1, 2, TOKEN_TILE), lambda s: (prev(s)[0], 0, prev(s)[1]))
    if first_layer:
        residual_specs, residual, aliases = _first_layer_specs(x, ctx, n_lat_tiles, cur), (x, ctx), {}
    else:
        residual_specs, residual, aliases = [tile], (xa,), {6: 0}
    return pl.pallas_call(
        kern,
        out_shape=[jax.ShapeDtypeStruct((nb, nt, d), F32),
                   jax.ShapeDtypeStruct((nb, rows * SUBLANES, LANES), F32),
                   jax.ShapeDtypeStruct((nb, 2, rows), jnp.int32),
                   jax.ShapeDtypeStruct((nb, 2, rows), jnp.int32),
                   jax.ShapeDtypeStruct((nb, 2, rows), F32),
                   jax.ShapeDtypeStruct((N_EXPERTS, LANES), F32)],
        grid=(n_steps + 1,),
        in_specs=[tile, tile, tile, tile, tile, tile, *residual_specs, whole(mod), square, square, square, square,
                  pl.BlockSpec((1, d), lambda s: (0, 0)), whole(wrt), whole(br)],
        out_specs=[tile,
                   pl.BlockSpec((1, TOKEN_TILE * SUBLANES, LANES), lambda s: (*cur(s), 0)),
                   small, small, small,
                   pl.BlockSpec((N_EXPERTS, LANES), lambda s: (0, 0))],
        scratch_shapes=[pltpu.VMEM((TOKEN_TILE, d), F32), pltpu.VMEM((N_EXPERTS, LANES), F32)],
        input_output_aliases=aliases,
        compiler_params=_params(dimension_semantics=("arbitrary",)),
        name="merge_route",
    )(oret, oswa, oga, ar, as_, aa, *residual, mod, w_ret, w_swa, w_ga, wout, g2.reshape(1, d), wrt, br)


def _row_copy(src, dst, sem):
    return pltpu.make_async_copy(src, dst, sem)


def _tile_rows(r):
    return pl.ds(pl.multiple_of(r * SUBLANES, SUBLANES), SUBLANES)


def _dispatch_kernel(tail_ref, nv_ref, dest_ref, h2t_ref, buf_out, zeros_ref, sem, tail_sem):
    rows = h2t_ref.shape[1] // SUBLANES

    @pl.when(pl.program_id(0) == 0)
    def _zero_tails():
        zeros_ref[...] = jnp.zeros_like(zeros_ref)

        def zero_block(first_slot):
            span = pl.ds(pl.multiple_of(first_slot * SUBLANES, SUBLANES), MOE_ROWS * SUBLANES)
            copy = pltpu.make_async_copy(zeros_ref, buf_out.at[span, :], tail_sem)
            copy.start()
            copy.wait()

        for e in range(N_EXPERTS):
            @pl.when(tail_ref[e] >= 0)
            def _():
                zero_block(tail_ref[e])

        def unused(blk, carry):
            zero_block(blk * MOE_ROWS)
            return carry

        lax.fori_loop(nv_ref[0], buf_out.shape[0] // (MOE_ROWS * SUBLANES), unused, 0)

    def issue(r, carry):
        for k in range(2):
            d = dest_ref[0, 0, k * rows + r]
            _row_copy(h2t_ref.at[0, _tile_rows(r), :], buf_out.at[_tile_rows(d), :], sem).start(priority=k)
        return carry

    lax.fori_loop(0, rows, issue, 0, unroll=8)

    def drain(r, carry):
        for k in range(2):
            _row_copy(h2t_ref.at[0, _tile_rows(0), :], buf_out.at[_tile_rows(0), :], sem).wait()
        return carry

    lax.fori_loop(0, rows, drain, 0, unroll=8)


def _dispatch(tail_start, n_valid, dest_rows, h2t, n_slots):
    nb, rows8, _ = h2t.shape
    grid_spec = pltpu.PrefetchScalarGridSpec(
        num_scalar_prefetch=2,
        grid=(nb,),
        in_specs=[pl.BlockSpec((1, 1, dest_rows.shape[2]), lambda b, tail, nv: (b, 0, 0), memory_space=pltpu.SMEM),
                  pl.BlockSpec((1, rows8, LANES), lambda b, tail, nv: (b, 0, 0))],
        out_specs=pl.BlockSpec(memory_space=pl.ANY),
        scratch_shapes=[pltpu.VMEM((MOE_ROWS * SUBLANES, LANES), F32),
                        pltpu.SemaphoreType.DMA(()), pltpu.SemaphoreType.DMA(())],
    )
    return pl.pallas_call(
        _dispatch_kernel,
        out_shape=jax.ShapeDtypeStruct((n_slots * SUBLANES, LANES), F32),
        grid_spec=grid_spec,
        compiler_params=_params(has_side_effects=True, dimension_semantics=("arbitrary",)),
        name="moe_dispatch",
    )(tail_start, n_valid, dest_rows, h2t)


def _expert_kernel(be_ref, nv_ref, x_ref, wg_ref, wu_ref, wd_ref, y_ref, wg_bf, wu_bf, wd_bf):
    i = pl.program_id(0)

    @pl.when(i < nv_ref[0])
    def _():
        @pl.when((i == 0) | (be_ref[i] != be_ref[jnp.maximum(i - 1, 0)]))
        def _():
            wg_bf[...] = wg_ref[0, 0].astype(BF16)
            wu_bf[...] = wu_ref[0, 0].astype(BF16)
            wd_bf[...] = wd_ref[0, 0].astype(BF16)

        x = _load_row_tiles(x_ref, MOE_ROWS).astype(BF16)
        hid = _silu(_dot(x, wg_bf[...])) * _dot(x, wu_bf[...])
        _store_row_tiles(y_ref, _dot(hid.astype(BF16), wd_bf[...]))

    @pl.when(i >= nv_ref[0])
    def _():
        y_ref[...] = jnp.zeros_like(y_ref)


def _experts(block_e, n_valid, buf, wg, wu, wd, layer):
    n_slots = buf.shape[0] // SUBLANES
    d = wg.shape[2]
    used = lambda i, nv: jnp.maximum(jnp.minimum(i, nv[0] - 1), 0)
    rows = pl.BlockSpec((MOE_ROWS * SUBLANES, LANES), lambda i, be, nv: (used(i, nv), 0))
    grid_spec = pltpu.PrefetchScalarGridSpec(
        num_scalar_prefetch=2,
        grid=(n_slots // MOE_ROWS,),
        in_specs=[rows,
                  pl.BlockSpec((1, 1, d, D_EXPERT), lambda i, be, nv: (layer, be[used(i, nv)], 0, 0)),
                  pl.BlockSpec((1, 1, d, D_EXPERT), lambda i, be, nv: (layer, be[used(i, nv)], 0, 0)),
                  pl.BlockSpec((1, 1, D_EXPERT, d), lambda i, be, nv: (layer, be[used(i, nv)], 0, 0))],
        out_specs=pl.BlockSpec((MOE_ROWS * SUBLANES, LANES), lambda i, be, nv: (i, 0)),
        scratch_shapes=[pltpu.VMEM((d, D_EXPERT), BF16), pltpu.VMEM((d, D_EXPERT), BF16),
                        pltpu.VMEM((D_EXPERT, d), BF16)],
    )
    return pl.pallas_call(
        _expert_kernel,
        out_shape=jax.ShapeDtypeStruct((n_slots * SUBLANES, LANES), F32),
        grid_spec=grid_spec,
        compiler_params=_params(dimension_semantics=("arbitrary",)),
        name="moe_experts",
    )(block_e, n_valid, buf, wg, wu, wd)


def _combine_kernel(dest_ref, next_ref, y_hbm, wt_ref, x_ref, mod_ref, g_ref, *rest,
                    n_lat_tiles, n_tiles, n_batch, last):
    if last:
        out_ref, gbuf, sems = rest
    else:
        xo_ref, h_ref, gbuf, sems = rest
    t = pl.program_id(0)
    slot = lax.rem(t, 2)

    def gather(idx_ref, into):
        def issue(r, carry):
            for k in range(2):
                d = idx_ref[0, 0, k * TOKEN_TILE + r]
                _row_copy(y_hbm.at[_tile_rows(d), :], gbuf.at[into, k, _tile_rows(r), :],
                          sems.at[into]).start(priority=k)
            return carry

        lax.fori_loop(0, TOKEN_TILE, issue, 0, unroll=8)

    @pl.when(t == 0)
    def _():
        gather(dest_ref, 0)

    @pl.when(t + 1 < n_batch * n_tiles)
    def _():
        gather(next_ref, 1 - slot)

    def drain(r, carry):
        for k in range(2):
            _row_copy(y_hbm.at[_tile_rows(0), :], gbuf.at[slot, 0, _tile_rows(0), :], sems.at[slot]).wait()
        return carry

    lax.fori_loop(0, TOKEN_TILE, drain, 0, unroll=8)

    b = t // n_tiles
    m = mod_ref[pl.ds(jnp.where(t - b * n_tiles < n_lat_tiles, b, n_batch), 1), :]
    wt = wt_ref[0]
    moe = (_load_row_tiles(gbuf.at[slot, 0], TOKEN_TILE) * wt[:, 0:1]
           + _load_row_tiles(gbuf.at[slot, 1], TOKEN_TILE) * wt[:, 1:2])
    x = x_ref[0] + _mod_chunk(m, 5) * moe
    if last:
        out_ref[0] = _rms_normalize(x) * g_ref[...]
    else:
        xo_ref[0] = x
        h = _rms_normalize(x) * g_ref[...] * (1.0 + _mod_chunk(m, 7)) + _mod_chunk(m, 6)
        h_ref[0] = h.astype(BF16)


def _combine(dest_tiles, y, wt, xa, mod, g, n_lat_tiles, n_tiles, last):
    nb, nt, d = xa.shape
    rows = n_tiles * TOKEN_TILE
    n_steps = nb * n_tiles
    kern = functools.partial(_combine_kernel, n_lat_tiles=n_lat_tiles, n_tiles=n_tiles, n_batch=nb, last=last)
    tile = pl.BlockSpec((1, TOKEN_TILE, d), lambda t: (t // n_tiles, t % n_tiles, 0))
    if last:
        out_shape = jax.ShapeDtypeStruct((nb, rows, d), F32)
        out_specs = tile
        aliases = {}
    else:
        out_shape = [jax.ShapeDtypeStruct((nb, nt, d), F32), jax.ShapeDtypeStruct((nb, nt, d), BF16)]
        out_specs = [tile, tile]
        aliases = {4: 0}
    return pl.pallas_call(
        kern,
        out_shape=out_shape,
        grid=(n_steps,),
        in_specs=[pl.BlockSpec((1, 1, 2 * TOKEN_TILE), lambda t: (t, 0, 0), memory_space=pltpu.SMEM),
                  pl.BlockSpec((1, 1, 2 * TOKEN_TILE), lambda t: (jnp.minimum(t + 1, n_steps - 1), 0, 0),
                               memory_space=pltpu.SMEM),
                  pl.BlockSpec(memory_space=pl.ANY),
                  pl.BlockSpec((1, TOKEN_TILE, 2), lambda t: (t // n_tiles, t % n_tiles, 0)),
                  tile,
                  pl.BlockSpec(mod.shape, lambda t: (0, 0)),
                  pl.BlockSpec((1, d), lambda t: (0, 0))],
        out_specs=out_specs,
        scratch_shapes=[pltpu.VMEM((2, 2, TOKEN_TILE * SUBLANES, LANES), F32), pltpu.SemaphoreType.DMA((2,))],
        input_output_aliases=aliases,
        compiler_params=_params(dimension_semantics=("arbitrary",)),
        name="moe_combine",
    )(dest_tiles, dest_tiles, y, wt, xa, mod, g.reshape(1, d))


def _slot_plan(experts, ranks, counts):
    nb, _, rows = experts.shape
    padded = (counts + MOE_ROWS - 1) // MOE_ROWS * MOE_ROWS
    pad_ends = jnp.cumsum(padded)
    pad_starts = pad_ends - padded
    one_hot = experts[..., None] == jnp.arange(N_EXPERTS, dtype=jnp.int32)
    dest = (jnp.sum(jnp.where(one_hot, pad_starts, 0), axis=-1) + ranks).astype(jnp.int32)
    n_assign = nb * 2 * rows
    n_blocks = (n_assign + N_EXPERTS * (MOE_ROWS - 1) + MOE_ROWS - 1) // MOE_ROWS
    first_slot = jnp.arange(n_blocks, dtype=jnp.int32) * MOE_ROWS
    block_e = jnp.minimum(jnp.sum((pad_ends[None, :] <= first_slot[:, None]).astype(jnp.int32), axis=1),
                          N_EXPERTS - 1).astype(jnp.int32)
    n_valid = (pad_ends[-1:] // MOE_ROWS).astype(jnp.int32)
    tail_start = jnp.where(padded > 0, pad_ends - MOE_ROWS, -1).astype(jnp.int32)
    n_tiles = rows // TOKEN_TILE
    dest_tiles = (dest.reshape(nb, 2, n_tiles, TOKEN_TILE)
                  .transpose(0, 2, 1, 3).reshape(nb * n_tiles, 1, 2 * TOKEN_TILE))
    dest_rows = dest.reshape(nb, 1, 2 * rows)
    return dest_tiles, dest_rows, block_e, n_valid, tail_start, n_blocks * MOE_ROWS


def _rope_tables(length, n_ctx):
    rows = length // GRID_W
    row = jnp.repeat(jnp.arange(rows, dtype=jnp.int32), GRID_W).astype(F32)
    col = jnp.tile(jnp.arange(GRID_W, dtype=jnp.int32), rows).astype(F32)

    def table(hd):
        quarter = hd // 4
        freqs = ROPE_THETA ** (-jnp.arange(quarter, dtype=F32) / quarter)
        ang_r = row[:, None] * freqs[None, :]
        ang_c = col[:, None] * freqs[None, :]
        cos = jnp.concatenate([jnp.cos(ang_r), jnp.cos(ang_r), jnp.cos(ang_c), jnp.cos(ang_c)], axis=-1)
        sin = jnp.concatenate([-jnp.sin(ang_r), jnp.sin(ang_r), -jnp.sin(ang_c), jnp.sin(ang_c)], axis=-1)
        cos = jnp.concatenate([cos, jnp.ones((n_ctx, hd), F32)], axis=0)
        sin = jnp.concatenate([sin, jnp.zeros((n_ctx, hd), F32)], axis=0)
        reps = LANES // hd
        return jnp.tile(cos, (1, reps)), jnp.tile(sin, (1, reps))

    c128, s128 = table(RET_DK)
    c64, s64 = table(SWA_HD)
    return c128, s128, c64, s64


def kernel(x, c, ctx, c_ctx, w_mod, b_mod, g_norm1, g_norm2, w_in, ret_decay_logit, swa_sink, g_qnorm, g_knorm,
           w_br_ret, w_br_swa, w_br_ga, w_out, w_router, b_router, w_gate, w_up, w_down, g_final):
    nb, length, d = x.shape
    n_ctx = ctx.shape[1]
    depth = w_mod.shape[0]
    nt = length + n_ctx
    n_lat_tiles = length // TOKEN_TILE
    n_all_tiles = nt // TOKEN_TILE
    assert GA_HD == RET_DK == LANES and 2 * SWA_HD == LANES
    assert length % TOKEN_TILE == 0 and n_ctx == TOKEN_TILE and length % n_ctx == 0

    mod_rows = 8
    c_rows = jnp.concatenate([c, c_ctx[None, :], jnp.zeros((mod_rows - nb - 1, d), F32)], axis=0)
    mods = _modulation(c_rows, w_mod, b_mod)
    tables = _rope_tables(length, n_ctx)
    wrt = w_router.astype(F32).T
    br = b_router.astype(F32).reshape(N_EXPERTS, 1)

    assert depth >= 2, "the first layer's merge kernel must also cover the context rows"
    xa = (x, ctx)
    h = (x, ctx, mods[0], g_norm1[0])
    w_in_bf, w_ret_bf, w_swa_bf, w_ga_bf, w_out_bf = (
        w.astype(BF16) for w in (w_in, w_br_ret, w_br_swa, w_br_ga, w_out))
    out = None
    for l in range(depth):
        need_ctx = l < depth - 1
        n_tiles = n_all_tiles if need_ctx else n_lat_tiles
        (rq, rk, rv, ru, sq, sk, sv, gq, gk, gv, ar, as_, aa) = _inproj(
            h, w_in_bf, l, tables, g_qnorm[l], g_knorm[l])
        o_ret = _retention(ret_decay_logit[l].astype(F32), rq, rk, rv, ru, length, n_ctx, need_ctx)
        o_swa = _window_attention(swa_sink[l].astype(F32), sq, sk, sv, length, n_ctx, need_ctx)
        o_ga = _global_attention(gq, gk, gv, length, n_ctx, need_ctx)
        xa, h2t, experts, ranks, weights, counts = _merge(
            o_ret, o_swa, o_ga, ar, as_, aa, xa, mods[l], w_ret_bf, w_swa_bf, w_ga_bf, w_out_bf, l,
            g_norm2[l], wrt, br, n_lat_tiles, n_tiles)
        dest_tiles, dest_rows, block_e, n_valid, tail_start, n_slots = _slot_plan(
            experts, ranks, counts[:, 0].astype(jnp.int32))
        buf = _dispatch(tail_start, n_valid, dest_rows, h2t, n_slots)
        y = _experts(block_e, n_valid, buf, w_gate, w_up, w_down, l)
        wt = weights.transpose(0, 2, 1)
        if need_ctx:
            mod_pair = jnp.concatenate([mods[l], mods[l + 1][:, :2 * d]], axis=1)
            xa, h = _combine(dest_tiles, y, wt, xa, mod_pair, g_norm1[l + 1], n_lat_tiles, n_tiles, last=False)
        else:
            out = _combine(dest_tiles, y, wt, xa, mods[l], g_final, n_lat_tiles, n_tiles, last=True)
    return out
```

```python
import functools

import jax
import jax.numpy as jnp
from jax import lax
from jax.experimental import pallas as pl
from jax.experimental.pallas import tpu as pltpu

F32 = jnp.float32
BF16 = jnp.bfloat16

D_MODEL = 1024
GRID_W = 64
NORM_EPS = 1e-6
ROPE_THETA = 10000.0
RET_HEADS, RET_DK, RET_DV, RET_CHUNK = 4, 128, 256, 256
SWA_HEADS, SWA_KV_HEADS, SWA_HD, WINDOW = 16, 2, 64, 128
GA_HEADS, GA_KV_HEADS, GA_HD = 8, 2, 128
N_EXPERTS, N_GROUPS, EXPERTS_PER_GROUP, D_EXPERT = 16, 4, 4, 512

RET_QK = RET_HEADS * RET_DK
RET_V = RET_HEADS * RET_DV
SWA_Q = SWA_HEADS * SWA_HD
SWA_KV = SWA_KV_HEADS * SWA_HD
GA_Q = GA_HEADS * GA_HD
GA_KV = GA_KV_HEADS * GA_HD
SPLITS = (RET_QK, RET_QK, RET_V, RET_V, SWA_Q, SWA_KV, SWA_KV, GA_Q, GA_KV, GA_KV,
          D_MODEL, D_MODEL, D_MODEL)

LANES = 128
SUBLANES = 8
TOKEN_TILE = 256
QUERY_TILE = 128
GA_QUERY_TILE = 256
MOE_ROWS = 512
DISPATCH_STEPS = 4
GA_KEY_CHUNK = 256
RET_HEADS_PER_STEP = 2
SWA_PARTS = 2
SWA_TILES_PER_CHUNK = 2
MOD_COLS = 1536
VMEM_LIMIT = 56 * 1024 * 1024
MASKED = -1e30
LOG2E = 1.4426950408889634

_W_WIDTHS = (RET_QK, RET_QK, RET_V, RET_V, SWA_Q, 2 * SWA_KV, 2 * SWA_KV, GA_Q, GA_KV, GA_KV,
             D_MODEL, D_MODEL, D_MODEL)
_IN_OFFS = tuple(sum(SPLITS[:i]) for i in range(len(SPLITS)))


def _dot(a, b):
    return jnp.dot(a, b, preferred_element_type=F32)


def _dot_nt(a, b):
    return lax.dot_general(a, b, (((1,), (1,)), ((), ())), preferred_element_type=F32)


def _dot_tn(a, b):
    return lax.dot_general(a, b, (((0,), (0,)), ((), ())), preferred_element_type=F32)


def _silu(x):
    return x * jax.nn.sigmoid(x)


def _rms_normalize(x):
    return x * lax.rsqrt(jnp.mean(x * x, axis=-1, keepdims=True) + NORM_EPS)


def _params(**kw):
    return pltpu.CompilerParams(vmem_limit_bytes=VMEM_LIMIT, **kw)


def _mod_kernel(c_ref, w_ref, b_ref, o_ref):
    a = _silu(c_ref[...])
    o_ref[0] = _dot(a.astype(BF16), w_ref[0].astype(BF16)) + b_ref[0]


def _modulation(c_rows, w_mod, b_mod):
    depth, d, n = w_mod.shape
    rows = c_rows.shape[0]
    return pl.pallas_call(
        _mod_kernel,
        out_shape=jax.ShapeDtypeStruct((depth, rows, n), F32),
        grid=(depth, n // MOD_COLS),
        in_specs=[pl.BlockSpec((rows, d), lambda l, j: (0, 0)),
                  pl.BlockSpec((1, d, MOD_COLS), lambda l, j: (l, 0, j)),
                  pl.BlockSpec((1, 1, MOD_COLS), lambda l, j: (l, 0, j))],
        out_specs=pl.BlockSpec((1, rows, MOD_COLS), lambda l, j: (l, 0, j)),
        compiler_params=_params(),
        name="modulation",
    )(c_rows, w_mod, b_mod.reshape(depth, 1, n))


def _mod_row(mod_ref, n_lat_tiles, n_batch):
    b = pl.program_id(0)
    i = pl.program_id(1)
    r = jnp.where(i < n_lat_tiles, b, n_batch)
    return mod_ref[pl.ds(r, 1), :]


def _mod_chunk(m, k):
    return m[:, k * D_MODEL:(k + 1) * D_MODEL]


def _rope(x, cos, sin, quarter):
    lane = lax.broadcasted_iota(jnp.int32, x.shape, 1)
    first = (lane % (2 * quarter)) < quarter
    partner = jnp.where(first, pltpu.roll(x, LANES - quarter, 1), pltpu.roll(x, quarter, 1))
    return x * cos + partner * sin


def _first_layer_tokens(x_ref, c_ref, i, n_lat_tiles):
    return jnp.where(i < n_lat_tiles, x_ref[0], c_ref[0])


def _inproj_kernel(*refs, first_layer, n_lat_tiles, n_batch):
    if first_layer:
        x_ref, c_ref, mod_ref, g_ref, *refs = refs
        m = _mod_row(mod_ref, n_lat_tiles, n_batch)
        x = _first_layer_tokens(x_ref, c_ref, pl.program_id(1), n_lat_tiles)
        h = (_rms_normalize(x) * g_ref[...] * (1.0 + _mod_chunk(m, 1)) + _mod_chunk(m, 0)).astype(BF16)
    else:
        h_ref, *refs = refs
        h = h_ref[0]
    (w_ref, c128_ref, s128_ref, c64_ref, s64_ref, gq_ref, gk_ref,
     rq, rk, rv, ru, sq, sk, sv, gq, gk, gv, ar, as_, aa) = refs
    c128, s128 = c128_ref[...], s128_ref[...]
    c64, s64 = c64_ref[...], s64_ref[...]

    def proj(idx):
        return _dot(h, w_ref[0, :, _IN_OFFS[idx]:_IN_OFFS[idx] + SPLITS[idx]])

    def slabs(acc):
        return [acc[:, s * LANES:(s + 1) * LANES] for s in range(acc.shape[1] // LANES)]

    def store(ref, s, val):
        ref[0, :, s * LANES:(s + 1) * LANES] = val.astype(ref.dtype)

    def head_per_slab(acc):
        low = lax.broadcasted_iota(jnp.int32, acc.shape, 1) < SWA_HD
        other = pltpu.roll(acc, SWA_HD, 1)
        return [jnp.where(low, acc, other), jnp.where(low, other, acc)]

    for s, xs in enumerate(slabs(proj(0))):
        store(rq, s, _rope(xs, c128, s128, RET_DK // 4))
    for s, xs in enumerate(slabs(proj(1))):
        store(rk, s, _rope(xs, c128, s128, RET_DK // 4) * (RET_DK ** -0.5))
    rv[0] = proj(2).astype(rv.dtype)
    ru[0] = proj(3).astype(ru.dtype)
    for s, xs in enumerate(slabs(proj(4))):
        store(sq, s, _rope(xs, c64, s64, SWA_HD // 4) * (SWA_HD ** -0.5 * LOG2E))
    for s, xs in enumerate(head_per_slab(proj(5))):
        store(sk, s, _rope(xs, c64, s64, SWA_HD // 4))
    for s, xs in enumerate(head_per_slab(proj(6))):
        store(sv, s, xs)
    for s, xs in enumerate(slabs(proj(7))):
        xn = _rms_normalize(xs) * gq_ref[...]
        store(gq, s, _rope(xn, c128, s128, GA_HD // 4) * (GA_HD ** -0.5 * LOG2E))
    for s, xs in enumerate(slabs(proj(8))):
        xn = _rms_normalize(xs) * gk_ref[...]
        store(gk, s, _rope(xn, c128, s128, GA_HD // 4))
    gv[0] = proj(9).astype(gv.dtype)
    ar[0] = proj(10).astype(ar.dtype)
    as_[0] = proj(11).astype(as_.dtype)
    aa[0] = proj(12).astype(aa.dtype)


def _first_layer_specs(x, ctx, n_lat_tiles, index):
    d = x.shape[2]
    return [pl.BlockSpec((1, TOKEN_TILE, d), lambda *ids: (index(*ids)[0],
                                                          jnp.minimum(index(*ids)[1], n_lat_tiles - 1), 0)),
            pl.BlockSpec((1, TOKEN_TILE, d), lambda *ids: (index(*ids)[0], 0, 0))]


def _inproj(tokens, w_in, layer, tables, g_q, g_k):
    first_layer = isinstance(tokens, tuple)
    tile = lambda width: pl.BlockSpec((1, TOKEN_TILE, width), lambda b, i: (b, i, 0))
    if first_layer:
        x, ctx, mod, g = tokens
        nb, length, d = x.shape
        nt = length + ctx.shape[1]
        n_lat_tiles = length // TOKEN_TILE
        lead_specs = _first_layer_specs(x, ctx, n_lat_tiles, lambda b, i: (b, i)) + [
            pl.BlockSpec(mod.shape, lambda b, i: (0, 0)), pl.BlockSpec((1, d), lambda b, i: (0, 0))]
        lead = (x, ctx, mod, g.reshape(1, d))
    else:
        nb, nt, d = tokens.shape
        n_lat_tiles = 0
        lead_specs = [tile(d)]
        lead = (tokens,)
    tab = pl.BlockSpec((TOKEN_TILE, LANES), lambda b, i: (i, 0))
    vec = pl.BlockSpec((1, LANES), lambda b, i: (0, 0))
    outs = [jax.ShapeDtypeStruct((nb, nt, w), BF16) for w in _W_WIDTHS]
    kern = functools.partial(_inproj_kernel, first_layer=first_layer, n_lat_tiles=n_lat_tiles, n_batch=nb)
    return pl.pallas_call(
        kern,
        out_shape=outs,
        grid=(nb, nt // TOKEN_TILE),
        in_specs=lead_specs + [
            pl.BlockSpec((1, d, w_in.shape[2]), lambda b, i: (layer, 0, 0), pipeline_mode=pl.Buffered(1)),
            tab, tab, tab, tab, vec, vec],
        out_specs=[tile(w) for w in _W_WIDTHS],
        compiler_params=_params(),
        name="inproj",
    )(*lead, w_in, *tables, g_q.reshape(1, LANES), g_k.reshape(1, LANES))


def _log_sigmoid(x):
    return jnp.minimum(x, 0.0) - jnp.log1p(jnp.exp(-jnp.abs(x)))


def _ret_kernel(logit_ref, q_ref, k_ref, v_ref, u_ref, o_ref, acc_ref, sf_ref, sb_ref,
                *, length, n_ctx, need_ctx):
    chunk = RET_CHUNK
    n_chunks = length // chunk
    half = n_chunks // 2

    def rows_f32(shape):
        return lax.broadcasted_iota(jnp.int32, shape, 0).astype(F32)

    def finish(o, u):
        return (_rms_normalize(o) * _silu(u.astype(F32))).astype(o_ref.dtype)

    ci = rows_f32((chunk, RET_DK))
    heads = []
    for hi in range(RET_HEADS_PER_STEP):
        hh = pl.program_id(1) * RET_HEADS_PER_STEP + hi
        lgf = _log_sigmoid(jnp.full((1, 1), logit_ref[0, hh], F32))
        lgb = _log_sigmoid(jnp.full((1, 1), logit_ref[1, hh], F32))
        qk = slice(hi * RET_DK, (hi + 1) * RET_DK)
        vu = slice(hi * RET_DV, (hi + 1) * RET_DV)

        def both_ways_decay(n, lgf=lgf, lgb=lgb):
            diff = rows_f32((n, n)) - lax.broadcasted_iota(jnp.int32, (n, n), 1).astype(F32)
            return jnp.where(diff >= 0, jnp.exp(jnp.maximum(diff, 0.0) * lgf),
                             jnp.exp(jnp.maximum(-diff, 0.0) * lgb))

        kx = k_ref[0, length:length + n_ctx, qk].astype(F32)
        vx = v_ref[0, length:length + n_ctx, vu]
        lx = rows_f32((n_ctx, RET_DK))
        sf_ref[hi] = _dot_tn((kx * jnp.exp((n_ctx - 1.0 - lx) * lgf)).astype(BF16), vx)
        sb_ref[hi] = _dot_tn((kx * jnp.exp(lx * lgb)).astype(BF16), vx)
        if need_ctx:
            qx = q_ref[0, length:length + n_ctx, qk]
            sx = _dot_nt(qx, kx.astype(BF16)) * both_ways_decay(n_ctx)
            o_ref[0, length:length + n_ctx, vu] = finish(_dot(sx.astype(BF16), vx),
                                                         u_ref[0, length:length + n_ctx, vu])
        heads.append(dict(
            qk=qk, vu=vu, intra=both_ways_decay(chunk),
            q_dec_f=jnp.exp((ci + 1.0) * lgf), k_dec_f=jnp.exp((chunk - 1.0 - ci) * lgf),
            q_dec_b=jnp.exp((chunk - ci) * lgb), k_dec_b=jnp.exp(ci * lgb),
            chunk_dec_f=jnp.exp(chunk * lgf), chunk_dec_b=jnp.exp(chunk * lgb)))

    def sweep(s, first_touch):
        rf = pl.ds(pl.multiple_of(s * chunk, chunk), chunk)
        rb = pl.ds(pl.multiple_of((n_chunks - 1 - s) * chunk, chunk), chunk)
        for hi, hd in enumerate(heads):
            qk, vu = hd["qk"], hd["vu"]
            qf, kf, vf = q_ref[0, rf, qk], k_ref[0, rf, qk], v_ref[0, rf, vu]
            sc = _dot_nt(qf, kf) * hd["intra"]
            state_f = sf_ref[hi]
            o_f = (_dot(sc.astype(BF16), vf)
                   + _dot((qf.astype(F32) * hd["q_dec_f"]).astype(BF16), state_f.astype(BF16)))
            sf_ref[hi] = state_f * hd["chunk_dec_f"] + _dot_tn((kf.astype(F32) * hd["k_dec_f"]).astype(BF16), vf)
            qb, kb, vb = q_ref[0, rb, qk], k_ref[0, rb, qk], v_ref[0, rb, vu]
            state_b = sb_ref[hi]
            o_b = _dot((qb.astype(F32) * hd["q_dec_b"]).astype(BF16), state_b.astype(BF16))
            sb_ref[hi] = state_b * hd["chunk_dec_b"] + _dot_tn((kb.astype(F32) * hd["k_dec_b"]).astype(BF16), vb)
            if first_touch:
                acc_ref[hi, rf, :] = o_f
                acc_ref[hi, rb, :] = o_b
            else:
                o_ref[0, rf, vu] = finish(acc_ref[hi, rf, :] + o_f, u_ref[0, rf, vu])
                o_ref[0, rb, vu] = finish(acc_ref[hi, rb, :] + o_b, u_ref[0, rb, vu])

    def first_half(s, carry):
        sweep(s, True)
        return carry

    def second_half(s, carry):
        sweep(s, False)
        return carry

    lax.fori_loop(0, half, first_half, 0)
    lax.fori_loop(half, n_chunks, second_half, 0)


def _retention(logit, rq, rk, rv, ru, length, n_ctx, need_ctx):
    nb, nt, _ = rq.shape
    assert (length // RET_CHUNK) % 2 == 0 and RET_HEADS % RET_HEADS_PER_STEP == 0
    out_rows = nt if need_ctx else length
    kern = functools.partial(_ret_kernel, length=length, n_ctx=n_ctx, need_ctx=need_ctx)
    head = lambda width: pl.BlockSpec((1, nt, RET_HEADS_PER_STEP * width), lambda b, h: (b, 0, h))
    return pl.pallas_call(
        kern,
        out_shape=jax.ShapeDtypeStruct((nb, out_rows, RET_V), BF16),
        grid=(nb, RET_HEADS // RET_HEADS_PER_STEP),
        in_specs=[pl.BlockSpec(memory_space=pltpu.SMEM),
                  head(RET_DK), head(RET_DK), head(RET_DV), head(RET_DV)],
        out_specs=pl.BlockSpec((1, out_rows, RET_HEADS_PER_STEP * RET_DV), lambda b, h: (b, 0, h)),
        scratch_shapes=[pltpu.VMEM((RET_HEADS_PER_STEP, length, RET_DV), F32),
                        pltpu.VMEM((RET_HEADS_PER_STEP, RET_DK, RET_DV), F32),
                        pltpu.VMEM((RET_HEADS_PER_STEP, RET_DK, RET_DV), F32)],
        compiler_params=_params(),
        name="retention",
    )(logit, rq, rk, rv, ru)


def _swa_group(q_ref, g, kcat, vcat, biases, sink_ref, o_ref):
    tq = q_ref.shape[1]
    heads_per_group = SWA_HEADS // SWA_KV_HEADS
    slabs_per_group = heads_per_group // 2
    low = lax.broadcasted_iota(jnp.int32, (tq, LANES), 1) < SWA_HD
    zero = jnp.zeros((tq, LANES), q_ref.dtype)
    stacked = []
    for a in range(slabs_per_group):
        s0 = (g * slabs_per_group + a) * LANES
        slab = q_ref[0, :, s0:s0 + LANES]
        stacked.append(jnp.where(low, slab, zero))
        stacked.append(jnp.where(low, zero, slab))
    heads_per_part = heads_per_group // SWA_PARTS
    rows = heads_per_part * tq
    n_key_tiles = kcat.shape[0] // LANES
    for part in range(SWA_PARTS):
        q = jnp.concatenate(stacked[part * heads_per_part:(part + 1) * heads_per_part], axis=0)
        sink = jnp.concatenate(
            [jnp.full((tq, LANES), sink_ref[g * heads_per_group + part * heads_per_part + h] * LOG2E, F32)
             for h in range(heads_per_part)], axis=0)
        m = l = acc = None
        for t0 in range(0, n_key_tiles, SWA_TILES_PER_CHUNK):
            t1 = min(t0 + SWA_TILES_PER_CHUNK, n_key_tiles)
            s = _dot_nt(q, kcat[t0 * LANES:t1 * LANES])
            tiles = [s[:, t * LANES:(t + 1) * LANES] for t in range(t1 - t0)]
            for t in range(t0, t1):
                if t in biases:
                    tiles[t - t0] = tiles[t - t0] + biases[t][:rows]
            mc = jnp.broadcast_to(jnp.max(functools.reduce(jnp.maximum, tiles), axis=-1, keepdims=True),
                                  (rows, LANES))
            m_new = jnp.maximum(mc, sink if m is None else m)
            ps = [jnp.exp2(tile - m_new) for tile in tiles]
            pv = _dot(jnp.concatenate(ps, axis=1).astype(BF16), vcat[t0 * LANES:t1 * LANES])
            psum = functools.reduce(jnp.add, ps)
            if m is None:
                l, acc = psum, pv
            else:
                alpha = jnp.exp2(m - m_new)
                l, acc = alpha * l + psum, alpha * acc + pv
            m = m_new
        den = jnp.broadcast_to(jnp.sum(l, axis=-1, keepdims=True), (rows, LANES)) + jnp.exp2(sink - m)
        o = acc / den
        for a in range(heads_per_part // 2):
            s0 = (g * slabs_per_group + part * (heads_per_part // 2) + a) * LANES
            even = o[(2 * a) * tq:(2 * a + 1) * tq]
            odd = o[(2 * a + 1) * tq:(2 * a + 2) * tq]
            o_ref[0, :, s0:s0 + LANES] = jnp.where(low, even, odd).astype(o_ref.dtype)


def _swa_kernel(sink_ref, q_ref, kp_ref, kc_ref, kn_ref, kx_ref, vp_ref, vc_ref, vn_ref, vx_ref, o_ref,
                *, n_lat_tiles, need_ctx):
    j = pl.program_id(1)
    tq = q_ref.shape[1]
    n_ctx = kx_ref.shape[1]

    def lanes_of(ref, g):
        return ref[0, :, g * LANES:(g + 1) * LANES]

    @pl.when(j < n_lat_tiles)
    def _latent():
        ci = lax.broadcasted_iota(jnp.int32, (tq, tq), 1)
        ri = lax.broadcasted_iota(jnp.int32, (tq, tq), 0)
        far = 4 * tq
        ri_prev = ri + jnp.where(j > 0, 0, far)
        ri_next = ri - jnp.where(j < n_lat_tiles - 1, 0, far)
        zero = jnp.zeros((tq, tq), F32)
        masked = jnp.full((tq, tq), MASKED, F32)
        heads_per_part = SWA_HEADS // SWA_KV_HEADS // SWA_PARTS
        bias = {0: jnp.concatenate([jnp.where(ci >= ri_prev, zero, masked)] * heads_per_part, axis=0),
                2: jnp.concatenate([jnp.where(ci <= ri_next, zero, masked)] * heads_per_part, axis=0)}
        for g in range(SWA_KV_HEADS):
            kcat = jnp.concatenate([lanes_of(kp_ref, g), lanes_of(kc_ref, g), lanes_of(kn_ref, g),
                                    lanes_of(kx_ref, g)], axis=0)
            vcat = jnp.concatenate([lanes_of(vp_ref, g), lanes_of(vc_ref, g), lanes_of(vn_ref, g),
                                    lanes_of(vx_ref, g)], axis=0)
            _swa_group(q_ref, g, kcat, vcat, bias, sink_ref, o_ref)

    if need_ctx:
        @pl.when(j >= n_lat_tiles)
        def _context():
            for g in range(SWA_KV_HEADS):
                _swa_group(q_ref, g, lanes_of(kx_ref, g), lanes_of(vx_ref, g), {}, sink_ref, o_ref)


def _window_attention(sink, sq, sk, sv, length, n_ctx, need_ctx):
    nb, nt, _ = sq.shape
    tq = QUERY_TILE
    n_lat = length // tq
    out_rows = nt if need_ctx else length
    kvw = 2 * SWA_KV
    kern = functools.partial(_swa_kernel, n_lat_tiles=n_lat, need_ctx=need_ctx)
    prev = pl.BlockSpec((1, tq, kvw), lambda b, j: (b, jnp.clip(j - 1, 0, n_lat - 1), 0))
    cur = pl.BlockSpec((1, tq, kvw), lambda b, j: (b, jnp.minimum(j, n_lat - 1), 0))
    nxt = pl.BlockSpec((1, tq, kvw), lambda b, j: (b, jnp.minimum(j + 1, n_lat - 1), 0))
    ctx = pl.BlockSpec((1, n_ctx, kvw), lambda b, j: (b, length // n_ctx, 0))
    return pl.pallas_call(
        kern,
        out_shape=jax.ShapeDtypeStruct((nb, out_rows, SWA_Q), BF16),
        grid=(nb, out_rows // tq),
        in_specs=[pl.BlockSpec(memory_space=pltpu.SMEM),
                  pl.BlockSpec((1, tq, SWA_Q), lambda b, j: (b, j, 0)),
                  prev, cur, nxt, ctx, prev, cur, nxt, ctx],
        out_specs=pl.BlockSpec((1, tq, SWA_Q), lambda b, j: (b, j, 0)),
        compiler_params=_params(),
        name="window_attention",
    )(sink, sq, sk, sk, sk, sk, sv, sv, sv, sv)


def _ga_kernel(q_ref, k_ref, v_ref, o_ref, *, length, n_ctx, n_lat_tiles, need_ctx):
    j = pl.program_id(2)
    tq = q_ref.shape[1]
    heads_per_group = GA_HEADS // GA_KV_HEADS
    q = jnp.concatenate([q_ref[0, :, h * LANES:(h + 1) * LANES] for h in range(heads_per_group)], axis=0)

    rows = heads_per_group * tq

    def attend(key_lo, key_hi):
        m = l = acc = None
        for c0 in range(key_lo, key_hi, GA_KEY_CHUNK):
            c1 = min(c0 + GA_KEY_CHUNK, key_hi)
            s = _dot_nt(q, k_ref[0, c0:c1, :])
            parts = [s[:, t * LANES:(t + 1) * LANES] for t in range((c1 - c0) // LANES)]
            lane_max = functools.reduce(jnp.maximum, parts)
            mc = jnp.broadcast_to(jnp.max(lane_max, axis=-1, keepdims=True), (rows, LANES))
            m_new = mc if m is None else jnp.maximum(m, mc)
            ps = [jnp.exp2(part - m_new) for part in parts]
            pv = _dot(jnp.concatenate(ps, axis=1).astype(BF16), v_ref[0, c0:c1, :])
            psum = functools.reduce(jnp.add, ps)
            if m is None:
                l, acc = psum, pv
            else:
                alpha = jnp.exp2(m - m_new)
                l, acc = alpha * l + psum, alpha * acc + pv
            m = m_new
        o = acc / jnp.sum(l, axis=-1, keepdims=True)
        for h in range(heads_per_group):
            o_ref[0, :, h * LANES:(h + 1) * LANES] = o[h * tq:(h + 1) * tq].astype(o_ref.dtype)

    @pl.when(j < n_lat_tiles)
    def _latent():
        attend(0, length + n_ctx)

    if need_ctx:
        @pl.when(j >= n_lat_tiles)
        def _context():
            attend(length, length + n_ctx)


def _global_attention(gq, gk, gv, length, n_ctx, need_ctx):
    nb, nt, _ = gq.shape
    tq = GA_QUERY_TILE
    out_rows = nt if need_ctx else length
    group_w = GA_Q // GA_KV_HEADS
    kern = functools.partial(_ga_kernel, length=length, n_ctx=n_ctx, n_lat_tiles=length // tq,
                             need_ctx=need_ctx)
    kv = pl.BlockSpec((1, nt, GA_HD), lambda b, g, j: (b, 0, g))
    return pl.pallas_call(
        kern,
        out_shape=jax.ShapeDtypeStruct((nb, out_rows, GA_Q), BF16),
        grid=(nb, GA_KV_HEADS, out_rows // tq),
        in_specs=[pl.BlockSpec((1, tq, group_w), lambda b, g, j: (b, j, g)), kv, kv],
        out_specs=pl.BlockSpec((1, tq, group_w), lambda b, g, j: (b, j, g)),
        compiler_params=_params(),
        name="global_attention",
    )(gq, gk, gv)


def _route(scores, biased):
    rows = [biased[r:r + 1, :] for r in range(N_EXPERTS)]
    raw = [scores[r:r + 1, :] for r in range(N_EXPERTS)]

    def top2_sum(vals):
        best = None
        for a in range(len(vals)):
            for b in range(a + 1, len(vals)):
                pair = vals[a] + vals[b]
                best = pair if best is None else jnp.maximum(best, pair)
        return best

    group_scores = [top2_sum(rows[g * EXPERTS_PER_GROUP:(g + 1) * EXPERTS_PER_GROUP]) for g in range(N_GROUPS)]
    group = jnp.zeros_like(group_scores[0], dtype=jnp.int32)
    best = group_scores[0]
    for g in range(1, N_GROUPS):
        better = group_scores[g] > best
        group = jnp.where(better, g, group)
        best = jnp.where(better, group_scores[g], best)

    def in_group(table, k):
        val = table[k]
        for g in range(1, N_GROUPS):
            val = jnp.where(group == g, table[g * EXPERTS_PER_GROUP + k], val)
        return val

    vals = [in_group(rows, k) for k in range(EXPERTS_PER_GROUP)]
    unbiased = [in_group(raw, k) for k in range(EXPERTS_PER_GROUP)]

    def first_argmax(cands):
        idx = jnp.zeros_like(group)
        top = cands[0]
        for k in range(1, len(cands)):
            better = cands[k] > top
            idx = jnp.where(better, k, idx)
            top = jnp.where(better, cands[k], top)
        return idx

    i1 = first_argmax(vals)
    i2 = first_argmax([jnp.where(i1 == k, -jnp.inf, vals[k]) for k in range(EXPERTS_PER_GROUP)])

    def pick(idx):
        val = unbiased[0]
        for k in range(1, EXPERTS_PER_GROUP):
            val = jnp.where(idx == k, unbiased[k], val)
        return val

    s1, s2 = pick(i1), pick(i2)
    total = s1 + s2
    experts = jnp.concatenate([group * EXPERTS_PER_GROUP + i1, group * EXPERTS_PER_GROUP + i2], axis=0)
    weights = jnp.concatenate([s1 / total, s2 / total], axis=0)
    return experts, weights


def _slot_ranks(experts, carry):
    tm = experts.shape[1]
    expert_id = lax.broadcasted_iota(jnp.int32, (N_EXPERTS, tm), 0)
    upper = jnp.where(lax.broadcasted_iota(jnp.int32, (tm, tm), 0) <= lax.broadcasted_iota(jnp.int32, (tm, tm), 1),
                      1.0, 0.0).astype(BF16)
    ranks = []
    for k in range(2):
        hit = jnp.where(expert_id == experts[k:k + 1, :], 1.0, 0.0)
        inclusive = _dot(hit.astype(BF16), upper)
        ranks.append(jnp.sum(hit * (carry + inclusive - 1.0), axis=0, keepdims=True))
        carry = carry + jnp.sum(hit, axis=1, keepdims=True)
    return jnp.concatenate(ranks, axis=0).astype(jnp.int32), carry


def _store_row_tiles(ref, val):
    rows = val.shape[0]
    for j in range(val.shape[1] // LANES):
        ref[pl.ds(j, rows, stride=SUBLANES), :] = val[:, j * LANES:(j + 1) * LANES]


def _load_row_tiles(ref, rows):
    return jnp.concatenate([ref[pl.ds(j, rows, stride=SUBLANES), :] for j in range(ref.shape[0] // rows)], axis=1)


def _merge_kernel(oret, oswa, oga, ar, as_, aa, *refs, first_layer, n_lat_tiles, n_tiles, n_batch):
    if first_layer:
        x_ref, c_ref, *refs = refs
    else:
        x_ref, *refs = refs
    (mod_ref, wr_ref, ws_ref, wa_ref, wout_ref, g2_ref, wrt_ref, br_ref,
     xo_ref, h2t_ref, e_ref, r_ref, w_ref, cnt_ref, h2_prev, carry_ref) = refs
    s = pl.program_id(0)
    n_steps = n_batch * n_tiles

    def merge_tile():
        b = s // n_tiles
        i = s - b * n_tiles
        m = mod_ref[pl.ds(jnp.where(i < n_lat_tiles, b, n_batch), 1), :]

        def gate(a_ref):
            return jax.nn.sigmoid(a_ref[0].astype(F32))

        y = (gate(ar) * _dot(oret[0], wr_ref[0]) + gate(as_) * _dot(oswa[0], ws_ref[0])
             + gate(aa) * _dot(oga[0], wa_ref[0]))
        residual = _first_layer_tokens(x_ref, c_ref, i, n_lat_tiles) if first_layer else x_ref[0]
        x = residual + _mod_chunk(m, 2) * _dot(y.astype(BF16), wout_ref[0])
        xo_ref[0] = x
        h2 = _rms_normalize(x) * g2_ref[...] * (1.0 + _mod_chunk(m, 4)) + _mod_chunk(m, 3)
        _store_row_tiles(h2t_ref.at[0], h2)
        h2_prev[...] = h2

    def route_previous_tile():
        h2 = h2_prev[...]
        h_hi = h2.astype(BF16)
        h_lo = (h2 - h_hi.astype(F32)).astype(BF16)
        w = wrt_ref[...]
        w_hi = w.astype(BF16)
        w_lo = (w - w_hi.astype(F32)).astype(BF16)
        logits = _dot_nt(w_hi, h_hi) + _dot_nt(w_hi, h_lo) + _dot_nt(w_lo, h_hi)
        scores = jax.nn.sigmoid(logits)
        experts, weights = _route(scores, scores + br_ref[...])
        ranks, carry = _slot_ranks(experts, carry_ref[:, 0:1])
        e_ref[0] = experts
        r_ref[0] = ranks
        w_ref[0] = weights
        carry_ref[...] = jnp.broadcast_to(carry, carry_ref.shape)
        cnt_ref[...] = jnp.broadcast_to(carry, cnt_ref.shape)

    @pl.when(s == 0)
    def _init():
        carry_ref[...] = jnp.zeros_like(carry_ref)

    @pl.when(s > 0)
    def _routing_step():
        route_previous_tile()

    @pl.when(s < n_steps)
    def _merge_step():
        merge_tile()


def _merge(oret, oswa, oga, ar, as_, aa, xa, mod, w_ret, w_swa, w_ga, wout, layer, g2, wrt, br,
           n_lat_tiles, n_tiles):
    first_layer = isinstance(xa, tuple)
    if first_layer:
        x, ctx = xa
        nb, length, d = x.shape
        nt = length + ctx.shape[1]
        assert n_tiles * TOKEN_TILE == nt, "the first layer must write every row of the joined stream"
    else:
        nb, nt, d = xa.shape
    rows = n_tiles * TOKEN_TILE
    n_steps = nb * n_tiles
    kern = functools.partial(_merge_kernel, first_layer=first_layer, n_lat_tiles=n_lat_tiles, n_tiles=n_tiles,
                             n_batch=nb)

    def cur(s):
        t = jnp.minimum(s, n_steps - 1)
        return t // n_tiles, t % n_tiles

    def prev(s):
        t = jnp.maximum(s - 1, 0)
        return t // n_tiles, t % n_tiles

    tile = pl.BlockSpec((1, TOKEN_TILE, d), lambda s: (*cur(s), 0))
    whole = lambda arr: pl.BlockSpec(arr.shape, lambda s: (0,) * arr.ndim)
    square = pl.BlockSpec((1, d, d), lambda s: (layer, 0, 0))
    small = pl.BlockSpec((1, 2, TOKEN_TILE), lambda s: (prev(s)[0], 0, prev(s)[1]))
    if first_layer:
        residual_specs, residual, aliases = _first_layer_specs(x, ctx, n_lat_tiles, cur), (x, ctx), {}
    else:
        residual_specs, residual, aliases = [tile], (xa,), {6: 0}
    return pl.pallas_call(
        kern,
        out_shape=[jax.ShapeDtypeStruct((nb, nt, d), F32),
                   jax.ShapeDtypeStruct((nb, rows * SUBLANES, LANES), F32),
                   jax.ShapeDtypeStruct((nb, 2, rows), jnp.int32),
                   jax.ShapeDtypeStruct((nb, 2, rows), jnp.int32),
                   jax.ShapeDtypeStruct((nb, 2, rows), F32),
                   jax.ShapeDtypeStruct((N_EXPERTS, LANES), F32)],
        grid=(n_steps + 1,),
        in_specs=[tile, tile, tile, tile, tile, tile, *residual_specs, whole(mod), square, square, square, square,
                  pl.BlockSpec((1, d), lambda s: (0, 0)), whole(wrt), whole(br)],
        out_specs=[tile,
                   pl.BlockSpec((1, TOKEN_TILE * SUBLANES, LANES), lambda s: (*cur(s), 0)),
                   small, small, small,
                   pl.BlockSpec((N_EXPERTS, LANES), lambda s: (0, 0))],
        scratch_shapes=[pltpu.VMEM((TOKEN_TILE, d), F32), pltpu.VMEM((N_EXPERTS, LANES), F32)],
        input_output_aliases=aliases,
        compiler_params=_params(dimension_semantics=("arbitrary",)),
        name="merge_route",
    )(oret, oswa, oga, ar, as_, aa, *residual, mod, w_ret, w_swa, w_ga, wout, g2.reshape(1, d), wrt, br)


def _row_copy(src, dst, sem):
    return pltpu.make_async_copy(src, dst, sem)


def _tile_rows(r):
    return pl.ds(pl.multiple_of(r * SUBLANES, SUBLANES), SUBLANES)


def _dispatch_kernel(tail_ref, nv_ref, dest_ref, h2t_ref, buf_out, zeros_ref, sem, tail_sem):
    rows = h2t_ref.shape[1] // SUBLANES

    @pl.when((pl.program_id(0) == 0) & (pl.program_id(1) == 0))
    def _zero_tails():
        zeros_ref[...] = jnp.zeros_like(zeros_ref)

        def zero_block(first_slot):
            span = pl.ds(pl.multiple_of(first_slot * SUBLANES, SUBLANES), MOE_ROWS * SUBLANES)
            copy = pltpu.make_async_copy(zeros_ref, buf_out.at[span, :], tail_sem)
            copy.start()
            copy.wait()

        for e in range(N_EXPERTS):
            @pl.when(tail_ref[e] >= 0)
            def _():
                zero_block(tail_ref[e])

        def unused(blk, carry):
            zero_block(blk * MOE_ROWS)
            return carry

        lax.fori_loop(nv_ref[0], buf_out.shape[0] // (MOE_ROWS * SUBLANES), unused, 0)

    def issue(r, carry):
        for k in range(2):
            d = dest_ref[0, 0, k * rows + r]
            _row_copy(h2t_ref.at[0, _tile_rows(r), :], buf_out.at[_tile_rows(d), :], sem).start(priority=k)
        return carry

    lax.fori_loop(0, rows, issue, 0, unroll=8)

    def drain(r, carry):
        for k in range(2):
            _row_copy(h2t_ref.at[0, _tile_rows(0), :], buf_out.at[_tile_rows(0), :], sem).wait()
        return carry

    lax.fori_loop(0, rows, drain, 0, unroll=8)


def _dispatch(tail_start, n_valid, dest_rows, h2t, n_slots):
    nb, rows8, _ = h2t.shape
    step_rows8 = rows8 // DISPATCH_STEPS
    grid_spec = pltpu.PrefetchScalarGridSpec(
        num_scalar_prefetch=2,
        grid=(nb, DISPATCH_STEPS),
        in_specs=[pl.BlockSpec((1, 1, dest_rows.shape[2]), lambda b, i, tail, nv: (b * DISPATCH_STEPS + i, 0, 0),
                               memory_space=pltpu.SMEM),
                  pl.BlockSpec((1, step_rows8, LANES), lambda b, i, tail, nv: (b, i, 0))],
        out_specs=pl.BlockSpec(memory_space=pl.ANY),
        scratch_shapes=[pltpu.VMEM((MOE_ROWS * SUBLANES, LANES), F32),
                        pltpu.SemaphoreType.DMA(()), pltpu.SemaphoreType.DMA(())],
    )
    return pl.pallas_call(
        _dispatch_kernel,
        out_shape=jax.ShapeDtypeStruct((n_slots * SUBLANES, LANES), F32),
        grid_spec=grid_spec,
        compiler_params=_params(has_side_effects=True, dimension_semantics=("arbitrary", "arbitrary")),
        name="moe_dispatch",
    )(tail_start, n_valid, dest_rows, h2t)


def _expert_kernel(be_ref, nv_ref, x_ref, wg_ref, wu_ref, wd_ref, y_ref, wg_bf, wu_bf, wd_bf):
    i = pl.program_id(0)

    @pl.when(i < nv_ref[0])
    def _():
        @pl.when((i == 0) | (be_ref[i] != be_ref[jnp.maximum(i - 1, 0)]))
        def _():
            wg_bf[...] = wg_ref[0, 0].astype(BF16)
            wu_bf[...] = wu_ref[0, 0].astype(BF16)
            wd_bf[...] = wd_ref[0, 0].astype(BF16)

        x = _load_row_tiles(x_ref, MOE_ROWS).astype(BF16)
        hid = _silu(_dot(x, wg_bf[...])) * _dot(x, wu_bf[...])
        _store_row_tiles(y_ref, _dot(hid.astype(BF16), wd_bf[...]))

    @pl.when(i >= nv_ref[0])
    def _():
        y_ref[...] = jnp.zeros_like(y_ref)


def _experts(block_e, n_valid, buf, wg, wu, wd, layer):
    n_slots = buf.shape[0] // SUBLANES
    d = wg.shape[2]
    used = lambda i, nv: jnp.maximum(jnp.minimum(i, nv[0] - 1), 0)
    rows = pl.BlockSpec((MOE_ROWS * SUBLANES, LANES), lambda i, be, nv: (used(i, nv), 0))
    grid_spec = pltpu.PrefetchScalarGridSpec(
        num_scalar_prefetch=2,
        grid=(n_slots // MOE_ROWS,),
        in_specs=[rows,
                  pl.BlockSpec((1, 1, d, D_EXPERT), lambda i, be, nv: (layer, be[used(i, nv)], 0, 0)),
                  pl.BlockSpec((1, 1, d, D_EXPERT), lambda i, be, nv: (layer, be[used(i, nv)], 0, 0)),
                  pl.BlockSpec((1, 1, D_EXPERT, d), lambda i, be, nv: (layer, be[used(i, nv)], 0, 0))],
        out_specs=pl.BlockSpec((MOE_ROWS * SUBLANES, LANES), lambda i, be, nv: (i, 0)),
        scratch_shapes=[pltpu.VMEM((d, D_EXPERT), BF16), pltpu.VMEM((d, D_EXPERT), BF16),
                        pltpu.VMEM((D_EXPERT, d), BF16)],
    )
    return pl.pallas_call(
        _expert_kernel,
        out_shape=jax.ShapeDtypeStruct((n_slots * SUBLANES, LANES), F32),
        grid_spec=grid_spec,
        compiler_params=_params(dimension_semantics=("arbitrary",)),
        name="moe_experts",
    )(block_e, n_valid, buf, wg, wu, wd)


def _combine_kernel(dest_ref, next_ref, y_hbm, wt_ref, x_ref, mod_ref, g_ref, *rest,
                    n_lat_tiles, n_tiles, n_batch, last):
    if last:
        out_ref, gbuf, sems = rest
    else:
        xo_ref, h_ref, gbuf, sems = rest
    t = pl.program_id(0)
    slot = lax.rem(t, 2)

    def gather(idx_ref, into):
        def issue(r, carry):
            for k in range(2):
                d = idx_ref[0, 0, k * TOKEN_TILE + r]
                _row_copy(y_hbm.at[_tile_rows(d), :], gbuf.at[into, k, _tile_rows(r), :],
                          sems.at[into]).start(priority=k)
            return carry

        lax.fori_loop(0, TOKEN_TILE, issue, 0, unroll=8)

    @pl.when(t == 0)
    def _():
        gather(dest_ref, 0)

    @pl.when(t + 1 < n_batch * n_tiles)
    def _():
        gather(next_ref, 1 - slot)

    def drain(r, carry):
        for k in range(2):
            _row_copy(y_hbm.at[_tile_rows(0), :], gbuf.at[slot, 0, _tile_rows(0), :], sems.at[slot]).wait()
        return carry

    lax.fori_loop(0, TOKEN_TILE, drain, 0, unroll=8)

    b = t // n_tiles
    m = mod_ref[pl.ds(jnp.where(t - b * n_tiles < n_lat_tiles, b, n_batch), 1), :]
    wt = wt_ref[0]
    moe = (_load_row_tiles(gbuf.at[slot, 0], TOKEN_TILE) * wt[:, 0:1]
           + _load_row_tiles(gbuf.at[slot, 1], TOKEN_TILE) * wt[:, 1:2])
    x = x_ref[0] + _mod_chunk(m, 5) * moe
    if last:
        out_ref[0] = _rms_normalize(x) * g_ref[...]
    else:
        xo_ref[0] = x
        h = _rms_normalize(x) * g_ref[...] * (1.0 + _mod_chunk(m, 7)) + _mod_chunk(m, 6)
        h_ref[0] = h.astype(BF16)


def _combine(dest_tiles, y, wt, xa, mod, g, n_lat_tiles, n_tiles, last):
    nb, nt, d = xa.shape
    rows = n_tiles * TOKEN_TILE
    n_steps = nb * n_tiles
    kern = functools.partial(_combine_kernel, n_lat_tiles=n_lat_tiles, n_tiles=n_tiles, n_batch=nb, last=last)
    tile = pl.BlockSpec((1, TOKEN_TILE, d), lambda t: (t // n_tiles, t % n_tiles, 0))
    if last:
        out_shape = jax.ShapeDtypeStruct((nb, rows, d), F32)
        out_specs = tile
        aliases = {}
    else:
        out_shape = [jax.ShapeDtypeStruct((nb, nt, d), F32), jax.ShapeDtypeStruct((nb, nt, d), BF16)]
        out_specs = [tile, tile]
        aliases = {4: 0}
    return pl.pallas_call(
        kern,
        out_shape=out_shape,
        grid=(n_steps,),
        in_specs=[pl.BlockSpec((1, 1, 2 * TOKEN_TILE), lambda t: (t, 0, 0), memory_space=pltpu.SMEM),
                  pl.BlockSpec((1, 1, 2 * TOKEN_TILE), lambda t: (jnp.minimum(t + 1, n_steps - 1), 0, 0),
                               memory_space=pltpu.SMEM),
                  pl.BlockSpec(memory_space=pl.ANY),
                  pl.BlockSpec((1, TOKEN_TILE, 2), lambda t: (t // n_tiles, t % n_tiles, 0)),
                  tile,
                  pl.BlockSpec(mod.shape, lambda t: (0, 0)),
                  pl.BlockSpec((1, d), lambda t: (0, 0))],
        out_specs=out_specs,
        scratch_shapes=[pltpu.VMEM((2, 2, TOKEN_TILE * SUBLANES, LANES), F32), pltpu.SemaphoreType.DMA((2,))],
        input_output_aliases=aliases,
        compiler_params=_params(dimension_semantics=("arbitrary",)),
        name="moe_combine",
    )(dest_tiles, dest_tiles, y, wt, xa, mod, g.reshape(1, d))


def _slot_plan(experts, ranks, counts):
    nb, _, rows = experts.shape
    padded = (counts + MOE_ROWS - 1) // MOE_ROWS * MOE_ROWS
    pad_ends = jnp.cumsum(padded)
    pad_starts = pad_ends - padded
    one_hot = experts[..., None] == jnp.arange(N_EXPERTS, dtype=jnp.int32)
    dest = (jnp.sum(jnp.where(one_hot, pad_starts, 0), axis=-1) + ranks).astype(jnp.int32)
    n_assign = nb * 2 * rows
    n_blocks = (n_assign + N_EXPERTS * (MOE_ROWS - 1) + MOE_ROWS - 1) // MOE_ROWS
    first_slot = jnp.arange(n_blocks, dtype=jnp.int32) * MOE_ROWS
    block_e = jnp.minimum(jnp.sum((pad_ends[None, :] <= first_slot[:, None]).astype(jnp.int32), axis=1),
                          N_EXPERTS - 1).astype(jnp.int32)
    n_valid = (pad_ends[-1:] // MOE_ROWS).astype(jnp.int32)
    tail_start = jnp.where(padded > 0, pad_ends - MOE_ROWS, -1).astype(jnp.int32)
    n_tiles = rows // TOKEN_TILE
    dest_tiles = (dest.reshape(nb, 2, n_tiles, TOKEN_TILE)
                  .transpose(0, 2, 1, 3).reshape(nb * n_tiles, 1, 2 * TOKEN_TILE))
    step_rows = rows // DISPATCH_STEPS
    dest_rows = (dest.reshape(nb, 2, DISPATCH_STEPS, step_rows)
                 .transpose(0, 2, 1, 3).reshape(nb * DISPATCH_STEPS, 1, 2 * step_rows))
    return dest_tiles, dest_rows, block_e, n_valid, tail_start, n_blocks * MOE_ROWS


def _rope_tables(length, n_ctx):
    rows = length // GRID_W
    row = jnp.repeat(jnp.arange(rows, dtype=jnp.int32), GRID_W).astype(F32)
    col = jnp.tile(jnp.arange(GRID_W, dtype=jnp.int32), rows).astype(F32)

    def table(hd):
        quarter = hd // 4
        freqs = ROPE_THETA ** (-jnp.arange(quarter, dtype=F32) / quarter)
        ang_r = row[:, None] * freqs[None, :]
        ang_c = col[:, None] * freqs[None, :]
        cos = jnp.concatenate([jnp.cos(ang_r), jnp.cos(ang_r), jnp.cos(ang_c), jnp.cos(ang_c)], axis=-1)
        sin = jnp.concatenate([-jnp.sin(ang_r), jnp.sin(ang_r), -jnp.sin(ang_c), jnp.sin(ang_c)], axis=-1)
        cos = jnp.concatenate([cos, jnp.ones((n_ctx, hd), F32)], axis=0)
        sin = jnp.concatenate([sin, jnp.zeros((n_ctx, hd), F32)], axis=0)
        reps = LANES // hd
        return jnp.tile(cos, (1, reps)), jnp.tile(sin, (1, reps))

    c128, s128 = table(RET_DK)
    c64, s64 = table(SWA_HD)
    return c128, s128, c64, s64


def kernel(x, c, ctx, c_ctx, w_mod, b_mod, g_norm1, g_norm2, w_in, ret_decay_logit, swa_sink, g_qnorm, g_knorm,
           w_br_ret, w_br_swa, w_br_ga, w_out, w_router, b_router, w_gate, w_up, w_down, g_final):
    nb, length, d = x.shape
    n_ctx = ctx.shape[1]
    depth = w_mod.shape[0]
    nt = length + n_ctx
    n_lat_tiles = length // TOKEN_TILE
    n_all_tiles = nt // TOKEN_TILE
    assert GA_HD == RET_DK == LANES and 2 * SWA_HD == LANES
    assert length % TOKEN_TILE == 0 and n_ctx == TOKEN_TILE and length % n_ctx == 0

    mod_rows = 8
    c_rows = jnp.concatenate([c, c_ctx[None, :], jnp.zeros((mod_rows - nb - 1, d), F32)], axis=0)
    mods = _modulation(c_rows, w_mod, b_mod)
    tables = _rope_tables(length, n_ctx)
    wrt = w_router.astype(F32).T
    br = b_router.astype(F32).reshape(N_EXPERTS, 1)

    assert depth >= 2, "the first layer's merge kernel must also cover the context rows"
    xa = (x, ctx)
    h = (x, ctx, mods[0], g_norm1[0])
    w_in_bf, w_ret_bf, w_swa_bf, w_ga_bf, w_out_bf = (
        w.astype(BF16) for w in (w_in, w_br_ret, w_br_swa, w_br_ga, w_out))
    out = None
    for l in range(depth):
        need_ctx = l < depth - 1
        n_tiles = n_all_tiles if need_ctx else n_lat_tiles
        (rq, rk, rv, ru, sq, sk, sv, gq, gk, gv, ar, as_, aa) = _inproj(
            h, w_in_bf, l, tables, g_qnorm[l], g_knorm[l])
        o_ret = _retention(ret_decay_logit[l].astype(F32), rq, rk, rv, ru, length, n_ctx, need_ctx)
        o_swa = _window_attention(swa_sink[l].astype(F32), sq, sk, sv, length, n_ctx, need_ctx)
        o_ga = _global_attention(gq, gk, gv, length, n_ctx, need_ctx)
        xa, h2t, experts, ranks, weights, counts = _merge(
            o_ret, o_swa, o_ga, ar, as_, aa, xa, mods[l], w_ret_bf, w_swa_bf, w_ga_bf, w_out_bf, l,
            g_norm2[l], wrt, br, n_lat_tiles, n_tiles)
        dest_tiles, dest_rows, block_e, n_valid, tail_start, n_slots = _slot_plan(
            experts, ranks, counts[:, 0].astype(jnp.int32))
        buf = _dispatch(tail_start, n_valid, dest_rows, h2t, n_slots)
        y = _experts(block_e, n_valid, buf, w_gate, w_up, w_down, l)
        wt = weights.transpose(0, 2, 1)
        if need_ctx:
            mod_pair = jnp.concatenate([mods[l], mods[l + 1][:, :2 * d]], axis=1)
            xa, h = _combine(dest_tiles, y, wt, xa, mod_pair, g_norm1[l + 1], n_lat_tiles, n_tiles, last=False)
        else:
            out = _combine(dest_tiles, y, wt, xa, mods[l], g_final, n_lat_tiles, n_tiles, last=True)
    return out
```

```python
import functools

import jax
import jax.numpy as jnp
from jax import lax
from jax.experimental import pallas as pl
from jax.experimental.pallas import tpu as pltpu

F32 = jnp.float32
BF16 = jnp.bfloat16

D_MODEL = 1024
GRID_W = 64
NORM_EPS = 1e-6
ROPE_THETA = 10000.0
RET_HEADS, RET_DK, RET_DV, RET_CHUNK = 4, 128, 256, 256
SWA_HEADS, SWA_KV_HEADS, SWA_HD, WINDOW = 16, 2, 64, 128
GA_HEADS, GA_KV_HEADS, GA_HD = 8, 2, 128
N_EXPERTS, N_GROUPS, EXPERTS_PER_GROUP, D_EXPERT = 16, 4, 4, 512

RET_QK = RET_HEADS * RET_DK
RET_V = RET_HEADS * RET_DV
SWA_Q = SWA_HEADS * SWA_HD
SWA_KV = SWA_KV_HEADS * SWA_HD
GA_Q = GA_HEADS * GA_HD
GA_KV = GA_KV_HEADS * GA_HD
SPLITS = (RET_QK, RET_QK, RET_V, RET_V, SWA_Q, SWA_KV, SWA_KV, GA_Q, GA_KV, GA_KV,
          D_MODEL, D_MODEL, D_MODEL)

LANES = 128
SUBLANES = 8
TOKEN_TILE = 256
QUERY_TILE = 128
GA_QUERY_TILE = 256
MOE_ROWS = 512
DISPATCH_STEPS = 4
GA_KEY_CHUNK = 256
RET_HEADS_PER_STEP = 2
SWA_PARTS = 2
SWA_TILES_PER_CHUNK = 2
MOD_COLS = 1536
VMEM_LIMIT = 56 * 1024 * 1024
MASKED = -1e30
LOG2E = 1.4426950408889634

_W_WIDTHS = (RET_QK, RET_QK, RET_V, RET_V, SWA_Q, 2 * SWA_KV, 2 * SWA_KV, GA_Q, GA_KV, GA_KV,
             D_MODEL, D_MODEL, D_MODEL)
_IN_OFFS = tuple(sum(SPLITS[:i]) for i in range(len(SPLITS)))


def _dot(a, b):
    return jnp.dot(a, b, preferred_element_type=F32)


def _dot_nt(a, b):
    return lax.dot_general(a, b, (((1,), (1,)), ((), ())), preferred_element_type=F32)


def _dot_tn(a, b):
    return lax.dot_general(a, b, (((0,), (0,)), ((), ())), preferred_element_type=F32)


def _silu(x):
    return x * jax.nn.sigmoid(x)


def _rms_normalize(x):
    return x * lax.rsqrt(jnp.mean(x * x, axis=-1, keepdims=True) + NORM_EPS)


def _params(**kw):
    return pltpu.CompilerParams(vmem_limit_bytes=VMEM_LIMIT, **kw)


def _mod_kernel(c_ref, w_ref, b_ref, o_ref):
    a = _silu(c_ref[...])
    o_ref[0] = _dot(a.astype(BF16), w_ref[0].astype(BF16)) + b_ref[0]


def _modulation(c_rows, w_mod, b_mod):
    depth, d, n = w_mod.shape
    rows = c_rows.shape[0]
    return pl.pallas_call(
        _mod_kernel,
        out_shape=jax.ShapeDtypeStruct((depth, rows, n), F32),
        grid=(depth, n // MOD_COLS),
        in_specs=[pl.BlockSpec((rows, d), lambda l, j: (0, 0)),
                  pl.BlockSpec((1, d, MOD_COLS), lambda l, j: (l, 0, j)),
                  pl.BlockSpec((1, 1, MOD_COLS), lambda l, j: (l, 0, j))],
        out_specs=pl.BlockSpec((1, rows, MOD_COLS), lambda l, j: (l, 0, j)),
        compiler_params=_params(),
        name="modulation",
    )(c_rows, w_mod, b_mod.reshape(depth, 1, n))


def _mod_row(mod_ref, n_lat_tiles, n_batch):
    b = pl.program_id(0)
    i = pl.program_id(1)
    r = jnp.where(i < n_lat_tiles, b, n_batch)
    return mod_ref[pl.ds(r, 1), :]


def _mod_chunk(m, k):
    return m[:, k * D_MODEL:(k + 1) * D_MODEL]


def _rope(x, cos, sin, quarter):
    lane = lax.broadcasted_iota(jnp.int32, x.shape, 1)
    first = (lane % (2 * quarter)) < quarter
    partner = jnp.where(first, pltpu.roll(x, LANES - quarter, 1), pltpu.roll(x, quarter, 1))
    return x * cos + partner * sin


def _first_layer_tokens(x_ref, c_ref, i, n_lat_tiles):
    return jnp.where(i < n_lat_tiles, x_ref[0], c_ref[0])


def _inproj_kernel(*refs, first_layer, n_lat_tiles, n_batch):
    if first_layer:
        x_ref, c_ref, mod_ref, g_ref, *refs = refs
        m = _mod_row(mod_ref, n_lat_tiles, n_batch)
        x = _first_layer_tokens(x_ref, c_ref, pl.program_id(1), n_lat_tiles)
        h = (_rms_normalize(x) * g_ref[...] * (1.0 + _mod_chunk(m, 1)) + _mod_chunk(m, 0)).astype(BF16)
    else:
        h_ref, *refs = refs
        h = h_ref[0]
    (w_ref, c128_ref, s128_ref, c64_ref, s64_ref, gq_ref, gk_ref,
     rq, rk, rv, ru, sq, sk, sv, gq, gk, gv, ar, as_, aa) = refs
    c128, s128 = c128_ref[...], s128_ref[...]
    c64, s64 = c64_ref[...], s64_ref[...]

    def proj(idx):
        return _dot(h, w_ref[0, :, _IN_OFFS[idx]:_IN_OFFS[idx] + SPLITS[idx]])

    def slabs(acc):
        return [acc[:, s * LANES:(s + 1) * LANES] for s in range(acc.shape[1] // LANES)]

    def store(ref, s, val):
        ref[0, :, s * LANES:(s + 1) * LANES] = val.astype(ref.dtype)

    def head_per_slab(acc):
        low = lax.broadcasted_iota(jnp.int32, acc.shape, 1) < SWA_HD
        other = pltpu.roll(acc, SWA_HD, 1)
        return [jnp.where(low, acc, other), jnp.where(low, other, acc)]

    for s, xs in enumerate(slabs(proj(0))):
        store(rq, s, _rope(xs, c128, s128, RET_DK // 4))
    for s, xs in enumerate(slabs(proj(1))):
        store(rk, s, _rope(xs, c128, s128, RET_DK // 4) * (RET_DK ** -0.5))
    rv[0] = proj(2).astype(rv.dtype)
    ru[0] = proj(3).astype(ru.dtype)
    for s, xs in enumerate(slabs(proj(4))):
        store(sq, s, _rope(xs, c64, s64, SWA_HD // 4) * (SWA_HD ** -0.5 * LOG2E))
    for s, xs in enumerate(head_per_slab(proj(5))):
        store(sk, s, _rope(xs, c64, s64, SWA_HD // 4))
    for s, xs in enumerate(head_per_slab(proj(6))):
        store(sv, s, xs)
    for s, xs in enumerate(slabs(proj(7))):
        xn = _rms_normalize(xs) * gq_ref[...]
        store(gq, s, _rope(xn, c128, s128, GA_HD // 4) * (GA_HD ** -0.5 * LOG2E))
    for s, xs in enumerate(slabs(proj(8))):
        xn = _rms_normalize(xs) * gk_ref[...]
        store(gk, s, _rope(xn, c128, s128, GA_HD // 4))
    gv[0] = proj(9).astype(gv.dtype)
    ar[0] = proj(10).astype(ar.dtype)
    as_[0] = proj(11).astype(as_.dtype)
    aa[0] = proj(12).astype(aa.dtype)


def _first_layer_specs(x, ctx, n_lat_tiles, index):
    d = x.shape[2]
    return [pl.BlockSpec((1, TOKEN_TILE, d), lambda *ids: (index(*ids)[0],
                                                          jnp.minimum(index(*ids)[1], n_lat_tiles - 1), 0)),
            pl.BlockSpec((1, TOKEN_TILE, d), lambda *ids: (index(*ids)[0], 0, 0))]


def _inproj(tokens, w_in, layer, tables, g_q, g_k):
    first_layer = isinstance(tokens, tuple)
    tile = lambda width: pl.BlockSpec((1, TOKEN_TILE, width), lambda b, i: (b, i, 0))
    if first_layer:
        x, ctx, mod, g = tokens
        nb, length, d = x.shape
        nt = length + ctx.shape[1]
        n_lat_tiles = length // TOKEN_TILE
        lead_specs = _first_layer_specs(x, ctx, n_lat_tiles, lambda b, i: (b, i)) + [
            pl.BlockSpec(mod.shape, lambda b, i: (0, 0)), pl.BlockSpec((1, d), lambda b, i: (0, 0))]
        lead = (x, ctx, mod, g.reshape(1, d))
    else:
        nb, nt, d = tokens.shape
        n_lat_tiles = 0
        lead_specs = [tile(d)]
        lead = (tokens,)
    tab = pl.BlockSpec((TOKEN_TILE, LANES), lambda b, i: (i, 0))
    vec = pl.BlockSpec((1, LANES), lambda b, i: (0, 0))
    outs = [jax.ShapeDtypeStruct((nb, nt, w), BF16) for w in _W_WIDTHS]
    kern = functools.partial(_inproj_kernel, first_layer=first_layer, n_lat_tiles=n_lat_tiles, n_batch=nb)
    return pl.pallas_call(
        kern,
        out_shape=outs,
        grid=(nb, nt // TOKEN_TILE),
        in_specs=lead_specs + [
            pl.BlockSpec((1, d, w_in.shape[2]), lambda b, i: (layer, 0, 0), pipeline_mode=pl.Buffered(1)),
            tab, tab, tab, tab, vec, vec],
        out_specs=[tile(w) for w in _W_WIDTHS],
        compiler_params=_params(),
        name="inproj",
    )(*lead, w_in, *tables, g_q.reshape(1, LANES), g_k.reshape(1, LANES))


def _log_sigmoid(x):
    return jnp.minimum(x, 0.0) - jnp.log1p(jnp.exp(-jnp.abs(x)))


def _ret_kernel(logit_ref, q_ref, k_ref, v_ref, u_ref, o_ref, acc_ref, sf_ref, sb_ref,
                *, length, n_ctx, need_ctx):
    chunk = RET_CHUNK
    n_chunks = length // chunk
    half = n_chunks // 2

    def rows_f32(shape):
        return lax.broadcasted_iota(jnp.int32, shape, 0).astype(F32)

    def finish(o, u):
        return (_rms_normalize(o) * _silu(u.astype(F32))).astype(o_ref.dtype)

    ci = rows_f32((chunk, RET_DK))
    heads = []
    for hi in range(RET_HEADS_PER_STEP):
        hh = pl.program_id(1) * RET_HEADS_PER_STEP + hi
        lgf = _log_sigmoid(jnp.full((1, 1), logit_ref[0, hh], F32))
        lgb = _log_sigmoid(jnp.full((1, 1), logit_ref[1, hh], F32))
        qk = slice(hi * RET_DK, (hi + 1) * RET_DK)
        vu = slice(hi * RET_DV, (hi + 1) * RET_DV)

        def both_ways_decay(n, lgf=lgf, lgb=lgb):
            diff = rows_f32((n, n)) - lax.broadcasted_iota(jnp.int32, (n, n), 1).astype(F32)
            return jnp.where(diff >= 0, jnp.exp(jnp.maximum(diff, 0.0) * lgf),
                             jnp.exp(jnp.maximum(-diff, 0.0) * lgb))

        kx = k_ref[0, length:length + n_ctx, qk].astype(F32)
        vx = v_ref[0, length:length + n_ctx, vu]
        lx = rows_f32((n_ctx, RET_DK))
        sf_ref[hi] = _dot_tn((kx * jnp.exp((n_ctx - 1.0 - lx) * lgf)).astype(BF16), vx)
        sb_ref[hi] = _dot_tn((kx * jnp.exp(lx * lgb)).astype(BF16), vx)
        if need_ctx:
            qx = q_ref[0, length:length + n_ctx, qk]
            sx = _dot_nt(qx, kx.astype(BF16)) * both_ways_decay(n_ctx)
            o_ref[0, length:length + n_ctx, vu] = finish(_dot(sx.astype(BF16), vx),
                                                         u_ref[0, length:length + n_ctx, vu])
        heads.append(dict(
            qk=qk, vu=vu, intra=both_ways_decay(chunk),
            q_dec_f=jnp.exp((ci + 1.0) * lgf), k_dec_f=jnp.exp((chunk - 1.0 - ci) * lgf),
            q_dec_b=jnp.exp((chunk - ci) * lgb), k_dec_b=jnp.exp(ci * lgb),
            chunk_dec_f=jnp.exp(chunk * lgf), chunk_dec_b=jnp.exp(chunk * lgb)))

    def sweep(s, first_touch):
        rf = pl.ds(pl.multiple_of(s * chunk, chunk), chunk)
        rb = pl.ds(pl.multiple_of((n_chunks - 1 - s) * chunk, chunk), chunk)
        for hi, hd in enumerate(heads):
            qk, vu = hd["qk"], hd["vu"]
            qf, kf, vf = q_ref[0, rf, qk], k_ref[0, rf, qk], v_ref[0, rf, vu]
            sc = _dot_nt(qf, kf) * hd["intra"]
            state_f = sf_ref[hi]
            o_f = (_dot(sc.astype(BF16), vf)
                   + _dot((qf.astype(F32) * hd["q_dec_f"]).astype(BF16), state_f.astype(BF16)))
            sf_ref[hi] = state_f * hd["chunk_dec_f"] + _dot_tn((kf.astype(F32) * hd["k_dec_f"]).astype(BF16), vf)
            qb, kb, vb = q_ref[0, rb, qk], k_ref[0, rb, qk], v_ref[0, rb, vu]
            state_b = sb_ref[hi]
            o_b = _dot((qb.astype(F32) * hd["q_dec_b"]).astype(BF16), state_b.astype(BF16))
            sb_ref[hi] = state_b * hd["chunk_dec_b"] + _dot_tn((kb.astype(F32) * hd["k_dec_b"]).astype(BF16), vb)
            if first_touch:
                acc_ref[hi, rf, :] = o_f
                acc_ref[hi, rb, :] = o_b
            else:
                o_ref[0, rf, vu] = finish(acc_ref[hi, rf, :] + o_f, u_ref[0, rf, vu])
                o_ref[0, rb, vu] = finish(acc_ref[hi, rb, :] + o_b, u_ref[0, rb, vu])

    def first_half(s, carry):
        sweep(s, True)
        return carry

    def second_half(s, carry):
        sweep(s, False)
        return carry

    lax.fori_loop(0, half, first_half, 0)
    lax.fori_loop(half, n_chunks, second_half, 0)


def _retention(logit, rq, rk, rv, ru, length, n_ctx, need_ctx):
    nb, nt, _ = rq.shape
    assert (length // RET_CHUNK) % 2 == 0 and RET_HEADS % RET_HEADS_PER_STEP == 0
    out_rows = nt if need_ctx else length
    kern = functools.partial(_ret_kernel, length=length, n_ctx=n_ctx, need_ctx=need_ctx)
    head = lambda width: pl.BlockSpec((1, nt, RET_HEADS_PER_STEP * width), lambda b, h: (b, 0, h))
    return pl.pallas_call(
        kern,
        out_shape=jax.ShapeDtypeStruct((nb, out_rows, RET_V), BF16),
        grid=(nb, RET_HEADS // RET_HEADS_PER_STEP),
        in_specs=[pl.BlockSpec(memory_space=pltpu.SMEM),
                  head(RET_DK), head(RET_DK), head(RET_DV), head(RET_DV)],
        out_specs=pl.BlockSpec((1, out_rows, RET_HEADS_PER_STEP * RET_DV), lambda b, h: (b, 0, h)),
        scratch_shapes=[pltpu.VMEM((RET_HEADS_PER_STEP, length, RET_DV), F32),
                        pltpu.VMEM((RET_HEADS_PER_STEP, RET_DK, RET_DV), F32),
                        pltpu.VMEM((RET_HEADS_PER_STEP, RET_DK, RET_DV), F32)],
        compiler_params=_params(),
        name="retention",
    )(logit, rq, rk, rv, ru)


def _swa_group(q_ref, g, kcat, vcat, biases, sink_ref, o_ref):
    tq = q_ref.shape[1]
    heads_per_group = SWA_HEADS // SWA_KV_HEADS
    slabs_per_group = heads_per_group // 2
    low = lax.broadcasted_iota(jnp.int32, (tq, LANES), 1) < SWA_HD
    zero = jnp.zeros((tq, LANES), q_ref.dtype)
    stacked = []
    for a in range(slabs_per_group):
        s0 = (g * slabs_per_group + a) * LANES
        slab = q_ref[0, :, s0:s0 + LANES]
        stacked.append(jnp.where(low, slab, zero))
        stacked.append(jnp.where(low, zero, slab))
    heads_per_part = heads_per_group // SWA_PARTS
    rows = heads_per_part * tq
    n_key_tiles = kcat.shape[0] // LANES
    for part in range(SWA_PARTS):
        q = jnp.concatenate(stacked[part * heads_per_part:(part + 1) * heads_per_part], axis=0)
        sink = jnp.concatenate(
            [jnp.full((tq, LANES), sink_ref[g * heads_per_group + part * heads_per_part + h] * LOG2E, F32)
             for h in range(heads_per_part)], axis=0)
        m = l = acc = None
        for t0 in range(0, n_key_tiles, SWA_TILES_PER_CHUNK):
            t1 = min(t0 + SWA_TILES_PER_CHUNK, n_key_tiles)
            s = _dot_nt(q, kcat[t0 * LANES:t1 * LANES])
            tiles = [s[:, t * LANES:(t + 1) * LANES] for t in range(t1 - t0)]
            for t in range(t0, t1):
                if t in biases:
                    tiles[t - t0] = tiles[t - t0] + biases[t][:rows]
            mc = jnp.broadcast_to(jnp.max(functools.reduce(jnp.maximum, tiles), axis=-1, keepdims=True),
                                  (rows, LANES))
            m_new = jnp.maximum(mc, sink if m is None else m)
            ps = [jnp.exp2(tile - m_new) for tile in tiles]
            pv = _dot(jnp.concatenate(ps, axis=1).astype(BF16), vcat[t0 * LANES:t1 * LANES])
            psum = functools.reduce(jnp.add, ps)
            if m is None:
                l, acc = psum, pv
            else:
                alpha = jnp.exp2(m - m_new)
                l, acc = alpha * l + psum, alpha * acc + pv
            m = m_new
        den = jnp.broadcast_to(jnp.sum(l, axis=-1, keepdims=True), (rows, LANES)) + jnp.exp2(sink - m)
        o = acc / den
        for a in range(heads_per_part // 2):
            s0 = (g * slabs_per_group + part * (heads_per_part // 2) + a) * LANES
            even = o[(2 * a) * tq:(2 * a + 1) * tq]
            odd = o[(2 * a + 1) * tq:(2 * a + 2) * tq]
            o_ref[0, :, s0:s0 + LANES] = jnp.where(low, even, odd).astype(o_ref.dtype)


def _swa_kernel(sink_ref, q_ref, kp_ref, kc_ref, kn_ref, kx_ref, vp_ref, vc_ref, vn_ref, vx_ref, o_ref,
                *, n_lat_tiles, need_ctx):
    j = pl.program_id(1)
    tq = q_ref.shape[1]
    n_ctx = kx_ref.shape[1]

    def lanes_of(ref, g):
        return ref[0, :, g * LANES:(g + 1) * LANES]

    @pl.when(j < n_lat_tiles)
    def _latent():
        ci = lax.broadcasted_iota(jnp.int32, (tq, tq), 1)
        ri = lax.broadcasted_iota(jnp.int32, (tq, tq), 0)
        far = 4 * tq
        ri_prev = ri + jnp.where(j > 0, 0, far)
        ri_next = ri - jnp.where(j < n_lat_tiles - 1, 0, far)
        zero = jnp.zeros((tq, tq), F32)
        masked = jnp.full((tq, tq), MASKED, F32)
        heads_per_part = SWA_HEADS // SWA_KV_HEADS // SWA_PARTS
        bias = {0: jnp.concatenate([jnp.where(ci >= ri_prev, zero, masked)] * heads_per_part, axis=0),
                2: jnp.concatenate([jnp.where(ci <= ri_next, zero, masked)] * heads_per_part, axis=0)}
        for g in range(SWA_KV_HEADS):
            kcat = jnp.concatenate([lanes_of(kp_ref, g), lanes_of(kc_ref, g), lanes_of(kn_ref, g),
                                    lanes_of(kx_ref, g)], axis=0)
            vcat = jnp.concatenate([lanes_of(vp_ref, g), lanes_of(vc_ref, g), lanes_of(vn_ref, g),
                                    lanes_of(vx_ref, g)], axis=0)
            _swa_group(q_ref, g, kcat, vcat, bias, sink_ref, o_ref)

    if need_ctx:
        @pl.when(j >= n_lat_tiles)
        def _context():
            for g in range(SWA_KV_HEADS):
                _swa_group(q_ref, g, lanes_of(kx_ref, g), lanes_of(vx_ref, g), {}, sink_ref, o_ref)


def _window_attention(sink, sq, sk, sv, length, n_ctx, need_ctx):
    nb, nt, _ = sq.shape
    tq = QUERY_TILE
    assert tq == WINDOW, "the previous / next key tile masks assume one window per query tile"
    n_lat = length // tq
    out_rows = nt if need_ctx else length
    kvw = 2 * SWA_KV
    kern = functools.partial(_swa_kernel, n_lat_tiles=n_lat, need_ctx=need_ctx)
    prev = pl.BlockSpec((1, tq, kvw), lambda b, j: (b, jnp.clip(j - 1, 0, n_lat - 1), 0))
    cur = pl.BlockSpec((1, tq, kvw), lambda b, j: (b, jnp.minimum(j, n_lat - 1), 0))
    nxt = pl.BlockSpec((1, tq, kvw), lambda b, j: (b, jnp.minimum(j + 1, n_lat - 1), 0))
    ctx = pl.BlockSpec((1, n_ctx, kvw), lambda b, j: (b, length // n_ctx, 0))
    return pl.pallas_call(
        kern,
        out_shape=jax.ShapeDtypeStruct((nb, out_rows, SWA_Q), BF16),
        grid=(nb, out_rows // tq),
        in_specs=[pl.BlockSpec(memory_space=pltpu.SMEM),
                  pl.BlockSpec((1, tq, SWA_Q), lambda b, j: (b, j, 0)),
                  prev, cur, nxt, ctx, prev, cur, nxt, ctx],
        out_specs=pl.BlockSpec((1, tq, SWA_Q), lambda b, j: (b, j, 0)),
        compiler_params=_params(),
        name="window_attention",
    )(sink, sq, sk, sk, sk, sk, sv, sv, sv, sv)


def _ga_kernel(q_ref, k_ref, v_ref, o_ref, *, length, n_ctx, n_lat_tiles, need_ctx):
    j = pl.program_id(2)
    tq = q_ref.shape[1]
    heads_per_group = GA_HEADS // GA_KV_HEADS
    q = jnp.concatenate([q_ref[0, :, h * LANES:(h + 1) * LANES] for h in range(heads_per_group)], axis=0)

    rows = heads_per_group * tq

    def attend(key_lo, key_hi):
        m = l = acc = None
        for c0 in range(key_lo, key_hi, GA_KEY_CHUNK):
            c1 = min(c0 + GA_KEY_CHUNK, key_hi)
            s = _dot_nt(q, k_ref[0, c0:c1, :])
            parts = [s[:, t * LANES:(t + 1) * LANES] for t in range((c1 - c0) // LANES)]
            lane_max = functools.reduce(jnp.maximum, parts)
            mc = jnp.broadcast_to(jnp.max(lane_max, axis=-1, keepdims=True), (rows, LANES))
            m_new = mc if m is None else jnp.maximum(m, mc)
            ps = [jnp.exp2(part - m_new) for part in parts]
            pv = _dot(jnp.concatenate(ps, axis=1).astype(BF16), v_ref[0, c0:c1, :])
            psum = functools.reduce(jnp.add, ps)
            if m is None:
                l, acc = psum, pv
            else:
                alpha = jnp.exp2(m - m_new)
                l, acc = alpha * l + psum, alpha * acc + pv
            m = m_new
        o = acc / jnp.sum(l, axis=-1, keepdims=True)
        for h in range(heads_per_group):
            o_ref[0, :, h * LANES:(h + 1) * LANES] = o[h * tq:(h + 1) * tq].astype(o_ref.dtype)

    @pl.when(j < n_lat_tiles)
    def _latent():
        attend(0, length + n_ctx)

    if need_ctx:
        @pl.when(j >= n_lat_tiles)
        def _context():
            attend(length, length + n_ctx)


def _global_attention(gq, gk, gv, length, n_ctx, need_ctx):
    nb, nt, _ = gq.shape
    tq = GA_QUERY_TILE
    out_rows = nt if need_ctx else length
    group_w = GA_Q // GA_KV_HEADS
    kern = functools.partial(_ga_kernel, length=length, n_ctx=n_ctx, n_lat_tiles=length // tq,
                             need_ctx=need_ctx)
    kv = pl.BlockSpec((1, nt, GA_HD), lambda b, g, j: (b, 0, g))
    return pl.pallas_call(
        kern,
        out_shape=jax.ShapeDtypeStruct((nb, out_rows, GA_Q), BF16),
        grid=(nb, GA_KV_HEADS, out_rows // tq),
        in_specs=[pl.BlockSpec((1, tq, group_w), lambda b, g, j: (b, j, g)), kv, kv],
        out_specs=pl.BlockSpec((1, tq, group_w), lambda b, g, j: (b, j, g)),
        compiler_params=_params(),
        name="global_attention",
    )(gq, gk, gv)


def _route(scores, biased):
    rows = [biased[r:r + 1, :] for r in range(N_EXPERTS)]
    raw = [scores[r:r + 1, :] for r in range(N_EXPERTS)]

    def top2_sum(vals):
        best = None
        for a in range(len(vals)):
            for b in range(a + 1, len(vals)):
                pair = vals[a] + vals[b]
                best = pair if best is None else jnp.maximum(best, pair)
        return best

    group_scores = [top2_sum(rows[g * EXPERTS_PER_GROUP:(g + 1) * EXPERTS_PER_GROUP]) for g in range(N_GROUPS)]
    group = jnp.zeros_like(group_scores[0], dtype=jnp.int32)
    best = group_scores[0]
    for g in range(1, N_GROUPS):
        better = group_scores[g] > best
        group = jnp.where(better, g, group)
        best = jnp.where(better, group_scores[g], best)

    def in_group(table, k):
        val = table[k]
        for g in range(1, N_GROUPS):
            val = jnp.where(group == g, table[g * EXPERTS_PER_GROUP + k], val)
        return val

    vals = [in_group(rows, k) for k in range(EXPERTS_PER_GROUP)]
    unbiased = [in_group(raw, k) for k in range(EXPERTS_PER_GROUP)]

    def first_argmax(cands):
        idx = jnp.zeros_like(group)
        top = cands[0]
        for k in range(1, len(cands)):
            better = cands[k] > top
            idx = jnp.where(better, k, idx)
            top = jnp.where(better, cands[k], top)
        return idx

    i1 = first_argmax(vals)
    i2 = first_argmax([jnp.where(i1 == k, -jnp.inf, vals[k]) for k in range(EXPERTS_PER_GROUP)])

    def pick(idx):
        val = unbiased[0]
        for k in range(1, EXPERTS_PER_GROUP):
            val = jnp.where(idx == k, unbiased[k], val)
        return val

    s1, s2 = pick(i1), pick(i2)
    total = s1 + s2
    experts = jnp.concatenate([group * EXPERTS_PER_GROUP + i1, group * EXPERTS_PER_GROUP + i2], axis=0)
    weights = jnp.concatenate([s1 / total, s2 / total], axis=0)
    return experts, weights


def _slot_ranks(experts, carry):
    tm = experts.shape[1]
    expert_id = lax.broadcasted_iota(jnp.int32, (N_EXPERTS, tm), 0)
    upper = jnp.where(lax.broadcasted_iota(jnp.int32, (tm, tm), 0) <= lax.broadcasted_iota(jnp.int32, (tm, tm), 1),
                      1.0, 0.0).astype(BF16)
    ranks = []
    for k in range(2):
        hit = jnp.where(expert_id == experts[k:k + 1, :], 1.0, 0.0)
        inclusive = _dot(hit.astype(BF16), upper)
        ranks.append(jnp.sum(hit * (carry + inclusive - 1.0), axis=0, keepdims=True))
        carry = carry + jnp.sum(hit, axis=1, keepdims=True)
    return jnp.concatenate(ranks, axis=0).astype(jnp.int32), carry


def _store_row_tiles(ref, val):
    rows = val.shape[0]
    for j in range(val.shape[1] // LANES):
        ref[pl.ds(j, rows, stride=SUBLANES), :] = val[:, j * LANES:(j + 1) * LANES]


def _load_row_tiles(ref, rows):
    return jnp.concatenate([ref[pl.ds(j, rows, stride=SUBLANES), :] for j in range(ref.shape[0] // rows)], axis=1)


def _merge_kernel(oret, oswa, oga, ar, as_, aa, *refs, first_layer, n_lat_tiles, n_tiles, n_batch):
    if first_layer:
        x_ref, c_ref, *refs = refs
    else:
        x_ref, *refs = refs
    (mod_ref, wr_ref, ws_ref, wa_ref, wout_ref, g2_ref, wrt_ref, br_ref,
     xo_ref, h2t_ref, e_ref, r_ref, w_ref, cnt_ref, h2_prev, carry_ref) = refs
    s = pl.program_id(0)
    n_steps = n_batch * n_tiles

    def merge_tile():
        b = s // n_tiles
        i = s - b * n_tiles
        m = mod_ref[pl.ds(jnp.where(i < n_lat_tiles, b, n_batch), 1), :]

        def gate(a_ref):
            return jax.nn.sigmoid(a_ref[0].astype(F32))

        y = (gate(ar) * _dot(oret[0], wr_ref[0]) + gate(as_) * _dot(oswa[0], ws_ref[0])
             + gate(aa) * _dot(oga[0], wa_ref[0]))
        residual = _first_layer_tokens(x_ref, c_ref, i, n_lat_tiles) if first_layer else x_ref[0]
        x = residual + _mod_chunk(m, 2) * _dot(y.astype(BF16), wout_ref[0])
        xo_ref[0] = x
        h2 = _rms_normalize(x) * g2_ref[...] * (1.0 + _mod_chunk(m, 4)) + _mod_chunk(m, 3)
        _store_row_tiles(h2t_ref.at[0], h2)
        h2_prev[...] = h2

    def route_previous_tile():
        h2 = h2_prev[...]
        h_hi = h2.astype(BF16)
        h_lo = (h2 - h_hi.astype(F32)).astype(BF16)
        w = wrt_ref[...]
        w_hi = w.astype(BF16)
        w_lo = (w - w_hi.astype(F32)).astype(BF16)
        logits = _dot_nt(w_hi, h_hi) + _dot_nt(w_hi, h_lo) + _dot_nt(w_lo, h_hi)
        scores = jax.nn.sigmoid(logits)
        experts, weights = _route(scores, scores + br_ref[...])
        ranks, carry = _slot_ranks(experts, carry_ref[:, 0:1])
        e_ref[0] = experts
        r_ref[0] = ranks
        w_ref[0] = weights
        carry_ref[...] = jnp.broadcast_to(carry, carry_ref.shape)
        cnt_ref[...] = jnp.broadcast_to(carry, cnt_ref.shape)

    @pl.when(s == 0)
    def _init():
        carry_ref[...] = jnp.zeros_like(carry_ref)

    @pl.when(s > 0)
    def _routing_step():
        route_previous_tile()

    @pl.when(s < n_steps)
    def _merge_step():
        merge_tile()


def _merge(oret, oswa, oga, ar, as_, aa, xa, mod, w_ret, w_swa, w_ga, wout, layer, g2, wrt, br,
           n_lat_tiles, n_tiles):
    first_layer = isinstance(xa, tuple)
    if first_layer:
        x, ctx = xa
        nb, length, d = x.shape
        nt = length + ctx.shape[1]
        assert n_tiles * TOKEN_TILE == nt, "the first layer must write every row of the joined stream"
    else:
        nb, nt, d = xa.shape
    rows = n_tiles * TOKEN_TILE
    n_steps = nb * n_tiles
    kern = functools.partial(_merge_kernel, first_layer=first_layer, n_lat_tiles=n_lat_tiles, n_tiles=n_tiles,
                             n_batch=nb)

    def cur(s):
        t = jnp.minimum(s, n_steps - 1)
        return t // n_tiles, t % n_tiles

    def prev(s):
        t = jnp.maximum(s - 1, 0)
        return t // n_tiles, t % n_tiles

    tile = pl.BlockSpec((1, TOKEN_TILE, d), lambda s: (*cur(s), 0))
    whole = lambda arr: pl.BlockSpec(arr.shape, lambda s: (0,) * arr.ndim)
    square = pl.BlockSpec((1, d, d), lambda s: (layer, 0, 0))
    small = pl.BlockSpec((1, 2, TOKEN_TILE), lambda s: (prev(s)[0], 0, prev(s)[1]))
    if first_layer:
        residual_specs, residual, aliases = _first_layer_specs(x, ctx, n_lat_tiles, cur), (x, ctx), {}
    else:
        residual_specs, residual, aliases = [tile], (xa,), {6: 0}
    return pl.pallas_call(
        kern,
        out_shape=[jax.ShapeDtypeStruct((nb, nt, d), F32),
                   jax.ShapeDtypeStruct((nb, rows * SUBLANES, LANES), F32),
                   jax.ShapeDtypeStruct((nb, 2, rows), jnp.int32),
                   jax.ShapeDtypeStruct((nb, 2, rows), jnp.int32),
                   jax.ShapeDtypeStruct((nb, 2, rows), F32),
                   jax.ShapeDtypeStruct((N_EXPERTS, LANES), F32)],
        grid=(n_steps + 1,),
        in_specs=[tile, tile, tile, tile, tile, tile, *residual_specs, whole(mod), square, square, square, square,
                  pl.BlockSpec((1, d), lambda s: (0, 0)), whole(wrt), whole(br)],
        out_specs=[tile,
                   pl.BlockSpec((1, TOKEN_TILE * SUBLANES, LANES), lambda s: (*cur(s), 0)),
                   small, small, small,
                   pl.BlockSpec((N_EXPERTS, LANES), lambda s: (0, 0))],
        scratch_shapes=[pltpu.VMEM((TOKEN_TILE, d), F32), pltpu.VMEM((N_EXPERTS, LANES), F32)],
        input_output_aliases=aliases,
        compiler_params=_params(dimension_semantics=("arbitrary",)),
        name="merge_route",
    )(oret, oswa, oga, ar, as_, aa, *residual, mod, w_ret, w_swa, w_ga, wout, g2.reshape(1, d), wrt, br)


def _row_copy(src, dst, sem):
    return pltpu.make_async_copy(src, dst, sem)


def _tile_rows(r):
    return pl.ds(pl.multiple_of(r * SUBLANES, SUBLANES), SUBLANES)


def _dispatch_kernel(tail_ref, nv_ref, dest_ref, h2t_ref, buf_out, zeros_ref, sem, tail_sem):
    rows = h2t_ref.shape[1] // SUBLANES

    @pl.when((pl.program_id(0) == 0) & (pl.program_id(1) == 0))
    def _zero_tails():
        zeros_ref[...] = jnp.zeros_like(zeros_ref)

        def zero_block(first_slot):
            span = pl.ds(pl.multiple_of(first_slot * SUBLANES, SUBLANES), MOE_ROWS * SUBLANES)
            copy = pltpu.make_async_copy(zeros_ref, buf_out.at[span, :], tail_sem)
            copy.start()
            copy.wait()

        for e in range(N_EXPERTS):
            @pl.when(tail_ref[e] >= 0)
            def _():
                zero_block(tail_ref[e])

        def unused(blk, carry):
            zero_block(blk * MOE_ROWS)
            return carry

        lax.fori_loop(nv_ref[0], buf_out.shape[0] // (MOE_ROWS * SUBLANES), unused, 0)

    def issue(r, carry):
        for k in range(2):
            d = dest_ref[0, 0, k * rows + r]
            _row_copy(h2t_ref.at[0, _tile_rows(r), :], buf_out.at[_tile_rows(d), :], sem).start(priority=k)
        return carry

    lax.fori_loop(0, rows, issue, 0, unroll=8)

    def drain(r, carry):
        for k in range(2):
            _row_copy(h2t_ref.at[0, _tile_rows(0), :], buf_out.at[_tile_rows(0), :], sem).wait()
        return carry

    lax.fori_loop(0, rows, drain, 0, unroll=8)


def _dispatch(tail_start, n_valid, dest_rows, h2t, n_slots):
    nb, rows8, _ = h2t.shape
    step_rows8 = rows8 // DISPATCH_STEPS
    grid_spec = pltpu.PrefetchScalarGridSpec(
        num_scalar_prefetch=2,
        grid=(nb, DISPATCH_STEPS),
        in_specs=[pl.BlockSpec((1, 1, dest_rows.shape[2]), lambda b, i, tail, nv: (b * DISPATCH_STEPS + i, 0, 0),
                               memory_space=pltpu.SMEM),
                  pl.BlockSpec((1, step_rows8, LANES), lambda b, i, tail, nv: (b, i, 0))],
        out_specs=pl.BlockSpec(memory_space=pl.ANY),
        scratch_shapes=[pltpu.VMEM((MOE_ROWS * SUBLANES, LANES), F32),
                        pltpu.SemaphoreType.DMA(()), pltpu.SemaphoreType.DMA(())],
    )
    return pl.pallas_call(
        _dispatch_kernel,
        out_shape=jax.ShapeDtypeStruct((n_slots * SUBLANES, LANES), F32),
        grid_spec=grid_spec,
        compiler_params=_params(has_side_effects=True, dimension_semantics=("arbitrary", "arbitrary")),
        name="moe_dispatch",
    )(tail_start, n_valid, dest_rows, h2t)


def _expert_kernel(be_ref, nv_ref, x_ref, wg_ref, wu_ref, wd_ref, y_ref, wg_bf, wu_bf, wd_bf):
    i = pl.program_id(0)

    @pl.when(i < nv_ref[0])
    def _():
        @pl.when((i == 0) | (be_ref[i] != be_ref[jnp.maximum(i - 1, 0)]))
        def _():
            wg_bf[...] = wg_ref[0, 0].astype(BF16)
            wu_bf[...] = wu_ref[0, 0].astype(BF16)
            wd_bf[...] = wd_ref[0, 0].astype(BF16)

        x = _load_row_tiles(x_ref, MOE_ROWS).astype(BF16)
        hid = _silu(_dot(x, wg_bf[...])) * _dot(x, wu_bf[...])
        _store_row_tiles(y_ref, _dot(hid.astype(BF16), wd_bf[...]))

    @pl.when(i >= nv_ref[0])
    def _():
        y_ref[...] = jnp.zeros_like(y_ref)


def _experts(block_e, n_valid, buf, wg, wu, wd, layer):
    n_slots = buf.shape[0] // SUBLANES
    d = wg.shape[2]
    used = lambda i, nv: jnp.maximum(jnp.minimum(i, nv[0] - 1), 0)
    rows = pl.BlockSpec((MOE_ROWS * SUBLANES, LANES), lambda i, be, nv: (used(i, nv), 0))
    grid_spec = pltpu.PrefetchScalarGridSpec(
        num_scalar_prefetch=2,
        grid=(n_slots // MOE_ROWS,),
        in_specs=[rows,
                  pl.BlockSpec((1, 1, d, D_EXPERT), lambda i, be, nv: (layer, be[used(i, nv)], 0, 0)),
                  pl.BlockSpec((1, 1, d, D_EXPERT), lambda i, be, nv: (layer, be[used(i, nv)], 0, 0)),
                  pl.BlockSpec((1, 1, D_EXPERT, d), lambda i, be, nv: (layer, be[used(i, nv)], 0, 0))],
        out_specs=pl.BlockSpec((MOE_ROWS * SUBLANES, LANES), lambda i, be, nv: (i, 0)),
        scratch_shapes=[pltpu.VMEM((d, D_EXPERT), BF16), pltpu.VMEM((d, D_EXPERT), BF16),
                        pltpu.VMEM((D_EXPERT, d), BF16)],
    )
    return pl.pallas_call(
        _expert_kernel,
        out_shape=jax.ShapeDtypeStruct((n_slots * SUBLANES, LANES), F32),
        grid_spec=grid_spec,
        compiler_params=_params(dimension_semantics=("arbitrary",)),
        name="moe_experts",
    )(block_e, n_valid, buf, wg, wu, wd)


def _combine_kernel(dest_ref, next_ref, y_hbm, wt_ref, x_ref, mod_ref, g_ref, *rest,
                    n_lat_tiles, n_tiles, n_batch, last):
    if last:
        out_ref, gbuf, sems = rest
    else:
        xo_ref, h_ref, gbuf, sems = rest
    t = pl.program_id(0)
    slot = lax.rem(t, 2)

    def gather(idx_ref, into):
        def issue(r, carry):
            for k in range(2):
                d = idx_ref[0, 0, k * TOKEN_TILE + r]
                _row_copy(y_hbm.at[_tile_rows(d), :], gbuf.at[into, k, _tile_rows(r), :],
                          sems.at[into]).start(priority=k)
            return carry

        lax.fori_loop(0, TOKEN_TILE, issue, 0, unroll=8)

    @pl.when(t == 0)
    def _():
        gather(dest_ref, 0)

    @pl.when(t + 1 < n_batch * n_tiles)
    def _():
        gather(next_ref, 1 - slot)

    def drain(r, carry):
        for k in range(2):
            _row_copy(y_hbm.at[_tile_rows(0), :], gbuf.at[slot, 0, _tile_rows(0), :], sems.at[slot]).wait()
        return carry

    lax.fori_loop(0, TOKEN_TILE, drain, 0, unroll=8)

    b = t // n_tiles
    m = mod_ref[pl.ds(jnp.where(t - b * n_tiles < n_lat_tiles, b, n_batch), 1), :]
    wt = wt_ref[0]
    moe = (_load_row_tiles(gbuf.at[slot, 0], TOKEN_TILE) * wt[:, 0:1]
           + _load_row_tiles(gbuf.at[slot, 1], TOKEN_TILE) * wt[:, 1:2])
    x = x_ref[0] + _mod_chunk(m, 5) * moe
    if last:
        out_ref[0] = _rms_normalize(x) * g_ref[...]
    else:
        xo_ref[0] = x
        h = _rms_normalize(x) * g_ref[...] * (1.0 + _mod_chunk(m, 7)) + _mod_chunk(m, 6)
        h_ref[0] = h.astype(BF16)


def _combine(dest_tiles, y, wt, xa, mod, g, n_lat_tiles, n_tiles, last):
    nb, nt, d = xa.shape
    rows = n_tiles * TOKEN_TILE
    n_steps = nb * n_tiles
    kern = functools.partial(_combine_kernel, n_lat_tiles=n_lat_tiles, n_tiles=n_tiles, n_batch=nb, last=last)
    tile = pl.BlockSpec((1, TOKEN_TILE, d), lambda t: (t // n_tiles, t % n_tiles, 0))
    if last:
        out_shape = jax.ShapeDtypeStruct((nb, rows, d), F32)
        out_specs = tile
        aliases = {}
    else:
        out_shape = [jax.ShapeDtypeStruct((nb, nt, d), F32), jax.ShapeDtypeStruct((nb, nt, d), BF16)]
        out_specs = [tile, tile]
        aliases = {4: 0}
    return pl.pallas_call(
        kern,
        out_shape=out_shape,
        grid=(n_steps,),
        in_specs=[pl.BlockSpec((1, 1, 2 * TOKEN_TILE), lambda t: (t, 0, 0), memory_space=pltpu.SMEM),
                  pl.BlockSpec((1, 1, 2 * TOKEN_TILE), lambda t: (jnp.minimum(t + 1, n_steps - 1), 0, 0),
                               memory_space=pltpu.SMEM),
                  pl.BlockSpec(memory_space=pl.ANY),
                  pl.BlockSpec((1, TOKEN_TILE, 2), lambda t: (t // n_tiles, t % n_tiles, 0)),
                  tile,
                  pl.BlockSpec(mod.shape, lambda t: (0, 0)),
                  pl.BlockSpec((1, d), lambda t: (0, 0))],
        out_specs=out_specs,
        scratch_shapes=[pltpu.VMEM((2, 2, TOKEN_TILE * SUBLANES, LANES), F32), pltpu.SemaphoreType.DMA((2,))],
        input_output_aliases=aliases,
        compiler_params=_params(dimension_semantics=("arbitrary",)),
        name="moe_combine",
    )(dest_tiles, dest_tiles, y, wt, xa, mod, g.reshape(1, d))


def _slot_plan(experts, ranks, counts):
    nb, _, rows = experts.shape
    padded = (counts + MOE_ROWS - 1) // MOE_ROWS * MOE_ROWS
    pad_ends = jnp.cumsum(padded)
    pad_starts = pad_ends - padded
    one_hot = experts[..., None] == jnp.arange(N_EXPERTS, dtype=jnp.int32)
    dest = (jnp.sum(jnp.where(one_hot, pad_starts, 0), axis=-1) + ranks).astype(jnp.int32)
    n_assign = nb * 2 * rows
    n_blocks = (n_assign + N_EXPERTS * (MOE_ROWS - 1) + MOE_ROWS - 1) // MOE_ROWS
    first_slot = jnp.arange(n_blocks, dtype=jnp.int32) * MOE_ROWS
    block_e = jnp.minimum(jnp.sum((pad_ends[None, :] <= first_slot[:, None]).astype(jnp.int32), axis=1),
                          N_EXPERTS - 1).astype(jnp.int32)
    n_valid = (pad_ends[-1:] // MOE_ROWS).astype(jnp.int32)
    tail_start = jnp.where(padded > 0, pad_ends - MOE_ROWS, -1).astype(jnp.int32)
    n_tiles = rows // TOKEN_TILE
    dest_tiles = (dest.reshape(nb, 2, n_tiles, TOKEN_TILE)
                  .transpose(0, 2, 1, 3).reshape(nb * n_tiles, 1, 2 * TOKEN_TILE))
    step_rows = rows // DISPATCH_STEPS
    dest_rows = (dest.reshape(nb, 2, DISPATCH_STEPS, step_rows)
                 .transpose(0, 2, 1, 3).reshape(nb * DISPATCH_STEPS, 1, 2 * step_rows))
    return dest_tiles, dest_rows, block_e, n_valid, tail_start, n_blocks * MOE_ROWS


def _rope_tables(length, n_ctx):
    rows = length // GRID_W
    row = jnp.repeat(jnp.arange(rows, dtype=jnp.int32), GRID_W).astype(F32)
    col = jnp.tile(jnp.arange(GRID_W, dtype=jnp.int32), rows).astype(F32)

    def table(hd):
        quarter = hd // 4
        freqs = ROPE_THETA ** (-jnp.arange(quarter, dtype=F32) / quarter)
        ang_r = row[:, None] * freqs[None, :]
        ang_c = col[:, None] * freqs[None, :]
        cos = jnp.concatenate([jnp.cos(ang_r), jnp.cos(ang_r), jnp.cos(ang_c), jnp.cos(ang_c)], axis=-1)
        sin = jnp.concatenate([-jnp.sin(ang_r), jnp.sin(ang_r), -jnp.sin(ang_c), jnp.sin(ang_c)], axis=-1)
        cos = jnp.concatenate([cos, jnp.ones((n_ctx, hd), F32)], axis=0)
        sin = jnp.concatenate([sin, jnp.zeros((n_ctx, hd), F32)], axis=0)
        reps = LANES // hd
        return jnp.tile(cos, (1, reps)), jnp.tile(sin, (1, reps))

    c128, s128 = table(RET_DK)
    c64, s64 = table(SWA_HD)
    return c128, s128, c64, s64


def kernel(x, c, ctx, c_ctx, w_mod, b_mod, g_norm1, g_norm2, w_in, ret_decay_logit, swa_sink, g_qnorm, g_knorm,
           w_br_ret, w_br_swa, w_br_ga, w_out, w_router, b_router, w_gate, w_up, w_down, g_final):
    nb, length, d = x.shape
    n_ctx = ctx.shape[1]
    depth = w_mod.shape[0]
    nt = length + n_ctx
    n_lat_tiles = length // TOKEN_TILE
    n_all_tiles = nt // TOKEN_TILE
    assert GA_HD == RET_DK == LANES and 2 * SWA_HD == LANES
    assert length % TOKEN_TILE == 0 and n_ctx == TOKEN_TILE and length % n_ctx == 0

    mod_rows = 8
    c_rows = jnp.concatenate([c, c_ctx[None, :], jnp.zeros((mod_rows - nb - 1, d), F32)], axis=0)
    mods = _modulation(c_rows, w_mod, b_mod)
    tables = _rope_tables(length, n_ctx)
    wrt = w_router.astype(F32).T
    br = b_router.astype(F32).reshape(N_EXPERTS, 1)

    assert depth >= 2, "the first layer's merge kernel must also cover the context rows"
    xa = (x, ctx)
    h = (x, ctx, mods[0], g_norm1[0])
    w_in_bf, w_ret_bf, w_swa_bf, w_ga_bf, w_out_bf = (
        w.astype(BF16) for w in (w_in, w_br_ret, w_br_swa, w_br_ga, w_out))
    out = None
    for l in range(depth):
        need_ctx = l < depth - 1
        n_tiles = n_all_tiles if need_ctx else n_lat_tiles
        (rq, rk, rv, ru, sq, sk, sv, gq, gk, gv, ar, as_, aa) = _inproj(
            h, w_in_bf, l, tables, g_qnorm[l], g_knorm[l])
        o_ret = _retention(ret_decay_logit[l].astype(F32), rq, rk, rv, ru, length, n_ctx, need_ctx)
        o_swa = _window_attention(swa_sink[l].astype(F32), sq, sk, sv, length, n_ctx, need_ctx)
        o_ga = _global_attention(gq, gk, gv, length, n_ctx, need_ctx)
        xa, h2t, experts, ranks, weights, counts = _merge(
            o_ret, o_swa, o_ga, ar, as_, aa, xa, mods[l], w_ret_bf, w_swa_bf, w_ga_bf, w_out_bf, l,
            g_norm2[l], wrt, br, n_lat_tiles, n_tiles)
        dest_tiles, dest_rows, block_e, n_valid, tail_start, n_slots = _slot_plan(
            experts, ranks, counts[:, 0].astype(jnp.int32))
        buf = _dispatch(tail_start, n_valid, dest_rows, h2t, n_slots)
        y = _experts(block_e, n_valid, buf, w_gate, w_up, w_down, l)
        wt = weights.transpose(0, 2, 1)
        if need_ctx:
            mod_pair = jnp.concatenate([mods[l], mods[l + 1][:, :2 * d]], axis=1)
            xa, h = _combine(dest_tiles, y, wt, xa, mod_pair, g_norm1[l + 1], n_lat_tiles, n_tiles, last=False)
        else:
            out = _combine(dest_tiles, y, wt, xa, mods[l], g_final, n_lat_tiles, n_tiles, last=True)
    return out
```
